```python
import math
import jax, jax.numpy as jnp
from jax import lax
import numpy as np

D_MODEL = 2048
BATCH = 4
SEQ = 2048
DEPTH = 4
DEC_BATCH = 8
DEC_SEQ = 8
PAST_LEN = 16384
PAGE_SIZE = 128

N_A_LAYERS = DEPTH // 2
N_B_LAYERS = DEPTH - N_A_LAYERS
HEAD_DIM = 128
N_HEADS = D_MODEL // HEAD_DIM
A_KV_HEADS = 4
A_GROUP = N_HEADS // A_KV_HEADS
IDX_HEADS = 16
IDX_DIM = 64
TOPK_MAX = 256
B_GROUPS = ((128, 1), (512, 4), (2048, 16))
N_B_GROUPS = len(B_GROUPS)
B_KV_HEADS = 4
B_GROUP = N_HEADS // B_KV_HEADS
B_WINDOW_MAX = max(w for w, _ in B_GROUPS)
D_FF = ((8 * D_MODEL // 3 + 127) // 128) * 128
CONV_WIDTH = 3
NUM_BUCKETS = 32
T5_MAX_DISTANCE = 2048
Q_BLOCK = 128
LN_EPS = 1e-5
ALPHA = (2 * DEPTH) ** 0.25
BETA = (8 * DEPTH) ** -0.25
NEG = -1e30

A_Q = N_HEADS * HEAD_DIM
A_KV = A_KV_HEADS * HEAD_DIM
A_QI = IDX_HEADS * IDX_DIM
A_SPLITS = (A_Q, A_Q + A_KV, A_Q + 2 * A_KV, A_Q + 2 * A_KV + A_QI, A_Q + 2 * A_KV + A_QI + IDX_DIM)
A_IN = A_SPLITS[-1] + IDX_HEADS
B_Q = N_B_GROUPS * N_HEADS * HEAD_DIM
B_KV = B_KV_HEADS * HEAD_DIM

kernel_name = 'yoco_dsa_dilated_convffn_step'


def layer_norm(x, g, b):
    x32 = x.astype(jnp.float32)
    mu = x32.mean(-1, keepdims=True)
    var = jnp.square(x32 - mu).mean(-1, keepdims=True)
    return ((x32 - mu) * lax.rsqrt(var + LN_EPS) * g + b).astype(x.dtype)


def t5_bucket(dist):
    dist = jnp.maximum(dist, 0)
    exact = NUM_BUCKETS // 2
    far = exact + (jnp.log(jnp.maximum(dist, 1).astype(jnp.float32) / exact)
                   / math.log(T5_MAX_DISTANCE / exact) * (NUM_BUCKETS - exact)).astype(jnp.int32)
    return jnp.where(dist < exact, dist, jnp.minimum(far, NUM_BUCKETS - 1))


def gather_rows(x, idx):
    return jax.vmap(lambda xb, ib: xb[ib])(x, idx)


def over_query_blocks(fn, arrays, qpos):
    T = qpos.shape[0]
    if T % Q_BLOCK:
        return fn(*arrays, qpos)
    nb = T // Q_BLOCK
    blocks = tuple(jnp.moveaxis(a.reshape(a.shape[0], nb, Q_BLOCK, *a.shape[2:]), 1, 0) for a in arrays)
    out = lax.map(lambda args: fn(*args), blocks + (qpos.reshape(nb, Q_BLOCK),))
    out = jnp.moveaxis(out, 0, 1)
    return out.reshape(out.shape[0], T, *out.shape[3:])


def conv_ffn(h, state, w_up, conv_w, conv_b, w_down):
    T = h.shape[1]
    gate, up = jnp.split(h @ w_up, 2, axis=-1)
    ext = jnp.concatenate([state.astype(gate.dtype), gate], axis=1)
    conv = conv_b + sum(ext[:, j:j + T] * conv_w[j] for j in range(CONV_WIDTH))
    return (jax.nn.silu(conv) * up) @ w_down, ext[:, ext.shape[1] - (CONV_WIDTH - 1):]


def a_project(h, w_in, kn_g, kn_b):
    B, T = h.shape[:2]
    q, k, v, qi, ki, wi = jnp.split(h @ w_in, A_SPLITS, axis=-1)
    return (q.reshape(B, T, A_KV_HEADS, A_GROUP, HEAD_DIM),
            k.reshape(B, T, A_KV_HEADS, HEAD_DIM),
            v.reshape(B, T, A_KV_HEADS, HEAD_DIM),
            qi.reshape(B, T, IDX_HEADS, IDX_DIM),
            layer_norm(ki, kn_g, kn_b),
            wi * IDX_HEADS ** -0.5)


def indexer_scores(qi, wi, ki):
    s = jnp.einsum('bqhd,bsd->bqhs', qi, ki, preferred_element_type=jnp.float32) * IDX_DIM ** -0.5
    return jnp.einsum('bqhs,bqh->bqs', jax.nn.relu(s), wi.astype(jnp.float32))


def attend_selected(q, k_sel, v_sel, valid, dist, rel_bias):
    G, R = q.shape[2], q.shape[3]
    logits = jnp.einsum('bqgrd,bqkgd->bqgrk', q, k_sel, preferred_element_type=jnp.float32) * HEAD_DIM ** -0.5
    bias = jnp.moveaxis(rel_bias[t5_bucket(dist)].reshape(*dist.shape, G, R), 2, -1)
    logits = jnp.where(valid[:, :, None, None, :], logits + bias, NEG)
    p = jax.nn.softmax(logits, axis=-1)
    return jnp.einsum('bqgrk,bqkgd->bqgrd', p.astype(v_sel.dtype), v_sel)


def mixer_a_prompt(h, w_in, w_o, kn_g, kn_b, rel_bias):
    B, T = h.shape[:2]
    q, k, v, qi, ki, wi = a_project(h, w_in, kn_g, kn_b)
    topk = min(TOPK_MAX, T // 4)
    key_pos = jnp.arange(T)

    def block(qb, qib, wib, qpos):
        sc = indexer_scores(qib, wib, ki)
        sc = jnp.where(key_pos[None, None, :] <= qpos[None, :, None], sc, -jnp.inf)
        _, idx = lax.top_k(sc, topk)
        return attend_selected(qb, gather_rows(k, idx), gather_rows(v, idx),
                               idx <= qpos[None, :, None], qpos[None, :, None] - idx, rel_bias)

    o = over_query_blocks(block, (q, qi, wi), jnp.arange(T))
    return o.reshape(B, T, -1) @ w_o, k, v, ki


def mixer_a_sample(h, w_in, w_o, kn_g, kn_b, ck, cv, cki, page_table, rel_bias):
    DB, T = h.shape[:2]
    page = ck.shape[1]
    past = page_table.shape[1] * page
    q, k, v, qi, ki, wi = a_project(h, w_in, kn_g, kn_b)
    topk = min(TOPK_MAX, (past + T) // 4)
    keys_idx = jnp.concatenate([cki[page_table].reshape(DB, past, IDX_DIM).astype(ki.dtype), ki], axis=1)
    qpos = past + jnp.arange(T)
    sc = indexer_scores(qi, wi, keys_idx)
    sc = jnp.where(jnp.arange(past + T)[None, None, :] <= qpos[None, :, None], sc, -jnp.inf)
    _, idx = lax.top_k(sc, topk)
    in_past = idx < past
    pidx = jnp.minimum(idx, past - 1)
    phys = page_table[jnp.arange(DB)[:, None, None], pidx // page] * page + pidx % page
    nidx = jnp.clip(idx - past, 0, T - 1)

    def pick(cache, new):
        old = cache.reshape(-1, *cache.shape[2:])[phys]
        return jnp.where(in_past[..., None, None], old.astype(new.dtype), gather_rows(new, nidx))

    o = attend_selected(q, pick(ck, k), pick(cv, v), idx <= qpos[None, :, None],
                        qpos[None, :, None] - idx, rel_bias)
    return o.reshape(DB, T, -1) @ w_o, k, v, ki


def shared_kv(h, w_kv):
    B, T = h.shape[:2]
    k, v = jnp.split(h @ w_kv, 2, axis=-1)
    return k.reshape(B, T, B_KV_HEADS, HEAD_DIM), v.reshape(B, T, B_KV_HEADS, HEAD_DIM)


def dilated_block(qb, qpos, k_all, v_all, rel_bias):
    lses, outs = [], []
    for g, (window, dil) in enumerate(B_GROUPS):
        dist = jnp.arange(window // dil + 1, dtype=jnp.int32) * dil
        idx = qpos[:, None] - dist[None, :]
        valid = idx >= 0
        idx = jnp.maximum(idx, 0)
        ks, vs = k_all[:, idx], v_all[:, idx]
        logits = jnp.einsum('bqgrd,bqjgd->bqgrj', qb[:, :, g], ks,
                            preferred_element_type=jnp.float32) * HEAD_DIM ** -0.5
        bias = rel_bias[t5_bucket(dist)].T.reshape(B_KV_HEADS, B_GROUP, -1)
        logits = jnp.where(valid[None, :, None, None, :], logits + bias, NEG)
        m = logits.max(-1, keepdims=True)
        e = jnp.exp(logits - m)
        s = e.sum(-1)
        outs.append(jnp.einsum('bqgrj,bqjgd->bqgrd', e, vs.astype(jnp.float32)) / s[..., None])
        lses.append(m[..., 0] + jnp.log(s))
    wgt = jax.nn.softmax(jnp.stack(lses), axis=0)
    return jnp.einsum('nbqgr,nbqgrd->bqgrd', wgt, jnp.stack(outs)).astype(qb.dtype)


def mixer_b(h, w_q, w_o, k_all, v_all, qpos, rel_bias):
    B, T = h.shape[:2]
    q = (h @ w_q).reshape(B, T, N_B_GROUPS, B_KV_HEADS, B_GROUP, HEAD_DIM)
    o = over_query_blocks(lambda qb, p: dilated_block(qb, p, k_all, v_all, rel_bias), (q,), qpos)
    return o.reshape(B, T, -1) @ w_o


def setup_inputs(seed: int = 0) -> dict:
    key = jax.random.key(seed)
    ks = jax.random.split(key, 24)
    f32 = jnp.float32
    nrm = lambda k, shape, s=1.0: s * jax.random.normal(k, shape, f32)
    n_pages = PAST_LEN // PAGE_SIZE
    n_used = DEC_BATCH * n_pages
    n_phys = n_used + max(1, n_used // 4)
    w_buf = min(B_WINDOW_MAX, PAST_LEN)
    a_col = jnp.ones((A_IN,), f32).at[A_SPLITS[1]:A_SPLITS[2]].set(BETA)
    kv_col = jnp.concatenate([jnp.ones((B_KV,), f32), jnp.full((B_KV,), BETA, f32)])
    page_table = jax.random.permutation(ks[5], n_phys)[:n_used].reshape(DEC_BATCH, n_pages).astype(jnp.int32)
    return {
        'x_prompt': nrm(ks[0], (BATCH, SEQ, D_MODEL)),
        'x_sample': nrm(ks[1], (DEC_BATCH, DEC_SEQ, D_MODEL)),
        'cache_k_a': nrm(ks[2], (N_A_LAYERS, n_phys, PAGE_SIZE, A_KV_HEADS, HEAD_DIM)),
        'cache_v_a': nrm(ks[3], (N_A_LAYERS, n_phys, PAGE_SIZE, A_KV_HEADS, HEAD_DIM), BETA),
        'cache_kidx_a': nrm(ks[4], (N_A_LAYERS, n_phys, PAGE_SIZE, IDX_DIM)),
        'cache_k_b': nrm(ks[6], (DEC_BATCH, w_buf, B_KV_HEADS, HEAD_DIM)),
        'cache_v_b': nrm(ks[7], (DEC_BATCH, w_buf, B_KV_HEADS, HEAD_DIM), BETA),
        'state_ffn': nrm(ks[8], (DEPTH, DEC_BATCH, CONV_WIDTH - 1, D_FF)),
        'page_table': page_table,
        'a_w_in': nrm(ks[9], (N_A_LAYERS, D_MODEL, A_IN), D_MODEL ** -0.5) * a_col,
        'a_w_o': nrm(ks[10], (N_A_LAYERS, A_Q, D_MODEL), A_Q ** -0.5 * BETA),
        'a_kn_g': 1.0 + nrm(ks[11], (N_A_LAYERS, IDX_DIM), 0.01),
        'a_kn_b': nrm(ks[12], (N_A_LAYERS, IDX_DIM), 0.01),
        'b_w_kv': nrm(ks[13], (D_MODEL, 2 * B_KV), D_MODEL ** -0.5) * kv_col,
        'b_w_q': nrm(ks[14], (N_B_LAYERS, D_MODEL, B_Q), D_MODEL ** -0.5),
        'b_w_o': nrm(ks[15], (N_B_LAYERS, N_HEADS * HEAD_DIM, D_MODEL), (N_HEADS * HEAD_DIM) ** -0.5 * BETA),
        'ffn_w_up': nrm(ks[16], (DEPTH, D_MODEL, 2 * D_FF), D_MODEL ** -0.5),
        'ffn_conv_w': nrm(ks[17], (DEPTH, CONV_WIDTH, D_FF), CONV_WIDTH ** -0.5),
        'ffn_conv_b': nrm(ks[18], (DEPTH, D_FF), 0.01),
        'ffn_w_down': nrm(ks[19], (DEPTH, D_FF, D_MODEL), D_FF ** -0.5 * BETA),
        'ln_g': 1.0 + nrm(ks[20], (DEPTH, 2, D_MODEL), 0.01),
        'ln_b': nrm(ks[21], (DEPTH, 2, D_MODEL), 0.01),
        'rel_bias': nrm(ks[22], (NUM_BUCKETS, N_HEADS), 0.5),
    }


def reference(x_prompt, x_sample, cache_k_a, cache_v_a, cache_kidx_a, cache_k_b, cache_v_b, state_ffn,
              page_table, a_w_in, a_w_o, a_kn_g, a_kn_b, b_w_kv, b_w_q, b_w_o, ffn_w_up, ffn_conv_w,
              ffn_conv_b, ffn_w_down, ln_g, ln_b, rel_bias):
    B, T = x_prompt.shape[:2]
    TS = x_sample.shape[1]
    hp, hs = x_prompt, x_sample
    zero_state = jnp.zeros((B, CONV_WIDTH - 1, D_FF), x_prompt.dtype)
    ka_p, va_p, kia_p, ka_s, va_s, kia_s, ffn_p, ffn_s = [], [], [], [], [], [], [], []
    for layer in range(DEPTH):
        if layer < N_A_LAYERS:
            a = layer
            mp, k, v, ki = mixer_a_prompt(hp, a_w_in[a], a_w_o[a], a_kn_g[a], a_kn_b[a], rel_bias)
            ka_p.append(k); va_p.append(v); kia_p.append(ki)
            ms, k, v, ki = mixer_a_sample(hs, a_w_in[a], a_w_o[a], a_kn_g[a], a_kn_b[a], cache_k_a[a],
                                          cache_v_a[a], cache_kidx_a[a], page_table, rel_bias)
            ka_s.append(k); va_s.append(v); kia_s.append(ki)
        else:
            if layer == N_A_LAYERS:
                kb_p, vb_p = shared_kv(hp, b_w_kv)
                kb_s, vb_s = shared_kv(hs, b_w_kv)
                kb_all_s = jnp.concatenate([cache_k_b.astype(kb_s.dtype), kb_s], axis=1)
                vb_all_s = jnp.concatenate([cache_v_b.astype(vb_s.dtype), vb_s], axis=1)
            bl = layer - N_A_LAYERS
            mp = mixer_b(hp, b_w_q[bl], b_w_o[bl], kb_p, vb_p, jnp.arange(T), rel_bias)
            ms = mixer_b(hs, b_w_q[bl], b_w_o[bl], kb_all_s, vb_all_s,
                         cache_k_b.shape[1] + jnp.arange(TS), rel_bias)
        hp = layer_norm(ALPHA * hp + mp, ln_g[layer, 0], ln_b[layer, 0])
        hs = layer_norm(ALPHA * hs + ms, ln_g[layer, 0], ln_b[layer, 0])
        fp, sp = conv_ffn(hp, zero_state, ffn_w_up[layer], ffn_conv_w[layer], ffn_conv_b[layer], ffn_w_down[layer])
        fs, ss = conv_ffn(hs, state_ffn[layer], ffn_w_up[layer], ffn_conv_w[layer], ffn_conv_b[layer], ffn_w_down[layer])
        ffn_p.append(sp); ffn_s.append(ss)
        hp = layer_norm(ALPHA * hp + fp, ln_g[layer, 1], ln_b[layer, 1])
        hs = layer_norm(ALPHA * hs + fs, ln_g[layer, 1], ln_b[layer, 1])
    keep = min(B_WINDOW_MAX, T)
    return (hp, hs, jnp.stack(ka_p), jnp.stack(va_p), jnp.stack(kia_p), jnp.stack(ka_s), jnp.stack(va_s),
            jnp.stack(kia_s), kb_p[:, T - keep:], vb_p[:, T - keep:], kb_s, vb_s,
            jnp.stack(ffn_p), jnp.stack(ffn_s))
```

```python
import functools
import math

import numpy as np
import jax
import jax.numpy as jnp
from jax import lax
from jax.experimental import pallas as pl
from jax.experimental.pallas import tpu as pltpu

F32 = jnp.float32
BF16 = jnp.bfloat16
I32 = jnp.int32

HEAD_DIM = 128
N_HEADS = 16
KV_HEADS = 4
GROUP = N_HEADS // KV_HEADS
IDX_HEADS = 16
IDX_DIM = 64
TOPK_MAX = 256
B_GROUPS = ((128, 1), (512, 4), (2048, 16))
NUM_BUCKETS = 32
T5_MAX_DISTANCE = 2048
LN_EPS = 1e-5
NEG = -1e30
CONV_WIDTH = 3

LANES = 128
QB = 128
KC = 512
TZ_OFF = KC - QB
SAT_DIST = T5_MAX_DISTANCE + QB
INT_MIN = -2 ** 31
KEY_NEG_INF = INT_MIN + 0x7FFFFF
VMEM_LIMIT = 56 * 1024 * 1024

NT_DIMS = (((1,), (1,)), ((), ()))


def _nt(a, b):
    return lax.dot_general(a, b, NT_DIMS, preferred_element_type=F32)


def _params(sem):
    return pltpu.CompilerParams(dimension_semantics=sem, vmem_limit_bytes=VMEM_LIMIT)


def _mm_kernel(x_ref, w_ref, o_ref):
    o_ref[...] = jnp.dot(x_ref[...], w_ref[...], preferred_element_type=F32).astype(o_ref.dtype)


def _mm(x, w, tm, tn, out_dtype=F32):
    M, K = x.shape
    N = w.shape[1]
    assert M % tm == 0 and N % tn == 0, (M, N, tm, tn)
    return pl.pallas_call(
        _mm_kernel,
        grid=(M // tm, N // tn),
        in_specs=[pl.BlockSpec((tm, K), lambda i, j: (i, 0)),
                  pl.BlockSpec((K, tn), lambda i, j: (0, j))],
        out_specs=pl.BlockSpec((tm, tn), lambda i, j: (i, j)),
        out_shape=jax.ShapeDtypeStruct((M, N), out_dtype),
        compiler_params=_params(("parallel", "arbitrary")),
        name="mm",
    )(x, w)


def _t5_bucket(dist):
    dist = jnp.maximum(dist, 0)
    exact = NUM_BUCKETS // 2
    far = exact + (jnp.log(jnp.maximum(dist, 1).astype(F32) / exact)
                   / math.log(T5_MAX_DISTANCE / exact) * (NUM_BUCKETS - exact)).astype(I32)
    return jnp.where(dist < exact, dist, jnp.minimum(far, NUM_BUCKETS - 1))


def _table_kernel(rb_ref, idx_ref, o_ref):
    idx = idx_ref[...]
    for h in range(N_HEADS):
        acc = jnp.full(idx.shape, NEG, F32)
        for k in range(NUM_BUCKETS):
            acc = jnp.where(idx == k, rb_ref[k, h], acc)
        o_ref[h] = acc


def _bias_tables(rel_bias, idx):
    R = idx.shape[0]
    tr = 128
    assert R % tr == 0
    return pl.pallas_call(
        _table_kernel,
        grid=(R // tr,),
        in_specs=[pl.BlockSpec(memory_space=pltpu.SMEM),
                  pl.BlockSpec((tr, LANES), lambda i: (i, 0))],
        out_specs=pl.BlockSpec((N_HEADS, tr, LANES), lambda i: (0, i, 0)),
        out_shape=jax.ShapeDtypeStruct((N_HEADS, R, LANES), F32),
        compiler_params=_params(("arbitrary",)),
        name="bias_tables",
    )(rel_bias, idx)


def _sortable(x):
    bits = lax.bitcast_convert_type(x, I32)
    return bits ^ ((bits >> 31) & jnp.int32(0x7FFFFFFF))


def _kth_largest(count_ge, shape, k):
    def body(t, ans):
        cand_u = ans | (jnp.int32(1) << (31 - t))
        cnt = count_ge(cand_u ^ jnp.int32(INT_MIN))
        return jnp.where(cnt >= k, cand_u, ans)
    ans = lax.fori_loop(0, 32, body, jnp.zeros(shape, I32))
    return ans ^ jnp.int32(INT_MIN)


def _tie_bound(count_eq_below, shape, need, nbits):
    def body(t, ans):
        cand = ans | (jnp.int32(1) << (nbits - 1 - t))
        return jnp.where(count_eq_below(cand) < need, cand, ans)
    return lax.fori_loop(0, nbits, body, jnp.zeros(shape, I32))


def _a1p_kernel(kin_ref, qi_ref, tail_ref, mask_ref, st_ref, jb_ref, *, topk):
    i = pl.program_id(1)
    T = kin_ref.shape[0]
    nchunk = T // KC
    nbits = (T - 1).bit_length()
    tail_t = tail_ref[...].T
    wi_t = tail_t[IDX_DIM:IDX_DIM + IDX_HEADS, :] * (IDX_HEADS ** -0.5 * IDX_DIM ** -0.5)
    qpos = i * QB + lax.broadcasted_iota(I32, (KC, QB), 1)
    row = lax.broadcasted_iota(I32, (KC, QB), 0)

    for c in range(nchunk):
        @pl.when(c * KC < (i + 1) * QB)
        def _():
            kc = kin_ref[c * KC:(c + 1) * KC, :]
            acc = jnp.zeros((KC, QB), F32)
            for h in range(IDX_HEADS):
                qh = qi_ref[:, h * IDX_DIM:(h + 1) * IDX_DIM].astype(BF16)
                acc = acc + jnp.maximum(_nt(kc, qh), 0.0) * wi_t[h:h + 1, :]
            acc = jnp.where(c * KC + row <= qpos, acc, -jnp.inf)
            st_ref[c * KC:(c + 1) * KC, :] = _sortable(acc)

        @pl.when(c * KC >= (i + 1) * QB)
        def _():
            st_ref[c * KC:(c + 1) * KC, :] = jnp.full((KC, QB), KEY_NEG_INF, I32)

    def count(pred):
        cnt = jnp.zeros((1, QB), F32)
        for c in range(nchunk):
            blk = st_ref[c * KC:(c + 1) * KC, :]
            cnt = cnt + jnp.sum(jnp.where(pred(blk, c * KC + row), 1.0, 0.0), axis=0, keepdims=True)
        return cnt

    thr = _kth_largest(lambda cand: count(lambda blk, _: blk >= cand), (1, QB), float(topk))
    need = float(topk) - count(lambda blk, _: blk > thr)
    n_eq = count(lambda blk, _: blk == thr)
    excess = jnp.where((n_eq > need) & (thr != KEY_NEG_INF), 1.0, 0.0)
    jb_ref[...] = jnp.full((1, QB), T, I32)

    @pl.when(jnp.max(excess) > 0.0)
    def _():
        jb_ref[...] = _tie_bound(
            lambda cand: count(lambda blk, kpos: jnp.where(blk == thr, kpos, T) < cand),
            (1, QB), need, nbits)

    jb = jb_ref[...]
    for c in range(nchunk):
        blk = st_ref[c * KC:(c + 1) * KC, :]
        kpos = c * KC + row
        rank_pos = jnp.where(blk == thr, kpos, jnp.where(blk > thr, -1, T + 1))
        sel = jnp.where(kpos <= qpos, rank_pos, T + 1) <= jb
        m_t = jnp.where(sel, 0.0, NEG)
        for s4 in range(KC // QB):
            mask_ref[c, :, s4 * QB:(s4 + 1) * QB] = m_t[s4 * QB:(s4 + 1) * QB, :].T.astype(mask_ref.dtype)


def _a1_prompt(kin_bf, proj, topk):
    B, T, _ = proj.shape
    nq, nc = T // QB, T // KC
    qi_blk = (N_HEADS * HEAD_DIM + 2 * KV_HEADS * HEAD_DIM) // (IDX_HEADS * IDX_DIM)
    tail_blk = (N_HEADS * HEAD_DIM + 2 * KV_HEADS * HEAD_DIM + IDX_HEADS * IDX_DIM) // LANES
    return pl.pallas_call(
        functools.partial(_a1p_kernel, topk=topk),
        grid=(B, nq),
        in_specs=[pl.BlockSpec((None, T, IDX_DIM), lambda b, i: (b, 0, 0)),
                  pl.BlockSpec((None, QB, IDX_HEADS * IDX_DIM), lambda b, i: (b, i, qi_blk)),
                  pl.BlockSpec((None, QB, LANES), lambda b, i: (b, i, tail_blk))],
        out_specs=pl.BlockSpec((None, None, nc, QB, KC), lambda b, i: (b, i, 0, 0, 0)),
        out_shape=jax.ShapeDtypeStruct((B, nq, nc, QB, KC), BF16),
        scratch_shapes=[pltpu.VMEM((T, QB), I32), pltpu.VMEM((1, QB), I32)],
        compiler_params=_params(("parallel", "arbitrary")),
        name="a1_prompt",
    )(kin_bf, proj, proj)


def _a2p_kernel(q_ref, k_ref, v_ref, mask_ref, tz_ref, o_ref):
    i = pl.program_id(2)
    nj = (i * QB) // KC + 1
    scale = HEAD_DIM ** -0.5
    for r in range(GROUP):
        qh = q_ref[:, r * HEAD_DIM:(r + 1) * HEAD_DIM].astype(BF16)

        def body(j, carry, r=r, qh=qh):
            m, l, acc = carry
            k0 = pl.multiple_of(j * KC, KC)
            kt = k_ref[pl.ds(k0, KC), :].astype(BF16)
            vt = v_ref[pl.ds(k0, KC), :].astype(BF16)
            base = i * QB - j * KC + TZ_OFF
            bias = jnp.concatenate(
                [tz_ref[r, pl.ds(pl.multiple_of(base - QB * s4, QB), QB), :] for s4 in range(KC // QB)],
                axis=1)
            s = _nt(qh, kt) * scale + bias + mask_ref[j].astype(F32)
            m_new = jnp.maximum(m, jnp.max(s, axis=1, keepdims=True))
            alpha = jnp.exp(m - m_new)
            p = jnp.exp(s - m_new)
            l = alpha * l + jnp.sum(p, axis=1, keepdims=True)
            acc = alpha * acc + jnp.dot(p.astype(BF16), vt, preferred_element_type=F32)
            return m_new, l, acc

        m, l, acc = lax.fori_loop(
            0, nj, body,
            (jnp.full((QB, 1), NEG, F32), jnp.zeros((QB, 1), F32), jnp.zeros((QB, HEAD_DIM), F32)))
        o_ref[:, r * HEAD_DIM:(r + 1) * HEAD_DIM] = (acc / l).astype(o_ref.dtype)


def _a2_prompt(proj, mask, tz):
    B, T, _ = proj.shape
    nq, nc = T // QB, T // KC
    gw = GROUP * HEAD_DIM
    k_blk0 = N_HEADS * HEAD_DIM // HEAD_DIM
    v_blk0 = k_blk0 + KV_HEADS
    U = tz.shape[1]
    return pl.pallas_call(
        _a2p_kernel,
        grid=(KV_HEADS, B, nq),
        in_specs=[pl.BlockSpec((None, QB, gw), lambda g, b, i: (b, i, g)),
                  pl.BlockSpec((None, T, HEAD_DIM), lambda g, b, i: (b, 0, k_blk0 + g)),
                  pl.BlockSpec((None, T, HEAD_DIM), lambda g, b, i: (b, 0, v_blk0 + g)),
                  pl.BlockSpec((None, None, nc, QB, KC), lambda g, b, i: (b, i, 0, 0, 0)),
                  pl.BlockSpec((GROUP, U, LANES), lambda g, b, i: (g, 0, 0))],
        out_specs=pl.BlockSpec((None, QB, gw), lambda g, b, i: (b, i, g)),
        out_shape=jax.ShapeDtypeStruct((B, T, N_HEADS * HEAD_DIM), BF16),
        compiler_params=_params(("parallel", "parallel", "arbitrary")),
        name="a2_prompt",
    )(proj, proj, proj, mask, tz)


PAGES_PER_STEP = 8


def _a1s_kernel(pt_ref, qi_ref, tail_ref, knew_ref, *rest, topk, past):
    page_refs = rest[:PAGES_PER_STEP]
    mask_ref, st_ref, jb_ref = rest[PAGES_PER_STEP:]
    s = pl.program_id(1)
    nt, TS, _ = st_ref.shape
    L = nt * LANES
    nbits = (L - 1).bit_length()
    qi = jnp.concatenate([qi_ref[:, h * IDX_DIM:(h + 1) * IDX_DIM] for h in range(IDX_HEADS)],
                         axis=0).astype(BF16)
    w_col = jnp.concatenate([tail_ref[:, IDX_DIM + h:IDX_DIM + h + 1] for h in range(IDX_HEADS)],
                            axis=0) * (IDX_HEADS ** -0.5 * IDX_DIM ** -0.5)

    def scores(keys_bf):
        sc = jnp.maximum(_nt(qi, keys_bf), 0.0) * w_col
        return jnp.sum(sc.reshape(IDX_HEADS, TS, LANES), axis=0)

    for p in range(PAGES_PER_STEP):
        st_ref[s * PAGES_PER_STEP + p] = _sortable(scores(page_refs[p][...].astype(BF16)))

    @pl.when(s == pl.num_programs(1) - 1)
    def _():
        t = lax.broadcasted_iota(I32, (TS, LANES), 0)
        c = lax.broadcasted_iota(I32, (TS, LANES), 1)
        sc = jnp.where(c <= t, scores(knew_ref[...]), -jnp.inf)
        st_ref[nt - 1] = _sortable(sc)

        kpos = (lax.broadcasted_iota(I32, (nt, TS, LANES), 0) * LANES
                + lax.broadcasted_iota(I32, (nt, TS, LANES), 2))
        qpos = past + lax.broadcasted_iota(I32, (nt, TS, LANES), 1)

        def count(pred):
            per_lane = jnp.sum(jnp.where(pred(st_ref[...]), 1.0, 0.0), axis=0)
            return jnp.sum(per_lane, axis=1, keepdims=True)

        thr = _kth_largest(lambda cand: count(lambda k: k >= cand[None]), (TS, 1), float(topk))
        thr3 = thr[None]
        need = float(topk) - count(lambda k: k > thr3)
        n_eq = count(lambda k: k == thr3)
        excess = jnp.where((n_eq > need) & (thr != KEY_NEG_INF), 1.0, 0.0)
        jb_ref[...] = jnp.full((TS, 1), L, I32)

        @pl.when(jnp.max(excess) > 0.0)
        def _():
            jb_ref[...] = _tie_bound(
                lambda cand: count(lambda k: jnp.where(k == thr3, kpos, L) < cand[None]),
                (TS, 1), need, nbits)

        keys = st_ref[...]
        rank_pos = jnp.where(keys == thr3, kpos, jnp.where(keys > thr3, -1, L + 1))
        sel = jnp.where(kpos <= qpos, rank_pos, L + 1) <= jb_ref[...][None]
        mask_ref[...] = jnp.where(sel, 0.0, NEG)


def _a1_sample(page_table, proj_s, knew_bf, cache_kidx, layer, topk):
    DB, TS, _ = proj_s.shape
    n_pages = page_table.shape[1]
    page = cache_kidx.shape[2]
    assert page == LANES and n_pages % PAGES_PER_STEP == 0
    nt = n_pages + 1
    qi_blk = (N_HEADS * HEAD_DIM + 2 * KV_HEADS * HEAD_DIM) // (IDX_HEADS * IDX_DIM)
    tail_blk = (N_HEADS * HEAD_DIM + 2 * KV_HEADS * HEAD_DIM + IDX_HEADS * IDX_DIM) // LANES

    def page_spec(p):
        return pl.BlockSpec((None, None, page, IDX_DIM),
                            lambda b, s, pt: (layer, pt[b, s * PAGES_PER_STEP + p], 0, 0))

    grid_spec = pltpu.PrefetchScalarGridSpec(
        num_scalar_prefetch=1,
        grid=(DB, n_pages // PAGES_PER_STEP),
        in_specs=[pl.BlockSpec((None, TS, IDX_HEADS * IDX_DIM), lambda b, s, pt: (b, 0, qi_blk)),
                  pl.BlockSpec((None, TS, LANES), lambda b, s, pt: (b, 0, tail_blk)),
                  pl.BlockSpec((None, LANES, IDX_DIM), lambda b, s, pt: (b, 0, 0))]
                 + [page_spec(p) for p in range(PAGES_PER_STEP)],
        out_specs=pl.BlockSpec((None, nt, TS, LANES), lambda b, s, pt: (b, 0, 0, 0)),
        scratch_shapes=[pltpu.VMEM((nt, TS, LANES), I32), pltpu.VMEM((TS, 1), I32)])
    return pl.pallas_call(
        functools.partial(_a1s_kernel, topk=topk, past=n_pages * page),
        grid_spec=grid_spec,
        out_shape=jax.ShapeDtypeStruct((DB, nt, TS, LANES), F32),
        compiler_params=_params(("parallel", "arbitrary")),
        name="a1_sample",
    )(page_table, proj_s, proj_s, knew_bf, *([cache_kidx] * PAGES_PER_STEP))


def _a2s_kernel(pt_ref, q_ref, mask_ref, tzs_ref, knew_ref, vnew_ref, *rest, past):
    k_pages = rest[:PAGES_PER_STEP]
    v_pages = rest[PAGES_PER_STEP:2 * PAGES_PER_STEP]
    o_ref, m_ref, l_ref, acc_ref = rest[2 * PAGES_PER_STEP:]
    s = pl.program_id(1)
    TS = q_ref.shape[0]
    n_slab = tzs_ref.shape[1]
    scale = HEAD_DIM ** -0.5

    @pl.when(s == 0)
    def _():
        m_ref[...] = jnp.full(m_ref.shape, NEG, F32)
        l_ref[...] = jnp.zeros(l_ref.shape, F32)
        acc_ref[...] = jnp.zeros(acc_ref.shape, F32)

    qg = [jnp.concatenate([q_ref[:, (g * GROUP + r) * HEAD_DIM:(g * GROUP + r + 1) * HEAD_DIM]
                           for r in range(GROUP)], axis=0).astype(BF16) for g in range(KV_HEADS)]

    def tile(k, v, tile_idx):
        slab = jnp.minimum((past // LANES) - tile_idx, n_slab - 1)
        mk = mask_ref[tile_idx]
        mk = jnp.concatenate([mk] * GROUP, axis=0)
        for g in range(KV_HEADS):
            kt = k[:, g * HEAD_DIM:(g + 1) * HEAD_DIM].astype(BF16)
            vt = v[:, g * HEAD_DIM:(g + 1) * HEAD_DIM].astype(BF16)
            bias = jnp.concatenate([tzs_ref[g * GROUP + r, slab] for r in range(GROUP)], axis=0)
            sc = _nt(qg[g], kt) * scale + bias + mk
            m_old = m_ref[g]
            m_new = jnp.maximum(m_old, jnp.max(sc, axis=1, keepdims=True))
            alpha = jnp.exp(m_old - m_new)
            p = jnp.exp(sc - m_new)
            l_ref[g] = alpha * l_ref[g] + jnp.sum(p, axis=1, keepdims=True)
            acc_ref[g] = alpha * acc_ref[g] + jnp.dot(p.astype(BF16), vt, preferred_element_type=F32)
            m_ref[g] = m_new

    for p in range(PAGES_PER_STEP):
        tile(k_pages[p][...], v_pages[p][...], s * PAGES_PER_STEP + p)

    @pl.when(s == pl.num_programs(1) - 1)
    def _():
        tile(knew_ref[...], vnew_ref[...], past // LANES)
        for g in range(KV_HEADS):
            res = acc_ref[g] / l_ref[g]
            for r in range(GROUP):
                h = g * GROUP + r
                o_ref[:, h * HEAD_DIM:(h + 1) * HEAD_DIM] = res[r * TS:(r + 1) * TS, :].astype(o_ref.dtype)


def _a2_sample(page_table, proj_s, mask_s, tzs, knew, vnew, cache_k, cache_v, layer):
    DB, TS, _ = proj_s.shape
    n_pages = page_table.shape[1]
    page = cache_k.shape[2]
    kvw = KV_HEADS * HEAD_DIM
    nt = n_pages + 1

    def page_spec(p):
        return pl.BlockSpec((None, None, page, kvw),
                            lambda b, s, pt: (layer, pt[b, s * PAGES_PER_STEP + p], 0, 0))

    grid_spec = pltpu.PrefetchScalarGridSpec(
        num_scalar_prefetch=1,
        grid=(DB, n_pages // PAGES_PER_STEP),
        in_specs=[pl.BlockSpec((None, TS, N_HEADS * HEAD_DIM), lambda b, s, pt: (b, 0, 0)),
                  pl.BlockSpec((None, nt, TS, LANES), lambda b, s, pt: (b, 0, 0, 0)),
                  pl.BlockSpec(tzs.shape, lambda b, s, pt: (0, 0, 0, 0)),
                  pl.BlockSpec((None, LANES, kvw), lambda b, s, pt: (b, 0, 0)),
                  pl.BlockSpec((None, LANES, kvw), lambda b, s, pt: (b, 0, 0))]
                 + [page_spec(p) for p in range(PAGES_PER_STEP)] * 2,
        out_specs=pl.BlockSpec((None, TS, N_HEADS * HEAD_DIM), lambda b, s, pt: (b, 0, 0)),
        scratch_shapes=[pltpu.VMEM((KV_HEADS, GROUP * TS, 1), F32),
                        pltpu.VMEM((KV_HEADS, GROUP * TS, 1), F32),
                        pltpu.VMEM((KV_HEADS, GROUP * TS, HEAD_DIM), F32)])
    return pl.pallas_call(
        functools.partial(_a2s_kernel, past=n_pages * page),
        grid_spec=grid_spec,
        out_shape=jax.ShapeDtypeStruct((DB, TS, N_HEADS * HEAD_DIM), BF16),
        compiler_params=_params(("parallel", "arbitrary")),
        name="a2_sample",
    )(page_table, proj_s, mask_s, tzs, knew, vnew,
      *([cache_k] * PAGES_PER_STEP), *([cache_v] * PAGES_PER_STEP))


def _bp_kernel(q0_ref, q1_ref, q2_ref, k_ref, v_ref, band_ref, o_ref, m_ref, l_ref, acc_ref):
    T = k_ref.shape[0]
    scale = HEAD_DIM ** -0.5
    m_ref[...] = jnp.full(m_ref.shape, NEG, F32)
    l_ref[...] = jnp.zeros(l_ref.shape, F32)
    acc_ref[...] = jnp.zeros(acc_ref.shape, F32)

    def update(q_ref, n, rows_q, key_blocks):
        q = q_ref[rows_q, :].astype(BF16)
        m_old = m_ref[rows_q, :]
        ss, vs = [], []
        m_new = m_old
        for rows_k, half, extra in key_blocks:
            kt = k_ref[rows_k, :].astype(BF16)
            s = _nt(q, kt) * scale + band_ref[n, half] + extra
            m_new = jnp.maximum(m_new, jnp.max(s, axis=1, keepdims=True))
            ss.append(s)
            vs.append(v_ref[rows_k, :].astype(BF16))
        alpha = jnp.exp(m_old - m_new)
        l_new = alpha * l_ref[rows_q, :]
        acc = alpha * acc_ref[rows_q, :]
        for s, vt in zip(ss, vs):
            p = jnp.exp(s - m_new)
            l_new = l_new + jnp.sum(p, axis=1, keepdims=True)
            acc = acc + jnp.dot(p.astype(BF16), vt, preferred_element_type=F32)
        m_ref[rows_q, :] = m_new
        l_ref[rows_q, :] = l_new
        acc_ref[rows_q, :] = acc

    for n, (window, dil) in enumerate(B_GROUPS):
        span = window // dil
        assert span == QB
        per_class = T // dil
        tiles = per_class // QB
        q_ref = (q0_ref, q1_ref, q2_ref)[n]

        def body(it, _, n=n, dil=dil, tiles=tiles, q_ref=q_ref):
            rho = it // tiles
            mt = it % tiles
            start = rho + dil * QB * mt
            prev = jnp.maximum(start - dil * QB, rho)
            mk = (lambda st: pl.ds(st, QB, stride=dil)) if dil > 1 else (lambda st: pl.ds(st, QB))
            blocks = [(mk(start), 1, 0.0)]
            if tiles > 1:
                blocks.insert(0, (mk(prev), 0, jnp.where(mt == 0, NEG, 0.0)))
            update(q_ref, n, mk(start), blocks)
            return 0

        lax.fori_loop(0, dil * tiles, body, 0)

    o_ref[...] = (acc_ref[...] / l_ref[...]).astype(o_ref.dtype)


def _b_prompt(qb, kv, band):
    B, T, _ = qb.shape
    nh = N_HEADS
    hd = HEAD_DIM

    def q_spec(n):
        return pl.BlockSpec((None, T, hd), lambda b, g, r: (b, 0, (n * KV_HEADS + g) * GROUP + r))

    return pl.pallas_call(
        _bp_kernel,
        grid=(B, KV_HEADS, GROUP),
        in_specs=[q_spec(0), q_spec(1), q_spec(2),
                  pl.BlockSpec((None, T, hd), lambda b, g, r: (b, 0, g)),
                  pl.BlockSpec((None, T, hd), lambda b, g, r: (b, 0, KV_HEADS + g)),
                  pl.BlockSpec((None, 3, 2, QB, LANES), lambda b, g, r: (g * GROUP + r, 0, 0, 0, 0))],
        out_specs=pl.BlockSpec((None, T, hd), lambda b, g, r: (b, 0, g * GROUP + r)),
        out_shape=jax.ShapeDtypeStruct((B, T, nh * hd), BF16),
        scratch_shapes=[pltpu.VMEM((T, LANES), F32), pltpu.VMEM((T, LANES), F32),
                        pltpu.VMEM((T, hd), F32)],
        compiler_params=_params(("parallel", "parallel", "arbitrary")),
        name="b_prompt",
    )(qb, qb, qb, kv, kv, band)


def _bs_kernel(q0_ref, q1_ref, q2_ref, kc_ref, vc_ref, kn_ref, vn_ref, tab_ref, o_ref):
    TS = q0_ref.shape[0]
    W = kc_ref.shape[0]
    scale = HEAD_DIM ** -0.5
    kc = kc_ref[...].astype(BF16)
    vc = vc_ref[...].astype(BF16)
    kn = kn_ref[...].astype(BF16)
    vn = vn_ref[...].astype(BF16)
    logits = []
    for n in range(len(B_GROUPS)):
        q_ref = (q0_ref, q1_ref, q2_ref)[n]
        q = jnp.concatenate([q_ref[:, r * HEAD_DIM:(r + 1) * HEAD_DIM] for r in range(GROUP)],
                            axis=0).astype(BF16)
        logits.append((_nt(q, kc) * scale + tab_ref[n, :, :W],
                       _nt(q, kn) * scale + tab_ref[n, :, W:]))
    m = functools.reduce(jnp.maximum,
                         [jnp.max(x, axis=1, keepdims=True) for pair in logits for x in pair])
    l = jnp.zeros((GROUP * TS, 1), F32)
    acc = jnp.zeros((GROUP * TS, HEAD_DIM), F32)
    for sc_c, sc_n in logits:
        pc = jnp.exp(sc_c - m)
        pn = jnp.exp(sc_n - m)
        l = l + jnp.sum(pc, axis=1, keepdims=True) + jnp.sum(pn, axis=1, keepdims=True)
        acc = (acc + jnp.dot(pc.astype(BF16), vc, preferred_element_type=F32)
               + jnp.dot(pn.astype(BF16), vn, preferred_element_type=F32))
    res = acc / l
    for r in range(GROUP):
        o_ref[:, r * HEAD_DIM:(r + 1) * HEAD_DIM] = res[r * TS:(r + 1) * TS, :].astype(o_ref.dtype)


def _b_sample(qb_s, cache_k, cache_v, kv_new, tab):
    DB, TS, NQ = qb_s.shape
    W = cache_k.shape[1]
    hd = HEAD_DIM
    return pl.pallas_call(
        _bs_kernel,
        grid=(DB, KV_HEADS),
        in_specs=[pl.BlockSpec((None, TS, GROUP * hd), lambda b, g: (b, 0, g)),
                  pl.BlockSpec((None, TS, GROUP * hd), lambda b, g: (b, 0, KV_HEADS + g)),
                  pl.BlockSpec((None, TS, GROUP * hd), lambda b, g: (b, 0, 2 * KV_HEADS + g)),
                  pl.BlockSpec((None, W, hd), lambda b, g: (b, 0, g)),
                  pl.BlockSpec((None, W, hd), lambda b, g: (b, 0, g)),
                  pl.BlockSpec((None, LANES, hd), lambda b, g: (b, 0, g)),
                  pl.BlockSpec((None, LANES, hd), lambda b, g: (b, 0, KV_HEADS + g)),
                  pl.BlockSpec((len(B_GROUPS), None, GROUP * TS, W + LANES), lambda b, g: (0, g, 0, 0))],
        out_specs=pl.BlockSpec((None, TS, GROUP * hd), lambda b, g: (b, 0, g)),
        out_shape=jax.ShapeDtypeStruct((DB, TS, N_HEADS * hd), BF16),
        compiler_params=_params(("parallel", "arbitrary")),
        name="b_sample",
    )(qb_s, qb_s, qb_s, cache_k, cache_v, kv_new, kv_new, tab)


def _layer_norm(x, g, b):
    mu = x.mean(-1, keepdims=True)
    var = jnp.square(x - mu).mean(-1, keepdims=True)
    return (x - mu) * lax.rsqrt(var + LN_EPS) * g + b


def _pick_tm(M):
    for tm in (1024, 512, 256, 128, 64, 32, 16):
        if M % tm == 0:
            return tm
    raise ValueError(M)


def _pick_tn(N):
    for tn in (512, 384, 256, 128):
        if N % tn == 0:
            return tn
    raise ValueError(N)


def _proj(x, w_bf):
    lead = x.shape[:-1]
    x2 = x.reshape(-1, x.shape[-1]).astype(BF16)
    y = _mm(x2, w_bf, _pick_tm(x2.shape[0]), _pick_tn(w_bf.shape[1]))
    return y.reshape(*lead, w_bf.shape[1])


def _conv_ffn(h, state, w_up_bf, conv_w, conv_b, w_down_bf, d_ff):
    T = h.shape[1]
    gu = _proj(h, w_up_bf)
    gate, up = gu[..., :d_ff], gu[..., d_ff:]
    ext = jnp.concatenate([state, gate], axis=1)
    conv = conv_b + sum(ext[:, j:j + T] * conv_w[j] for j in range(CONV_WIDTH))
    act = jax.nn.silu(conv) * up
    return _proj(act, w_down_bf), ext[:, ext.shape[1] - (CONV_WIDTH - 1):]


def _table_indices(T, W, TS):
    U = SAT_DIST + TZ_OFF + QB
    assert T <= T5_MAX_DISTANCE
    u = np.arange(U)[:, None]
    c = np.arange(LANES)[None, :]
    tz_idx = _t5_bucket(jnp.asarray(np.maximum(u - c - TZ_OFF, 0), I32))
    a = np.arange(QB)[:, None]
    band = []
    for window, dil in B_GROUPS:
        for off in (QB, 0):
            d = a + off - c
            ok = (d >= 0) & (d <= window // dil)
            band.append(jnp.where(jnp.asarray(ok), _t5_bucket(jnp.asarray(np.maximum(d, 0) * dil, I32)),
                                  NUM_BUCKETS))
    band_idx = jnp.concatenate(band, axis=0)
    nkt = W // LANES + 1
    t = np.arange(TS)[None, :, None]
    key = (np.arange(nkt)[:, None, None] * LANES + np.arange(LANES)[None, None, :])
    d = W + t - key
    tabs = []
    for window, dil in B_GROUPS:
        ok = (d >= 0) & (d % dil == 0) & (d <= window)
        tabs.append(jnp.where(jnp.asarray(ok), _t5_bucket(jnp.asarray(np.maximum(d, 0), I32)), NUM_BUCKETS))
    samp_idx = jnp.concatenate(tabs, axis=0).reshape(-1, LANES)
    rows = [tz_idx, band_idx, samp_idx]
    total = sum(r.shape[0] for r in rows)
    pad = (-total) % 128
    if pad:
        rows.append(jnp.full((pad, LANES), NUM_BUCKETS, I32))
    return jnp.concatenate(rows, axis=0), U, nkt


def kernel(x_prompt, x_sample, cache_k_a, cache_v_a, cache_kidx_a, cache_k_b, cache_v_b, state_ffn, page_table, a_w_in, a_w_o, a_kn_g, a_kn_b, b_w_kv, b_w_q, b_w_o, ffn_w_up, ffn_conv_w, ffn_conv_b, ffn_w_down, ln_g, ln_b, rel_bias):
    B, T, D = x_prompt.shape
    DB, TS, _ = x_sample.shape
    depth = ffn_w_up.shape[0]
    n_a = a_w_in.shape[0]
    d_ff = ffn_w_down.shape[1]
    W = cache_k_b.shape[1]
    n_pages = page_table.shape[1]
    page = cache_k_a.shape[2]
    past = n_pages * page
    alpha = (2 * depth) ** 0.25
    kvw = KV_HEADS * HEAD_DIM
    a_q = N_HEADS * HEAD_DIM
    a_in = a_w_in.shape[2]
    np_cols = ((a_in + LANES - 1) // LANES) * LANES
    ki0 = a_q + 2 * kvw + IDX_HEADS * IDX_DIM

    idx_all, U, nkt = _table_indices(T, W, TS)
    tabs = _bias_tables(rel_bias, idx_all)
    tz = tabs[:, :U]
    band = tabs[:, U:U + 6 * QB].reshape(N_HEADS, 3, 2, QB, LANES)
    samp = tabs[:, U + 6 * QB:U + 6 * QB + 3 * nkt * TS].reshape(KV_HEADS, GROUP, 3, nkt, TS, LANES)
    samp = samp.transpose(2, 0, 1, 4, 3, 5).reshape(3, KV_HEADS, GROUP * TS, nkt * LANES)
    n_slab = SAT_DIST // LANES + 1
    tzs = tz[:, TZ_OFF:TZ_OFF + n_slab * LANES].reshape(N_HEADS, n_slab, LANES, LANES)[:, :, :TS, :]

    cache_k_a2 = cache_k_a.reshape(n_a, -1, page, kvw)
    cache_v_a2 = cache_v_a.reshape(n_a, -1, page, kvw)
    cache_k_b2 = cache_k_b.reshape(DB, W, kvw)
    cache_v_b2 = cache_v_b.reshape(DB, W, kvw)

    def pad_rows(x, n):
        return jnp.pad(x, ((0, 0), (0, n - x.shape[1]), (0, 0)))

    hp, hs = x_prompt, x_sample
    zero_state = jnp.zeros((B, CONV_WIDTH - 1, d_ff), F32)
    ka_p, va_p, kia_p, ka_s, va_s, kia_s, ffn_p, ffn_s = [], [], [], [], [], [], [], []
    for layer in range(depth):
        if layer < n_a:
            a = layer
            w_in_bf = jnp.pad(a_w_in[a], ((0, 0), (0, np_cols - a_in))).astype(BF16)
            w_o_bf = a_w_o[a].astype(BF16)
            proj = _proj(hp, w_in_bf)
            k = proj[..., a_q:a_q + kvw].reshape(B, T, KV_HEADS, HEAD_DIM)
            v = proj[..., a_q + kvw:a_q + 2 * kvw].reshape(B, T, KV_HEADS, HEAD_DIM)
            ki = _layer_norm(proj[..., ki0:ki0 + IDX_DIM], a_kn_g[a], a_kn_b[a])
            ka_p.append(k); va_p.append(v); kia_p.append(ki)
            mask = _a1_prompt(ki.astype(BF16), proj, min(TOPK_MAX, T // 4))
            o = _a2_prompt(proj, mask, tz)
            mp = _mm(o.reshape(B * T, a_q), w_o_bf, _pick_tm(B * T), 512).reshape(B, T, D)
            proj_s = _proj(hs, w_in_bf)
            k_s = proj_s[..., a_q:a_q + kvw]
            v_s = proj_s[..., a_q + kvw:a_q + 2 * kvw]
            ki_s = _layer_norm(proj_s[..., ki0:ki0 + IDX_DIM], a_kn_g[a], a_kn_b[a])
            ka_s.append(k_s.reshape(DB, TS, KV_HEADS, HEAD_DIM))
            va_s.append(v_s.reshape(DB, TS, KV_HEADS, HEAD_DIM))
            kia_s.append(ki_s)
            mask_s = _a1_sample(page_table, proj_s, pad_rows(ki_s, LANES).astype(BF16), cache_kidx_a, a,
                                min(TOPK_MAX, (past + TS) // 4))
            o_s = _a2_sample(page_table, proj_s, mask_s, tzs, pad_rows(k_s, LANES), pad_rows(v_s, LANES),
                             cache_k_a2, cache_v_a2, a)
            ms = _mm(o_s.reshape(DB * TS, a_q), w_o_bf, _pick_tm(DB * TS), 512).reshape(DB, TS, D)
        else:
            if layer == n_a:
                w_kv_bf = b_w_kv.astype(BF16)
                kv_p = _proj(hp, w_kv_bf)
                kv_s = _proj(hs, w_kv_bf)
                kv_s_pad = pad_rows(kv_s, LANES)
            bl = layer - n_a
            w_q_bf = b_w_q[bl].astype(BF16)
            w_o_bf = b_w_o[bl].astype(BF16)
            o = _b_prompt(_proj(hp, w_q_bf), kv_p, band)
            mp = _mm(o.reshape(B * T, a_q), w_o_bf, _pick_tm(B * T), 512).reshape(B, T, D)
            o_s = _b_sample(_proj(hs, w_q_bf), cache_k_b2, cache_v_b2, kv_s_pad, samp)
            ms = _mm(o_s.reshape(DB * TS, a_q), w_o_bf, _pick_tm(DB * TS), 512).reshape(DB, TS, D)
        hp = _layer_norm(alpha * hp + mp, ln_g[layer, 0], ln_b[layer, 0])
        hs = _layer_norm(alpha * hs + ms, ln_g[layer, 0], ln_b[layer, 0])
        w_up_bf = ffn_w_up[layer].astype(BF16)
        w_down_bf = ffn_w_down[layer].astype(BF16)
        fp, sp = _conv_ffn(hp, zero_state, w_up_bf, ffn_conv_w[layer], ffn_conv_b[layer], w_down_bf, d_ff)
        fs, ss = _conv_ffn(hs, state_ffn[layer], w_up_bf, ffn_conv_w[layer], ffn_conv_b[layer], w_down_bf, d_ff)
        ffn_p.append(sp); ffn_s.append(ss)
        hp = _layer_norm(alpha * hp + fp, ln_g[layer, 1], ln_b[layer, 1])
        hs = _layer_norm(alpha * hs + fs, ln_g[layer, 1], ln_b[layer, 1])
    keep = min(max(w for w, _ in B_GROUPS), T)
    kb_p = kv_p[..., :kvw].reshape(B, T, KV_HEADS, HEAD_DIM)
    vb_p = kv_p[..., kvw:].reshape(B, T, KV_HEADS, HEAD_DIM)
    kb_s = kv_s[..., :kvw].reshape(DB, TS, KV_HEADS, HEAD_DIM)
    vb_s = kv_s[..., kvw:].reshape(DB, TS, KV_HEADS, HEAD_DIM)
    return (hp, hs, jnp.stack(ka_p), jnp.stack(va_p), jnp.stack(kia_p), jnp.stack(ka_s), jnp.stack(va_s),
            jnp.stack(kia_s), kb_p[:, T - keep:], vb_p[:, T - keep:], kb_s, vb_s,
            jnp.stack(ffn_p), jnp.stack(ffn_s))
```

```python
import functools
import math

import numpy as np
import jax
import jax.numpy as jnp
from jax import lax
from jax.experimental import pallas as pl
from jax.experimental.pallas import tpu as pltpu

F32 = jnp.float32
BF16 = jnp.bfloat16
I32 = jnp.int32

HEAD_DIM = 128
N_HEADS = 16
KV_HEADS = 4
GROUP = N_HEADS // KV_HEADS
IDX_HEADS = 16
IDX_DIM = 64
TOPK_MAX = 256
B_GROUPS = ((128, 1), (512, 4), (2048, 16))
NUM_BUCKETS = 32
T5_MAX_DISTANCE = 2048
LN_EPS = 1e-5
NEG = -1e30
CONV_WIDTH = 3

LANES = 128
QB = 128
KC = 512
TZ_OFF = KC - QB
SAT_DIST = T5_MAX_DISTANCE + QB
INT_MIN = -2 ** 31
KEY_NEG_INF = INT_MIN + 0x7FFFFF
VMEM_LIMIT = 56 * 1024 * 1024

NT_DIMS = (((1,), (1,)), ((), ()))


def _nt(a, b):
    return lax.dot_general(a, b, NT_DIMS, preferred_element_type=F32)


def _params(sem):
    return pltpu.CompilerParams(dimension_semantics=sem, vmem_limit_bytes=VMEM_LIMIT)


def _mm_kernel(x_ref, w_ref, o_ref):
    o_ref[...] = jnp.dot(x_ref[...], w_ref[...], preferred_element_type=F32).astype(o_ref.dtype)


def _mm(x, w, tm, tn, out_dtype=F32):
    M, K = x.shape
    N = w.shape[1]
    assert M % tm == 0 and N % tn == 0, (M, N, tm, tn)
    return pl.pallas_call(
        _mm_kernel,
        grid=(M // tm, N // tn),
        in_specs=[pl.BlockSpec((tm, K), lambda i, j: (i, 0)),
                  pl.BlockSpec((K, tn), lambda i, j: (0, j))],
        out_specs=pl.BlockSpec((tm, tn), lambda i, j: (i, j)),
        out_shape=jax.ShapeDtypeStruct((M, N), out_dtype),
        compiler_params=_params(("parallel", "arbitrary")),
        name="mm",
    )(x, w)


def _t5_bucket(dist):
    dist = jnp.maximum(dist, 0)
    exact = NUM_BUCKETS // 2
    far = exact + (jnp.log(jnp.maximum(dist, 1).astype(F32) / exact)
                   / math.log(T5_MAX_DISTANCE / exact) * (NUM_BUCKETS - exact)).astype(I32)
    return jnp.where(dist < exact, dist, jnp.minimum(far, NUM_BUCKETS - 1))


def _table_kernel(rb_ref, idx_ref, o_ref):
    idx = idx_ref[...]
    for h in range(N_HEADS):
        acc = jnp.full(idx.shape, NEG, F32)
        for k in range(NUM_BUCKETS):
            acc = jnp.where(idx == k, rb_ref[k, h], acc)
        o_ref[h] = acc


def _bias_tables(rel_bias, idx):
    R = idx.shape[0]
    tr = 128
    assert R % tr == 0
    return pl.pallas_call(
        _table_kernel,
        grid=(R // tr,),
        in_specs=[pl.BlockSpec(memory_space=pltpu.SMEM),
                  pl.BlockSpec((tr, LANES), lambda i: (i, 0))],
        out_specs=pl.BlockSpec((N_HEADS, tr, LANES), lambda i: (0, i, 0)),
        out_shape=jax.ShapeDtypeStruct((N_HEADS, R, LANES), F32),
        compiler_params=_params(("arbitrary",)),
        name="bias_tables",
    )(rel_bias, idx)


def _sortable(x):
    bits = lax.bitcast_convert_type(x, I32)
    return bits ^ ((bits >> 31) & jnp.int32(0x7FFFFFFF))


def _kth_largest(count_ge, shape, k):
    def body(t, ans):
        cand_u = ans | (jnp.int32(1) << (31 - t))
        cnt = count_ge(cand_u ^ jnp.int32(INT_MIN))
        return jnp.where(cnt >= k, cand_u, ans)
    ans = lax.fori_loop(0, 32, body, jnp.zeros(shape, I32))
    return ans ^ jnp.int32(INT_MIN)


def _tie_bound(count_eq_below, shape, need, nbits):
    def body(t, ans):
        cand = ans | (jnp.int32(1) << (nbits - 1 - t))
        return jnp.where(count_eq_below(cand) < need, cand, ans)
    return lax.fori_loop(0, nbits, body, jnp.zeros(shape, I32))


def _a1p_kernel(kin_ref, qi_ref, tail_ref, mask_ref, st_ref, jb_ref, *, topk):
    i = pl.program_id(1)
    T = kin_ref.shape[0]
    nchunk = T // KC
    nbits = (T - 1).bit_length()
    tail_t = tail_ref[...].T
    wi_t = tail_t[IDX_DIM:IDX_DIM + IDX_HEADS, :] * (IDX_HEADS ** -0.5 * IDX_DIM ** -0.5)
    qpos = i * QB + lax.broadcasted_iota(I32, (KC, QB), 1)
    row = lax.broadcasted_iota(I32, (KC, QB), 0)

    for c in range(nchunk):
        @pl.when(c * KC < (i + 1) * QB)
        def _():
            kc = kin_ref[c * KC:(c + 1) * KC, :]
            acc = jnp.zeros((KC, QB), F32)
            for h in range(IDX_HEADS):
                qh = qi_ref[:, h * IDX_DIM:(h + 1) * IDX_DIM].astype(BF16)
                acc = acc + jnp.maximum(_nt(kc, qh), 0.0) * wi_t[h:h + 1, :]
            acc = jnp.where(c * KC + row <= qpos, acc, -jnp.inf)
            st_ref[c * KC:(c + 1) * KC, :] = _sortable(acc)

        @pl.when(c * KC >= (i + 1) * QB)
        def _():
            st_ref[c * KC:(c + 1) * KC, :] = jnp.full((KC, QB), KEY_NEG_INF, I32)

    def count(pred):
        cnt = jnp.zeros((1, QB), F32)
        for c in range(nchunk):
            blk = st_ref[c * KC:(c + 1) * KC, :]
            cnt = cnt + jnp.sum(jnp.where(pred(blk, c * KC + row), 1.0, 0.0), axis=0, keepdims=True)
        return cnt

    thr = _kth_largest(lambda cand: count(lambda blk, _: blk >= cand), (1, QB), float(topk))
    need = float(topk) - count(lambda blk, _: blk > thr)
    n_eq = count(lambda blk, _: blk == thr)
    excess = jnp.where((n_eq > need) & (thr != KEY_NEG_INF), 1.0, 0.0)
    jb_ref[...] = jnp.full((1, QB), T, I32)

    @pl.when(jnp.max(excess) > 0.0)
    def _():
        jb_ref[...] = _tie_bound(
            lambda cand: count(lambda blk, kpos: jnp.where(blk == thr, kpos, T) < cand),
            (1, QB), need, nbits)

    jb = jb_ref[...]
    for c in range(nchunk):
        blk = st_ref[c * KC:(c + 1) * KC, :]
        kpos = c * KC + row
        rank_pos = jnp.where(blk == thr, kpos, jnp.where(blk > thr, -1, T + 1))
        sel = jnp.where(kpos <= qpos, rank_pos, T + 1) <= jb
        m_t = jnp.where(sel, 0.0, NEG)
        for s4 in range(KC // QB):
            mask_ref[c, :, s4 * QB:(s4 + 1) * QB] = m_t[s4 * QB:(s4 + 1) * QB, :].T.astype(mask_ref.dtype)


def _a1_prompt(kin_bf, proj, topk):
    B, T, _ = proj.shape
    nq, nc = T // QB, T // KC
    qi_blk = (N_HEADS * HEAD_DIM + 2 * KV_HEADS * HEAD_DIM) // (IDX_HEADS * IDX_DIM)
    tail_blk = (N_HEADS * HEAD_DIM + 2 * KV_HEADS * HEAD_DIM + IDX_HEADS * IDX_DIM) // LANES
    return pl.pallas_call(
        functools.partial(_a1p_kernel, topk=topk),
        grid=(B, nq),
        in_specs=[pl.BlockSpec((None, T, IDX_DIM), lambda b, i: (b, 0, 0)),
                  pl.BlockSpec((None, QB, IDX_HEADS * IDX_DIM), lambda b, i: (b, i, qi_blk)),
                  pl.BlockSpec((None, QB, LANES), lambda b, i: (b, i, tail_blk))],
        out_specs=pl.BlockSpec((None, None, nc, QB, KC), lambda b, i: (b, i, 0, 0, 0)),
        out_shape=jax.ShapeDtypeStruct((B, nq, nc, QB, KC), BF16),
        scratch_shapes=[pltpu.VMEM((T, QB), I32), pltpu.VMEM((1, QB), I32)],
        compiler_params=_params(("parallel", "arbitrary")),
        name="a1_prompt",
    )(kin_bf, proj, proj)


def _a2p_kernel(q_ref, k_ref, v_ref, mask_ref, tz_ref, o_ref):
    i = pl.program_id(2)
    nj = (i * QB) // KC + 1
    scale = HEAD_DIM ** -0.5
    qs = jnp.concatenate([q_ref[:, r * HEAD_DIM:(r + 1) * HEAD_DIM] for r in range(GROUP)],
                         axis=0).astype(BF16)
    R = GROUP * QB

    def body(j, carry):
        m, l, acc = carry
        k0 = pl.multiple_of(j * KC, KC)
        kt = k_ref[pl.ds(k0, KC), :].astype(BF16)
        vt = v_ref[pl.ds(k0, KC), :].astype(BF16)
        base = i * QB - j * KC + TZ_OFF
        bias = jnp.concatenate(
            [jnp.concatenate(
                [tz_ref[r, pl.ds(pl.multiple_of(base - QB * s4, QB), QB), :] for s4 in range(KC // QB)],
                axis=1) for r in range(GROUP)], axis=0)
        mk = mask_ref[j].astype(F32)
        s = _nt(qs, kt) * scale + bias + jnp.concatenate([mk] * GROUP, axis=0)
        m_new = jnp.maximum(m, jnp.max(s, axis=1, keepdims=True))
        alpha = jnp.exp(m - m_new)
        p = jnp.exp(s - m_new)
        l = alpha * l + jnp.sum(p, axis=1, keepdims=True)
        acc = alpha * acc + jnp.dot(p.astype(BF16), vt, preferred_element_type=F32)
        return m_new, l, acc

    m, l, acc = lax.fori_loop(
        0, nj, body,
        (jnp.full((R, 1), NEG, F32), jnp.zeros((R, 1), F32), jnp.zeros((R, HEAD_DIM), F32)))
    res = acc / l
    for r in range(GROUP):
        o_ref[:, r * HEAD_DIM:(r + 1) * HEAD_DIM] = res[r * QB:(r + 1) * QB, :].astype(o_ref.dtype)


def _a2_prompt(proj, mask, tz):
    B, T, _ = proj.shape
    nq, nc = T // QB, T // KC
    gw = GROUP * HEAD_DIM
    k_blk0 = N_HEADS * HEAD_DIM // HEAD_DIM
    v_blk0 = k_blk0 + KV_HEADS
    U = tz.shape[1]
    return pl.pallas_call(
        _a2p_kernel,
        grid=(KV_HEADS, B, nq),
        in_specs=[pl.BlockSpec((None, QB, gw), lambda g, b, i: (b, i, g)),
                  pl.BlockSpec((None, T, HEAD_DIM), lambda g, b, i: (b, 0, k_blk0 + g)),
                  pl.BlockSpec((None, T, HEAD_DIM), lambda g, b, i: (b, 0, v_blk0 + g)),
                  pl.BlockSpec((None, None, nc, QB, KC), lambda g, b, i: (b, i, 0, 0, 0)),
                  pl.BlockSpec((GROUP, U, LANES), lambda g, b, i: (g, 0, 0))],
        out_specs=pl.BlockSpec((None, QB, gw), lambda g, b, i: (b, i, g)),
        out_shape=jax.ShapeDtypeStruct((B, T, N_HEADS * HEAD_DIM), BF16),
        compiler_params=_params(("parallel", "parallel", "arbitrary")),
        name="a2_prompt",
    )(proj, proj, proj, mask, tz)


PAGES_PER_STEP = 16


def _a1s_kernel(pt_ref, qi_ref, tail_ref, knew_ref, *rest, topk, past):
    page_refs = rest[:PAGES_PER_STEP]
    mask_ref, st_ref, jb_ref = rest[PAGES_PER_STEP:]
    s = pl.program_id(1)
    nt, TS, _ = st_ref.shape
    L = nt * LANES
    nbits = (L - 1).bit_length()
    qi = jnp.concatenate([qi_ref[:, h * IDX_DIM:(h + 1) * IDX_DIM] for h in range(IDX_HEADS)],
                         axis=0).astype(BF16)
    w_col = jnp.concatenate([tail_ref[:, IDX_DIM + h:IDX_DIM + h + 1] for h in range(IDX_HEADS)],
                            axis=0) * (IDX_HEADS ** -0.5 * IDX_DIM ** -0.5)

    def scores(keys_bf):
        sc = jnp.maximum(_nt(qi, keys_bf), 0.0) * w_col
        return jnp.sum(sc.reshape(IDX_HEADS, TS, LANES), axis=0)

    for p in range(PAGES_PER_STEP):
        st_ref[s * PAGES_PER_STEP + p] = _sortable(scores(page_refs[p][...].astype(BF16)))

    @pl.when(s == pl.num_programs(1) - 1)
    def _():
        t = lax.broadcasted_iota(I32, (TS, LANES), 0)
        c = lax.broadcasted_iota(I32, (TS, LANES), 1)
        sc = jnp.where(c <= t, scores(knew_ref[...]), -jnp.inf)
        st_ref[nt - 1] = _sortable(sc)

        kpos = (lax.broadcasted_iota(I32, (nt, TS, LANES), 0) * LANES
                + lax.broadcasted_iota(I32, (nt, TS, LANES), 2))
        qpos = past + lax.broadcasted_iota(I32, (nt, TS, LANES), 1)

        def count(pred):
            per_lane = jnp.sum(jnp.where(pred(st_ref[...]), 1.0, 0.0), axis=0)
            return jnp.sum(per_lane, axis=1, keepdims=True)

        thr = _kth_largest(lambda cand: count(lambda k: k >= cand[None]), (TS, 1), float(topk))
        thr3 = thr[None]
        need = float(topk) - count(lambda k: k > thr3)
        n_eq = count(lambda k: k == thr3)
        excess = jnp.where((n_eq > need) & (thr != KEY_NEG_INF), 1.0, 0.0)
        jb_ref[...] = jnp.full((TS, 1), L, I32)

        @pl.when(jnp.max(excess) > 0.0)
        def _():
            jb_ref[...] = _tie_bound(
                lambda cand: count(lambda k: jnp.where(k == thr3, kpos, L) < cand[None]),
                (TS, 1), need, nbits)

        keys = st_ref[...]
        rank_pos = jnp.where(keys == thr3, kpos, jnp.where(keys > thr3, -1, L + 1))
        sel = jnp.where(kpos <= qpos, rank_pos, L + 1) <= jb_ref[...][None]
        mask_ref[...] = jnp.where(sel, 0.0, NEG)


def _a1_sample(page_table, proj_s, knew_bf, cache_kidx, layer, topk):
    DB, TS, _ = proj_s.shape
    n_pages = page_table.shape[1]
    page = cache_kidx.shape[2]
    assert page == LANES and n_pages % PAGES_PER_STEP == 0
    nt = n_pages + 1
    qi_blk = (N_HEADS * HEAD_DIM + 2 * KV_HEADS * HEAD_DIM) // (IDX_HEADS * IDX_DIM)
    tail_blk = (N_HEADS * HEAD_DIM + 2 * KV_HEADS * HEAD_DIM + IDX_HEADS * IDX_DIM) // LANES

    def page_spec(p):
        return pl.BlockSpec((None, None, page, IDX_DIM),
                            lambda b, s, pt: (layer, pt[b, s * PAGES_PER_STEP + p], 0, 0))

    grid_spec = pltpu.PrefetchScalarGridSpec(
        num_scalar_prefetch=1,
        grid=(DB, n_pages // PAGES_PER_STEP),
        in_specs=[pl.BlockSpec((None, TS, IDX_HEADS * IDX_DIM), lambda b, s, pt: (b, 0, qi_blk)),
                  pl.BlockSpec((None, TS, LANES), lambda b, s, pt: (b, 0, tail_blk)),
                  pl.BlockSpec((None, LANES, IDX_DIM), lambda b, s, pt: (b, 0, 0))]
                 + [page_spec(p) for p in range(PAGES_PER_STEP)],
        out_specs=pl.BlockSpec((None, nt, TS, LANES), lambda b, s, pt: (b, 0, 0, 0)),
        scratch_shapes=[pltpu.VMEM((nt, TS, LANES), I32), pltpu.VMEM((TS, 1), I32)])
    return pl.pallas_call(
        functools.partial(_a1s_kernel, topk=topk, past=n_pages * page),
        grid_spec=grid_spec,
        out_shape=jax.ShapeDtypeStruct((DB, nt, TS, LANES), F32),
        compiler_params=_params(("parallel", "arbitrary")),
        name="a1_sample",
    )(page_table, proj_s, proj_s, knew_bf, *([cache_kidx] * PAGES_PER_STEP))


def _a2s_kernel(pt_ref, q_ref, mask_ref, tzs_ref, knew_ref, vnew_ref, *rest, past):
    k_pages = rest[:PAGES_PER_STEP]
    v_pages = rest[PAGES_PER_STEP:2 * PAGES_PER_STEP]
    o_ref, m_ref, l_ref, acc_ref = rest[2 * PAGES_PER_STEP:]
    s = pl.program_id(1)
    TS = q_ref.shape[0]
    n_slab = tzs_ref.shape[1]
    scale = HEAD_DIM ** -0.5

    @pl.when(s == 0)
    def _():
        m_ref[...] = jnp.full(m_ref.shape, NEG, F32)
        l_ref[...] = jnp.zeros(l_ref.shape, F32)
        acc_ref[...] = jnp.zeros(acc_ref.shape, F32)

    qg = [jnp.concatenate([q_ref[:, (g * GROUP + r) * HEAD_DIM:(g * GROUP + r + 1) * HEAD_DIM]
                           for r in range(GROUP)], axis=0).astype(BF16) for g in range(KV_HEADS)]

    def update(g, kt, vt, tiles):
        slabs = [jnp.minimum((past // LANES) - t, n_slab - 1) for t in tiles]
        mk = jnp.concatenate([mask_ref[t] for t in tiles], axis=1)
        mk = jnp.concatenate([mk] * GROUP, axis=0)
        bias = jnp.concatenate(
            [jnp.concatenate([tzs_ref[g * GROUP + r, sl] for r in range(GROUP)], axis=0) for sl in slabs],
            axis=1)
        sc = _nt(qg[g], kt) * scale + bias + mk
        m_old = m_ref[g]
        m_new = jnp.maximum(m_old, jnp.max(sc, axis=1, keepdims=True))
        alpha = jnp.exp(m_old - m_new)
        p = jnp.exp(sc - m_new)
        l_ref[g] = alpha * l_ref[g] + jnp.sum(p, axis=1, keepdims=True)
        acc_ref[g] = alpha * acc_ref[g] + jnp.dot(p.astype(BF16), vt, preferred_element_type=F32)
        m_ref[g] = m_new

    tiles = [s * PAGES_PER_STEP + p for p in range(PAGES_PER_STEP)]
    for g in range(KV_HEADS):
        rows = pl.ds(g, LANES, stride=KV_HEADS)
        kt = jnp.concatenate([kp[rows, :] for kp in k_pages], axis=0).astype(BF16)
        vt = jnp.concatenate([vp[rows, :] for vp in v_pages], axis=0).astype(BF16)
        update(g, kt, vt, tiles)

    @pl.when(s == pl.num_programs(1) - 1)
    def _():
        for g in range(KV_HEADS):
            update(g, knew_ref[:, g * HEAD_DIM:(g + 1) * HEAD_DIM].astype(BF16),
                   vnew_ref[:, g * HEAD_DIM:(g + 1) * HEAD_DIM].astype(BF16), [past // LANES])
        for g in range(KV_HEADS):
            res = acc_ref[g] / l_ref[g]
            for r in range(GROUP):
                h = g * GROUP + r
                o_ref[:, h * HEAD_DIM:(h + 1) * HEAD_DIM] = res[r * TS:(r + 1) * TS, :].astype(o_ref.dtype)


def _a2_sample(page_table, proj_s, mask_s, tzs, knew, vnew, cache_k, cache_v, layer):
    DB, TS, _ = proj_s.shape
    n_pages = page_table.shape[1]
    page = cache_k.shape[2] // KV_HEADS
    kvw = KV_HEADS * HEAD_DIM
    nt = n_pages + 1

    def page_spec(p):
        return pl.BlockSpec((None, None, page * KV_HEADS, HEAD_DIM),
                            lambda b, s, pt: (layer, pt[b, s * PAGES_PER_STEP + p], 0, 0))

    grid_spec = pltpu.PrefetchScalarGridSpec(
        num_scalar_prefetch=1,
        grid=(DB, n_pages // PAGES_PER_STEP),
        in_specs=[pl.BlockSpec((None, TS, N_HEADS * HEAD_DIM), lambda b, s, pt: (b, 0, 0)),
                  pl.BlockSpec((None, nt, TS, LANES), lambda b, s, pt: (b, 0, 0, 0)),
                  pl.BlockSpec(tzs.shape, lambda b, s, pt: (0, 0, 0, 0)),
                  pl.BlockSpec((None, LANES, kvw), lambda b, s, pt: (b, 0, 0)),
                  pl.BlockSpec((None, LANES, kvw), lambda b, s, pt: (b, 0, 0))]
                 + [page_spec(p) for p in range(PAGES_PER_STEP)] * 2,
        out_specs=pl.BlockSpec((None, TS, N_HEADS * HEAD_DIM), lambda b, s, pt: (b, 0, 0)),
        scratch_shapes=[pltpu.VMEM((KV_HEADS, GROUP * TS, 1), F32),
                        pltpu.VMEM((KV_HEADS, GROUP * TS, 1), F32),
                        pltpu.VMEM((KV_HEADS, GROUP * TS, HEAD_DIM), F32)])
    return pl.pallas_call(
        functools.partial(_a2s_kernel, past=n_pages * page),
        grid_spec=grid_spec,
        out_shape=jax.ShapeDtypeStruct((DB, TS, N_HEADS * HEAD_DIM), BF16),
        compiler_params=_params(("parallel", "arbitrary")),
        name="a2_sample",
    )(page_table, proj_s, mask_s, tzs, knew, vnew,
      *([cache_k] * PAGES_PER_STEP), *([cache_v] * PAGES_PER_STEP))


def _bp_kernel(q0_ref, q1_ref, q2_ref, k_ref, v_ref, band_ref, o_ref, m_ref, l_ref, acc_ref):
    T = k_ref.shape[0]
    scale = HEAD_DIM ** -0.5
    m_ref[...] = jnp.full(m_ref.shape, NEG, F32)
    l_ref[...] = jnp.zeros(l_ref.shape, F32)
    acc_ref[...] = jnp.zeros(acc_ref.shape, F32)

    def update(q_ref, n, rows_q, key_blocks):
        q = q_ref[rows_q, :].astype(BF16)
        m_old = m_ref[rows_q, :]
        ss, vs = [], []
        m_new = m_old
        for rows_k, half, extra in key_blocks:
            kt = k_ref[rows_k, :].astype(BF16)
            s = _nt(q, kt) * scale + band_ref[n, half] + extra
            m_new = jnp.maximum(m_new, jnp.max(s, axis=1, keepdims=True))
            ss.append(s)
            vs.append(v_ref[rows_k, :].astype(BF16))
        alpha = jnp.exp(m_old - m_new)
        l_new = alpha * l_ref[rows_q, :]
        acc = alpha * acc_ref[rows_q, :]
        for s, vt in zip(ss, vs):
            p = jnp.exp(s - m_new)
            l_new = l_new + jnp.sum(p, axis=1, keepdims=True)
            acc = acc + jnp.dot(p.astype(BF16), vt, preferred_element_type=F32)
        m_ref[rows_q, :] = m_new
        l_ref[rows_q, :] = l_new
        acc_ref[rows_q, :] = acc

    for n, (window, dil) in enumerate(B_GROUPS):
        span = window // dil
        assert span == QB
        per_class = T // dil
        tiles = per_class // QB
        q_ref = (q0_ref, q1_ref, q2_ref)[n]

        def body(it, _, n=n, dil=dil, tiles=tiles, q_ref=q_ref):
            rho = it // tiles
            mt = it % tiles
            start = rho + dil * QB * mt
            prev = jnp.maximum(start - dil * QB, rho)
            mk = (lambda st: pl.ds(st, QB, stride=dil)) if dil > 1 else (lambda st: pl.ds(st, QB))
            blocks = [(mk(start), 1, 0.0)]
            if tiles > 1:
                blocks.insert(0, (mk(prev), 0, jnp.where(mt == 0, NEG, 0.0)))
            update(q_ref, n, mk(start), blocks)
            return 0

        lax.fori_loop(0, dil * tiles, body, 0)

    o_ref[...] = (acc_ref[...] / l_ref[...]).astype(o_ref.dtype)


def _b_prompt(qb, kv, band):
    B, T, _ = qb.shape
    nh = N_HEADS
    hd = HEAD_DIM

    def q_spec(n):
        return pl.BlockSpec((None, T, hd), lambda b, g, r: (b, 0, (n * KV_HEADS + g) * GROUP + r))

    return pl.pallas_call(
        _bp_kernel,
        grid=(B, KV_HEADS, GROUP),
        in_specs=[q_spec(0), q_spec(1), q_spec(2),
                  pl.BlockSpec((None, T, hd), lambda b, g, r: (b, 0, g)),
                  pl.BlockSpec((None, T, hd), lambda b, g, r: (b, 0, KV_HEADS + g)),
                  pl.BlockSpec((None, 3, 2, QB, LANES), lambda b, g, r: (g * GROUP + r, 0, 0, 0, 0))],
        out_specs=pl.BlockSpec((None, T, hd), lambda b, g, r: (b, 0, g * GROUP + r)),
        out_shape=jax.ShapeDtypeStruct((B, T, nh * hd), BF16),
        scratch_shapes=[pltpu.VMEM((T, LANES), F32), pltpu.VMEM((T, LANES), F32),
                        pltpu.VMEM((T, hd), F32)],
        compiler_params=_params(("parallel", "parallel", "arbitrary")),
        name="b_prompt",
    )(qb, qb, qb, kv, kv, band)


def _bs_kernel(q0_ref, q1_ref, q2_ref, kc_ref, vc_ref, kn_ref, vn_ref, tab_ref, o_ref):
    TS = q0_ref.shape[0]
    W = kc_ref.shape[0] // KV_HEADS
    scale = HEAD_DIM ** -0.5
    rows = pl.ds(pl.program_id(1), W, stride=KV_HEADS)
    kc = kc_ref[rows, :].astype(BF16)
    vc = vc_ref[rows, :].astype(BF16)
    kn = kn_ref[...].astype(BF16)
    vn = vn_ref[...].astype(BF16)
    logits = []
    for n in range(len(B_GROUPS)):
        q_ref = (q0_ref, q1_ref, q2_ref)[n]
        q = jnp.concatenate([q_ref[:, r * HEAD_DIM:(r + 1) * HEAD_DIM] for r in range(GROUP)],
                            axis=0).astype(BF16)
        logits.append((_nt(q, kc) * scale + tab_ref[n, :, :W],
                       _nt(q, kn) * scale + tab_ref[n, :, W:]))
    m = functools.reduce(jnp.maximum,
                         [jnp.max(x, axis=1, keepdims=True) for pair in logits for x in pair])
    l = jnp.zeros((GROUP * TS, 1), F32)
    acc = jnp.zeros((GROUP * TS, HEAD_DIM), F32)
    for sc_c, sc_n in logits:
        pc = jnp.exp(sc_c - m)
        pn = jnp.exp(sc_n - m)
        l = l + jnp.sum(pc, axis=1, keepdims=True) + jnp.sum(pn, axis=1, keepdims=True)
        acc = (acc + jnp.dot(pc.astype(BF16), vc, preferred_element_type=F32)
               + jnp.dot(pn.astype(BF16), vn, preferred_element_type=F32))
    res = acc / l
    for r in range(GROUP):
        o_ref[:, r * HEAD_DIM:(r + 1) * HEAD_DIM] = res[r * TS:(r + 1) * TS, :].astype(o_ref.dtype)


def _b_sample(qb_s, cache_k, cache_v, kv_new, tab):
    DB, TS, NQ = qb_s.shape
    W = cache_k.shape[1] // KV_HEADS
    hd = HEAD_DIM
    return pl.pallas_call(
        _bs_kernel,
        grid=(DB, KV_HEADS),
        in_specs=[pl.BlockSpec((None, TS, GROUP * hd), lambda b, g: (b, 0, g)),
                  pl.BlockSpec((None, TS, GROUP * hd), lambda b, g: (b, 0, KV_HEADS + g)),
                  pl.BlockSpec((None, TS, GROUP * hd), lambda b, g: (b, 0, 2 * KV_HEADS + g)),
                  pl.BlockSpec((None, W * KV_HEADS, hd), lambda b, g: (b, 0, 0)),
                  pl.BlockSpec((None, W * KV_HEADS, hd), lambda b, g: (b, 0, 0)),
                  pl.BlockSpec((None, LANES, hd), lambda b, g: (b, 0, g)),
                  pl.BlockSpec((None, LANES, hd), lambda b, g: (b, 0, KV_HEADS + g)),
                  pl.BlockSpec((len(B_GROUPS), None, GROUP * TS, W + LANES), lambda b, g: (0, g, 0, 0))],
        out_specs=pl.BlockSpec((None, TS, GROUP * hd), lambda b, g: (b, 0, g)),
        out_shape=jax.ShapeDtypeStruct((DB, TS, N_HEADS * hd), BF16),
        compiler_params=_params(("parallel", "arbitrary")),
        name="b_sample",
    )(qb_s, qb_s, qb_s, cache_k, cache_v, kv_new, kv_new, tab)


FFN_TF = 512
HALO = 16


def _ffn_up_kernel(xm_ref, xh_ref, wg_ref, wu_ref, cw_ref, cb_ref, act_ref, st_ref, xe_ref, *,
                   tiles_per_batch):
    i = pl.program_id(0)
    tm = xm_ref.shape[0]

    @pl.when(pl.program_id(1) == 0)
    def _():
        first = (i % tiles_per_batch) == 0
        xh = xh_ref[...]
        xe_ref[0:HALO, :] = jnp.where(first, jnp.zeros_like(xh), xh)
        xe_ref[HALO:, :] = xm_ref[...]

    gate = jnp.dot(xe_ref[...], wg_ref[...], preferred_element_type=F32)
    up = jnp.dot(xm_ref[...], wu_ref[...], preferred_element_type=F32)
    cw = cw_ref[...]
    conv = cb_ref[...] + ((gate[HALO - 2:HALO - 2 + tm] * cw[0:1] + gate[HALO - 1:HALO - 1 + tm] * cw[1:2])
                          + gate[HALO:] * cw[2:3])
    act_ref[...] = (jax.nn.silu(conv) * up).astype(act_ref.dtype)
    st_ref[...] = gate[HALO + tm - (CONV_WIDTH - 1):, :]


def _ffn_up(x_bf, w_gu_bf, conv_w, conv_b, T):
    M, D = x_bf.shape
    F = conv_w.shape[1]
    tm = 1024
    assert T % tm == 0 and M % T == 0 and F % FFN_TF == 0 and tm % HALO == 0
    nf = F // FFN_TF
    hb = tm // HALO
    return pl.pallas_call(
        functools.partial(_ffn_up_kernel, tiles_per_batch=T // tm),
        grid=(M // tm, nf),
        in_specs=[pl.BlockSpec((tm, D), lambda i, j: (i, 0)),
                  pl.BlockSpec((HALO, D), lambda i, j: (jnp.maximum(i * hb - 1, 0), 0)),
                  pl.BlockSpec((D, FFN_TF), lambda i, j: (0, j)),
                  pl.BlockSpec((D, FFN_TF), lambda i, j: (0, nf + j)),
                  pl.BlockSpec((CONV_WIDTH, FFN_TF), lambda i, j: (0, j)),
                  pl.BlockSpec((1, FFN_TF), lambda i, j: (0, j))],
        out_specs=[pl.BlockSpec((tm, FFN_TF), lambda i, j: (i, j)),
                   pl.BlockSpec((None, CONV_WIDTH - 1, FFN_TF), lambda i, j: (i, 0, j))],
        out_shape=[jax.ShapeDtypeStruct((M, F), BF16),
                   jax.ShapeDtypeStruct((M // tm, CONV_WIDTH - 1, F), F32)],
        scratch_shapes=[pltpu.VMEM((HALO + tm, D), BF16)],
        compiler_params=_params(("parallel", "arbitrary")),
        name="ffn_up",
    )(x_bf, x_bf, w_gu_bf, w_gu_bf, conv_w, conv_b)


def _mm_ln_kernel(a_ref, w_ref, x_ref, g_ref, b_ref, h_ref, hb_ref, acc_ref, *, alpha):
    k = pl.program_id(1)

    @pl.when(k == 0)
    def _():
        acc_ref[...] = jnp.zeros_like(acc_ref)

    acc_ref[...] += jnp.dot(a_ref[...], w_ref[...], preferred_element_type=F32)

    @pl.when(k == pl.num_programs(1) - 1)
    def _():
        y = alpha * x_ref[...] + acc_ref[...]
        mu = jnp.mean(y, axis=-1, keepdims=True)
        d = y - mu
        var = jnp.mean(d * d, axis=-1, keepdims=True)
        h = d * lax.rsqrt(var + LN_EPS) * g_ref[...] + b_ref[...]
        h_ref[...] = h
        hb_ref[...] = h.astype(hb_ref.dtype)


def _mm_ln(a_bf, w_bf, x, g, b, alpha):
    M, K = a_bf.shape
    D = w_bf.shape[1]
    tm = min(512, M)
    tk = K // 4 if K > 2048 else K // 2
    assert M % tm == 0 and K % tk == 0 and tk % LANES == 0
    return pl.pallas_call(
        functools.partial(_mm_ln_kernel, alpha=alpha),
        grid=(M // tm, K // tk),
        in_specs=[pl.BlockSpec((tm, tk), lambda i, k: (i, k)),
                  pl.BlockSpec((tk, D), lambda i, k: (k, 0)),
                  pl.BlockSpec((tm, D), lambda i, k: (i, 0)),
                  pl.BlockSpec((1, D), lambda i, k: (0, 0)),
                  pl.BlockSpec((1, D), lambda i, k: (0, 0))],
        out_specs=[pl.BlockSpec((tm, D), lambda i, k: (i, 0)),
                   pl.BlockSpec((tm, D), lambda i, k: (i, 0))],
        out_shape=[jax.ShapeDtypeStruct((M, D), F32), jax.ShapeDtypeStruct((M, D), BF16)],
        scratch_shapes=[pltpu.VMEM((tm, D), F32)],
        compiler_params=_params(("parallel", "arbitrary")),
        name="mm_ln",
    )(a_bf, w_bf, x, g.reshape(1, D), b.reshape(1, D))


def _layer_norm(x, g, b):
    mu = x.mean(-1, keepdims=True)
    var = jnp.square(x - mu).mean(-1, keepdims=True)
    return (x - mu) * lax.rsqrt(var + LN_EPS) * g + b


def _pick_tm(M):
    for tm in (1024, 512, 256, 128, 64, 32, 16):
        if M % tm == 0:
            return tm
    raise ValueError(M)


def _pick_tn(N):
    for tn in (512, 384, 256, 128):
        if N % tn == 0:
            return tn
    raise ValueError(N)


def _proj(x, w_bf):
    lead = x.shape[:-1]
    x2 = x.reshape(-1, x.shape[-1]).astype(BF16)
    y = _mm(x2, w_bf, _pick_tm(x2.shape[0]), _pick_tn(w_bf.shape[1]))
    return y.reshape(*lead, w_bf.shape[1])


def _ffn_act_sample(hs_bf, state, w_gu_bf, conv_w, conv_b, DB, TS):
    F = conv_w.shape[1]
    gu = _proj(hs_bf, w_gu_bf)
    gate = gu[:, :F].reshape(DB, TS, F)
    up = gu[:, F:].reshape(DB, TS, F)
    ext = jnp.concatenate([state, gate], axis=1)
    conv = conv_b + sum(ext[:, j:j + TS] * conv_w[j] for j in range(CONV_WIDTH))
    act = (jax.nn.silu(conv) * up).astype(BF16).reshape(DB * TS, F)
    return act, ext[:, ext.shape[1] - (CONV_WIDTH - 1):]


def _table_indices(T, W, TS):
    U = SAT_DIST + TZ_OFF + QB
    assert T <= T5_MAX_DISTANCE
    u = np.arange(U)[:, None]
    c = np.arange(LANES)[None, :]
    tz_idx = _t5_bucket(jnp.asarray(np.maximum(u - c - TZ_OFF, 0), I32))
    a = np.arange(QB)[:, None]
    band = []
    for window, dil in B_GROUPS:
        for off in (QB, 0):
            d = a + off - c
            ok = (d >= 0) & (d <= window // dil)
            band.append(jnp.where(jnp.asarray(ok), _t5_bucket(jnp.asarray(np.maximum(d, 0) * dil, I32)),
                                  NUM_BUCKETS))
    band_idx = jnp.concatenate(band, axis=0)
    nkt = W // LANES + 1
    t = np.arange(TS)[None, :, None]
    key = (np.arange(nkt)[:, None, None] * LANES + np.arange(LANES)[None, None, :])
    d = W + t - key
    tabs = []
    for window, dil in B_GROUPS:
        ok = (d >= 0) & (d % dil == 0) & (d <= window)
        tabs.append(jnp.where(jnp.asarray(ok), _t5_bucket(jnp.asarray(np.maximum(d, 0), I32)), NUM_BUCKETS))
    samp_idx = jnp.concatenate(tabs, axis=0).reshape(-1, LANES)
    rows = [tz_idx, band_idx, samp_idx]
    total = sum(r.shape[0] for r in rows)
    pad = (-total) % 128
    if pad:
        rows.append(jnp.full((pad, LANES), NUM_BUCKETS, I32))
    return jnp.concatenate(rows, axis=0), U, nkt


def kernel(x_prompt, x_sample, cache_k_a, cache_v_a, cache_kidx_a, cache_k_b, cache_v_b, state_ffn, page_table, a_w_in, a_w_o, a_kn_g, a_kn_b, b_w_kv, b_w_q, b_w_o, ffn_w_up, ffn_conv_w, ffn_conv_b, ffn_w_down, ln_g, ln_b, rel_bias):
    B, T, D = x_prompt.shape
    DB, TS, _ = x_sample.shape
    depth = ffn_w_up.shape[0]
    n_a = a_w_in.shape[0]
    d_ff = ffn_w_down.shape[1]
    W = cache_k_b.shape[1]
    n_pages = page_table.shape[1]
    page = cache_k_a.shape[2]
    past = n_pages * page
    alpha = (2 * depth) ** 0.25
    kvw = KV_HEADS * HEAD_DIM
    a_q = N_HEADS * HEAD_DIM
    a_in = a_w_in.shape[2]
    np_cols = ((a_in + LANES - 1) // LANES) * LANES
    ki0 = a_q + 2 * kvw + IDX_HEADS * IDX_DIM

    idx_all, U, nkt = _table_indices(T, W, TS)
    tabs = _bias_tables(rel_bias, idx_all)
    tz = tabs[:, :U]
    band = tabs[:, U:U + 6 * QB].reshape(N_HEADS, 3, 2, QB, LANES)
    samp = tabs[:, U + 6 * QB:U + 6 * QB + 3 * nkt * TS].reshape(KV_HEADS, GROUP, 3, nkt, TS, LANES)
    samp = samp.transpose(2, 0, 1, 4, 3, 5).reshape(3, KV_HEADS, GROUP * TS, nkt * LANES)
    n_slab = SAT_DIST // LANES + 1
    tzs = tz[:, TZ_OFF:TZ_OFF + n_slab * LANES].reshape(N_HEADS, n_slab, LANES, LANES)[:, :, :TS, :]

    cache_k_a2 = cache_k_a.reshape(n_a, -1, page * KV_HEADS, HEAD_DIM)
    cache_v_a2 = cache_v_a.reshape(n_a, -1, page * KV_HEADS, HEAD_DIM)
    cache_k_b2 = cache_k_b.reshape(DB, W * KV_HEADS, HEAD_DIM)
    cache_v_b2 = cache_v_b.reshape(DB, W * KV_HEADS, HEAD_DIM)

    def pad_rows(x, n):
        return jnp.pad(x, ((0, 0), (0, n - x.shape[1]), (0, 0)))

    f_pad = ((d_ff + FFN_TF - 1) // FFN_TF) * FFN_TF
    pad_f = f_pad - d_ff

    hp = x_prompt.reshape(B * T, D)
    hs = x_sample.reshape(DB * TS, D)
    hp_bf, hs_bf = hp.astype(BF16), hs.astype(BF16)
    ka_p, va_p, kia_p, ka_s, va_s, kia_s, ffn_p, ffn_s = [], [], [], [], [], [], [], []
    for layer in range(depth):
        if layer < n_a:
            a = layer
            w_in_bf = jnp.pad(a_w_in[a], ((0, 0), (0, np_cols - a_in))).astype(BF16)
            w_o_bf = a_w_o[a].astype(BF16)
            proj = _proj(hp_bf, w_in_bf).reshape(B, T, np_cols)
            k = proj[..., a_q:a_q + kvw].reshape(B, T, KV_HEADS, HEAD_DIM)
            v = proj[..., a_q + kvw:a_q + 2 * kvw].reshape(B, T, KV_HEADS, HEAD_DIM)
            ki = _layer_norm(proj[..., ki0:ki0 + IDX_DIM], a_kn_g[a], a_kn_b[a])
            ka_p.append(k); va_p.append(v); kia_p.append(ki)
            mask = _a1_prompt(ki.astype(BF16), proj, min(TOPK_MAX, T // 4))
            o = _a2_prompt(proj, mask, tz).reshape(B * T, a_q)
            proj_s = _proj(hs_bf, w_in_bf).reshape(DB, TS, np_cols)
            k_s = proj_s[..., a_q:a_q + kvw]
            v_s = proj_s[..., a_q + kvw:a_q + 2 * kvw]
            ki_s = _layer_norm(proj_s[..., ki0:ki0 + IDX_DIM], a_kn_g[a], a_kn_b[a])
            ka_s.append(k_s.reshape(DB, TS, KV_HEADS, HEAD_DIM))
            va_s.append(v_s.reshape(DB, TS, KV_HEADS, HEAD_DIM))
            kia_s.append(ki_s)
            mask_s = _a1_sample(page_table, proj_s, pad_rows(ki_s, LANES).astype(BF16), cache_kidx_a, a,
                                min(TOPK_MAX, (past + TS) // 4))
            o_s = _a2_sample(page_table, proj_s, mask_s, tzs, pad_rows(k_s, LANES), pad_rows(v_s, LANES),
                             cache_k_a2, cache_v_a2, a).reshape(DB * TS, a_q)
        else:
            if layer == n_a:
                w_kv_bf = b_w_kv.astype(BF16)
                kv_p = _proj(hp_bf, w_kv_bf).reshape(B, T, 2 * kvw)
                kv_s = _proj(hs_bf, w_kv_bf).reshape(DB, TS, 2 * kvw)
                kv_s_pad = pad_rows(kv_s, LANES)
            bl = layer - n_a
            w_q_bf = b_w_q[bl].astype(BF16)
            w_o_bf = b_w_o[bl].astype(BF16)
            o = _b_prompt(_proj(hp_bf, w_q_bf).reshape(B, T, -1), kv_p, band).reshape(B * T, a_q)
            o_s = _b_sample(_proj(hs_bf, w_q_bf).reshape(DB, TS, -1), cache_k_b2, cache_v_b2, kv_s_pad,
                            samp).reshape(DB * TS, a_q)
        hp, hp_bf = _mm_ln(o, w_o_bf, hp, ln_g[layer, 0], ln_b[layer, 0], alpha)
        hs, hs_bf = _mm_ln(o_s, w_o_bf, hs, ln_g[layer, 0], ln_b[layer, 0], alpha)
        w_gu_bf = jnp.concatenate([jnp.pad(ffn_w_up[layer][:, :d_ff], ((0, 0), (0, pad_f))),
                                   jnp.pad(ffn_w_up[layer][:, d_ff:], ((0, 0), (0, pad_f)))],
                                  axis=1).astype(BF16)
        w_down_bf = jnp.pad(ffn_w_down[layer], ((0, pad_f), (0, 0))).astype(BF16)
        conv_w = jnp.pad(ffn_conv_w[layer], ((0, 0), (0, pad_f)))
        conv_b = jnp.pad(ffn_conv_b[layer], ((0, pad_f),)).reshape(1, f_pad)
        act, st = _ffn_up(hp_bf, w_gu_bf, conv_w, conv_b, T)
        ffn_p.append(st.reshape(B, -1, CONV_WIDTH - 1, f_pad)[:, -1, :, :d_ff])
        act_s, st_s = _ffn_act_sample(hs_bf, jnp.pad(state_ffn[layer], ((0, 0), (0, 0), (0, pad_f))),
                                      w_gu_bf, conv_w, conv_b, DB, TS)
        ffn_s.append(st_s[..., :d_ff])
        hp, hp_bf = _mm_ln(act, w_down_bf, hp, ln_g[layer, 1], ln_b[layer, 1], alpha)
        hs, hs_bf = _mm_ln(act_s, w_down_bf, hs, ln_g[layer, 1], ln_b[layer, 1], alpha)
    keep = min(max(w for w, _ in B_GROUPS), T)
    kb_p = kv_p[..., :kvw].reshape(B, T, KV_HEADS, HEAD_DIM)
    vb_p = kv_p[..., kvw:].reshape(B, T, KV_HEADS, HEAD_DIM)
    kb_s = kv_s[..., :kvw].reshape(DB, TS, KV_HEADS, HEAD_DIM)
    vb_s = kv_s[..., kvw:].reshape(DB, TS, KV_HEADS, HEAD_DIM)
    return (hp.reshape(B, T, D), hs.reshape(DB, TS, D), jnp.stack(ka_p), jnp.stack(va_p), jnp.stack(kia_p),
            jnp.stack(ka_s), jnp.stack(va_s), jnp.stack(kia_s), kb_p[:, T - keep:], vb_p[:, T - keep:],
            kb_s, vb_s, jnp.stack(ffn_p), jnp.stack(ffn_s))
```

```python
import functools
import math

import numpy as np
import jax
import jax.numpy as jnp
from jax import lax
from jax.experimental import pallas as pl
from jax.experimental.pallas import tpu as pltpu

F32 = jnp.float32
BF16 = jnp.bfloat16
I32 = jnp.int32

HEAD_DIM = 128
N_HEADS = 16
KV_HEADS = 4
GROUP = N_HEADS // KV_HEADS
IDX_HEADS = 16
IDX_DIM = 64
TOPK_MAX = 256
B_GROUPS = ((128, 1), (512, 4), (2048, 16))
NUM_BUCKETS = 32
T5_MAX_DISTANCE = 2048
LN_EPS = 1e-5
NEG = -1e30
CONV_WIDTH = 3

LANES = 128
QB = 128
KC = 512
TZ_OFF = KC - QB
SAT_DIST = T5_MAX_DISTANCE + QB
INT_MIN = -2 ** 31
KEY_NEG_INF = INT_MIN + 0x7FFFFF
VMEM_LIMIT = 56 * 1024 * 1024

NT_DIMS = (((1,), (1,)), ((), ()))


def _nt(a, b):
    return lax.dot_general(a, b, NT_DIMS, preferred_element_type=F32)


def _params(sem):
    return pltpu.CompilerParams(dimension_semantics=sem, vmem_limit_bytes=VMEM_LIMIT)


def _mm_kernel(x_ref, w_ref, o_ref):
    o_ref[...] = jnp.dot(x_ref[...], w_ref[...], preferred_element_type=F32).astype(o_ref.dtype)


def _mm(x, w, layer, tm, tn, out_dtype=F32):
    M, K = x.shape
    N = w.shape[2]
    assert M % tm == 0 and N % tn == 0, (M, N, tm, tn)
    return pl.pallas_call(
        _mm_kernel,
        grid=(M // tm, N // tn),
        in_specs=[pl.BlockSpec((tm, K), lambda i, j: (i, 0)),
                  pl.BlockSpec((None, K, tn), lambda i, j: (layer, 0, j))],
        out_specs=pl.BlockSpec((tm, tn), lambda i, j: (i, j)),
        out_shape=jax.ShapeDtypeStruct((M, N), out_dtype),
        compiler_params=_params(("parallel", "arbitrary")),
        name="mm",
    )(x, w)


def _t5_bucket(dist):
    dist = jnp.maximum(dist, 0)
    exact = NUM_BUCKETS // 2
    far = exact + (jnp.log(jnp.maximum(dist, 1).astype(F32) / exact)
                   / math.log(T5_MAX_DISTANCE / exact) * (NUM_BUCKETS - exact)).astype(I32)
    return jnp.where(dist < exact, dist, jnp.minimum(far, NUM_BUCKETS - 1))


def _table_kernel(rb_ref, idx_ref, o_ref):
    idx = idx_ref[...]
    for h in range(N_HEADS):
        acc = jnp.full(idx.shape, NEG, F32)
        for k in range(NUM_BUCKETS):
            acc = jnp.where(idx == k, rb_ref[k, h], acc)
        o_ref[h] = acc


def _bias_tables(rel_bias, idx):
    R = idx.shape[0]
    tr = 128
    assert R % tr == 0
    return pl.pallas_call(
        _table_kernel,
        grid=(R // tr,),
        in_specs=[pl.BlockSpec(memory_space=pltpu.SMEM),
                  pl.BlockSpec((tr, LANES), lambda i: (i, 0))],
        out_specs=pl.BlockSpec((N_HEADS, tr, LANES), lambda i: (0, i, 0)),
        out_shape=jax.ShapeDtypeStruct((N_HEADS, R, LANES), F32),
        compiler_params=_params(("arbitrary",)),
        name="bias_tables",
    )(rel_bias, idx)


def _sortable(x):
    bits = lax.bitcast_convert_type(x, I32)
    return bits ^ ((bits >> 31) & jnp.int32(0x7FFFFFFF))


def _kth_largest(count_ge, shape, k):
    def body(t, ans):
        cand_u = ans | (jnp.int32(1) << (31 - t))
        cnt = count_ge(cand_u ^ jnp.int32(INT_MIN))
        return jnp.where(cnt >= k, cand_u, ans)
    ans = lax.fori_loop(0, 32, body, jnp.zeros(shape, I32))
    return ans ^ jnp.int32(INT_MIN)


def _tie_bound(count_eq_below, shape, need, nbits):
    def body(t, ans):
        cand = ans | (jnp.int32(1) << (nbits - 1 - t))
        return jnp.where(count_eq_below(cand) < need, cand, ans)
    return lax.fori_loop(0, nbits, body, jnp.zeros(shape, I32))


def _a1p_kernel(kin_ref, qi_ref, tail_ref, mask_ref, st_ref, jb_ref, *, topk):
    i = pl.program_id(1)
    T = kin_ref.shape[0]
    nchunk = T // KC
    nbits = (T - 1).bit_length()
    tail_t = tail_ref[...].T
    wi_t = tail_t[IDX_DIM:IDX_DIM + IDX_HEADS, :] * (IDX_HEADS ** -0.5 * IDX_DIM ** -0.5)
    qpos = i * QB + lax.broadcasted_iota(I32, (KC, QB), 1)
    row = lax.broadcasted_iota(I32, (KC, QB), 0)

    nj = (i * QB) // KC + 1

    for c in range(nchunk):
        @pl.when(c < nj)
        def _():
            kc = kin_ref[c * KC:(c + 1) * KC, :]
            acc = jnp.zeros((KC, QB), F32)
            for h in range(IDX_HEADS):
                qh = qi_ref[:, h * IDX_DIM:(h + 1) * IDX_DIM].astype(BF16)
                acc = acc + jnp.maximum(_nt(kc, qh), 0.0) * wi_t[h:h + 1, :]
            acc = jnp.where(c * KC + row <= qpos, acc, -jnp.inf)
            st_ref[c * KC:(c + 1) * KC, :] = _sortable(acc)

    def count(pred):
        def chunk(c, cnt):
            r0 = pl.multiple_of(c * KC, KC)
            blk = st_ref[pl.ds(r0, KC), :]
            return cnt + jnp.sum(jnp.where(pred(blk, r0 + row), 1.0, 0.0), axis=0, keepdims=True)
        return lax.fori_loop(0, nj, chunk, jnp.zeros((1, QB), F32))

    thr = _kth_largest(lambda cand: count(lambda blk, _: blk >= cand), (1, QB), float(topk))
    need = float(topk) - count(lambda blk, _: blk > thr)
    n_eq = count(lambda blk, _: blk == thr)
    excess = jnp.where((n_eq > need) & (thr != KEY_NEG_INF), 1.0, 0.0)
    jb_ref[...] = jnp.full((1, QB), T, I32)

    @pl.when(jnp.max(excess) > 0.0)
    def _():
        jb_ref[...] = _tie_bound(
            lambda cand: count(lambda blk, kpos: jnp.where(blk == thr, kpos, T) < cand),
            (1, QB), need, nbits)

    jb = jb_ref[...]
    for c in range(nchunk):
        @pl.when(c < nj)
        def _():
            blk = st_ref[c * KC:(c + 1) * KC, :]
            kpos = c * KC + row
            rank_pos = jnp.where(blk == thr, kpos, jnp.where(blk > thr, -1, T + 1))
            sel = jnp.where(kpos <= qpos, rank_pos, T + 1) <= jb
            m_t = jnp.where(sel, 0.0, NEG)
            for s4 in range(KC // QB):
                mask_ref[c, :, s4 * QB:(s4 + 1) * QB] = (
                    m_t[s4 * QB:(s4 + 1) * QB, :].T.astype(mask_ref.dtype))

        @pl.when(c >= nj)
        def _():
            mask_ref[c] = jnp.full((QB, KC), NEG, mask_ref.dtype)


def _a1_prompt(kin_bf, proj, topk):
    B, T, _ = proj.shape
    nq, nc = T // QB, T // KC
    assert topk <= KC
    qi_blk =(N_HEADS * HEAD_DIM + 2 * KV_HEADS * HEAD_DIM) // (IDX_HEADS * IDX_DIM)
    tail_blk = (N_HEADS * HEAD_DIM + 2 * KV_HEADS * HEAD_DIM + IDX_HEADS * IDX_DIM) // LANES
    return pl.pallas_call(
        functools.partial(_a1p_kernel, topk=topk),
        grid=(B, nq),
        in_specs=[pl.BlockSpec((None, T, IDX_DIM), lambda b, i: (b, 0, 0)),
                  pl.BlockSpec((None, QB, IDX_HEADS * IDX_DIM), lambda b, i: (b, i, qi_blk)),
                  pl.BlockSpec((None, QB, LANES), lambda b, i: (b, i, tail_blk))],
        out_specs=pl.BlockSpec((None, None, nc, QB, KC), lambda b, i: (b, i, 0, 0, 0)),
        out_shape=jax.ShapeDtypeStruct((B, nq, nc, QB, KC), BF16),
        scratch_shapes=[pltpu.VMEM((T, QB), I32), pltpu.VMEM((1, QB), I32)],
        compiler_params=_params(("parallel", "arbitrary")),
        name="a1_prompt",
    )(kin_bf, proj, proj)


def _a2p_kernel(q_ref, k_ref, v_ref, mask_ref, tz_ref, o_ref):
    i = pl.program_id(2)
    nj = (i * QB) // KC + 1
    scale = HEAD_DIM ** -0.5
    HP = GROUP
    R = HP * QB
    heads = [list(range(c * HP, (c + 1) * HP)) for c in range(GROUP // HP)]
    qs = [jnp.concatenate([q_ref[:, r * HEAD_DIM:(r + 1) * HEAD_DIM] for r in hs], axis=0).astype(BF16)
          for hs in heads]

    def body(j, carry):
        k0 = pl.multiple_of(j * KC, KC)
        kt = k_ref[pl.ds(k0, KC), :].astype(BF16)
        vt = v_ref[pl.ds(k0, KC), :].astype(BF16)
        base = i * QB - j * KC + TZ_OFF
        mk = mask_ref[j].astype(F32)
        mk = jnp.concatenate([mk] * HP, axis=0)
        out = []
        for hs, q, (m, l, acc) in zip(heads, qs, carry):
            bias = jnp.concatenate(
                [jnp.concatenate(
                    [tz_ref[r, pl.ds(pl.multiple_of(base - QB * s4, QB), QB), :] for s4 in range(KC // QB)],
                    axis=1) for r in hs], axis=0)
            s = _nt(q, kt) * scale + bias + mk
            m_new = jnp.maximum(m, jnp.max(s, axis=1, keepdims=True))
            alpha = jnp.exp(m - m_new)
            p = jnp.exp(s - m_new)
            l = alpha * l + jnp.sum(p, axis=1, keepdims=True)
            acc = alpha * acc + jnp.dot(p.astype(BF16), vt, preferred_element_type=F32)
            out.append((m_new, l, acc))
        return tuple(out)

    init = tuple((jnp.full((R, 1), NEG, F32), jnp.zeros((R, 1), F32), jnp.zeros((R, HEAD_DIM), F32))
                 for _ in heads)
    final = lax.fori_loop(0, nj, body, init)
    for hs, (m, l, acc) in zip(heads, final):
        res = acc / l
        for n, r in enumerate(hs):
            o_ref[:, r * HEAD_DIM:(r + 1) * HEAD_DIM] = res[n * QB:(n + 1) * QB, :].astype(o_ref.dtype)


def _a2_prompt(proj, mask, tz):
    B, T, _ = proj.shape
    nq, nc = T // QB, T // KC
    gw = GROUP * HEAD_DIM
    k_blk0 = N_HEADS * HEAD_DIM // HEAD_DIM
    v_blk0 = k_blk0 + KV_HEADS
    U = tz.shape[1]
    return pl.pallas_call(
        _a2p_kernel,
        grid=(KV_HEADS, B, nq),
        in_specs=[pl.BlockSpec((None, QB, gw), lambda g, b, i: (b, i, g)),
                  pl.BlockSpec((None, T, HEAD_DIM), lambda g, b, i: (b, 0, k_blk0 + g)),
                  pl.BlockSpec((None, T, HEAD_DIM), lambda g, b, i: (b, 0, v_blk0 + g)),
                  pl.BlockSpec((None, None, nc, QB, KC), lambda g, b, i: (b, i, 0, 0, 0)),
                  pl.BlockSpec((GROUP, U, LANES), lambda g, b, i: (g, 0, 0))],
        out_specs=pl.BlockSpec((None, QB, gw), lambda g, b, i: (b, i, g)),
        out_shape=jax.ShapeDtypeStruct((B, T, N_HEADS * HEAD_DIM), BF16),
        compiler_params=_params(("parallel", "parallel", "arbitrary")),
        name="a2_prompt",
    )(proj, proj, proj, mask, tz)


PAGES_PER_STEP = 16


def _a1s_kernel(pt_ref, qi_ref, tail_ref, knew_ref, *rest, topk, past):
    page_refs = rest[:PAGES_PER_STEP]
    mask_ref, st_ref, jb_ref = rest[PAGES_PER_STEP:]
    s = pl.program_id(1)
    nt, TS, _ = st_ref.shape
    L = nt * LANES
    nbits = (L - 1).bit_length()
    qi = jnp.concatenate([qi_ref[:, h * IDX_DIM:(h + 1) * IDX_DIM] for h in range(IDX_HEADS)],
                         axis=0).astype(BF16)
    w_col = jnp.concatenate([tail_ref[:, IDX_DIM + h:IDX_DIM + h + 1] for h in range(IDX_HEADS)],
                            axis=0) * (IDX_HEADS ** -0.5 * IDX_DIM ** -0.5)

    def scores(qk):
        sc = jnp.maximum(qk, 0.0) * w_col
        return jnp.sum(sc.reshape(IDX_HEADS, TS, LANES), axis=0)

    for p in range(PAGES_PER_STEP):
        qk = jnp.dot(qi, page_refs[p][...].astype(BF16), preferred_element_type=F32)
        st_ref[s * PAGES_PER_STEP + p] = _sortable(scores(qk))

    @pl.when(s == pl.num_programs(1) - 1)
    def _():
        t = lax.broadcasted_iota(I32, (TS, LANES), 0)
        c = lax.broadcasted_iota(I32, (TS, LANES), 1)
        sc = jnp.where(c <= t, scores(_nt(qi, knew_ref[...])), -jnp.inf)
        st_ref[nt - 1] = _sortable(sc)

        kpos = (lax.broadcasted_iota(I32, (nt, TS, LANES), 0) * LANES
                + lax.broadcasted_iota(I32, (nt, TS, LANES), 2))
        qpos = past + lax.broadcasted_iota(I32, (nt, TS, LANES), 1)

        def count(pred):
            per_lane = jnp.sum(jnp.where(pred(st_ref[...]), 1.0, 0.0), axis=0)
            return jnp.sum(per_lane, axis=1, keepdims=True)

        thr = _kth_largest(lambda cand: count(lambda k: k >= cand[None]), (TS, 1), float(topk))
        thr3 = thr[None]
        need = float(topk) - count(lambda k: k > thr3)
        n_eq = count(lambda k: k == thr3)
        excess = jnp.where((n_eq > need) & (thr != KEY_NEG_INF), 1.0, 0.0)
        jb_ref[...] = jnp.full((TS, 1), L, I32)

        @pl.when(jnp.max(excess) > 0.0)
        def _():
            jb_ref[...] = _tie_bound(
                lambda cand: count(lambda k: jnp.where(k == thr3, kpos, L) < cand[None]),
                (TS, 1), need, nbits)

        keys = st_ref[...]
        rank_pos = jnp.where(keys == thr3, kpos, jnp.where(keys > thr3, -1, L + 1))
        sel = jnp.where(kpos <= qpos, rank_pos, L + 1) <= jb_ref[...][None]
        mask_ref[...] = jnp.where(sel, 0.0, NEG)


def _a1_sample(page_table, proj_s, knew_bf, cache_kidx, layer, topk):
    DB, TS, _ = proj_s.shape
    n_pages = page_table.shape[1]
    page = cache_kidx.shape[3]
    assert page == LANES and n_pages % PAGES_PER_STEP == 0 and topk <= n_pages * page
    nt = n_pages + 1
    qi_blk = (N_HEADS * HEAD_DIM + 2 * KV_HEADS * HEAD_DIM) // (IDX_HEADS * IDX_DIM)
    tail_blk = (N_HEADS * HEAD_DIM + 2 * KV_HEADS * HEAD_DIM + IDX_HEADS * IDX_DIM) // LANES

    def page_spec(p):
        return pl.BlockSpec((None, None, IDX_DIM, page),
                            lambda b, s, pt: (layer, pt[b, s * PAGES_PER_STEP + p], 0, 0))

    grid_spec = pltpu.PrefetchScalarGridSpec(
        num_scalar_prefetch=1,
        grid=(DB, n_pages // PAGES_PER_STEP),
        in_specs=[pl.BlockSpec((None, TS, IDX_HEADS * IDX_DIM), lambda b, s, pt: (b, 0, qi_blk)),
                  pl.BlockSpec((None, TS, LANES), lambda b, s, pt: (b, 0, tail_blk)),
                  pl.BlockSpec((None, LANES, IDX_DIM), lambda b, s, pt: (b, 0, 0))]
                 + [page_spec(p) for p in range(PAGES_PER_STEP)],
        out_specs=pl.BlockSpec((None, nt, TS, LANES), lambda b, s, pt: (b, 0, 0, 0)),
        scratch_shapes=[pltpu.VMEM((nt, TS, LANES), I32), pltpu.VMEM((TS, 1), I32)])
    return pl.pallas_call(
        functools.partial(_a1s_kernel, topk=topk, past=n_pages * page),
        grid_spec=grid_spec,
        out_shape=jax.ShapeDtypeStruct((DB, nt, TS, LANES), F32),
        compiler_params=_params(("parallel", "arbitrary")),
        name="a1_sample",
    )(page_table, proj_s, proj_s, knew_bf, *([cache_kidx] * PAGES_PER_STEP))


def _a2s_kernel(pt_ref, q_ref, mask_ref, tzs_ref, knew_ref, vnew_ref, *rest, past):
    k_pages = rest[:PAGES_PER_STEP]
    v_pages = rest[PAGES_PER_STEP:2 * PAGES_PER_STEP]
    o_ref, m_ref, l_ref, acc_ref = rest[2 * PAGES_PER_STEP:]
    s = pl.program_id(1)
    TS = q_ref.shape[0]
    n_slab = tzs_ref.shape[1]
    scale = HEAD_DIM ** -0.5

    @pl.when(s == 0)
    def _():
        m_ref[...] = jnp.full(m_ref.shape, NEG, F32)
        l_ref[...] = jnp.zeros(l_ref.shape, F32)
        acc_ref[...] = jnp.zeros(acc_ref.shape, F32)

    qg = [jnp.concatenate([q_ref[:, (g * GROUP + r) * HEAD_DIM:(g * GROUP + r + 1) * HEAD_DIM]
                           for r in range(GROUP)], axis=0).astype(BF16) for g in range(KV_HEADS)]

    def update(g, kt, vt, tiles):
        slabs = [jnp.minimum((past // LANES) - t, n_slab - 1) for t in tiles]
        mk = jnp.concatenate([mask_ref[t] for t in tiles], axis=1)
        mk = jnp.concatenate([mk] * GROUP, axis=0)
        bias = jnp.concatenate(
            [jnp.concatenate([tzs_ref[g * GROUP + r, sl] for r in range(GROUP)], axis=0) for sl in slabs],
            axis=1)
        sc = _nt(qg[g], kt) * scale + bias + mk
        m_old = m_ref[g]
        m_new = jnp.maximum(m_old, jnp.max(sc, axis=1, keepdims=True))
        alpha = jnp.exp(m_old - m_new)
        p = jnp.exp(sc - m_new)
        l_ref[g] = alpha * l_ref[g] + jnp.sum(p, axis=1, keepdims=True)
        acc_ref[g] = alpha * acc_ref[g] + jnp.dot(p.astype(BF16), vt, preferred_element_type=F32)
        m_ref[g] = m_new

    tiles = [s * PAGES_PER_STEP + p for p in range(PAGES_PER_STEP)]
    for g in range(KV_HEADS):
        rows = pl.ds(g, LANES, stride=KV_HEADS)
        kt = jnp.concatenate([kp[rows, :] for kp in k_pages], axis=0).astype(BF16)
        vt = jnp.concatenate([vp[rows, :] for vp in v_pages], axis=0).astype(BF16)
        update(g, kt, vt, tiles)

    @pl.when(s == pl.num_programs(1) - 1)
    def _():
        for g in range(KV_HEADS):
            update(g, knew_ref[:, g * HEAD_DIM:(g + 1) * HEAD_DIM].astype(BF16),
                   vnew_ref[:, g * HEAD_DIM:(g + 1) * HEAD_DIM].astype(BF16), [past // LANES])
        for g in range(KV_HEADS):
            res = acc_ref[g] / l_ref[g]
            for r in range(GROUP):
                h = g * GROUP + r
                o_ref[:, h * HEAD_DIM:(h + 1) * HEAD_DIM] = res[r * TS:(r + 1) * TS, :].astype(o_ref.dtype)


def _a2_sample(page_table, proj_s, mask_s, tzs, knew, vnew, cache_k, cache_v, layer):
    DB, TS, _ = proj_s.shape
    n_pages = page_table.shape[1]
    page = cache_k.shape[2] // KV_HEADS
    kvw = KV_HEADS * HEAD_DIM
    nt = n_pages + 1

    def page_spec(p):
        return pl.BlockSpec((None, None, page * KV_HEADS, HEAD_DIM),
                            lambda b, s, pt: (layer, pt[b, s * PAGES_PER_STEP + p], 0, 0))

    grid_spec = pltpu.PrefetchScalarGridSpec(
        num_scalar_prefetch=1,
        grid=(DB, n_pages // PAGES_PER_STEP),
        in_specs=[pl.BlockSpec((None, TS, N_HEADS * HEAD_DIM), lambda b, s, pt: (b, 0, 0)),
                  pl.BlockSpec((None, nt, TS, LANES), lambda b, s, pt: (b, 0, 0, 0)),
                  pl.BlockSpec(tzs.shape, lambda b, s, pt: (0, 0, 0, 0)),
                  pl.BlockSpec((None, LANES, kvw), lambda b, s, pt: (b, 0, 0)),
                  pl.BlockSpec((None, LANES, kvw), lambda b, s, pt: (b, 0, 0))]
                 + [page_spec(p) for p in range(PAGES_PER_STEP)] * 2,
        out_specs=pl.BlockSpec((None, TS, N_HEADS * HEAD_DIM), lambda b, s, pt: (b, 0, 0)),
        scratch_shapes=[pltpu.VMEM((KV_HEADS, GROUP * TS, 1), F32),
                        pltpu.VMEM((KV_HEADS, GROUP * TS, 1), F32),
                        pltpu.VMEM((KV_HEADS, GROUP * TS, HEAD_DIM), F32)])
    return pl.pallas_call(
        functools.partial(_a2s_kernel, past=n_pages * page),
        grid_spec=grid_spec,
        out_shape=jax.ShapeDtypeStruct((DB, TS, N_HEADS * HEAD_DIM), BF16),
        compiler_params=_params(("parallel", "arbitrary")),
        name="a2_sample",
    )(page_table, proj_s, mask_s, tzs, knew, vnew,
      *([cache_k] * PAGES_PER_STEP), *([cache_v] * PAGES_PER_STEP))


BQ = 512


def _bp_kernel(*refs):
    n_q = len(B_GROUPS) * GROUP
    q_refs = refs[:n_q]
    k_ref, v_ref, band_ref, o_ref, m_ref, l_ref, acc_ref = refs[n_q:]
    t0 = pl.program_id(2) * BQ
    first = t0 == 0
    scale = HEAD_DIM ** -0.5
    m_ref[...] = jnp.full(m_ref.shape, NEG, F32)
    l_ref[...] = jnp.zeros(l_ref.shape, F32)
    acc_ref[...] = jnp.zeros(acc_ref.shape, F32)

    def stacked(ref, rows):
        return jnp.concatenate([ref[r, rows, :] for r in range(GROUP)], axis=0)

    def update(q_ref, rows_q, nrow, key_rows, biases):
        qs = jnp.concatenate([qr[rows_q, :] for qr in q_ref], axis=0).astype(BF16)
        kt = jnp.concatenate([k_ref[kr, :] for kr in key_rows], axis=0).astype(BF16)
        vt = jnp.concatenate([v_ref[kr, :] for kr in key_rows], axis=0).astype(BF16)
        s = _nt(qs, kt) * scale + jnp.concatenate(biases, axis=1)
        m_old = stacked(m_ref, rows_q)
        m_new = jnp.maximum(m_old, jnp.max(s, axis=1, keepdims=True))
        alpha = jnp.exp(m_old - m_new)
        p = jnp.exp(s - jnp.concatenate([m_new] * len(key_rows), axis=1))
        l_new = alpha * stacked(l_ref, rows_q) + jnp.sum(p, axis=1, keepdims=True)
        acc = alpha * stacked(acc_ref, rows_q) + jnp.dot(p.astype(BF16), vt, preferred_element_type=F32)
        for r in range(GROUP):
            m_ref[r, rows_q, :] = m_new[r * nrow:(r + 1) * nrow]
            l_ref[r, rows_q, :] = l_new[r * nrow:(r + 1) * nrow]
            acc_ref[r, rows_q, :] = acc[r * nrow:(r + 1) * nrow]

    def band(n, half, rows=slice(None)):
        return jnp.concatenate([band_ref[r, n, half, rows, :] for r in range(GROUP)], axis=0)

    for n, (window, dil) in enumerate(B_GROUPS):
        assert window // dil == QB
        q_ref = q_refs[n * GROUP:(n + 1) * GROUP]
        span = dil * QB
        if span <= BQ:
            for rho in range(dil):
                for mt in range(BQ // span):
                    def rows(start):
                        return pl.ds(start, QB, stride=dil) if dil > 1 else pl.ds(start, QB)
                    off = rho + span * mt
                    prev_invalid = first if mt == 0 else False
                    prev = jnp.maximum(t0 + off - span, rho) if mt == 0 else t0 + off - span
                    bias_a = band(n, 0)
                    if mt == 0:
                        bias_a = bias_a + jnp.where(prev_invalid, NEG, 0.0)
                    update(q_ref, rows(off), QB, [rows(prev), rows(t0 + off)], [bias_a, band(n, 1)])
        else:
            nrow = BQ // dil
            assert span == k_ref.shape[0]
            brow = pl.ds(pl.multiple_of(pl.program_id(2) * nrow, nrow), nrow)
            for rho in range(dil):
                update(q_ref, pl.ds(rho, nrow, stride=dil), nrow, [pl.ds(rho, QB, stride=dil)],
                       [band(n, 1, brow)])

    for r in range(GROUP):
        o_ref[:, r * HEAD_DIM:(r + 1) * HEAD_DIM] = (acc_ref[r] / l_ref[r]).astype(o_ref.dtype)


def _b_prompt(qb, kv, band):
    B, T, _ = qb.shape
    nh = N_HEADS
    hd = HEAD_DIM
    gw = GROUP * hd
    assert T % BQ == 0

    def q_spec(n, r):
        return pl.BlockSpec((None, BQ, hd), lambda b, g, c: (b, c, (n * KV_HEADS + g) * GROUP + r))

    q_specs = [q_spec(n, r) for n in range(len(B_GROUPS)) for r in range(GROUP)]
    return pl.pallas_call(
        _bp_kernel,
        grid=(B, KV_HEADS, T // BQ),
        in_specs=q_specs + [
                  pl.BlockSpec((None, T, hd), lambda b, g, c: (b, 0, g)),
                  pl.BlockSpec((None, T, hd), lambda b, g, c: (b, 0, KV_HEADS + g)),
                  pl.BlockSpec((GROUP, 3, 2, QB, LANES), lambda b, g, c: (g, 0, 0, 0, 0))],
        out_specs=pl.BlockSpec((None, BQ, gw), lambda b, g, c: (b, c, g)),
        out_shape=jax.ShapeDtypeStruct((B, T, nh * hd), BF16),
        scratch_shapes=[pltpu.VMEM((GROUP, BQ, LANES), F32), pltpu.VMEM((GROUP, BQ, LANES), F32),
                        pltpu.VMEM((GROUP, BQ, hd), F32)],
        compiler_params=_params(("parallel", "parallel", "arbitrary")),
        name="b_prompt",
    )(*([qb] * len(q_specs)), kv, kv, band)


def _bs_kernel(q0_ref, q1_ref, q2_ref, kc_ref, vc_ref, kn_ref, vn_ref, tab_ref, o_ref):
    TS = q0_ref.shape[0]
    W = kc_ref.shape[0] // KV_HEADS
    scale = HEAD_DIM ** -0.5
    rows = pl.ds(pl.program_id(1), W, stride=KV_HEADS)
    kc = kc_ref[rows, :].astype(BF16)
    vc = vc_ref[rows, :].astype(BF16)
    kn = kn_ref[...].astype(BF16)
    vn = vn_ref[...].astype(BF16)
    logits = []
    for n in range(len(B_GROUPS)):
        q_ref = (q0_ref, q1_ref, q2_ref)[n]
        q = jnp.concatenate([q_ref[:, r * HEAD_DIM:(r + 1) * HEAD_DIM] for r in range(GROUP)],
                            axis=0).astype(BF16)
        logits.append((_nt(q, kc) * scale + tab_ref[n, :, :W],
                       _nt(q, kn) * scale + tab_ref[n, :, W:]))
    m = functools.reduce(jnp.maximum,
                         [jnp.max(x, axis=1, keepdims=True) for pair in logits for x in pair])
    l = jnp.zeros((GROUP * TS, 1), F32)
    acc = jnp.zeros((GROUP * TS, HEAD_DIM), F32)
    for sc_c, sc_n in logits:
        pc = jnp.exp(sc_c - m)
        pn = jnp.exp(sc_n - m)
        l = l + jnp.sum(pc, axis=1, keepdims=True) + jnp.sum(pn, axis=1, keepdims=True)
        acc = (acc + jnp.dot(pc.astype(BF16), vc, preferred_element_type=F32)
               + jnp.dot(pn.astype(BF16), vn, preferred_element_type=F32))
    res = acc / l
    for r in range(GROUP):
        o_ref[:, r * HEAD_DIM:(r + 1) * HEAD_DIM] = res[r * TS:(r + 1) * TS, :].astype(o_ref.dtype)


def _b_sample(qb_s, cache_k, cache_v, kv_new, tab):
    DB, TS, NQ = qb_s.shape
    W = cache_k.shape[1] // KV_HEADS
    hd = HEAD_DIM
    return pl.pallas_call(
        _bs_kernel,
        grid=(DB, KV_HEADS),
        in_specs=[pl.BlockSpec((None, TS, GROUP * hd), lambda b, g: (b, 0, g)),
                  pl.BlockSpec((None, TS, GROUP * hd), lambda b, g: (b, 0, KV_HEADS + g)),
                  pl.BlockSpec((None, TS, GROUP * hd), lambda b, g: (b, 0, 2 * KV_HEADS + g)),
                  pl.BlockSpec((None, W * KV_HEADS, hd), lambda b, g: (b, 0, 0)),
                  pl.BlockSpec((None, W * KV_HEADS, hd), lambda b, g: (b, 0, 0)),
                  pl.BlockSpec((None, LANES, hd), lambda b, g: (b, 0, g)),
                  pl.BlockSpec((None, LANES, hd), lambda b, g: (b, 0, KV_HEADS + g)),
                  pl.BlockSpec((len(B_GROUPS), None, GROUP * TS, W + LANES), lambda b, g: (0, g, 0, 0))],
        out_specs=pl.BlockSpec((None, TS, GROUP * hd), lambda b, g: (b, 0, g)),
        out_shape=jax.ShapeDtypeStruct((DB, TS, N_HEADS * hd), BF16),
        compiler_params=_params(("parallel", "arbitrary")),
        name="b_sample",
    )(qb_s, qb_s, qb_s, cache_k, cache_v, kv_new, kv_new, tab)


FFN_TF = 512
HALO = 16


def _ffn_up_kernel(xm_ref, xh_ref, wg_ref, wu_ref, cw_ref, cb_ref, act_ref, st_ref, xe_ref, *,
                   tiles_per_batch):
    i = pl.program_id(0)
    tm = xm_ref.shape[0]

    @pl.when(pl.program_id(1) == 0)
    def _():
        first = (i % tiles_per_batch) == 0
        xh = xh_ref[...]
        xe_ref[0:HALO, :] = jnp.where(first, jnp.zeros_like(xh), xh)
        xe_ref[HALO:, :] = xm_ref[...]

    gate = jnp.dot(xe_ref[...], wg_ref[...], preferred_element_type=F32)
    up = jnp.dot(xm_ref[...], wu_ref[...], preferred_element_type=F32)
    cw = cw_ref[...]
    conv = cb_ref[...] + ((gate[HALO - 2:HALO - 2 + tm] * cw[0:1] + gate[HALO - 1:HALO - 1 + tm] * cw[1:2])
                          + gate[HALO:] * cw[2:3])
    act_ref[...] = (jax.nn.silu(conv) * up).astype(act_ref.dtype)
    st_ref[...] = gate[HALO + tm - (CONV_WIDTH - 1):, :]


def _ffn_up(x_bf, w_gu_bf, conv_w, conv_b, layer, T):
    M, D = x_bf.shape
    F = conv_w.shape[2]
    tm = 1024
    assert T % tm == 0 and M % T == 0 and F % FFN_TF == 0 and tm % HALO == 0
    nf = F // FFN_TF
    hb = tm // HALO
    return pl.pallas_call(
        functools.partial(_ffn_up_kernel, tiles_per_batch=T // tm),
        grid=(M // tm, nf),
        in_specs=[pl.BlockSpec((tm, D), lambda i, j: (i, 0)),
                  pl.BlockSpec((HALO, D), lambda i, j: (jnp.maximum(i * hb - 1, 0), 0)),
                  pl.BlockSpec((None, D, FFN_TF), lambda i, j: (layer, 0, j)),
                  pl.BlockSpec((None, D, FFN_TF), lambda i, j: (layer, 0, nf + j)),
                  pl.BlockSpec((None, CONV_WIDTH, FFN_TF), lambda i, j: (layer, 0, j)),
                  pl.BlockSpec((None, 1, FFN_TF), lambda i, j: (layer, 0, j))],
        out_specs=[pl.BlockSpec((tm, FFN_TF), lambda i, j: (i, j)),
                   pl.BlockSpec((None, CONV_WIDTH - 1, FFN_TF), lambda i, j: (i, 0, j))],
        out_shape=[jax.ShapeDtypeStruct((M, F), BF16),
                   jax.ShapeDtypeStruct((M // tm, CONV_WIDTH - 1, F), F32)],
        scratch_shapes=[pltpu.VMEM((HALO + tm, D), BF16)],
        compiler_params=_params(("parallel", "arbitrary")),
        name="ffn_up",
    )(x_bf, x_bf, w_gu_bf, w_gu_bf, conv_w, conv_b)


def _mm_ln_kernel(a_ref, w_ref, x_ref, g_ref, b_ref, h_ref, hb_ref, acc_ref, *, alpha):
    k = pl.program_id(1)

    @pl.when(k == 0)
    def _():
        acc_ref[...] = jnp.zeros_like(acc_ref)

    acc_ref[...] += jnp.dot(a_ref[...], w_ref[...], preferred_element_type=F32)

    @pl.when(k == pl.num_programs(1) - 1)
    def _():
        y = alpha * x_ref[...] + acc_ref[...]
        mu = jnp.mean(y, axis=-1, keepdims=True)
        d = y - mu
        var = jnp.mean(d * d, axis=-1, keepdims=True)
        h = d * lax.rsqrt(var + LN_EPS) * g_ref[...] + b_ref[...]
        h_ref[...] = h
        hb_ref[...] = h.astype(hb_ref.dtype)


def _mm_ln(a_bf, w_bf, layer, x, g, b, alpha):
    M, K = a_bf.shape
    D = w_bf.shape[2]
    tm = min(512, M)
    tk = K // 4 if K > 2048 else K // 2
    assert M % tm == 0 and K % tk == 0 and tk % LANES == 0
    return pl.pallas_call(
        functools.partial(_mm_ln_kernel, alpha=alpha),
        grid=(M // tm, K // tk),
        in_specs=[pl.BlockSpec((tm, tk), lambda i, k: (i, k)),
                  pl.BlockSpec((None, tk, D), lambda i, k: (layer, k, 0)),
                  pl.BlockSpec((tm, D), lambda i, k: (i, 0)),
                  pl.BlockSpec((1, D), lambda i, k: (0, 0)),
                  pl.BlockSpec((1, D), lambda i, k: (0, 0))],
        out_specs=[pl.BlockSpec((tm, D), lambda i, k: (i, 0)),
                   pl.BlockSpec((tm, D), lambda i, k: (i, 0))],
        out_shape=[jax.ShapeDtypeStruct((M, D), F32), jax.ShapeDtypeStruct((M, D), BF16)],
        scratch_shapes=[pltpu.VMEM((tm, D), F32)],
        compiler_params=_params(("parallel", "arbitrary")),
        name="mm_ln",
    )(a_bf, w_bf, x, g.reshape(1, D), b.reshape(1, D))


def _layer_norm(x, g, b):
    mu = x.mean(-1, keepdims=True)
    var = jnp.square(x - mu).mean(-1, keepdims=True)
    return (x - mu) * lax.rsqrt(var + LN_EPS) * g + b


def _pick_tm(M):
    for tm in (1024, 512, 256, 128, 64, 32, 16):
        if M % tm == 0:
            return tm
    raise ValueError(M)


def _pick_tn(N):
    for tn in (512, 384, 256, 128):
        if N % tn == 0:
            return tn
    raise ValueError(N)


def _proj(x_bf, w_bf, layer=0):
    return _mm(x_bf, w_bf, layer, _pick_tm(x_bf.shape[0]), _pick_tn(w_bf.shape[2]))


def _ffn_act_sample(hs_bf, state, w_gu_bf, conv_w, conv_b, layer, DB, TS):
    conv_w, conv_b = conv_w[layer], conv_b[layer]
    F = conv_w.shape[1]
    gu = _proj(hs_bf, w_gu_bf, layer)
    gate = gu[:, :F].reshape(DB, TS, F)
    up = gu[:, F:].reshape(DB, TS, F)
    ext = jnp.concatenate([state, gate], axis=1)
    conv = conv_b + sum(ext[:, j:j + TS] * conv_w[j] for j in range(CONV_WIDTH))
    act = (jax.nn.silu(conv) * up).astype(BF16).reshape(DB * TS, F)
    return act, ext[:, ext.shape[1] - (CONV_WIDTH - 1):]


def _table_indices(T, W, TS):
    U = SAT_DIST + TZ_OFF + QB
    assert T <= T5_MAX_DISTANCE
    u = np.arange(U)[:, None]
    c = np.arange(LANES)[None, :]
    tz_idx = _t5_bucket(jnp.asarray(np.maximum(u - c - TZ_OFF, 0), I32))
    a = np.arange(QB)[:, None]
    band = []
    for window, dil in B_GROUPS:
        for off in (QB, 0):
            d = a + off - c
            ok = (d >= 0) & (d <= window // dil)
            band.append(jnp.where(jnp.asarray(ok), _t5_bucket(jnp.asarray(np.maximum(d, 0) * dil, I32)),
                                  NUM_BUCKETS))
    band_idx = jnp.concatenate(band, axis=0)
    nkt = W // LANES + 1
    t = np.arange(TS)[None, :, None]
    key = (np.arange(nkt)[:, None, None] * LANES + np.arange(LANES)[None, None, :])
    d = W + t - key
    tabs = []
    for window, dil in B_GROUPS:
        ok = (d >= 0) & (d % dil == 0) & (d <= window)
        tabs.append(jnp.where(jnp.asarray(ok), _t5_bucket(jnp.asarray(np.maximum(d, 0), I32)), NUM_BUCKETS))
    samp_idx = jnp.concatenate(tabs, axis=0).reshape(-1, LANES)
    rows = [tz_idx, band_idx, samp_idx]
    total = sum(r.shape[0] for r in rows)
    pad = (-total) % 128
    if pad:
        rows.append(jnp.full((pad, LANES), NUM_BUCKETS, I32))
    return jnp.concatenate(rows, axis=0), U, nkt


def kernel(x_prompt, x_sample, cache_k_a, cache_v_a, cache_kidx_a, cache_k_b, cache_v_b, state_ffn, page_table, a_w_in, a_w_o, a_kn_g, a_kn_b, b_w_kv, b_w_q, b_w_o, ffn_w_up, ffn_conv_w, ffn_conv_b, ffn_w_down, ln_g, ln_b, rel_bias):
    B, T, D = x_prompt.shape
    DB, TS, _ = x_sample.shape
    depth = ffn_w_up.shape[0]
    n_a = a_w_in.shape[0]
    d_ff = ffn_w_down.shape[1]
    W = cache_k_b.shape[1]
    n_pages = page_table.shape[1]
    page = cache_k_a.shape[2]
    past = n_pages * page
    alpha = (2 * depth) ** 0.25
    kvw = KV_HEADS * HEAD_DIM
    a_q = N_HEADS * HEAD_DIM
    a_in = a_w_in.shape[2]
    np_cols = ((a_in + LANES - 1) // LANES) * LANES
    ki0 = a_q + 2 * kvw + IDX_HEADS * IDX_DIM

    idx_all, U, nkt = _table_indices(T, W, TS)
    tabs = _bias_tables(rel_bias, idx_all)
    tz = tabs[:, :U]
    band = tabs[:, U:U + 6 * QB].reshape(N_HEADS, 3, 2, QB, LANES)
    samp = tabs[:, U + 6 * QB:U + 6 * QB + 3 * nkt * TS].reshape(KV_HEADS, GROUP, 3, nkt, TS, LANES)
    samp = samp.transpose(2, 0, 1, 4, 3, 5).reshape(3, KV_HEADS, GROUP * TS, nkt * LANES)
    n_slab = SAT_DIST // LANES + 1
    tzs = tz[:, TZ_OFF:TZ_OFF + n_slab * LANES].reshape(N_HEADS, n_slab, LANES, LANES)[:, :, :TS, :]

    cache_k_a2 = cache_k_a.reshape(n_a, -1, page * KV_HEADS, HEAD_DIM)
    cache_v_a2 = cache_v_a.reshape(n_a, -1, page * KV_HEADS, HEAD_DIM)
    cache_k_b2 = cache_k_b.reshape(DB, W * KV_HEADS, HEAD_DIM)
    cache_v_b2 = cache_v_b.reshape(DB, W * KV_HEADS, HEAD_DIM)
    cache_kidx_t = jnp.swapaxes(cache_kidx_a, 2, 3)

    def pad_rows(x, n):
        return jnp.pad(x, ((0, 0), (0, n - x.shape[1]), (0, 0)))

    f_pad = ((d_ff + FFN_TF - 1) // FFN_TF) * FFN_TF
    pad_f = f_pad - d_ff
    w_gu_all = jnp.pad(ffn_w_up.reshape(depth, D, 2, d_ff), ((0, 0), (0, 0), (0, 0), (0, pad_f))
                       ).astype(BF16).reshape(depth, D, 2 * f_pad)
    w_down_all = jnp.pad(ffn_w_down, ((0, 0), (0, pad_f), (0, 0))).astype(BF16)
    conv_w_all = jnp.pad(ffn_conv_w, ((0, 0), (0, 0), (0, pad_f)))
    conv_b_all = jnp.pad(ffn_conv_b, ((0, 0), (0, pad_f))).reshape(depth, 1, f_pad)
    w_in_all = jnp.pad(a_w_in, ((0, 0), (0, 0), (0, np_cols - a_in))).astype(BF16)
    w_o_a, w_o_b = a_w_o.astype(BF16), b_w_o.astype(BF16)
    w_q_all = b_w_q.astype(BF16)
    w_kv_bf = b_w_kv.astype(BF16)[None]

    hp = x_prompt.reshape(B * T, D)
    hs = x_sample.reshape(DB * TS, D)
    hp_bf, hs_bf = hp.astype(BF16), hs.astype(BF16)
    ka_p, va_p, kia_p, ka_s, va_s, kia_s, ffn_p, ffn_s = [], [], [], [], [], [], [], []
    for layer in range(depth):
        if layer < n_a:
            a = layer
            w_o_bf, w_o_layer = w_o_a, a
            proj = _proj(hp_bf, w_in_all, a).reshape(B, T, np_cols)
            k = proj[..., a_q:a_q + kvw].reshape(B, T, KV_HEADS, HEAD_DIM)
            v = proj[..., a_q + kvw:a_q + 2 * kvw].reshape(B, T, KV_HEADS, HEAD_DIM)
            ki = _layer_norm(proj[..., ki0:ki0 + IDX_DIM], a_kn_g[a], a_kn_b[a])
            ka_p.append(k); va_p.append(v); kia_p.append(ki)
            mask = _a1_prompt(ki.astype(BF16), proj, min(TOPK_MAX, T // 4))
            o = _a2_prompt(proj, mask, tz).reshape(B * T, a_q)
            proj_s = _proj(hs_bf, w_in_all, a).reshape(DB, TS, np_cols)
            k_s = proj_s[..., a_q:a_q + kvw]
            v_s = proj_s[..., a_q + kvw:a_q + 2 * kvw]
            ki_s = _layer_norm(proj_s[..., ki0:ki0 + IDX_DIM], a_kn_g[a], a_kn_b[a])
            ka_s.append(k_s.reshape(DB, TS, KV_HEADS, HEAD_DIM))
            va_s.append(v_s.reshape(DB, TS, KV_HEADS, HEAD_DIM))
            kia_s.append(ki_s)
            mask_s = _a1_sample(page_table, proj_s, pad_rows(ki_s, LANES).astype(BF16), cache_kidx_t, a,
                                min(TOPK_MAX, (past + TS) // 4))
            o_s = _a2_sample(page_table, proj_s, mask_s, tzs, pad_rows(k_s, LANES), pad_rows(v_s, LANES),
                             cache_k_a2, cache_v_a2, a).reshape(DB * TS, a_q)
        else:
            if layer == n_a:
                kv_p = _proj(hp_bf, w_kv_bf).reshape(B, T, 2 * kvw)
                kv_s = _proj(hs_bf, w_kv_bf).reshape(DB, TS, 2 * kvw)
                kv_s_pad = pad_rows(kv_s, LANES)
            bl = layer - n_a
            w_o_bf, w_o_layer = w_o_b, bl
            o = _b_prompt(_proj(hp_bf, w_q_all, bl).reshape(B, T, -1), kv_p, band).reshape(B * T, a_q)
            o_s = _b_sample(_proj(hs_bf, w_q_all, bl).reshape(DB, TS, -1), cache_k_b2, cache_v_b2, kv_s_pad,
                            samp).reshape(DB * TS, a_q)
        hp, hp_bf = _mm_ln(o, w_o_bf, w_o_layer, hp, ln_g[layer, 0], ln_b[layer, 0], alpha)
        hs, hs_bf = _mm_ln(o_s, w_o_bf, w_o_layer, hs, ln_g[layer, 0], ln_b[layer, 0], alpha)
        act, st = _ffn_up(hp_bf, w_gu_all, conv_w_all, conv_b_all, layer, T)
        ffn_p.append(st.reshape(B, -1, CONV_WIDTH - 1, f_pad)[:, -1, :, :d_ff])
        act_s, st_s = _ffn_act_sample(hs_bf, jnp.pad(state_ffn[layer], ((0, 0), (0, 0), (0, pad_f))),
                                      w_gu_all, conv_w_all, conv_b_all, layer, DB, TS)
        ffn_s.append(st_s[..., :d_ff])
        hp, hp_bf = _mm_ln(act, w_down_all, layer, hp, ln_g[layer, 1], ln_b[layer, 1], alpha)
        hs, hs_bf = _mm_ln(act_s, w_down_all, layer, hs, ln_g[layer, 1], ln_b[layer, 1], alpha)
    keep = min(max(w for w, _ in B_GROUPS), T)
    kb_p = kv_p[..., :kvw].reshape(B, T, KV_HEADS, HEAD_DIM)
    vb_p = kv_p[..., kvw:].reshape(B, T, KV_HEADS, HEAD_DIM)
    kb_s = kv_s[..., :kvw].reshape(DB, TS, KV_HEADS, HEAD_DIM)
    vb_s = kv_s[..., kvw:].reshape(DB, TS, KV_HEADS, HEAD_DIM)
    return (hp.reshape(B, T, D), hs.reshape(DB, TS, D), jnp.stack(ka_p), jnp.stack(va_p), jnp.stack(kia_p),
            jnp.stack(ka_s), jnp.stack(va_s), jnp.stack(kia_s), kb_p[:, T - keep:], vb_p[:, T - keep:],
            kb_s, vb_s, jnp.stack(ffn_p), jnp.stack(ffn_s))
```

```python
import functools
import math

import numpy as np
import jax
import jax.numpy as jnp
from jax import lax
from jax.experimental import pallas as pl
from jax.experimental.pallas import tpu as pltpu

F32 = jnp.float32
BF16 = jnp.bfloat16
I32 = jnp.int32

HEAD_DIM = 128
N_HEADS = 16
KV_HEADS = 4
GROUP = N_HEADS // KV_HEADS
IDX_HEADS = 16
IDX_DIM = 64
TOPK_MAX = 256
B_GROUPS = ((128, 1), (512, 4), (2048, 16))
NUM_BUCKETS = 32
T5_MAX_DISTANCE = 2048
LN_EPS = 1e-5
NEG = -1e30
CONV_WIDTH = 3

LANES = 128
QB = 128
KC = 512
TZ_OFF = KC - QB
SAT_DIST = T5_MAX_DISTANCE + QB
INT_MIN = -2 ** 31
KEY_NEG_INF = INT_MIN + 0x7FFFFF
VMEM_LIMIT = 56 * 1024 * 1024

NT_DIMS = (((1,), (1,)), ((), ()))


def _nt(a, b):
    return lax.dot_general(a, b, NT_DIMS, preferred_element_type=F32)


def _params(sem):
    return pltpu.CompilerParams(dimension_semantics=sem, vmem_limit_bytes=VMEM_LIMIT)


def _mm_kernel(x_ref, w_ref, o_ref):
    o_ref[...] = jnp.dot(x_ref[...], w_ref[...], preferred_element_type=F32).astype(o_ref.dtype)


def _mm(x, w, layer, tm, tn, out_dtype=F32):
    M, K = x.shape
    N = w.shape[2]
    assert M % tm == 0 and N % tn == 0, (M, N, tm, tn)
    return pl.pallas_call(
        _mm_kernel,
        grid=(M // tm, N // tn),
        in_specs=[pl.BlockSpec((tm, K), lambda i, j: (i, 0)),
                  pl.BlockSpec((None, K, tn), lambda i, j: (layer, 0, j))],
        out_specs=pl.BlockSpec((tm, tn), lambda i, j: (i, j)),
        out_shape=jax.ShapeDtypeStruct((M, N), out_dtype),
        compiler_params=_params(("parallel", "arbitrary")),
        name="mm",
    )(x, w)


def _t5_bucket(dist):
    dist = jnp.maximum(dist, 0)
    exact = NUM_BUCKETS // 2
    far = exact + (jnp.log(jnp.maximum(dist, 1).astype(F32) / exact)
                   / math.log(T5_MAX_DISTANCE / exact) * (NUM_BUCKETS - exact)).astype(I32)
    return jnp.where(dist < exact, dist, jnp.minimum(far, NUM_BUCKETS - 1))


def _table_kernel(rb_ref, idx_ref, o_ref):
    idx = idx_ref[...]
    for h in range(N_HEADS):
        acc = jnp.full(idx.shape, NEG, F32)
        for k in range(NUM_BUCKETS):
            acc = jnp.where(idx == k, rb_ref[k, h], acc)
        o_ref[h] = acc


def _bias_tables(rel_bias, idx):
    R = idx.shape[0]
    tr = 128
    assert R % tr == 0
    return pl.pallas_call(
        _table_kernel,
        grid=(R // tr,),
        in_specs=[pl.BlockSpec(memory_space=pltpu.SMEM),
                  pl.BlockSpec((tr, LANES), lambda i: (i, 0))],
        out_specs=pl.BlockSpec((N_HEADS, tr, LANES), lambda i: (0, i, 0)),
        out_shape=jax.ShapeDtypeStruct((N_HEADS, R, LANES), F32),
        compiler_params=_params(("arbitrary",)),
        name="bias_tables",
    )(rel_bias, idx)


def _sortable(x):
    bits = lax.bitcast_convert_type(x, I32)
    return bits ^ ((bits >> 31) & jnp.int32(0x7FFFFFFF))


def _kth_largest(count_ge, shape, k):
    def body(t, ans):
        cand_u = ans | (jnp.int32(1) << (31 - t))
        cnt = count_ge(cand_u ^ jnp.int32(INT_MIN))
        return jnp.where(cnt >= k, cand_u, ans)
    ans = lax.fori_loop(0, 32, body, jnp.zeros(shape, I32))
    return ans ^ jnp.int32(INT_MIN)


def _tie_bound(count_eq_below, shape, need, nbits):
    def body(t, ans):
        cand = ans | (jnp.int32(1) << (nbits - 1 - t))
        return jnp.where(count_eq_below(cand) < need, cand, ans)
    return lax.fori_loop(0, nbits, body, jnp.zeros(shape, I32))


def _a1p_kernel(kin_ref, qi_ref, tail_ref, mask_ref, st_ref, jb_ref, *, topk):
    i = pl.program_id(1)
    T = kin_ref.shape[0]
    nchunk = T // KC
    nbits = (T - 1).bit_length()
    tail_t = tail_ref[...].T
    wi_t = tail_t[IDX_DIM:IDX_DIM + IDX_HEADS, :] * (IDX_HEADS ** -0.5 * IDX_DIM ** -0.5)
    qpos = i * QB + lax.broadcasted_iota(I32, (KC, QB), 1)
    row = lax.broadcasted_iota(I32, (KC, QB), 0)

    nj = (i * QB) // KC + 1

    for c in range(nchunk):
        @pl.when(c < nj)
        def _():
            kc = kin_ref[c * KC:(c + 1) * KC, :]
            acc = jnp.zeros((KC, QB), F32)
            for h in range(IDX_HEADS):
                qh = qi_ref[:, h * IDX_DIM:(h + 1) * IDX_DIM].astype(BF16)
                acc = acc + jnp.maximum(_nt(kc, qh), 0.0) * wi_t[h:h + 1, :]
            acc = jnp.where(c * KC + row <= qpos, acc, -jnp.inf)
            st_ref[c * KC:(c + 1) * KC, :] = _sortable(acc)

    def select(nvis):
        def count(pred):
            cnt = jnp.zeros((1, QB), F32)
            for c in range(nvis):
                blk = st_ref[c * KC:(c + 1) * KC, :]
                cnt = cnt + jnp.sum(jnp.where(pred(blk, c * KC + row), 1.0, 0.0), axis=0, keepdims=True)
            return cnt

        thr = _kth_largest(lambda cand: count(lambda blk, _: blk >= cand), (1, QB), float(topk))
        need = float(topk) - count(lambda blk, _: blk > thr)
        n_eq = count(lambda blk, _: blk == thr)
        excess = jnp.where((n_eq > need) & (thr != KEY_NEG_INF), 1.0, 0.0)
        jb_ref[...] = jnp.full((1, QB), T, I32)

        @pl.when(jnp.max(excess) > 0.0)
        def _():
            jb_ref[...] = _tie_bound(
                lambda cand: count(lambda blk, kpos: jnp.where(blk == thr, kpos, T) < cand),
                (1, QB), need, nbits)

        jb = jb_ref[...]
        for c in range(nchunk):
            if c < nvis:
                blk = st_ref[c * KC:(c + 1) * KC, :]
                kpos = c * KC + row
                rank_pos = jnp.where(blk == thr, kpos, jnp.where(blk > thr, -1, T + 1))
                sel = jnp.where(kpos <= qpos, rank_pos, T + 1) <= jb
                m_t = jnp.where(sel, 0.0, NEG)
                for s4 in range(KC // QB):
                    mask_ref[c, :, s4 * QB:(s4 + 1) * QB] = (
                        m_t[s4 * QB:(s4 + 1) * QB, :].T.astype(mask_ref.dtype))
            else:
                mask_ref[c] = jnp.full((QB, KC), NEG, mask_ref.dtype)

    for nvis in range(1, nchunk + 1):
        pl.when(nj == nvis)(functools.partial(select, nvis))


def _a1_prompt(kin_bf, proj, topk):
    B, T, _ = proj.shape
    nq, nc = T // QB, T // KC
    assert topk <= KC
    qi_blk =(N_HEADS * HEAD_DIM + 2 * KV_HEADS * HEAD_DIM) // (IDX_HEADS * IDX_DIM)
    tail_blk = (N_HEADS * HEAD_DIM + 2 * KV_HEADS * HEAD_DIM + IDX_HEADS * IDX_DIM) // LANES
    return pl.pallas_call(
        functools.partial(_a1p_kernel, topk=topk),
        grid=(B, nq),
        in_specs=[pl.BlockSpec((None, T, IDX_DIM), lambda b, i: (b, 0, 0)),
                  pl.BlockSpec((None, QB, IDX_HEADS * IDX_DIM), lambda b, i: (b, i, qi_blk)),
                  pl.BlockSpec((None, QB, LANES), lambda b, i: (b, i, tail_blk))],
        out_specs=pl.BlockSpec((None, None, nc, QB, KC), lambda b, i: (b, i, 0, 0, 0)),
        out_shape=jax.ShapeDtypeStruct((B, nq, nc, QB, KC), BF16),
        scratch_shapes=[pltpu.VMEM((T, QB), I32), pltpu.VMEM((1, QB), I32)],
        compiler_params=_params(("parallel", "arbitrary")),
        name="a1_prompt",
    )(kin_bf, proj, proj)


def _a2p_kernel(q_ref, k_ref, v_ref, mask_ref, tz_ref, o_ref):
    i = pl.program_id(2)
    nj = (i * QB) // KC + 1
    scale = HEAD_DIM ** -0.5
    HP = GROUP
    R = HP * QB
    heads = [list(range(c * HP, (c + 1) * HP)) for c in range(GROUP // HP)]
    qs = [jnp.concatenate([q_ref[:, r * HEAD_DIM:(r + 1) * HEAD_DIM] for r in hs], axis=0).astype(BF16)
          for hs in heads]

    def body(j, carry):
        k0 = pl.multiple_of(j * KC, KC)
        kt = k_ref[pl.ds(k0, KC), :].astype(BF16)
        vt = v_ref[pl.ds(k0, KC), :].astype(BF16)
        base = i * QB - j * KC + TZ_OFF
        mk = mask_ref[j].astype(F32)
        mk = jnp.concatenate([mk] * HP, axis=0)
        out = []
        for hs, q, (m, l, acc) in zip(heads, qs, carry):
            bias = jnp.concatenate(
                [jnp.concatenate(
                    [tz_ref[r, pl.ds(pl.multiple_of(base - QB * s4, QB), QB), :] for s4 in range(KC // QB)],
                    axis=1) for r in hs], axis=0)
            s = _nt(q, kt) * scale + bias + mk
            m_new = jnp.maximum(m, jnp.max(s, axis=1, keepdims=True))
            alpha = jnp.exp(m - m_new)
            p = jnp.exp(s - m_new)
            l = alpha * l + jnp.sum(p, axis=1, keepdims=True)
            acc = alpha * acc + jnp.dot(p.astype(BF16), vt, preferred_element_type=F32)
            out.append((m_new, l, acc))
        return tuple(out)

    init = tuple((jnp.full((R, 1), NEG, F32), jnp.zeros((R, 1), F32), jnp.zeros((R, HEAD_DIM), F32))
                 for _ in heads)
    final = lax.fori_loop(0, nj, body, init)
    for hs, (m, l, acc) in zip(heads, final):
        res = acc / l
        for n, r in enumerate(hs):
            o_ref[:, r * HEAD_DIM:(r + 1) * HEAD_DIM] = res[n * QB:(n + 1) * QB, :].astype(o_ref.dtype)


def _a2_prompt(proj, mask, tz):
    B, T, _ = proj.shape
    nq, nc = T // QB, T // KC
    gw = GROUP * HEAD_DIM
    k_blk0 = N_HEADS * HEAD_DIM // HEAD_DIM
    v_blk0 = k_blk0 + KV_HEADS
    U = tz.shape[1]
    return pl.pallas_call(
        _a2p_kernel,
        grid=(KV_HEADS, B, nq),
        in_specs=[pl.BlockSpec((None, QB, gw), lambda g, b, i: (b, i, g)),
                  pl.BlockSpec((None, T, HEAD_DIM), lambda g, b, i: (b, 0, k_blk0 + g)),
                  pl.BlockSpec((None, T, HEAD_DIM), lambda g, b, i: (b, 0, v_blk0 + g)),
                  pl.BlockSpec((None, None, nc, QB, KC), lambda g, b, i: (b, i, 0, 0, 0)),
                  pl.BlockSpec((GROUP, U, LANES), lambda g, b, i: (g, 0, 0))],
        out_specs=pl.BlockSpec((None, QB, gw), lambda g, b, i: (b, i, g)),
        out_shape=jax.ShapeDtypeStruct((B, T, N_HEADS * HEAD_DIM), BF16),
        compiler_params=_params(("parallel", "parallel", "arbitrary")),
        name="a2_prompt",
    )(proj, proj, proj, mask, tz)


PAGES_PER_STEP = 16


def _a1s_kernel(pt_ref, qi_ref, tail_ref, knew_ref, *rest, topk, past):
    page_refs = rest[:PAGES_PER_STEP]
    mask_ref, st_ref, jb_ref = rest[PAGES_PER_STEP:]
    s = pl.program_id(1)
    nt, TS, _ = st_ref.shape
    L = nt * LANES
    nbits = (L - 1).bit_length()
    qi = jnp.concatenate([qi_ref[:, h * IDX_DIM:(h + 1) * IDX_DIM] for h in range(IDX_HEADS)],
                         axis=0).astype(BF16)
    w_col = jnp.concatenate([tail_ref[:, IDX_DIM + h:IDX_DIM + h + 1] for h in range(IDX_HEADS)],
                            axis=0) * (IDX_HEADS ** -0.5 * IDX_DIM ** -0.5)

    def scores(qk):
        sc = jnp.maximum(qk, 0.0) * w_col
        return jnp.sum(sc.reshape(IDX_HEADS, TS, LANES), axis=0)

    for p in range(PAGES_PER_STEP):
        qk = jnp.dot(qi, page_refs[p][...].astype(BF16), preferred_element_type=F32)
        st_ref[s * PAGES_PER_STEP + p] = _sortable(scores(qk))

    @pl.when(s == pl.num_programs(1) - 1)
    def _():
        t = lax.broadcasted_iota(I32, (TS, LANES), 0)
        c = lax.broadcasted_iota(I32, (TS, LANES), 1)
        sc = jnp.where(c <= t, scores(_nt(qi, knew_ref[...])), -jnp.inf)
        st_ref[nt - 1] = _sortable(sc)

        kpos = (lax.broadcasted_iota(I32, (nt, TS, LANES), 0) * LANES
                + lax.broadcasted_iota(I32, (nt, TS, LANES), 2))
        qpos = past + lax.broadcasted_iota(I32, (nt, TS, LANES), 1)

        def count(pred):
            per_lane = jnp.sum(jnp.where(pred(st_ref[...]), 1.0, 0.0), axis=0)
            return jnp.sum(per_lane, axis=1, keepdims=True)

        thr = _kth_largest(lambda cand: count(lambda k: k >= cand[None]), (TS, 1), float(topk))
        thr3 = thr[None]
        need = float(topk) - count(lambda k: k > thr3)
        n_eq = count(lambda k: k == thr3)
        excess = jnp.where((n_eq > need) & (thr != KEY_NEG_INF), 1.0, 0.0)
        jb_ref[...] = jnp.full((TS, 1), L, I32)

        @pl.when(jnp.max(excess) > 0.0)
        def _():
            jb_ref[...] = _tie_bound(
                lambda cand: count(lambda k: jnp.where(k == thr3, kpos, L) < cand[None]),
                (TS, 1), need, nbits)

        keys = st_ref[...]
        rank_pos = jnp.where(keys == thr3, kpos, jnp.where(keys > thr3, -1, L + 1))
        sel = jnp.where(kpos <= qpos, rank_pos, L + 1) <= jb_ref[...][None]
        mask_ref[...] = jnp.where(sel, 0.0, NEG)


def _a1_sample(page_table, proj_s, knew_bf, cache_kidx, layer, topk):
    DB, TS, _ = proj_s.shape
    n_pages = page_table.shape[1]
    page = cache_kidx.shape[3]
    assert page == LANES and n_pages % PAGES_PER_STEP == 0 and topk <= n_pages * page
    nt = n_pages + 1
    qi_blk = (N_HEADS * HEAD_DIM + 2 * KV_HEADS * HEAD_DIM) // (IDX_HEADS * IDX_DIM)
    tail_blk = (N_HEADS * HEAD_DIM + 2 * KV_HEADS * HEAD_DIM + IDX_HEADS * IDX_DIM) // LANES

    def page_spec(p):
        return pl.BlockSpec((None, None, IDX_DIM, page),
                            lambda b, s, pt: (layer, pt[b, s * PAGES_PER_STEP + p], 0, 0))

    grid_spec = pltpu.PrefetchScalarGridSpec(
        num_scalar_prefetch=1,
        grid=(DB, n_pages // PAGES_PER_STEP),
        in_specs=[pl.BlockSpec((None, TS, IDX_HEADS * IDX_DIM), lambda b, s, pt: (b, 0, qi_blk)),
                  pl.BlockSpec((None, TS, LANES), lambda b, s, pt: (b, 0, tail_blk)),
                  pl.BlockSpec((None, LANES, IDX_DIM), lambda b, s, pt: (b, 0, 0))]
                 + [page_spec(p) for p in range(PAGES_PER_STEP)],
        out_specs=pl.BlockSpec((None, nt, TS, LANES), lambda b, s, pt: (b, 0, 0, 0)),
        scratch_shapes=[pltpu.VMEM((nt, TS, LANES), I32), pltpu.VMEM((TS, 1), I32)])
    return pl.pallas_call(
        functools.partial(_a1s_kernel, topk=topk, past=n_pages * page),
        grid_spec=grid_spec,
        out_shape=jax.ShapeDtypeStruct((DB, nt, TS, LANES), F32),
        compiler_params=_params(("parallel", "arbitrary")),
        name="a1_sample",
    )(page_table, proj_s, proj_s, knew_bf, *([cache_kidx] * PAGES_PER_STEP))


def _a2s_kernel(pt_ref, q_ref, mask_ref, tzs_ref, knew_ref, vnew_ref, *rest, past):
    k_pages = rest[:PAGES_PER_STEP]
    v_pages = rest[PAGES_PER_STEP:2 * PAGES_PER_STEP]
    o_ref, m_ref, l_ref, acc_ref = rest[2 * PAGES_PER_STEP:]
    s = pl.program_id(1)
    TS = q_ref.shape[0]
    n_slab = tzs_ref.shape[1]
    scale = HEAD_DIM ** -0.5

    @pl.when(s == 0)
    def _():
        m_ref[...] = jnp.full(m_ref.shape, NEG, F32)
        l_ref[...] = jnp.zeros(l_ref.shape, F32)
        acc_ref[...] = jnp.zeros(acc_ref.shape, F32)

    qg = [jnp.concatenate([q_ref[:, (g * GROUP + r) * HEAD_DIM:(g * GROUP + r + 1) * HEAD_DIM]
                           for r in range(GROUP)], axis=0).astype(BF16) for g in range(KV_HEADS)]

    def update(g, kt, vt, tiles):
        slabs = [jnp.minimum((past // LANES) - t, n_slab - 1) for t in tiles]
        mk = jnp.concatenate([mask_ref[t] for t in tiles], axis=1)
        mk = jnp.concatenate([mk] * GROUP, axis=0)
        bias = jnp.concatenate(
            [jnp.concatenate([tzs_ref[g * GROUP + r, sl] for r in range(GROUP)], axis=0) for sl in slabs],
            axis=1)
        sc = _nt(qg[g], kt) * scale + bias + mk
        m_old = m_ref[g]
        m_new = jnp.maximum(m_old, jnp.max(sc, axis=1, keepdims=True))
        alpha = jnp.exp(m_old - m_new)
        p = jnp.exp(sc - m_new)
        l_ref[g] = alpha * l_ref[g] + jnp.sum(p, axis=1, keepdims=True)
        acc_ref[g] = alpha * acc_ref[g] + jnp.dot(p.astype(BF16), vt, preferred_element_type=F32)
        m_ref[g] = m_new

    tiles = [s * PAGES_PER_STEP + p for p in range(PAGES_PER_STEP)]
    for g in range(KV_HEADS):
        rows = pl.ds(g, LANES, stride=KV_HEADS)
        kt = jnp.concatenate([kp[rows, :] for kp in k_pages], axis=0).astype(BF16)
        vt = jnp.concatenate([vp[rows, :] for vp in v_pages], axis=0).astype(BF16)
        update(g, kt, vt, tiles)

    @pl.when(s == pl.num_programs(1) - 1)
    def _():
        for g in range(KV_HEADS):
            update(g, knew_ref[:, g * HEAD_DIM:(g + 1) * HEAD_DIM].astype(BF16),
                   vnew_ref[:, g * HEAD_DIM:(g + 1) * HEAD_DIM].astype(BF16), [past // LANES])
        for g in range(KV_HEADS):
            res = acc_ref[g] / l_ref[g]
            for r in range(GROUP):
                h = g * GROUP + r
                o_ref[:, h * HEAD_DIM:(h + 1) * HEAD_DIM] = res[r * TS:(r + 1) * TS, :].astype(o_ref.dtype)


def _a2_sample(page_table, proj_s, mask_s, tzs, knew, vnew, cache_k, cache_v, layer):
    DB, TS, _ = proj_s.shape
    n_pages = page_table.shape[1]
    page = cache_k.shape[2] // KV_HEADS
    kvw = KV_HEADS * HEAD_DIM
    nt = n_pages + 1

    def page_spec(p):
        return pl.BlockSpec((None, None, page * KV_HEADS, HEAD_DIM),
                            lambda b, s, pt: (layer, pt[b, s * PAGES_PER_STEP + p], 0, 0))

    grid_spec = pltpu.PrefetchScalarGridSpec(
        num_scalar_prefetch=1,
        grid=(DB, n_pages // PAGES_PER_STEP),
        in_specs=[pl.BlockSpec((None, TS, N_HEADS * HEAD_DIM), lambda b, s, pt: (b, 0, 0)),
                  pl.BlockSpec((None, nt, TS, LANES), lambda b, s, pt: (b, 0, 0, 0)),
                  pl.BlockSpec(tzs.shape, lambda b, s, pt: (0, 0, 0, 0)),
                  pl.BlockSpec((None, LANES, kvw), lambda b, s, pt: (b, 0, 0)),
                  pl.BlockSpec((None, LANES, kvw), lambda b, s, pt: (b, 0, 0))]
                 + [page_spec(p) for p in range(PAGES_PER_STEP)] * 2,
        out_specs=pl.BlockSpec((None, TS, N_HEADS * HEAD_DIM), lambda b, s, pt: (b, 0, 0)),
        scratch_shapes=[pltpu.VMEM((KV_HEADS, GROUP * TS, 1), F32),
                        pltpu.VMEM((KV_HEADS, GROUP * TS, 1), F32),
                        pltpu.VMEM((KV_HEADS, GROUP * TS, HEAD_DIM), F32)])
    return pl.pallas_call(
        functools.partial(_a2s_kernel, past=n_pages * page),
        grid_spec=grid_spec,
        out_shape=jax.ShapeDtypeStruct((DB, TS, N_HEADS * HEAD_DIM), BF16),
        compiler_params=_params(("parallel", "arbitrary")),
        name="a2_sample",
    )(page_table, proj_s, mask_s, tzs, knew, vnew,
      *([cache_k] * PAGES_PER_STEP), *([cache_v] * PAGES_PER_STEP))


BQ = 512


def _bp_kernel(*refs):
    n_q = len(B_GROUPS) * GROUP
    q_refs = refs[:n_q]
    k_ref, v_ref, band_ref, o_ref, m_ref, l_ref, acc_ref = refs[n_q:]
    t0 = pl.program_id(2) * BQ
    first = t0 == 0
    scale = HEAD_DIM ** -0.5
    m_ref[...] = jnp.full(m_ref.shape, NEG, F32)
    l_ref[...] = jnp.zeros(l_ref.shape, F32)
    acc_ref[...] = jnp.zeros(acc_ref.shape, F32)

    def stacked(ref, rows):
        return jnp.concatenate([ref[r, rows, :] for r in range(GROUP)], axis=0)

    def update(q_ref, rows_q, nrow, key_rows, biases):
        qs = jnp.concatenate([qr[rows_q, :] for qr in q_ref], axis=0).astype(BF16)
        kt = jnp.concatenate([k_ref[kr, :] for kr in key_rows], axis=0).astype(BF16)
        vt = jnp.concatenate([v_ref[kr, :] for kr in key_rows], axis=0).astype(BF16)
        s = _nt(qs, kt) * scale + jnp.concatenate(biases, axis=1)
        m_old = stacked(m_ref, rows_q)
        m_new = jnp.maximum(m_old, jnp.max(s, axis=1, keepdims=True))
        alpha = jnp.exp(m_old - m_new)
        p = jnp.exp(s - jnp.concatenate([m_new] * len(key_rows), axis=1))
        l_new = alpha * stacked(l_ref, rows_q) + jnp.sum(p, axis=1, keepdims=True)
        acc = alpha * stacked(acc_ref, rows_q) + jnp.dot(p.astype(BF16), vt, preferred_element_type=F32)
        for r in range(GROUP):
            m_ref[r, rows_q, :] = m_new[r * nrow:(r + 1) * nrow]
            l_ref[r, rows_q, :] = l_new[r * nrow:(r + 1) * nrow]
            acc_ref[r, rows_q, :] = acc[r * nrow:(r + 1) * nrow]

    def band(n, half, rows=slice(None)):
        return jnp.concatenate([band_ref[r, n, half, rows, :] for r in range(GROUP)], axis=0)

    for n, (window, dil) in enumerate(B_GROUPS):
        assert window // dil == QB
        q_ref = q_refs[n * GROUP:(n + 1) * GROUP]
        span = dil * QB
        if span <= BQ:
            for rho in range(dil):
                for mt in range(BQ // span):
                    def rows(start):
                        return pl.ds(start, QB, stride=dil) if dil > 1 else pl.ds(start, QB)
                    off = rho + span * mt
                    prev_invalid = first if mt == 0 else False
                    prev = jnp.maximum(t0 + off - span, rho) if mt == 0 else t0 + off - span
                    bias_a = band(n, 0)
                    if mt == 0:
                        bias_a = bias_a + jnp.where(prev_invalid, NEG, 0.0)
                    update(q_ref, rows(off), QB, [rows(prev), rows(t0 + off)], [bias_a, band(n, 1)])
        else:
            nrow = BQ // dil
            assert span == k_ref.shape[0]
            brow = pl.ds(pl.multiple_of(pl.program_id(2) * nrow, nrow), nrow)
            for rho in range(dil):
                update(q_ref, pl.ds(rho, nrow, stride=dil), nrow, [pl.ds(rho, QB, stride=dil)],
                       [band(n, 1, brow)])

    for r in range(GROUP):
        o_ref[:, r * HEAD_DIM:(r + 1) * HEAD_DIM] = (acc_ref[r] / l_ref[r]).astype(o_ref.dtype)


def _b_prompt(qb, kv, band):
    B, T, _ = qb.shape
    nh = N_HEADS
    hd = HEAD_DIM
    gw = GROUP * hd
    assert T % BQ == 0

    def q_spec(n, r):
        return pl.BlockSpec((None, BQ, hd), lambda b, g, c: (b, c, (n * KV_HEADS + g) * GROUP + r))

    q_specs = [q_spec(n, r) for n in range(len(B_GROUPS)) for r in range(GROUP)]
    return pl.pallas_call(
        _bp_kernel,
        grid=(B, KV_HEADS, T // BQ),
        in_specs=q_specs + [
                  pl.BlockSpec((None, T, hd), lambda b, g, c: (b, 0, g)),
                  pl.BlockSpec((None, T, hd), lambda b, g, c: (b, 0, KV_HEADS + g)),
                  pl.BlockSpec((GROUP, 3, 2, QB, LANES), lambda b, g, c: (g, 0, 0, 0, 0))],
        out_specs=pl.BlockSpec((None, BQ, gw), lambda b, g, c: (b, c, g)),
        out_shape=jax.ShapeDtypeStruct((B, T, nh * hd), BF16),
        scratch_shapes=[pltpu.VMEM((GROUP, BQ, LANES), F32), pltpu.VMEM((GROUP, BQ, LANES), F32),
                        pltpu.VMEM((GROUP, BQ, hd), F32)],
        compiler_params=_params(("parallel", "parallel", "arbitrary")),
        name="b_prompt",
    )(*([qb] * len(q_specs)), kv, kv, band)


def _bs_kernel(q0_ref, q1_ref, q2_ref, kc_ref, vc_ref, kn_ref, vn_ref, tab_ref, o_ref):
    TS = q0_ref.shape[0]
    W = kc_ref.shape[0] // KV_HEADS
    scale = HEAD_DIM ** -0.5
    rows = pl.ds(pl.program_id(1), W, stride=KV_HEADS)
    kc = kc_ref[rows, :].astype(BF16)
    vc = vc_ref[rows, :].astype(BF16)
    kn = kn_ref[...].astype(BF16)
    vn = vn_ref[...].astype(BF16)
    logits = []
    for n in range(len(B_GROUPS)):
        q_ref = (q0_ref, q1_ref, q2_ref)[n]
        q = jnp.concatenate([q_ref[:, r * HEAD_DIM:(r + 1) * HEAD_DIM] for r in range(GROUP)],
                            axis=0).astype(BF16)
        logits.append((_nt(q, kc) * scale + tab_ref[n, :, :W],
                       _nt(q, kn) * scale + tab_ref[n, :, W:]))
    m = functools.reduce(jnp.maximum,
                         [jnp.max(x, axis=1, keepdims=True) for pair in logits for x in pair])
    l = jnp.zeros((GROUP * TS, 1), F32)
    acc = jnp.zeros((GROUP * TS, HEAD_DIM), F32)
    for sc_c, sc_n in logits:
        pc = jnp.exp(sc_c - m)
        pn = jnp.exp(sc_n - m)
        l = l + jnp.sum(pc, axis=1, keepdims=True) + jnp.sum(pn, axis=1, keepdims=True)
        acc = (acc + jnp.dot(pc.astype(BF16), vc, preferred_element_type=F32)
               + jnp.dot(pn.astype(BF16), vn, preferred_element_type=F32))
    res = acc / l
    for r in range(GROUP):
        o_ref[:, r * HEAD_DIM:(r + 1) * HEAD_DIM] = res[r * TS:(r + 1) * TS, :].astype(o_ref.dtype)


def _b_sample(qb_s, cache_k, cache_v, kv_new, tab):
    DB, TS, NQ = qb_s.shape
    W = cache_k.shape[1] // KV_HEADS
    hd = HEAD_DIM
    return pl.pallas_call(
        _bs_kernel,
        grid=(DB, KV_HEADS),
        in_specs=[pl.BlockSpec((None, TS, GROUP * hd), lambda b, g: (b, 0, g)),
                  pl.BlockSpec((None, TS, GROUP * hd), lambda b, g: (b, 0, KV_HEADS + g)),
                  pl.BlockSpec((None, TS, GROUP * hd), lambda b, g: (b, 0, 2 * KV_HEADS + g)),
                  pl.BlockSpec((None, W * KV_HEADS, hd), lambda b, g: (b, 0, 0)),
                  pl.BlockSpec((None, W * KV_HEADS, hd), lambda b, g: (b, 0, 0)),
                  pl.BlockSpec((None, LANES, hd), lambda b, g: (b, 0, g)),
                  pl.BlockSpec((None, LANES, hd), lambda b, g: (b, 0, KV_HEADS + g)),
                  pl.BlockSpec((len(B_GROUPS), None, GROUP * TS, W + LANES), lambda b, g: (0, g, 0, 0))],
        out_specs=pl.BlockSpec((None, TS, GROUP * hd), lambda b, g: (b, 0, g)),
        out_shape=jax.ShapeDtypeStruct((DB, TS, N_HEADS * hd), BF16),
        compiler_params=_params(("parallel", "arbitrary")),
        name="b_sample",
    )(qb_s, qb_s, qb_s, cache_k, cache_v, kv_new, kv_new, tab)


FFN_TF = 512
HALO = 16


def _ffn_up_kernel(xm_ref, xh_ref, wg_ref, wu_ref, cw_ref, cb_ref, act_ref, st_ref, xe_ref, *,
                   tiles_per_batch):
    i = pl.program_id(0)
    tm = xm_ref.shape[0]

    @pl.when(pl.program_id(1) == 0)
    def _():
        first = (i % tiles_per_batch) == 0
        xh = xh_ref[...]
        xe_ref[0:HALO, :] = jnp.where(first, jnp.zeros_like(xh), xh)
        xe_ref[HALO:, :] = xm_ref[...]

    gate = jnp.dot(xe_ref[...], wg_ref[...], preferred_element_type=F32)
    up = jnp.dot(xm_ref[...], wu_ref[...], preferred_element_type=F32)
    cw = cw_ref[...]
    conv = cb_ref[...] + ((gate[HALO - 2:HALO - 2 + tm] * cw[0:1] + gate[HALO - 1:HALO - 1 + tm] * cw[1:2])
                          + gate[HALO:] * cw[2:3])
    act_ref[...] = (jax.nn.silu(conv) * up).astype(act_ref.dtype)
    st_ref[...] = gate[HALO + tm - (CONV_WIDTH - 1):, :]


def _ffn_up(x_bf, w_g_bf, w_u_bf, conv_w, conv_b, layer, T):
    M, D = x_bf.shape
    F = conv_w.shape[2]
    tm = 1024
    assert T % tm == 0 and M % T == 0 and F % FFN_TF == 0 and tm % HALO == 0
    nf = F // FFN_TF
    hb = tm // HALO
    return pl.pallas_call(
        functools.partial(_ffn_up_kernel, tiles_per_batch=T // tm),
        grid=(M // tm, nf),
        in_specs=[pl.BlockSpec((tm, D), lambda i, j: (i, 0)),
                  pl.BlockSpec((HALO, D), lambda i, j: (jnp.maximum(i * hb - 1, 0), 0)),
                  pl.BlockSpec((None, D, FFN_TF), lambda i, j: (layer, 0, j)),
                  pl.BlockSpec((None, D, FFN_TF), lambda i, j: (layer, 0, j)),
                  pl.BlockSpec((None, CONV_WIDTH, FFN_TF), lambda i, j: (layer, 0, j)),
                  pl.BlockSpec((None, 1, FFN_TF), lambda i, j: (layer, 0, j))],
        out_specs=[pl.BlockSpec((tm, FFN_TF), lambda i, j: (i, j)),
                   pl.BlockSpec((None, CONV_WIDTH - 1, FFN_TF), lambda i, j: (i, 0, j))],
        out_shape=[jax.ShapeDtypeStruct((M, F), BF16),
                   jax.ShapeDtypeStruct((M // tm, CONV_WIDTH - 1, F), F32)],
        scratch_shapes=[pltpu.VMEM((HALO + tm, D), BF16)],
        compiler_params=_params(("parallel", "arbitrary")),
        name="ffn_up",
    )(x_bf, x_bf, w_g_bf, w_u_bf, conv_w, conv_b)


def _mm_ln_kernel(a_ref, w_ref, x_ref, g_ref, b_ref, h_ref, hb_ref, acc_ref, *, alpha):
    k = pl.program_id(1)

    @pl.when(k == 0)
    def _():
        acc_ref[...] = jnp.zeros_like(acc_ref)

    acc_ref[...] += jnp.dot(a_ref[...], w_ref[...], preferred_element_type=F32)

    @pl.when(k == pl.num_programs(1) - 1)
    def _():
        y = alpha * x_ref[...] + acc_ref[...]
        mu = jnp.mean(y, axis=-1, keepdims=True)
        d = y - mu
        var = jnp.mean(d * d, axis=-1, keepdims=True)
        h = d * lax.rsqrt(var + LN_EPS) * g_ref[...] + b_ref[...]
        h_ref[...] = h
        hb_ref[...] = h.astype(hb_ref.dtype)


def _mm_ln(a_bf, w_bf, layer, x, g, b, alpha):
    M, K = a_bf.shape
    D = w_bf.shape[2]
    tm = min(512, M)
    tk = K // 4 if K > 2048 else K // 2
    assert M % tm == 0 and K % tk == 0 and tk % LANES == 0
    return pl.pallas_call(
        functools.partial(_mm_ln_kernel, alpha=alpha),
        grid=(M // tm, K // tk),
        in_specs=[pl.BlockSpec((tm, tk), lambda i, k: (i, k)),
                  pl.BlockSpec((None, tk, D), lambda i, k: (layer, k, 0)),
                  pl.BlockSpec((tm, D), lambda i, k: (i, 0)),
                  pl.BlockSpec((1, D), lambda i, k: (0, 0)),
                  pl.BlockSpec((1, D), lambda i, k: (0, 0))],
        out_specs=[pl.BlockSpec((tm, D), lambda i, k: (i, 0)),
                   pl.BlockSpec((tm, D), lambda i, k: (i, 0))],
        out_shape=[jax.ShapeDtypeStruct((M, D), F32), jax.ShapeDtypeStruct((M, D), BF16)],
        scratch_shapes=[pltpu.VMEM((tm, D), F32)],
        compiler_params=_params(("parallel", "arbitrary")),
        name="mm_ln",
    )(a_bf, w_bf, x, g.reshape(1, D), b.reshape(1, D))


def _layer_norm(x, g, b):
    mu = x.mean(-1, keepdims=True)
    var = jnp.square(x - mu).mean(-1, keepdims=True)
    return (x - mu) * lax.rsqrt(var + LN_EPS) * g + b


def _pick_tm(M):
    for tm in (1024, 512, 256, 128, 64, 32, 16):
        if M % tm == 0:
            return tm
    raise ValueError(M)


def _pick_tn(N):
    for tn in (512, 384, 256, 128):
        if N % tn == 0:
            return tn
    raise ValueError(N)


def _proj(x_bf, w_bf, layer=0):
    return _mm(x_bf, w_bf, layer, _pick_tm(x_bf.shape[0]), _pick_tn(w_bf.shape[2]))


def _ffn_act_sample(hs_bf, state, w_g_bf, w_u_bf, conv_w, conv_b, layer, DB, TS):
    conv_w, conv_b = conv_w[layer], conv_b[layer]
    F = conv_w.shape[1]
    gate = _proj(hs_bf, w_g_bf, layer).reshape(DB, TS, F)
    up = _proj(hs_bf, w_u_bf, layer).reshape(DB, TS, F)
    ext = jnp.concatenate([state, gate], axis=1)
    conv = conv_b + sum(ext[:, j:j + TS] * conv_w[j] for j in range(CONV_WIDTH))
    act = (jax.nn.silu(conv) * up).astype(BF16).reshape(DB * TS, F)
    return act, ext[:, ext.shape[1] - (CONV_WIDTH - 1):]


def _table_indices(T, W, TS):
    U = SAT_DIST + TZ_OFF + QB
    assert T <= T5_MAX_DISTANCE
    u = np.arange(U)[:, None]
    c = np.arange(LANES)[None, :]
    tz_idx = _t5_bucket(jnp.asarray(np.maximum(u - c - TZ_OFF, 0), I32))
    a = np.arange(QB)[:, None]
    band = []
    for window, dil in B_GROUPS:
        for off in (QB, 0):
            d = a + off - c
            ok = (d >= 0) & (d <= window // dil)
            band.append(jnp.where(jnp.asarray(ok), _t5_bucket(jnp.asarray(np.maximum(d, 0) * dil, I32)),
                                  NUM_BUCKETS))
    band_idx = jnp.concatenate(band, axis=0)
    nkt = W // LANES + 1
    t = np.arange(TS)[None, :, None]
    key = (np.arange(nkt)[:, None, None] * LANES + np.arange(LANES)[None, None, :])
    d = W + t - key
    tabs = []
    for window, dil in B_GROUPS:
        ok = (d >= 0) & (d % dil == 0) & (d <= window)
        tabs.append(jnp.where(jnp.asarray(ok), _t5_bucket(jnp.asarray(np.maximum(d, 0), I32)), NUM_BUCKETS))
    samp_idx = jnp.concatenate(tabs, axis=0).reshape(-1, LANES)
    rows = [tz_idx, band_idx, samp_idx]
    total = sum(r.shape[0] for r in rows)
    pad = (-total) % 128
    if pad:
        rows.append(jnp.full((pad, LANES), NUM_BUCKETS, I32))
    return jnp.concatenate(rows, axis=0), U, nkt


def kernel(x_prompt, x_sample, cache_k_a, cache_v_a, cache_kidx_a, cache_k_b, cache_v_b, state_ffn, page_table, a_w_in, a_w_o, a_kn_g, a_kn_b, b_w_kv, b_w_q, b_w_o, ffn_w_up, ffn_conv_w, ffn_conv_b, ffn_w_down, ln_g, ln_b, rel_bias):
    B, T, D = x_prompt.shape
    DB, TS, _ = x_sample.shape
    depth = ffn_w_up.shape[0]
    n_a = a_w_in.shape[0]
    d_ff = ffn_w_down.shape[1]
    W = cache_k_b.shape[1]
    n_pages = page_table.shape[1]
    page = cache_k_a.shape[2]
    past = n_pages * page
    alpha = (2 * depth) ** 0.25
    kvw = KV_HEADS * HEAD_DIM
    a_q = N_HEADS * HEAD_DIM
    a_in = a_w_in.shape[2]
    np_cols = ((a_in + 511) // 512) * 512
    ki0 = a_q + 2 * kvw + IDX_HEADS * IDX_DIM

    idx_all, U, nkt = _table_indices(T, W, TS)
    tabs = _bias_tables(rel_bias, idx_all)
    tz = tabs[:, :U]
    band = tabs[:, U:U + 6 * QB].reshape(N_HEADS, 3, 2, QB, LANES)
    samp = tabs[:, U + 6 * QB:U + 6 * QB + 3 * nkt * TS].reshape(KV_HEADS, GROUP, 3, nkt, TS, LANES)
    samp = samp.transpose(2, 0, 1, 4, 3, 5).reshape(3, KV_HEADS, GROUP * TS, nkt * LANES)
    n_slab = SAT_DIST // LANES + 1
    tzs = tz[:, TZ_OFF:TZ_OFF + n_slab * LANES].reshape(N_HEADS, n_slab, LANES, LANES)[:, :, :TS, :]

    cache_k_a2 = cache_k_a.reshape(n_a, -1, page * KV_HEADS, HEAD_DIM)
    cache_v_a2 = cache_v_a.reshape(n_a, -1, page * KV_HEADS, HEAD_DIM)
    cache_k_b2 = cache_k_b.reshape(DB, W * KV_HEADS, HEAD_DIM)
    cache_v_b2 = cache_v_b.reshape(DB, W * KV_HEADS, HEAD_DIM)
    cache_kidx_t = jnp.swapaxes(cache_kidx_a, 2, 3)

    def pad_rows(x, n):
        return jnp.pad(x, ((0, 0), (0, n - x.shape[1]), (0, 0)))

    f_pad = ((d_ff + FFN_TF - 1) // FFN_TF) * FFN_TF
    pad_f = f_pad - d_ff
    w_g_all = jnp.pad(ffn_w_up[:, :, :d_ff], ((0, 0), (0, 0), (0, pad_f))).astype(BF16)
    w_u_all = jnp.pad(ffn_w_up[:, :, d_ff:], ((0, 0), (0, 0), (0, pad_f))).astype(BF16)
    w_down_all = jnp.pad(ffn_w_down, ((0, 0), (0, pad_f), (0, 0))).astype(BF16)
    conv_w_all = jnp.pad(ffn_conv_w, ((0, 0), (0, 0), (0, pad_f)))
    conv_b_all = jnp.pad(ffn_conv_b, ((0, 0), (0, pad_f))).reshape(depth, 1, f_pad)
    w_in_all = jnp.pad(a_w_in, ((0, 0), (0, 0), (0, np_cols - a_in))).astype(BF16)
    w_o_a, w_o_b = a_w_o.astype(BF16), b_w_o.astype(BF16)
    w_q_all = b_w_q.astype(BF16)
    w_kv_bf = b_w_kv.astype(BF16)[None]

    hp = x_prompt.reshape(B * T, D)
    hs = x_sample.reshape(DB * TS, D)
    hp_bf, hs_bf = hp.astype(BF16), hs.astype(BF16)
    ka_p, va_p, kia_p, ka_s, va_s, kia_s, ffn_p, ffn_s = [], [], [], [], [], [], [], []
    for layer in range(depth):
        if layer < n_a:
            a = layer
            w_o_bf, w_o_layer = w_o_a, a
            proj = _proj(hp_bf, w_in_all, a).reshape(B, T, np_cols)
            k = proj[..., a_q:a_q + kvw].reshape(B, T, KV_HEADS, HEAD_DIM)
            v = proj[..., a_q + kvw:a_q + 2 * kvw].reshape(B, T, KV_HEADS, HEAD_DIM)
            ki = _layer_norm(proj[..., ki0:ki0 + IDX_DIM], a_kn_g[a], a_kn_b[a])
            ka_p.append(k); va_p.append(v); kia_p.append(ki)
            mask = _a1_prompt(ki.astype(BF16), proj, min(TOPK_MAX, T // 4))
            o = _a2_prompt(proj, mask, tz).reshape(B * T, a_q)
            proj_s = _proj(hs_bf, w_in_all, a).reshape(DB, TS, np_cols)
            k_s = proj_s[..., a_q:a_q + kvw]
            v_s = proj_s[..., a_q + kvw:a_q + 2 * kvw]
            ki_s = _layer_norm(proj_s[..., ki0:ki0 + IDX_DIM], a_kn_g[a], a_kn_b[a])
            ka_s.append(k_s.reshape(DB, TS, KV_HEADS, HEAD_DIM))
            va_s.append(v_s.reshape(DB, TS, KV_HEADS, HEAD_DIM))
            kia_s.append(ki_s)
            mask_s = _a1_sample(page_table, proj_s, pad_rows(ki_s, LANES).astype(BF16), cache_kidx_t, a,
                                min(TOPK_MAX, (past + TS) // 4))
            o_s = _a2_sample(page_table, proj_s, mask_s, tzs, pad_rows(k_s, LANES), pad_rows(v_s, LANES),
                             cache_k_a2, cache_v_a2, a).reshape(DB * TS, a_q)
        else:
            if layer == n_a:
                kv_p = _proj(hp_bf, w_kv_bf).reshape(B, T, 2 * kvw)
                kv_s = _proj(hs_bf, w_kv_bf).reshape(DB, TS, 2 * kvw)
                kv_s_pad = pad_rows(kv_s, LANES)
            bl = layer - n_a
            w_o_bf, w_o_layer = w_o_b, bl
            o = _b_prompt(_proj(hp_bf, w_q_all, bl).reshape(B, T, -1), kv_p, band).reshape(B * T, a_q)
            o_s = _b_sample(_proj(hs_bf, w_q_all, bl).reshape(DB, TS, -1), cache_k_b2, cache_v_b2, kv_s_pad,
                            samp).reshape(DB * TS, a_q)
        hp, hp_bf = _mm_ln(o, w_o_bf, w_o_layer, hp, ln_g[layer, 0], ln_b[layer, 0], alpha)
        hs, hs_bf = _mm_ln(o_s, w_o_bf, w_o_layer, hs, ln_g[layer, 0], ln_b[layer, 0], alpha)
        act, st = _ffn_up(hp_bf, w_g_all, w_u_all, conv_w_all, conv_b_all, layer, T)
        ffn_p.append(st.reshape(B, -1, CONV_WIDTH - 1, f_pad)[:, -1, :, :d_ff])
        act_s, st_s = _ffn_act_sample(hs_bf, jnp.pad(state_ffn[layer], ((0, 0), (0, 0), (0, pad_f))),
                                      w_g_all, w_u_all, conv_w_all, conv_b_all, layer, DB, TS)
        ffn_s.append(st_s[..., :d_ff])
        hp, hp_bf = _mm_ln(act, w_down_all, layer, hp, ln_g[layer, 1], ln_b[layer, 1], alpha)
        hs, hs_bf = _mm_ln(act_s, w_down_all, layer, hs, ln_g[layer, 1], ln_b[layer, 1], alpha)
    keep = min(max(w for w, _ in B_GROUPS), T)
    kb_p = kv_p[..., :kvw].reshape(B, T, KV_HEADS, HEAD_DIM)
    vb_p = kv_p[..., kvw:].reshape(B, T, KV_HEADS, HEAD_DIM)
    kb_s = kv_s[..., :kvw].reshape(DB, TS, KV_HEADS, HEAD_DIM)
    vb_s = kv_s[..., kvw:].reshape(DB, TS, KV_HEADS, HEAD_DIM)
    return (hp.reshape(B, T, D), hs.reshape(DB, TS, D), jnp.stack(ka_p), jnp.stack(va_p), jnp.stack(kia_p),
            jnp.stack(ka_s), jnp.stack(va_s), jnp.stack(kia_s), kb_p[:, T - keep:], vb_p[:, T - keep:],
            kb_s, vb_s, jnp.stack(ffn_p), jnp.stack(ffn_s))
```

```python
import functools
import math

import numpy as np
import jax
import jax.numpy as jnp
from jax import lax
from jax.experimental import pallas as pl
from jax.experimental.pallas import tpu as pltpu

F32 = jnp.float32
BF16 = jnp.bfloat16
I32 = jnp.int32

HEAD_DIM = 128
N_HEADS = 16
KV_HEADS = 4
GROUP = N_HEADS // KV_HEADS
IDX_HEADS = 16
IDX_DIM = 64
TOPK_MAX = 256
B_GROUPS = ((128, 1), (512, 4), (2048, 16))
NUM_BUCKETS = 32
T5_MAX_DISTANCE = 2048
LN_EPS = 1e-5
NEG = -1e30
CONV_WIDTH = 3

LANES = 128
QB = 128
KC = 512
TZ_OFF = KC - QB
SAT_DIST = T5_MAX_DISTANCE + QB
INT_MIN = -2 ** 31
KEY_NEG_INF = INT_MIN + 0x7FFFFF
VMEM_LIMIT = 56 * 1024 * 1024

LOG2E = 1.4426950408889634
QSCALE = HEAD_DIM ** -0.5 * LOG2E

NT_DIMS = (((1,), (1,)), ((), ()))


def _nt(a, b):
    return lax.dot_general(a, b, NT_DIMS, preferred_element_type=F32)


def _params(sem):
    return pltpu.CompilerParams(dimension_semantics=sem, vmem_limit_bytes=VMEM_LIMIT)


def _mm_kernel(x_ref, w_ref, o_ref):
    o_ref[...] = jnp.dot(x_ref[...], w_ref[...], preferred_element_type=F32).astype(o_ref.dtype)


def _mm(x, w, layer, tm, tn, out_dtype=F32):
    M, K = x.shape
    N = w.shape[2]
    assert M % tm == 0 and N % tn == 0, (M, N, tm, tn)
    return pl.pallas_call(
        _mm_kernel,
        grid=(M // tm, N // tn),
        in_specs=[pl.BlockSpec((tm, K), lambda i, j: (i, 0)),
                  pl.BlockSpec((None, K, tn), lambda i, j: (layer, 0, j))],
        out_specs=pl.BlockSpec((tm, tn), lambda i, j: (i, j)),
        out_shape=jax.ShapeDtypeStruct((M, N), out_dtype),
        compiler_params=_params(("parallel", "arbitrary")),
        name="mm",
    )(x, w)


def _t5_bucket(dist):
    dist = jnp.maximum(dist, 0)
    exact = NUM_BUCKETS // 2
    far = exact + (jnp.log(jnp.maximum(dist, 1).astype(F32) / exact)
                   / math.log(T5_MAX_DISTANCE / exact) * (NUM_BUCKETS - exact)).astype(I32)
    return jnp.where(dist < exact, dist, jnp.minimum(far, NUM_BUCKETS - 1))


def _table_kernel(rb_ref, idx_ref, o_ref):
    slab = 16

    def body(i, carry):
        r0 = pl.multiple_of(i * slab, slab)
        idx = idx_ref[pl.ds(r0, slab), :]
        accs = [jnp.full(idx.shape, NEG, F32)] * N_HEADS
        for k in range(NUM_BUCKETS):
            hit = idx == k
            accs = [jnp.where(hit, rb_ref[k, h], accs[h]) for h in range(N_HEADS)]
        for h in range(N_HEADS):
            o_ref[h, pl.ds(r0, slab), :] = accs[h]
        return carry

    lax.fori_loop(0, idx_ref.shape[0] // slab, body, 0)


def _bias_tables(rel_bias, idx):
    R = idx.shape[0]
    tr = 128
    assert R % tr == 0
    return pl.pallas_call(
        _table_kernel,
        grid=(R // tr,),
        in_specs=[pl.BlockSpec(memory_space=pltpu.SMEM),
                  pl.BlockSpec((tr, LANES), lambda i: (i, 0))],
        out_specs=pl.BlockSpec((N_HEADS, tr, LANES), lambda i: (0, i, 0)),
        out_shape=jax.ShapeDtypeStruct((N_HEADS, R, LANES), F32),
        compiler_params=_params(("arbitrary",)),
        name="bias_tables",
    )(rel_bias, idx)


def _sortable(x):
    bits = lax.bitcast_convert_type(x, I32)
    return bits ^ ((bits >> 31) & jnp.int32(0x7FFFFFFF))


def _kth_largest(count_ge, shape, k):
    def body(t, ans):
        cand_u = ans | (jnp.int32(1) << (31 - t))
        cnt = count_ge(cand_u ^ jnp.int32(INT_MIN))
        return jnp.where(cnt >= k, cand_u, ans)
    ans = lax.fori_loop(0, 32, body, jnp.zeros(shape, I32))
    return ans ^ jnp.int32(INT_MIN)


def _tie_bound(count_eq_below, shape, need, nbits):
    def body(t, ans):
        cand = ans | (jnp.int32(1) << (nbits - 1 - t))
        return jnp.where(count_eq_below(cand) < need, cand, ans)
    return lax.fori_loop(0, nbits, body, jnp.zeros(shape, I32))


def _a1p_kernel(kin_ref, qi_ref, tail_ref, mask_ref, st_ref, jb_ref, *, topk):
    i = pl.program_id(1)
    T = kin_ref.shape[0]
    nchunk = T // KC
    nbits = (T - 1).bit_length()
    tail_t = tail_ref[...].T
    wi_t = tail_t[IDX_DIM:IDX_DIM + IDX_HEADS, :] * (IDX_HEADS ** -0.5 * IDX_DIM ** -0.5)
    qpos = i * QB + lax.broadcasted_iota(I32, (KC, QB), 1)
    row = lax.broadcasted_iota(I32, (KC, QB), 0)

    nj = (i * QB) // KC + 1

    for c in range(nchunk):
        @pl.when(c < nj)
        def _():
            kc = kin_ref[c * KC:(c + 1) * KC, :]
            acc = jnp.zeros((KC, QB), F32)
            for h in range(IDX_HEADS):
                qh = qi_ref[:, h * IDX_DIM:(h + 1) * IDX_DIM].astype(BF16)
                acc = acc + jnp.maximum(_nt(kc, qh), 0.0) * wi_t[h:h + 1, :]
            acc = jnp.where(c * KC + row <= qpos, acc, -jnp.inf)
            st_ref[c * KC:(c + 1) * KC, :] = _sortable(acc)

    def select(nvis):
        def count(pred):
            cnt = jnp.zeros((1, QB), F32)
            for c in range(nvis):
                blk = st_ref[c * KC:(c + 1) * KC, :]
                cnt = cnt + jnp.sum(jnp.where(pred(blk, c * KC + row), 1.0, 0.0), axis=0, keepdims=True)
            return cnt

        thr = _kth_largest(lambda cand: count(lambda blk, _: blk >= cand), (1, QB), float(topk))
        need = float(topk) - count(lambda blk, _: blk > thr)
        n_eq = count(lambda blk, _: blk == thr)
        excess = jnp.where((n_eq > need) & (thr != KEY_NEG_INF), 1.0, 0.0)
        jb_ref[...] = jnp.full((1, QB), T, I32)

        @pl.when(jnp.max(excess) > 0.0)
        def _():
            jb_ref[...] = _tie_bound(
                lambda cand: count(lambda blk, kpos: jnp.where(blk == thr, kpos, T) < cand),
                (1, QB), need, nbits)

        jb = jb_ref[...]
        for c in range(nchunk):
            if c < nvis:
                blk = st_ref[c * KC:(c + 1) * KC, :]
                kpos = c * KC + row
                rank_pos = jnp.where(blk == thr, kpos, jnp.where(blk > thr, -1, T + 1))
                sel = jnp.where(kpos <= qpos, rank_pos, T + 1) <= jb
                m_t = jnp.where(sel, 0.0, NEG)
                for s4 in range(KC // QB):
                    mask_ref[c, :, s4 * QB:(s4 + 1) * QB] = (
                        m_t[s4 * QB:(s4 + 1) * QB, :].T.astype(mask_ref.dtype))
            else:
                mask_ref[c] = jnp.full((QB, KC), NEG, mask_ref.dtype)

    for nvis in range(1, nchunk + 1):
        pl.when(nj == nvis)(functools.partial(select, nvis))


def _a1_prompt(kin_bf, proj, topk):
    B, T, _ = proj.shape
    nq, nc = T // QB, T // KC
    assert topk <= KC
    qi_blk =(N_HEADS * HEAD_DIM + 2 * KV_HEADS * HEAD_DIM) // (IDX_HEADS * IDX_DIM)
    tail_blk = (N_HEADS * HEAD_DIM + 2 * KV_HEADS * HEAD_DIM + IDX_HEADS * IDX_DIM) // LANES
    return pl.pallas_call(
        functools.partial(_a1p_kernel, topk=topk),
        grid=(B, nq),
        in_specs=[pl.BlockSpec((None, T, IDX_DIM), lambda b, i: (b, 0, 0)),
                  pl.BlockSpec((None, QB, IDX_HEADS * IDX_DIM), lambda b, i: (b, i, qi_blk)),
                  pl.BlockSpec((None, QB, LANES), lambda b, i: (b, i, tail_blk))],
        out_specs=pl.BlockSpec((None, None, nc, QB, KC), lambda b, i: (b, i, 0, 0, 0)),
        out_shape=jax.ShapeDtypeStruct((B, nq, nc, QB, KC), BF16),
        scratch_shapes=[pltpu.VMEM((T, QB), I32), pltpu.VMEM((1, QB), I32)],
        compiler_params=_params(("parallel", "arbitrary")),
        name="a1_prompt",
    )(kin_bf, proj, proj)


def _a2p_kernel(q_ref, k_ref, v_ref, mask_ref, tz_ref, o_ref):
    i = pl.program_id(2)
    nj = (i * QB) // KC + 1
    HP = GROUP
    R = HP * QB
    heads = [list(range(c * HP, (c + 1) * HP)) for c in range(GROUP // HP)]
    qs = [(jnp.concatenate([q_ref[:, r * HEAD_DIM:(r + 1) * HEAD_DIM] for r in hs], axis=0)
           * QSCALE).astype(BF16) for hs in heads]

    def body(j, carry):
        k0 = pl.multiple_of(j * KC, KC)
        kt = k_ref[pl.ds(k0, KC), :].astype(BF16)
        vt = v_ref[pl.ds(k0, KC), :].astype(BF16)
        base = i * QB - j * KC + TZ_OFF
        mk = mask_ref[j].astype(F32)
        mk = jnp.concatenate([mk] * HP, axis=0)
        out = []
        for hs, q, (m, l, acc) in zip(heads, qs, carry):
            bias = jnp.concatenate(
                [jnp.concatenate(
                    [tz_ref[r, pl.ds(pl.multiple_of(base - QB * s4, QB), QB), :] for s4 in range(KC // QB)],
                    axis=1) for r in hs], axis=0)
            s = _nt(q, kt) + bias + mk
            m_new = jnp.maximum(m, jnp.max(s, axis=1, keepdims=True))
            alpha = jnp.exp2(m - m_new)
            p = jnp.exp2(s - m_new)
            l = alpha * l + jnp.sum(p, axis=1, keepdims=True)
            acc = alpha * acc + jnp.dot(p.astype(BF16), vt, preferred_element_type=F32)
            out.append((m_new, l, acc))
        return tuple(out)

    init = tuple((jnp.full((R, 1), NEG, F32), jnp.zeros((R, 1), F32), jnp.zeros((R, HEAD_DIM), F32))
                 for _ in heads)
    final = lax.fori_loop(0, nj, body, init)
    for hs, (m, l, acc) in zip(heads, final):
        res = acc / l
        for n, r in enumerate(hs):
            o_ref[:, r * HEAD_DIM:(r + 1) * HEAD_DIM] = res[n * QB:(n + 1) * QB, :].astype(o_ref.dtype)


def _a2_prompt(proj, mask, tz):
    B, T, _ = proj.shape
    nq, nc = T // QB, T // KC
    gw = GROUP * HEAD_DIM
    k_blk0 = N_HEADS * HEAD_DIM // HEAD_DIM
    v_blk0 = k_blk0 + KV_HEADS
    U = tz.shape[1]
    return pl.pallas_call(
        _a2p_kernel,
        grid=(KV_HEADS, B, nq),
        in_specs=[pl.BlockSpec((None, QB, gw), lambda g, b, i: (b, i, g)),
                  pl.BlockSpec((None, T, HEAD_DIM), lambda g, b, i: (b, 0, k_blk0 + g)),
                  pl.BlockSpec((None, T, HEAD_DIM), lambda g, b, i: (b, 0, v_blk0 + g)),
                  pl.BlockSpec((None, None, nc, QB, KC), lambda g, b, i: (b, i, 0, 0, 0)),
                  pl.BlockSpec((GROUP, U, LANES), lambda g, b, i: (g, 0, 0))],
        out_specs=pl.BlockSpec((None, QB, gw), lambda g, b, i: (b, i, g)),
        out_shape=jax.ShapeDtypeStruct((B, T, N_HEADS * HEAD_DIM), BF16),
        compiler_params=_params(("parallel", "parallel", "arbitrary")),
        name="a2_prompt",
    )(proj, proj, proj, mask, tz)


PAGES_PER_STEP = 32


def _a1s_kernel(pt_ref, qi_ref, tail_ref, knew_ref, *rest, topk, past):
    page_refs = rest[:PAGES_PER_STEP]
    mask_ref, st_ref, jb_ref = rest[PAGES_PER_STEP:]
    s = pl.program_id(1)
    nt, TS, _ = st_ref.shape
    L = nt * LANES
    nbits = (L - 1).bit_length()
    qi = jnp.concatenate([qi_ref[:, h * IDX_DIM:(h + 1) * IDX_DIM] for h in range(IDX_HEADS)],
                         axis=0).astype(BF16)
    w_col = jnp.concatenate([tail_ref[:, IDX_DIM + h:IDX_DIM + h + 1] for h in range(IDX_HEADS)],
                            axis=0) * (IDX_HEADS ** -0.5 * IDX_DIM ** -0.5)

    def scores(qk):
        sc = jnp.maximum(qk, 0.0) * w_col
        return jnp.sum(sc.reshape(IDX_HEADS, TS, LANES), axis=0)

    for p in range(PAGES_PER_STEP):
        qk = jnp.dot(qi, page_refs[p][...].astype(BF16), preferred_element_type=F32)
        st_ref[s * PAGES_PER_STEP + p] = _sortable(scores(qk))

    @pl.when(s == pl.num_programs(1) - 1)
    def _():
        t = lax.broadcasted_iota(I32, (TS, LANES), 0)
        c = lax.broadcasted_iota(I32, (TS, LANES), 1)
        sc = jnp.where(c <= t, scores(_nt(qi, knew_ref[...])), -jnp.inf)
        st_ref[nt - 1] = _sortable(sc)

        kpos = (lax.broadcasted_iota(I32, (nt, TS, LANES), 0) * LANES
                + lax.broadcasted_iota(I32, (nt, TS, LANES), 2))
        qpos = past + lax.broadcasted_iota(I32, (nt, TS, LANES), 1)

        def count(pred):
            per_lane = jnp.sum(jnp.where(pred(st_ref[...]), 1.0, 0.0), axis=0)
            return jnp.sum(per_lane, axis=1, keepdims=True)

        thr = _kth_largest(lambda cand: count(lambda k: k >= cand[None]), (TS, 1), float(topk))
        thr3 = thr[None]
        need = float(topk) - count(lambda k: k > thr3)
        n_eq = count(lambda k: k == thr3)
        excess = jnp.where((n_eq > need) & (thr != KEY_NEG_INF), 1.0, 0.0)
        jb_ref[...] = jnp.full((TS, 1), L, I32)

        @pl.when(jnp.max(excess) > 0.0)
        def _():
            jb_ref[...] = _tie_bound(
                lambda cand: count(lambda k: jnp.where(k == thr3, kpos, L) < cand[None]),
                (TS, 1), need, nbits)

        keys = st_ref[...]
        rank_pos = jnp.where(keys == thr3, kpos, jnp.where(keys > thr3, -1, L + 1))
        sel = jnp.where(kpos <= qpos, rank_pos, L + 1) <= jb_ref[...][None]
        mask_ref[...] = jnp.where(sel, 0.0, NEG)


def _a1_sample(page_table, proj_s, knew_bf, cache_kidx, layer, topk):
    DB, TS, _ = proj_s.shape
    n_pages = page_table.shape[1]
    page = cache_kidx.shape[3]
    assert page == LANES and n_pages % PAGES_PER_STEP == 0 and topk <= n_pages * page
    nt = n_pages + 1
    qi_blk = (N_HEADS * HEAD_DIM + 2 * KV_HEADS * HEAD_DIM) // (IDX_HEADS * IDX_DIM)
    tail_blk = (N_HEADS * HEAD_DIM + 2 * KV_HEADS * HEAD_DIM + IDX_HEADS * IDX_DIM) // LANES

    def page_spec(p):
        return pl.BlockSpec((None, None, IDX_DIM, page),
                            lambda b, s, pt: (layer, pt[b, s * PAGES_PER_STEP + p], 0, 0))

    grid_spec = pltpu.PrefetchScalarGridSpec(
        num_scalar_prefetch=1,
        grid=(DB, n_pages // PAGES_PER_STEP),
        in_specs=[pl.BlockSpec((None, TS, IDX_HEADS * IDX_DIM), lambda b, s, pt: (b, 0, qi_blk)),
                  pl.BlockSpec((None, TS, LANES), lambda b, s, pt: (b, 0, tail_blk)),
                  pl.BlockSpec((None, LANES, IDX_DIM), lambda b, s, pt: (b, 0, 0))]
                 + [page_spec(p) for p in range(PAGES_PER_STEP)],
        out_specs=pl.BlockSpec((None, nt, TS, LANES), lambda b, s, pt: (b, 0, 0, 0)),
        scratch_shapes=[pltpu.VMEM((nt, TS, LANES), I32), pltpu.VMEM((TS, 1), I32)])
    return pl.pallas_call(
        functools.partial(_a1s_kernel, topk=topk, past=n_pages * page),
        grid_spec=grid_spec,
        out_shape=jax.ShapeDtypeStruct((DB, nt, TS, LANES), F32),
        compiler_params=_params(("parallel", "arbitrary")),
        name="a1_sample",
    )(page_table, proj_s, proj_s, knew_bf, *([cache_kidx] * PAGES_PER_STEP))


def _a2s_kernel(pt_ref, q_ref, mask_ref, tzs_ref, knew_ref, vnew_ref, *rest, past):
    k_pages = rest[:PAGES_PER_STEP]
    v_pages = rest[PAGES_PER_STEP:2 * PAGES_PER_STEP]
    o_ref, m_ref, l_ref, acc_ref = rest[2 * PAGES_PER_STEP:]
    s = pl.program_id(1)
    TS = q_ref.shape[0]
    n_slab = tzs_ref.shape[1]

    @pl.when(s == 0)
    def _():
        m_ref[...] = jnp.full(m_ref.shape, NEG, F32)
        l_ref[...] = jnp.zeros(l_ref.shape, F32)
        acc_ref[...] = jnp.zeros(acc_ref.shape, F32)

    qg = [(jnp.concatenate([q_ref[:, (g * GROUP + r) * HEAD_DIM:(g * GROUP + r + 1) * HEAD_DIM]
                            for r in range(GROUP)], axis=0) * QSCALE).astype(BF16)
          for g in range(KV_HEADS)]

    def update(g, kt, vt, tiles):
        slabs = [jnp.minimum((past // LANES) - t, n_slab - 1) for t in tiles]
        mk = jnp.concatenate([mask_ref[t] for t in tiles], axis=1)
        mk = jnp.concatenate([mk] * GROUP, axis=0)
        bias = jnp.concatenate(
            [jnp.concatenate([tzs_ref[g * GROUP + r, sl] for r in range(GROUP)], axis=0) for sl in slabs],
            axis=1)
        sc = _nt(qg[g], kt) + bias + mk
        m_old = m_ref[g]
        m_new = jnp.maximum(m_old, jnp.max(sc, axis=1, keepdims=True))
        alpha = jnp.exp2(m_old - m_new)
        p = jnp.exp2(sc - m_new)
        l_ref[g] = alpha * l_ref[g] + jnp.sum(p, axis=1, keepdims=True)
        acc_ref[g] = alpha * acc_ref[g] + jnp.dot(p.astype(BF16), vt, preferred_element_type=F32)
        m_ref[g] = m_new

    tiles = [s * PAGES_PER_STEP + p for p in range(PAGES_PER_STEP)]
    for g in range(KV_HEADS):
        rows = pl.ds(g, LANES, stride=KV_HEADS)
        kt = jnp.concatenate([kp[rows, :] for kp in k_pages], axis=0).astype(BF16)
        vt = jnp.concatenate([vp[rows, :] for vp in v_pages], axis=0).astype(BF16)
        update(g, kt, vt, tiles)

    @pl.when(s == pl.num_programs(1) - 1)
    def _():
        for g in range(KV_HEADS):
            update(g, knew_ref[:, g * HEAD_DIM:(g + 1) * HEAD_DIM].astype(BF16),
                   vnew_ref[:, g * HEAD_DIM:(g + 1) * HEAD_DIM].astype(BF16), [past // LANES])
        for g in range(KV_HEADS):
            res = acc_ref[g] / l_ref[g]
            for r in range(GROUP):
                h = g * GROUP + r
                o_ref[:, h * HEAD_DIM:(h + 1) * HEAD_DIM] = res[r * TS:(r + 1) * TS, :].astype(o_ref.dtype)


def _a2_sample(page_table, proj_s, mask_s, tzs, knew, vnew, cache_k, cache_v, layer):
    DB, TS, _ = proj_s.shape
    n_pages = page_table.shape[1]
    page = cache_k.shape[2] // KV_HEADS
    kvw = KV_HEADS * HEAD_DIM
    nt = n_pages + 1

    def page_spec(p):
        return pl.BlockSpec((None, None, page * KV_HEADS, HEAD_DIM),
                            lambda b, s, pt: (layer, pt[b, s * PAGES_PER_STEP + p], 0, 0))

    grid_spec = pltpu.PrefetchScalarGridSpec(
        num_scalar_prefetch=1,
        grid=(DB, n_pages // PAGES_PER_STEP),
        in_specs=[pl.BlockSpec((None, TS, N_HEADS * HEAD_DIM), lambda b, s, pt: (b, 0, 0)),
                  pl.BlockSpec((None, nt, TS, LANES), lambda b, s, pt: (b, 0, 0, 0)),
                  pl.BlockSpec(tzs.shape, lambda b, s, pt: (0, 0, 0, 0)),
                  pl.BlockSpec((None, LANES, kvw), lambda b, s, pt: (b, 0, 0)),
                  pl.BlockSpec((None, LANES, kvw), lambda b, s, pt: (b, 0, 0))]
                 + [page_spec(p) for p in range(PAGES_PER_STEP)] * 2,
        out_specs=pl.BlockSpec((None, TS, N_HEADS * HEAD_DIM), lambda b, s, pt: (b, 0, 0)),
        scratch_shapes=[pltpu.VMEM((KV_HEADS, GROUP * TS, 1), F32),
                        pltpu.VMEM((KV_HEADS, GROUP * TS, 1), F32),
                        pltpu.VMEM((KV_HEADS, GROUP * TS, HEAD_DIM), F32)])
    return pl.pallas_call(
        functools.partial(_a2s_kernel, past=n_pages * page),
        grid_spec=grid_spec,
        out_shape=jax.ShapeDtypeStruct((DB, TS, N_HEADS * HEAD_DIM), BF16),
        compiler_params=_params(("parallel", "arbitrary")),
        name="a2_sample",
    )(page_table, proj_s, mask_s, tzs, knew, vnew,
      *([cache_k] * PAGES_PER_STEP), *([cache_v] * PAGES_PER_STEP))


BQ = 512


def _bp_kernel(*refs):
    n_q = len(B_GROUPS) * GROUP
    q_refs = refs[:n_q]
    k_ref, v_ref, band_ref, o_ref, m_ref, l_ref, acc_ref = refs[n_q:]
    t0 = pl.program_id(2) * BQ
    first = t0 == 0
    m_ref[...] = jnp.full(m_ref.shape, NEG, F32)
    l_ref[...] = jnp.zeros(l_ref.shape, F32)
    acc_ref[...] = jnp.zeros(acc_ref.shape, F32)

    def stacked(ref, rows):
        return jnp.concatenate([ref[r, rows, :] for r in range(GROUP)], axis=0)

    def update(q_ref, rows_q, nrow, key_rows, biases):
        qs = (jnp.concatenate([qr[rows_q, :] for qr in q_ref], axis=0) * QSCALE).astype(BF16)
        kt = jnp.concatenate([k_ref[kr, :] for kr in key_rows], axis=0).astype(BF16)
        vt = jnp.concatenate([v_ref[kr, :] for kr in key_rows], axis=0).astype(BF16)
        s = _nt(qs, kt) + jnp.concatenate(biases, axis=1)
        m_old = stacked(m_ref, rows_q)
        m_new = jnp.maximum(m_old, jnp.max(s, axis=1, keepdims=True))
        alpha = jnp.exp2(m_old - m_new)
        p = jnp.exp2(s - jnp.concatenate([m_new] * len(key_rows), axis=1))
        l_new = alpha * stacked(l_ref, rows_q) + jnp.sum(p, axis=1, keepdims=True)
        acc = alpha * stacked(acc_ref, rows_q) + jnp.dot(p.astype(BF16), vt, preferred_element_type=F32)
        for r in range(GROUP):
            m_ref[r, rows_q, :] = m_new[r * nrow:(r + 1) * nrow]
            l_ref[r, rows_q, :] = l_new[r * nrow:(r + 1) * nrow]
            acc_ref[r, rows_q, :] = acc[r * nrow:(r + 1) * nrow]

    def band(n, half, rows=slice(None)):
        return jnp.concatenate([band_ref[r, n, half, rows, :] for r in range(GROUP)], axis=0)

    for n, (window, dil) in enumerate(B_GROUPS):
        assert window // dil == QB
        q_ref = q_refs[n * GROUP:(n + 1) * GROUP]
        span = dil * QB
        if span <= BQ:
            for rho in range(dil):
                for mt in range(BQ // span):
                    def rows(start):
                        return pl.ds(start, QB, stride=dil) if dil > 1 else pl.ds(start, QB)
                    off = rho + span * mt
                    prev_invalid = first if mt == 0 else False
                    prev = jnp.maximum(t0 + off - span, rho) if mt == 0 else t0 + off - span
                    bias_a = band(n, 0)
                    if mt == 0:
                        bias_a = bias_a + jnp.where(prev_invalid, NEG, 0.0)
                    update(q_ref, rows(off), QB, [rows(prev), rows(t0 + off)], [bias_a, band(n, 1)])
        else:
            nrow = BQ // dil
            assert span == k_ref.shape[0]
            brow = pl.ds(pl.multiple_of(pl.program_id(2) * nrow, nrow), nrow)
            for rho in range(dil):
                update(q_ref, pl.ds(rho, nrow, stride=dil), nrow, [pl.ds(rho, QB, stride=dil)],
                       [band(n, 1, brow)])

    for r in range(GROUP):
        o_ref[:, r * HEAD_DIM:(r + 1) * HEAD_DIM] = (acc_ref[r] / l_ref[r]).astype(o_ref.dtype)


def _b_prompt(qb, kv, band):
    B, T, _ = qb.shape
    nh = N_HEADS
    hd = HEAD_DIM
    gw = GROUP * hd
    assert T % BQ == 0

    def q_spec(n, r):
        return pl.BlockSpec((None, BQ, hd), lambda b, g, c: (b, c, (n * KV_HEADS + g) * GROUP + r))

    q_specs = [q_spec(n, r) for n in range(len(B_GROUPS)) for r in range(GROUP)]
    return pl.pallas_call(
        _bp_kernel,
        grid=(B, KV_HEADS, T // BQ),
        in_specs=q_specs + [
                  pl.BlockSpec((None, T, hd), lambda b, g, c: (b, 0, g)),
                  pl.BlockSpec((None, T, hd), lambda b, g, c: (b, 0, KV_HEADS + g)),
                  pl.BlockSpec((GROUP, 3, 2, QB, LANES), lambda b, g, c: (g, 0, 0, 0, 0))],
        out_specs=pl.BlockSpec((None, BQ, gw), lambda b, g, c: (b, c, g)),
        out_shape=jax.ShapeDtypeStruct((B, T, nh * hd), BF16),
        scratch_shapes=[pltpu.VMEM((GROUP, BQ, LANES), F32), pltpu.VMEM((GROUP, BQ, LANES), F32),
                        pltpu.VMEM((GROUP, BQ, hd), F32)],
        compiler_params=_params(("parallel", "parallel", "arbitrary")),
        name="b_prompt",
    )(*([qb] * len(q_specs)), kv, kv, band)


def _bs_kernel(q0_ref, q1_ref, q2_ref, kc_ref, vc_ref, kn_ref, vn_ref, tab_ref, o_ref):
    TS = q0_ref.shape[0]
    W = kc_ref.shape[0] // KV_HEADS
    rows = pl.ds(pl.program_id(1), W, stride=KV_HEADS)
    kc = kc_ref[rows, :].astype(BF16)
    vc = vc_ref[rows, :].astype(BF16)
    kn = kn_ref[...].astype(BF16)
    vn = vn_ref[...].astype(BF16)
    logits = []
    for n in range(len(B_GROUPS)):
        q_ref = (q0_ref, q1_ref, q2_ref)[n]
        q = (jnp.concatenate([q_ref[:, r * HEAD_DIM:(r + 1) * HEAD_DIM] for r in range(GROUP)],
                             axis=0) * QSCALE).astype(BF16)
        logits.append((_nt(q, kc) + tab_ref[n, :, :W], _nt(q, kn) + tab_ref[n, :, W:]))
    m = functools.reduce(jnp.maximum,
                         [jnp.max(x, axis=1, keepdims=True) for pair in logits for x in pair])
    l = jnp.zeros((GROUP * TS, 1), F32)
    acc = jnp.zeros((GROUP * TS, HEAD_DIM), F32)
    for sc_c, sc_n in logits:
        pc = jnp.exp2(sc_c - m)
        pn = jnp.exp2(sc_n - m)
        l = l + jnp.sum(pc, axis=1, keepdims=True) + jnp.sum(pn, axis=1, keepdims=True)
        acc = (acc + jnp.dot(pc.astype(BF16), vc, preferred_element_type=F32)
               + jnp.dot(pn.astype(BF16), vn, preferred_element_type=F32))
    res = acc / l
    for r in range(GROUP):
        o_ref[:, r * HEAD_DIM:(r + 1) * HEAD_DIM] = res[r * TS:(r + 1) * TS, :].astype(o_ref.dtype)


def _b_sample(qb_s, cache_k, cache_v, kv_new, tab):
    DB, TS, NQ = qb_s.shape
    W = cache_k.shape[1] // KV_HEADS
    hd = HEAD_DIM
    return pl.pallas_call(
        _bs_kernel,
        grid=(DB, KV_HEADS),
        in_specs=[pl.BlockSpec((None, TS, GROUP * hd), lambda b, g: (b, 0, g)),
                  pl.BlockSpec((None, TS, GROUP * hd), lambda b, g: (b, 0, KV_HEADS + g)),
                  pl.BlockSpec((None, TS, GROUP * hd), lambda b, g: (b, 0, 2 * KV_HEADS + g)),
                  pl.BlockSpec((None, W * KV_HEADS, hd), lambda b, g: (b, 0, 0)),
                  pl.BlockSpec((None, W * KV_HEADS, hd), lambda b, g: (b, 0, 0)),
                  pl.BlockSpec((None, LANES, hd), lambda b, g: (b, 0, g)),
                  pl.BlockSpec((None, LANES, hd), lambda b, g: (b, 0, KV_HEADS + g)),
                  pl.BlockSpec((len(B_GROUPS), None, GROUP * TS, W + LANES), lambda b, g: (0, g, 0, 0))],
        out_specs=pl.BlockSpec((None, TS, GROUP * hd), lambda b, g: (b, 0, g)),
        out_shape=jax.ShapeDtypeStruct((DB, TS, N_HEADS * hd), BF16),
        compiler_params=_params(("parallel", "arbitrary")),
        name="b_sample",
    )(qb_s, qb_s, qb_s, cache_k, cache_v, kv_new, kv_new, tab)


FFN_TF = 512
HALO = 16


def _ffn_up_kernel(xm_ref, xh_ref, wg_ref, wu_ref, cw_ref, cb_ref, act_ref, st_ref, xe_ref, *,
                   tiles_per_batch):
    i = pl.program_id(0)
    tm = xm_ref.shape[0]

    @pl.when(pl.program_id(1) == 0)
    def _():
        first = (i % tiles_per_batch) == 0
        xh = xh_ref[...]
        xe_ref[0:HALO, :] = jnp.where(first, jnp.zeros_like(xh), xh)
        xe_ref[HALO:, :] = xm_ref[...]

    gate = jnp.dot(xe_ref[...], wg_ref[...], preferred_element_type=F32)
    up = jnp.dot(xm_ref[...], wu_ref[...], preferred_element_type=F32)
    cw = cw_ref[...]
    conv = cb_ref[...] + ((gate[HALO - 2:HALO - 2 + tm] * cw[0:1] + gate[HALO - 1:HALO - 1 + tm] * cw[1:2])
                          + gate[HALO:] * cw[2:3])
    act_ref[...] = (jax.nn.silu(conv) * up).astype(act_ref.dtype)
    st_ref[...] = gate[HALO + tm - (CONV_WIDTH - 1):, :]


def _ffn_up(x_bf, w_g_bf, w_u_bf, conv_w, conv_b, layer, T):
    M, D = x_bf.shape
    F = conv_w.shape[2]
    tm = 1024
    assert T % tm == 0 and M % T == 0 and F % FFN_TF == 0 and tm % HALO == 0
    nf = F // FFN_TF
    hb = tm // HALO
    return pl.pallas_call(
        functools.partial(_ffn_up_kernel, tiles_per_batch=T // tm),
        grid=(M // tm, nf),
        in_specs=[pl.BlockSpec((tm, D), lambda i, j: (i, 0)),
                  pl.BlockSpec((HALO, D), lambda i, j: (jnp.maximum(i * hb - 1, 0), 0)),
                  pl.BlockSpec((None, D, FFN_TF), lambda i, j: (layer, 0, j)),
                  pl.BlockSpec((None, D, FFN_TF), lambda i, j: (layer, 0, j)),
                  pl.BlockSpec((None, CONV_WIDTH, FFN_TF), lambda i, j: (layer, 0, j)),
                  pl.BlockSpec((None, 1, FFN_TF), lambda i, j: (layer, 0, j))],
        out_specs=[pl.BlockSpec((tm, FFN_TF), lambda i, j: (i, j)),
                   pl.BlockSpec((None, CONV_WIDTH - 1, FFN_TF), lambda i, j: (i, 0, j))],
        out_shape=[jax.ShapeDtypeStruct((M, F), BF16),
                   jax.ShapeDtypeStruct((M // tm, CONV_WIDTH - 1, F), F32)],
        scratch_shapes=[pltpu.VMEM((HALO + tm, D), BF16)],
        compiler_params=_params(("parallel", "arbitrary")),
        name="ffn_up",
    )(x_bf, x_bf, w_g_bf, w_u_bf, conv_w, conv_b)


def _mm_ln_kernel(a_ref, w_ref, x_ref, g_ref, b_ref, h_ref, hb_ref, acc_ref, *, alpha):
    k = pl.program_id(1)

    @pl.when(k == 0)
    def _():
        acc_ref[...] = jnp.zeros_like(acc_ref)

    acc_ref[...] += jnp.dot(a_ref[...], w_ref[...], preferred_element_type=F32)

    @pl.when(k == pl.num_programs(1) - 1)
    def _():
        y = alpha * x_ref[...] + acc_ref[...]
        mu = jnp.mean(y, axis=-1, keepdims=True)
        d = y - mu
        var = jnp.mean(d * d, axis=-1, keepdims=True)
        h = d * lax.rsqrt(var + LN_EPS) * g_ref[...] + b_ref[...]
        h_ref[...] = h
        hb_ref[...] = h.astype(hb_ref.dtype)


def _mm_ln(a_bf, w_bf, layer, x, g, b, alpha):
    M, K = a_bf.shape
    D = w_bf.shape[2]
    tm = min(512, M)
    tk = K // 4 if K > 2048 else K // 2
    assert M % tm == 0 and K % tk == 0 and tk % LANES == 0
    return pl.pallas_call(
        functools.partial(_mm_ln_kernel, alpha=alpha),
        grid=(M // tm, K // tk),
        in_specs=[pl.BlockSpec((tm, tk), lambda i, k: (i, k)),
                  pl.BlockSpec((None, tk, D), lambda i, k: (layer, k, 0)),
                  pl.BlockSpec((tm, D), lambda i, k: (i, 0)),
                  pl.BlockSpec((1, D), lambda i, k: (0, 0)),
                  pl.BlockSpec((1, D), lambda i, k: (0, 0))],
        out_specs=[pl.BlockSpec((tm, D), lambda i, k: (i, 0)),
                   pl.BlockSpec((tm, D), lambda i, k: (i, 0))],
        out_shape=[jax.ShapeDtypeStruct((M, D), F32), jax.ShapeDtypeStruct((M, D), BF16)],
        scratch_shapes=[pltpu.VMEM((tm, D), F32)],
        compiler_params=_params(("parallel", "arbitrary")),
        name="mm_ln",
    )(a_bf, w_bf, x, g.reshape(1, D), b.reshape(1, D))


def _layer_norm(x, g, b):
    mu = x.mean(-1, keepdims=True)
    var = jnp.square(x - mu).mean(-1, keepdims=True)
    return (x - mu) * lax.rsqrt(var + LN_EPS) * g + b


def _pick_tm(M):
    for tm in (1024, 512, 256, 128, 64, 32, 16):
        if M % tm == 0:
            return tm
    raise ValueError(M)


def _pick_tn(N):
    for tn in (512, 384, 256, 128):
        if N % tn == 0:
            return tn
    raise ValueError(N)


def _proj(x_bf, w_bf, layer=0):
    return _mm(x_bf, w_bf, layer, _pick_tm(x_bf.shape[0]), _pick_tn(w_bf.shape[2]))


def _ffn_act_sample(hs_bf, state, w_g_bf, w_u_bf, conv_w, conv_b, layer, DB, TS):
    conv_w, conv_b = conv_w[layer], conv_b[layer]
    F = conv_w.shape[1]
    gate = _proj(hs_bf, w_g_bf, layer).reshape(DB, TS, F)
    up = _proj(hs_bf, w_u_bf, layer).reshape(DB, TS, F)
    ext = jnp.concatenate([state, gate], axis=1)
    conv = conv_b + sum(ext[:, j:j + TS] * conv_w[j] for j in range(CONV_WIDTH))
    act = (jax.nn.silu(conv) * up).astype(BF16).reshape(DB * TS, F)
    return act, ext[:, ext.shape[1] - (CONV_WIDTH - 1):]


def _table_indices(T, W, TS):
    U = SAT_DIST + TZ_OFF + QB
    assert T <= T5_MAX_DISTANCE
    u = np.arange(U)[:, None]
    c = np.arange(LANES)[None, :]
    tz_idx = _t5_bucket(jnp.asarray(np.maximum(u - c - TZ_OFF, 0), I32))
    a = np.arange(QB)[:, None]
    band = []
    for window, dil in B_GROUPS:
        for off in (QB, 0):
            d = a + off - c
            ok = (d >= 0) & (d <= window // dil)
            band.append(jnp.where(jnp.asarray(ok), _t5_bucket(jnp.asarray(np.maximum(d, 0) * dil, I32)),
                                  NUM_BUCKETS))
    band_idx = jnp.concatenate(band, axis=0)
    nkt = W // LANES + 1
    t = np.arange(TS)[None, :, None]
    key = (np.arange(nkt)[:, None, None] * LANES + np.arange(LANES)[None, None, :])
    d = W + t - key
    tabs = []
    for window, dil in B_GROUPS:
        ok = (d >= 0) & (d % dil == 0) & (d <= window)
        tabs.append(jnp.where(jnp.asarray(ok), _t5_bucket(jnp.asarray(np.maximum(d, 0), I32)), NUM_BUCKETS))
    samp_idx = jnp.concatenate(tabs, axis=0).reshape(-1, LANES)
    rows = [tz_idx, band_idx, samp_idx]
    total = sum(r.shape[0] for r in rows)
    pad = (-total) % 128
    if pad:
        rows.append(jnp.full((pad, LANES), NUM_BUCKETS, I32))
    return jnp.concatenate(rows, axis=0), U, nkt


def kernel(x_prompt, x_sample, cache_k_a, cache_v_a, cache_kidx_a, cache_k_b, cache_v_b, state_ffn, page_table, a_w_in, a_w_o, a_kn_g, a_kn_b, b_w_kv, b_w_q, b_w_o, ffn_w_up, ffn_conv_w, ffn_conv_b, ffn_w_down, ln_g, ln_b, rel_bias):
    B, T, D = x_prompt.shape
    DB, TS, _ = x_sample.shape
    depth = ffn_w_up.shape[0]
    n_a = a_w_in.shape[0]
    d_ff = ffn_w_down.shape[1]
    W = cache_k_b.shape[1]
    n_pages = page_table.shape[1]
    page = cache_k_a.shape[2]
    past = n_pages * page
    alpha = (2 * depth) ** 0.25
    kvw = KV_HEADS * HEAD_DIM
    a_q = N_HEADS * HEAD_DIM
    a_in = a_w_in.shape[2]
    np_cols = ((a_in + 511) // 512) * 512
    ki0 = a_q + 2 * kvw + IDX_HEADS * IDX_DIM

    idx_all, U, nkt = _table_indices(T, W, TS)
    tabs = _bias_tables(rel_bias * LOG2E, idx_all)
    tz = tabs[:, :U]
    band = tabs[:, U:U + 6 * QB].reshape(N_HEADS, 3, 2, QB, LANES)
    samp = tabs[:, U + 6 * QB:U + 6 * QB + 3 * nkt * TS].reshape(KV_HEADS, GROUP, 3, nkt, TS, LANES)
    samp = samp.transpose(2, 0, 1, 4, 3, 5).reshape(3, KV_HEADS, GROUP * TS, nkt * LANES)
    n_slab = SAT_DIST // LANES + 1
    tzs = tz[:, TZ_OFF:TZ_OFF + n_slab * LANES].reshape(N_HEADS, n_slab, LANES, LANES)[:, :, :TS, :]

    cache_k_a2 = cache_k_a.reshape(n_a, -1, page * KV_HEADS, HEAD_DIM)
    cache_v_a2 = cache_v_a.reshape(n_a, -1, page * KV_HEADS, HEAD_DIM)
    cache_k_b2 = cache_k_b.reshape(DB, W * KV_HEADS, HEAD_DIM)
    cache_v_b2 = cache_v_b.reshape(DB, W * KV_HEADS, HEAD_DIM)
    cache_kidx_t = jnp.swapaxes(cache_kidx_a, 2, 3)

    def pad_rows(x, n):
        return jnp.pad(x, ((0, 0), (0, n - x.shape[1]), (0, 0)))

    f_pad = ((d_ff + FFN_TF - 1) // FFN_TF) * FFN_TF
    pad_f = f_pad - d_ff
    w_g_all = jnp.pad(ffn_w_up[:, :, :d_ff], ((0, 0), (0, 0), (0, pad_f))).astype(BF16)
    w_u_all = jnp.pad(ffn_w_up[:, :, d_ff:], ((0, 0), (0, 0), (0, pad_f))).astype(BF16)
    w_down_all = jnp.pad(ffn_w_down, ((0, 0), (0, pad_f), (0, 0))).astype(BF16)
    conv_w_all = jnp.pad(ffn_conv_w, ((0, 0), (0, 0), (0, pad_f)))
    conv_b_all = jnp.pad(ffn_conv_b, ((0, 0), (0, pad_f))).reshape(depth, 1, f_pad)
    w_in_all = jnp.pad(a_w_in, ((0, 0), (0, 0), (0, np_cols - a_in))).astype(BF16)
    w_o_a, w_o_b = a_w_o.astype(BF16), b_w_o.astype(BF16)
    w_q_all = b_w_q.astype(BF16)
    w_kv_bf = b_w_kv.astype(BF16)[None]

    hp = x_prompt.reshape(B * T, D)
    hs = x_sample.reshape(DB * TS, D)
    hp_bf, hs_bf = hp.astype(BF16), hs.astype(BF16)
    ka_p, va_p, kia_p, ka_s, va_s, kia_s, ffn_p, ffn_s = [], [], [], [], [], [], [], []
    for layer in range(depth):
        if layer < n_a:
            a = layer
            w_o_bf, w_o_layer = w_o_a, a
            proj = _proj(hp_bf, w_in_all, a).reshape(B, T, np_cols)
            k = proj[..., a_q:a_q + kvw].reshape(B, T, KV_HEADS, HEAD_DIM)
            v = proj[..., a_q + kvw:a_q + 2 * kvw].reshape(B, T, KV_HEADS, HEAD_DIM)
            ki = _layer_norm(proj[..., ki0:ki0 + IDX_DIM], a_kn_g[a], a_kn_b[a])
            ka_p.append(k); va_p.append(v); kia_p.append(ki)
            mask = _a1_prompt(ki.astype(BF16), proj, min(TOPK_MAX, T // 4))
            o = _a2_prompt(proj, mask, tz).reshape(B * T, a_q)
            proj_s = _proj(hs_bf, w_in_all, a).reshape(DB, TS, np_cols)
            k_s = proj_s[..., a_q:a_q + kvw]
            v_s = proj_s[..., a_q + kvw:a_q + 2 * kvw]
            ki_s = _layer_norm(proj_s[..., ki0:ki0 + IDX_DIM], a_kn_g[a], a_kn_b[a])
            ka_s.append(k_s.reshape(DB, TS, KV_HEADS, HEAD_DIM))
            va_s.append(v_s.reshape(DB, TS, KV_HEADS, HEAD_DIM))
            kia_s.append(ki_s)
            mask_s = _a1_sample(page_table, proj_s, pad_rows(ki_s, LANES).astype(BF16), cache_kidx_t, a,
                                min(TOPK_MAX, (past + TS) // 4))
            o_s = _a2_sample(page_table, proj_s, mask_s, tzs, pad_rows(k_s, LANES), pad_rows(v_s, LANES),
                             cache_k_a2, cache_v_a2, a).reshape(DB * TS, a_q)
        else:
            if layer == n_a:
                kv_p = _proj(hp_bf, w_kv_bf).reshape(B, T, 2 * kvw)
                kv_s = _proj(hs_bf, w_kv_bf).reshape(DB, TS, 2 * kvw)
                kv_s_pad = pad_rows(kv_s, LANES)
            bl = layer - n_a
            w_o_bf, w_o_layer = w_o_b, bl
            o = _b_prompt(_proj(hp_bf, w_q_all, bl).reshape(B, T, -1), kv_p, band).reshape(B * T, a_q)
            o_s = _b_sample(_proj(hs_bf, w_q_all, bl).reshape(DB, TS, -1), cache_k_b2, cache_v_b2, kv_s_pad,
                            samp).reshape(DB * TS, a_q)
        hp, hp_bf = _mm_ln(o, w_o_bf, w_o_layer, hp, ln_g[layer, 0], ln_b[layer, 0], alpha)
        hs, hs_bf = _mm_ln(o_s, w_o_bf, w_o_layer, hs, ln_g[layer, 0], ln_b[layer, 0], alpha)
        act, st = _ffn_up(hp_bf, w_g_all, w_u_all, conv_w_all, conv_b_all, layer, T)
        ffn_p.append(st.reshape(B, -1, CONV_WIDTH - 1, f_pad)[:, -1, :, :d_ff])
        act_s, st_s = _ffn_act_sample(hs_bf, jnp.pad(state_ffn[layer], ((0, 0), (0, 0), (0, pad_f))),
                                      w_g_all, w_u_all, conv_w_all, conv_b_all, layer, DB, TS)
        ffn_s.append(st_s[..., :d_ff])
        hp, hp_bf = _mm_ln(act, w_down_all, layer, hp, ln_g[layer, 1], ln_b[layer, 1], alpha)
        hs, hs_bf = _mm_ln(act_s, w_down_all, layer, hs, ln_g[layer, 1], ln_b[layer, 1], alpha)
    keep = min(max(w for w, _ in B_GROUPS), T)
    kb_p = kv_p[..., :kvw].reshape(B, T, KV_HEADS, HEAD_DIM)
    vb_p = kv_p[..., kvw:].reshape(B, T, KV_HEADS, HEAD_DIM)
    kb_s = kv_s[..., :kvw].reshape(DB, TS, KV_HEADS, HEAD_DIM)
    vb_s = kv_s[..., kvw:].reshape(DB, TS, KV_HEADS, HEAD_DIM)
    return (hp.reshape(B, T, D), hs.reshape(DB, TS, D), jnp.stack(ka_p), jnp.stack(va_p), jnp.stack(kia_p),
            jnp.stack(ka_s), jnp.stack(va_s), jnp.stack(kia_s), kb_p[:, T - keep:], vb_p[:, T - keep:],
            kb_s, vb_s, jnp.stack(ffn_p), jnp.stack(ffn_s))
```

```python
import functools
import math

import numpy as np
import jax
import jax.numpy as jnp
from jax import lax
from jax.experimental import pallas as pl
from jax.experimental.pallas import tpu as pltpu

F32 = jnp.float32
BF16 = jnp.bfloat16
I32 = jnp.int32

HEAD_DIM = 128
N_HEADS = 16
KV_HEADS = 4
GROUP = N_HEADS // KV_HEADS
IDX_HEADS = 16
IDX_DIM = 64
TOPK_MAX = 256
B_GROUPS = ((128, 1), (512, 4), (2048, 16))
NUM_BUCKETS = 32
T5_MAX_DISTANCE = 2048
LN_EPS = 1e-5
NEG = -1e30
CONV_WIDTH = 3

LANES = 128
QB = 128
KC = 512
TZ_OFF = KC - QB
SAT_DIST = T5_MAX_DISTANCE + QB
INT_MIN = -2 ** 31
KEY_NEG_INF = INT_MIN + 0x7FFFFF
VMEM_LIMIT = 56 * 1024 * 1024

LOG2E = 1.4426950408889634
QSCALE = HEAD_DIM ** -0.5 * LOG2E

NT_DIMS = (((1,), (1,)), ((), ()))


def _nt(a, b):
    return lax.dot_general(a, b, NT_DIMS, preferred_element_type=F32)


def _params(sem):
    return pltpu.CompilerParams(dimension_semantics=sem, vmem_limit_bytes=VMEM_LIMIT)


def _mm_kernel(x_ref, w_ref, o_ref):
    o_ref[...] = jnp.dot(x_ref[...], w_ref[...], preferred_element_type=F32).astype(o_ref.dtype)


def _mm(x, w, layer, tm, tn, out_dtype=F32):
    M, K = x.shape
    N = w.shape[2]
    assert M % tm == 0 and N % tn == 0, (M, N, tm, tn)
    return pl.pallas_call(
        _mm_kernel,
        grid=(M // tm, N // tn),
        in_specs=[pl.BlockSpec((tm, K), lambda i, j: (i, 0)),
                  pl.BlockSpec((None, K, tn), lambda i, j: (layer, 0, j))],
        out_specs=pl.BlockSpec((tm, tn), lambda i, j: (i, j)),
        out_shape=jax.ShapeDtypeStruct((M, N), out_dtype),
        compiler_params=_params(("parallel", "arbitrary")),
        name="mm",
    )(x, w)


def _t5_bucket(dist):
    dist = jnp.maximum(dist, 0)
    exact = NUM_BUCKETS // 2
    far = exact + (jnp.log(jnp.maximum(dist, 1).astype(F32) / exact)
                   / math.log(T5_MAX_DISTANCE / exact) * (NUM_BUCKETS - exact)).astype(I32)
    return jnp.where(dist < exact, dist, jnp.minimum(far, NUM_BUCKETS - 1))


def _table_kernel(rb_ref, idx_ref, o_ref):
    slab = 16

    def body(i, carry):
        r0 = pl.multiple_of(i * slab, slab)
        idx = idx_ref[pl.ds(r0, slab), :]
        accs = [jnp.full(idx.shape, NEG, F32)] * N_HEADS
        for k in range(NUM_BUCKETS):
            hit = idx == k
            accs = [jnp.where(hit, rb_ref[k, h], accs[h]) for h in range(N_HEADS)]
        for h in range(N_HEADS):
            o_ref[h, pl.ds(r0, slab), :] = accs[h]
        return carry

    lax.fori_loop(0, idx_ref.shape[0] // slab, body, 0)


def _bias_tables(rel_bias, idx):
    R = idx.shape[0]
    tr = 128
    assert R % tr == 0
    return pl.pallas_call(
        _table_kernel,
        grid=(R // tr,),
        in_specs=[pl.BlockSpec(memory_space=pltpu.SMEM),
                  pl.BlockSpec((tr, LANES), lambda i: (i, 0))],
        out_specs=pl.BlockSpec((N_HEADS, tr, LANES), lambda i: (0, i, 0)),
        out_shape=jax.ShapeDtypeStruct((N_HEADS, R, LANES), F32),
        compiler_params=_params(("arbitrary",)),
        name="bias_tables",
    )(rel_bias, idx)


def _sortable(x):
    bits = lax.bitcast_convert_type(x, I32)
    return bits ^ ((bits >> 31) & jnp.int32(0x7FFFFFFF))


def _kth_largest(count_ge, shape, k):
    def body(t, ans):
        cand_u = ans | (jnp.int32(1) << (31 - t))
        cnt = count_ge(cand_u ^ jnp.int32(INT_MIN))
        return jnp.where(cnt >= k, cand_u, ans)
    ans = lax.fori_loop(0, 32, body, jnp.zeros(shape, I32))
    return ans ^ jnp.int32(INT_MIN)


def _tie_bound(count_eq_below, shape, need, nbits):
    def body(t, ans):
        cand = ans | (jnp.int32(1) << (nbits - 1 - t))
        return jnp.where(count_eq_below(cand) < need, cand, ans)
    return lax.fori_loop(0, nbits, body, jnp.zeros(shape, I32))


def _a1p_kernel(kin_ref, qi_ref, tail_ref, mask_ref, st_ref, jb_ref, *, topk):
    i = pl.program_id(1)
    T = kin_ref.shape[0]
    nchunk = T // KC
    nbits = (T - 1).bit_length()
    tail_t = tail_ref[...].T
    wi_t = tail_t[IDX_DIM:IDX_DIM + IDX_HEADS, :] * (IDX_HEADS ** -0.5 * IDX_DIM ** -0.5)
    qpos = i * QB + lax.broadcasted_iota(I32, (KC, QB), 1)
    row = lax.broadcasted_iota(I32, (KC, QB), 0)

    nj = (i * QB) // KC + 1

    for c in range(nchunk):
        @pl.when(c < nj)
        def _():
            kc = kin_ref[c * KC:(c + 1) * KC, :]
            acc = jnp.zeros((KC, QB), F32)
            for h in range(IDX_HEADS):
                qh = qi_ref[:, h * IDX_DIM:(h + 1) * IDX_DIM].astype(BF16)
                acc = acc + jnp.maximum(_nt(kc, qh), 0.0) * wi_t[h:h + 1, :]
            acc = jnp.where(c * KC + row <= qpos, acc, -jnp.inf)
            st_ref[c * KC:(c + 1) * KC, :] = _sortable(acc)

    def select(nvis):
        def count(pred):
            cnt = jnp.zeros((1, QB), F32)
            for c in range(nvis):
                blk = st_ref[c * KC:(c + 1) * KC, :]
                cnt = cnt + jnp.sum(jnp.where(pred(blk, c * KC + row), 1.0, 0.0), axis=0, keepdims=True)
            return cnt

        thr = _kth_largest(lambda cand: count(lambda blk, _: blk >= cand), (1, QB), float(topk))
        need = float(topk) - count(lambda blk, _: blk > thr)
        n_eq = count(lambda blk, _: blk == thr)
        excess = jnp.where((n_eq > need) & (thr != KEY_NEG_INF), 1.0, 0.0)
        jb_ref[...] = jnp.full((1, QB), T, I32)

        @pl.when(jnp.max(excess) > 0.0)
        def _():
            jb_ref[...] = _tie_bound(
                lambda cand: count(lambda blk, kpos: jnp.where(blk == thr, kpos, T) < cand),
                (1, QB), need, nbits)

        jb = jb_ref[...]
        for c in range(nchunk):
            if c < nvis:
                blk = st_ref[c * KC:(c + 1) * KC, :]
                kpos = c * KC + row
                rank_pos = jnp.where(blk == thr, kpos, jnp.where(blk > thr, -1, T + 1))
                sel = jnp.where(kpos <= qpos, rank_pos, T + 1) <= jb
                m_t = jnp.where(sel, 0.0, NEG)
                for s4 in range(KC // QB):
                    mask_ref[c, :, s4 * QB:(s4 + 1) * QB] = (
                        m_t[s4 * QB:(s4 + 1) * QB, :].T.astype(mask_ref.dtype))
            else:
                mask_ref[c] = jnp.full((QB, KC), NEG, mask_ref.dtype)

    for nvis in range(1, nchunk + 1):
        pl.when(nj == nvis)(functools.partial(select, nvis))


def _a1_prompt(kin_bf, proj, topk):
    B, T, _ = proj.shape
    nq, nc = T // QB, T // KC
    assert topk <= KC
    qi_blk =(N_HEADS * HEAD_DIM + 2 * KV_HEADS * HEAD_DIM) // (IDX_HEADS * IDX_DIM)
    tail_blk = (N_HEADS * HEAD_DIM + 2 * KV_HEADS * HEAD_DIM + IDX_HEADS * IDX_DIM) // LANES
    return pl.pallas_call(
        functools.partial(_a1p_kernel, topk=topk),
        grid=(B, nq),
        in_specs=[pl.BlockSpec((None, T, IDX_DIM), lambda b, i: (b, 0, 0)),
                  pl.BlockSpec((None, QB, IDX_HEADS * IDX_DIM), lambda b, i: (b, i, qi_blk)),
                  pl.BlockSpec((None, QB, LANES), lambda b, i: (b, i, tail_blk))],
        out_specs=pl.BlockSpec((None, None, nc, QB, KC), lambda b, i: (b, i, 0, 0, 0)),
        out_shape=jax.ShapeDtypeStruct((B, nq, nc, QB, KC), BF16),
        scratch_shapes=[pltpu.VMEM((T, QB), I32), pltpu.VMEM((1, QB), I32)],
        compiler_params=_params(("parallel", "arbitrary")),
        name="a1_prompt",
    )(kin_bf, proj, proj)


def _a2p_kernel(q_ref, k_ref, v_ref, mask_ref, tz_ref, o_ref):
    i = pl.program_id(2)
    nj = (i * QB) // KC + 1
    HP = GROUP
    R = HP * QB
    heads = [list(range(c * HP, (c + 1) * HP)) for c in range(GROUP // HP)]
    qs = [(jnp.concatenate([q_ref[:, r * HEAD_DIM:(r + 1) * HEAD_DIM] for r in hs], axis=0)
           * QSCALE).astype(BF16) for hs in heads]

    def body(j, carry):
        k0 = pl.multiple_of(j * KC, KC)
        kt = k_ref[pl.ds(k0, KC), :].astype(BF16)
        vt = v_ref[pl.ds(k0, KC), :].astype(BF16)
        base = i * QB - j * KC + TZ_OFF
        mk = mask_ref[j].astype(F32)
        mk = jnp.concatenate([mk] * HP, axis=0)
        out = []
        for hs, q, (m, l, acc) in zip(heads, qs, carry):
            bias = jnp.concatenate(
                [jnp.concatenate(
                    [tz_ref[r, pl.ds(pl.multiple_of(base - QB * s4, QB), QB), :] for s4 in range(KC // QB)],
                    axis=1) for r in hs], axis=0)
            s = _nt(q, kt) + bias + mk
            m_new = jnp.maximum(m, jnp.max(s, axis=1, keepdims=True))
            alpha = jnp.exp2(m - m_new)
            p = jnp.exp2(s - m_new)
            l = alpha * l + jnp.sum(p, axis=1, keepdims=True)
            acc = alpha * acc + jnp.dot(p.astype(BF16), vt, preferred_element_type=F32)
            out.append((m_new, l, acc))
        return tuple(out)

    init = tuple((jnp.full((R, 1), NEG, F32), jnp.zeros((R, 1), F32), jnp.zeros((R, HEAD_DIM), F32))
                 for _ in heads)
    final = lax.fori_loop(0, nj, body, init)
    for hs, (m, l, acc) in zip(heads, final):
        res = acc / l
        for n, r in enumerate(hs):
            o_ref[:, r * HEAD_DIM:(r + 1) * HEAD_DIM] = res[n * QB:(n + 1) * QB, :].astype(o_ref.dtype)


def _a2_prompt(proj, mask, tz):
    B, T, _ = proj.shape
    nq, nc = T // QB, T // KC
    gw = GROUP * HEAD_DIM
    k_blk0 = N_HEADS * HEAD_DIM // HEAD_DIM
    v_blk0 = k_blk0 + KV_HEADS
    U = tz.shape[1]
    return pl.pallas_call(
        _a2p_kernel,
        grid=(KV_HEADS, B, nq),
        in_specs=[pl.BlockSpec((None, QB, gw), lambda g, b, i: (b, i, g)),
                  pl.BlockSpec((None, T, HEAD_DIM), lambda g, b, i: (b, 0, k_blk0 + g)),
                  pl.BlockSpec((None, T, HEAD_DIM), lambda g, b, i: (b, 0, v_blk0 + g)),
                  pl.BlockSpec((None, None, nc, QB, KC), lambda g, b, i: (b, i, 0, 0, 0)),
                  pl.BlockSpec((GROUP, U, LANES), lambda g, b, i: (g, 0, 0))],
        out_specs=pl.BlockSpec((None, QB, gw), lambda g, b, i: (b, i, g)),
        out_shape=jax.ShapeDtypeStruct((B, T, N_HEADS * HEAD_DIM), BF16),
        compiler_params=_params(("parallel", "parallel", "arbitrary")),
        name="a2_prompt",
    )(proj, proj, proj, mask, tz)


PAGES_PER_STEP = 32


def _a1s_kernel(pt_ref, qi_ref, tail_ref, knew_ref, *rest, topk, past):
    page_refs = rest[:PAGES_PER_STEP]
    mask_ref, st_ref, jb_ref = rest[PAGES_PER_STEP:]
    s = pl.program_id(1)
    nt, TS, _ = st_ref.shape
    L = nt * LANES
    nbits = (L - 1).bit_length()
    qi = jnp.concatenate([qi_ref[:, h * IDX_DIM:(h + 1) * IDX_DIM] for h in range(IDX_HEADS)],
                         axis=0).astype(BF16)
    w_col = jnp.concatenate([tail_ref[:, IDX_DIM + h:IDX_DIM + h + 1] for h in range(IDX_HEADS)],
                            axis=0) * (IDX_HEADS ** -0.5 * IDX_DIM ** -0.5)

    def scores(qk):
        sc = jnp.maximum(qk, 0.0) * w_col
        return jnp.sum(sc.reshape(IDX_HEADS, TS, LANES), axis=0)

    for p in range(PAGES_PER_STEP):
        qk = jnp.dot(qi, page_refs[p][...].astype(BF16), preferred_element_type=F32)
        st_ref[s * PAGES_PER_STEP + p] = _sortable(scores(qk))

    @pl.when(s == pl.num_programs(1) - 1)
    def _():
        t = lax.broadcasted_iota(I32, (TS, LANES), 0)
        c = lax.broadcasted_iota(I32, (TS, LANES), 1)
        sc = jnp.where(c <= t, scores(_nt(qi, knew_ref[...])), -jnp.inf)
        st_ref[nt - 1] = _sortable(sc)

        kpos = (lax.broadcasted_iota(I32, (nt, TS, LANES), 0) * LANES
                + lax.broadcasted_iota(I32, (nt, TS, LANES), 2))
        qpos = past + lax.broadcasted_iota(I32, (nt, TS, LANES), 1)

        def count(pred):
            per_lane = jnp.sum(jnp.where(pred(st_ref[...]), 1.0, 0.0), axis=0)
            return jnp.sum(per_lane, axis=1, keepdims=True)

        thr = _kth_largest(lambda cand: count(lambda k: k >= cand[None]), (TS, 1), float(topk))
        thr3 = thr[None]
        need = float(topk) - count(lambda k: k > thr3)
        n_eq = count(lambda k: k == thr3)
        excess = jnp.where((n_eq > need) & (thr != KEY_NEG_INF), 1.0, 0.0)
        jb_ref[...] = jnp.full((TS, 1), L, I32)

        @pl.when(jnp.max(excess) > 0.0)
        def _():
            jb_ref[...] = _tie_bound(
                lambda cand: count(lambda k: jnp.where(k == thr3, kpos, L) < cand[None]),
                (TS, 1), need, nbits)

        keys = st_ref[...]
        rank_pos = jnp.where(keys == thr3, kpos, jnp.where(keys > thr3, -1, L + 1))
        sel = jnp.where(kpos <= qpos, rank_pos, L + 1) <= jb_ref[...][None]
        mask_ref[...] = jnp.where(sel, 0.0, NEG)


def _a1_sample(page_table, proj_s, knew_bf, cache_kidx, layer, topk):
    DB, TS, _ = proj_s.shape
    n_pages = page_table.shape[1]
    page = cache_kidx.shape[3]
    assert page == LANES and n_pages % PAGES_PER_STEP == 0 and topk <= n_pages * page
    nt = n_pages + 1
    qi_blk = (N_HEADS * HEAD_DIM + 2 * KV_HEADS * HEAD_DIM) // (IDX_HEADS * IDX_DIM)
    tail_blk = (N_HEADS * HEAD_DIM + 2 * KV_HEADS * HEAD_DIM + IDX_HEADS * IDX_DIM) // LANES

    def page_spec(p):
        return pl.BlockSpec((None, None, IDX_DIM, page),
                            lambda b, s, pt: (layer, pt[b, s * PAGES_PER_STEP + p], 0, 0))

    grid_spec = pltpu.PrefetchScalarGridSpec(
        num_scalar_prefetch=1,
        grid=(DB, n_pages // PAGES_PER_STEP),
        in_specs=[pl.BlockSpec((None, TS, IDX_HEADS * IDX_DIM), lambda b, s, pt: (b, 0, qi_blk)),
                  pl.BlockSpec((None, TS, LANES), lambda b, s, pt: (b, 0, tail_blk)),
                  pl.BlockSpec((None, LANES, IDX_DIM), lambda b, s, pt: (b, 0, 0))]
                 + [page_spec(p) for p in range(PAGES_PER_STEP)],
        out_specs=pl.BlockSpec((None, nt, TS, LANES), lambda b, s, pt: (b, 0, 0, 0)),
        scratch_shapes=[pltpu.VMEM((nt, TS, LANES), I32), pltpu.VMEM((TS, 1), I32)])
    return pl.pallas_call(
        functools.partial(_a1s_kernel, topk=topk, past=n_pages * page),
        grid_spec=grid_spec,
        out_shape=jax.ShapeDtypeStruct((DB, nt, TS, LANES), F32),
        compiler_params=_params(("parallel", "arbitrary")),
        name="a1_sample",
    )(page_table, proj_s, proj_s, knew_bf, *([cache_kidx] * PAGES_PER_STEP))


def _a2s_kernel(pt_ref, q_ref, mask_ref, tzs_ref, knew_ref, vnew_ref, *rest, past):
    k_pages = rest[:PAGES_PER_STEP]
    v_pages = rest[PAGES_PER_STEP:2 * PAGES_PER_STEP]
    o_ref, m_ref, l_ref, acc_ref = rest[2 * PAGES_PER_STEP:]
    s = pl.program_id(1)
    TS = q_ref.shape[0]
    n_slab = tzs_ref.shape[1]

    @pl.when(s == 0)
    def _():
        m_ref[...] = jnp.full(m_ref.shape, NEG, F32)
        l_ref[...] = jnp.zeros(l_ref.shape, F32)
        acc_ref[...] = jnp.zeros(acc_ref.shape, F32)

    qg = [(jnp.concatenate([q_ref[:, (g * GROUP + r) * HEAD_DIM:(g * GROUP + r + 1) * HEAD_DIM]
                            for r in range(GROUP)], axis=0) * QSCALE).astype(BF16)
          for g in range(KV_HEADS)]

    def update(g, kt, vt, tiles):
        slabs = [jnp.minimum((past // LANES) - t, n_slab - 1) for t in tiles]
        mk = jnp.concatenate([mask_ref[t] for t in tiles], axis=1)
        mk = jnp.concatenate([mk] * GROUP, axis=0)
        bias = jnp.concatenate(
            [jnp.concatenate([tzs_ref[g * GROUP + r, sl] for r in range(GROUP)], axis=0) for sl in slabs],
            axis=1)
        sc = _nt(qg[g], kt) + bias + mk
        m_old = m_ref[g]
        m_new = jnp.maximum(m_old, jnp.max(sc, axis=1, keepdims=True))
        alpha = jnp.exp2(m_old - m_new)
        p = jnp.exp2(sc - m_new)
        l_ref[g] = alpha * l_ref[g] + jnp.sum(p, axis=1, keepdims=True)
        acc_ref[g] = alpha * acc_ref[g] + jnp.dot(p.astype(BF16), vt, preferred_element_type=F32)
        m_ref[g] = m_new

    tiles = [s * PAGES_PER_STEP + p for p in range(PAGES_PER_STEP)]
    for g in range(KV_HEADS):
        rows = pl.ds(g, LANES, stride=KV_HEADS)
        kt = jnp.concatenate([kp[rows, :] for kp in k_pages], axis=0).astype(BF16)
        vt = jnp.concatenate([vp[rows, :] for vp in v_pages], axis=0).astype(BF16)
        update(g, kt, vt, tiles)

    @pl.when(s == pl.num_programs(1) - 1)
    def _():
        for g in range(KV_HEADS):
            update(g, knew_ref[:, g * HEAD_DIM:(g + 1) * HEAD_DIM].astype(BF16),
                   vnew_ref[:, g * HEAD_DIM:(g + 1) * HEAD_DIM].astype(BF16), [past // LANES])
        for g in range(KV_HEADS):
            res = acc_ref[g] / l_ref[g]
            for r in range(GROUP):
                h = g * GROUP + r
                o_ref[:, h * HEAD_DIM:(h + 1) * HEAD_DIM] = res[r * TS:(r + 1) * TS, :].astype(o_ref.dtype)


def _a2_sample(page_table, proj_s, mask_s, tzs, knew, vnew, cache_k, cache_v, layer):
    DB, TS, _ = proj_s.shape
    n_pages = page_table.shape[1]
    page = cache_k.shape[2] // KV_HEADS
    kvw = KV_HEADS * HEAD_DIM
    nt = n_pages + 1

    def page_spec(p):
        return pl.BlockSpec((None, None, page * KV_HEADS, HEAD_DIM),
                            lambda b, s, pt: (layer, pt[b, s * PAGES_PER_STEP + p], 0, 0))

    grid_spec = pltpu.PrefetchScalarGridSpec(
        num_scalar_prefetch=1,
        grid=(DB, n_pages // PAGES_PER_STEP),
        in_specs=[pl.BlockSpec((None, TS, N_HEADS * HEAD_DIM), lambda b, s, pt: (b, 0, 0)),
                  pl.BlockSpec((None, nt, TS, LANES), lambda b, s, pt: (b, 0, 0, 0)),
                  pl.BlockSpec(tzs.shape, lambda b, s, pt: (0, 0, 0, 0)),
                  pl.BlockSpec((None, LANES, kvw), lambda b, s, pt: (b, 0, 0)),
                  pl.BlockSpec((None, LANES, kvw), lambda b, s, pt: (b, 0, 0))]
                 + [page_spec(p) for p in range(PAGES_PER_STEP)] * 2,
        out_specs=pl.BlockSpec((None, TS, N_HEADS * HEAD_DIM), lambda b, s, pt: (b, 0, 0)),
        scratch_shapes=[pltpu.VMEM((KV_HEADS, GROUP * TS, 1), F32),
                        pltpu.VMEM((KV_HEADS, GROUP * TS, 1), F32),
                        pltpu.VMEM((KV_HEADS, GROUP * TS, HEAD_DIM), F32)])
    return pl.pallas_call(
        functools.partial(_a2s_kernel, past=n_pages * page),
        grid_spec=grid_spec,
        out_shape=jax.ShapeDtypeStruct((DB, TS, N_HEADS * HEAD_DIM), BF16),
        compiler_params=_params(("parallel", "arbitrary")),
        name="a2_sample",
    )(page_table, proj_s, mask_s, tzs, knew, vnew,
      *([cache_k] * PAGES_PER_STEP), *([cache_v] * PAGES_PER_STEP))


BQ = 512


def _bp_kernel(*refs):
    n_q = len(B_GROUPS) * GROUP
    q_refs = refs[:n_q]
    k_ref, v_ref, band_ref, o_ref, m_ref, l_ref, acc_ref, og_ref, lseg_ref = refs[n_q:]
    step = pl.program_id(2)
    t0 = step * BQ
    first = t0 == 0
    whole = [n for n, (_, dil) in enumerate(B_GROUPS) if dil * QB > BQ]
    assert len(whole) == 1 and B_GROUPS[whole[0]][1] * QB == k_ref.shape[0]

    def band(n, half):
        return jnp.concatenate([band_ref[r, n, half] for r in range(GROUP)], axis=0)

    @pl.when(first)
    def _():
        n = whole[0]
        dil = B_GROUPS[n][1]
        for rho in range(dil):
            cls = pl.ds(rho, QB, stride=dil)
            qs = (jnp.concatenate([qr[cls, :] for qr in q_refs[n * GROUP:(n + 1) * GROUP]], axis=0)
                  * QSCALE).astype(BF16)
            s = _nt(qs, k_ref[cls, :].astype(BF16)) + band(n, 1)
            m = jnp.max(s, axis=1, keepdims=True)
            p = jnp.exp2(s - m)
            l = jnp.sum(p, axis=1, keepdims=True)
            out = jnp.dot(p.astype(BF16), v_ref[cls, :].astype(BF16), preferred_element_type=F32) / l
            lse = jnp.broadcast_to(m + jnp.log2(l), out.shape)
            for r in range(GROUP):
                og_ref[r, cls, :] = out[r * QB:(r + 1) * QB]
                lseg_ref[r, cls, :] = lse[r * QB:(r + 1) * QB]

    rows_step = pl.ds(pl.multiple_of(t0, BQ), BQ)
    for r in range(GROUP):
        m_ref[r] = lseg_ref[r, rows_step, :]
        l_ref[r] = jnp.ones(l_ref.shape[1:], F32)
        acc_ref[r] = og_ref[r, rows_step, :]

    def stacked(ref, rows):
        return jnp.concatenate([ref[r, rows, :] for r in range(GROUP)], axis=0)

    def update(q_ref, rows_q, nrow, key_rows, biases):
        qs = (jnp.concatenate([qr[rows_q, :] for qr in q_ref], axis=0) * QSCALE).astype(BF16)
        kt = jnp.concatenate([k_ref[kr, :] for kr in key_rows], axis=0).astype(BF16)
        vt = jnp.concatenate([v_ref[kr, :] for kr in key_rows], axis=0).astype(BF16)
        s = _nt(qs, kt) + jnp.concatenate(biases, axis=1)
        m_old = stacked(m_ref, rows_q)
        m_new = jnp.maximum(m_old, jnp.max(s, axis=1, keepdims=True))
        alpha = jnp.exp2(m_old - m_new)
        p = jnp.exp2(s - jnp.concatenate([m_new] * len(key_rows), axis=1))
        l_new = alpha * stacked(l_ref, rows_q) + jnp.sum(p, axis=1, keepdims=True)
        acc = alpha * stacked(acc_ref, rows_q) + jnp.dot(p.astype(BF16), vt, preferred_element_type=F32)
        for r in range(GROUP):
            m_ref[r, rows_q, :] = m_new[r * nrow:(r + 1) * nrow]
            l_ref[r, rows_q, :] = l_new[r * nrow:(r + 1) * nrow]
            acc_ref[r, rows_q, :] = acc[r * nrow:(r + 1) * nrow]

    for n, (window, dil) in enumerate(B_GROUPS):
        assert window // dil == QB
        span = dil * QB
        if n in whole:
            continue
        q_ref = q_refs[n * GROUP:(n + 1) * GROUP]
        for rho in range(dil):
            for mt in range(BQ // span):
                def rows(start):
                    return pl.ds(start, QB, stride=dil) if dil > 1 else pl.ds(start, QB)
                off = rho + span * mt
                prev = jnp.maximum(t0 + off - span, rho) if mt == 0 else t0 + off - span
                bias_a = band(n, 0)
                if mt == 0:
                    bias_a = bias_a + jnp.where(first, NEG, 0.0)
                update(q_ref, rows(off), QB, [rows(prev), rows(t0 + off)], [bias_a, band(n, 1)])

    for r in range(GROUP):
        o_ref[:, r * HEAD_DIM:(r + 1) * HEAD_DIM] = (acc_ref[r] / l_ref[r]).astype(o_ref.dtype)


def _b_prompt(qb, kv, band):
    B, T, _ = qb.shape
    nh = N_HEADS
    hd = HEAD_DIM
    gw = GROUP * hd
    assert T % BQ == 0

    def q_spec(n, r):
        whole =B_GROUPS[n][1] * QB > BQ
        if whole:
            return pl.BlockSpec((None, T, hd), lambda b, g, c: (b, 0, (n * KV_HEADS + g) * GROUP + r))
        return pl.BlockSpec((None, BQ, hd), lambda b, g, c: (b, c, (n * KV_HEADS + g) * GROUP + r))

    q_specs = [q_spec(n, r) for n in range(len(B_GROUPS)) for r in range(GROUP)]
    return pl.pallas_call(
        _bp_kernel,
        grid=(B, KV_HEADS, T // BQ),
        in_specs=q_specs + [
                  pl.BlockSpec((None, T, hd), lambda b, g, c: (b, 0, g)),
                  pl.BlockSpec((None, T, hd), lambda b, g, c: (b, 0, KV_HEADS + g)),
                  pl.BlockSpec((GROUP, 3, 2, QB, LANES), lambda b, g, c: (g, 0, 0, 0, 0))],
        out_specs=pl.BlockSpec((None, BQ, gw), lambda b, g, c: (b, c, g)),
        out_shape=jax.ShapeDtypeStruct((B, T, nh * hd), BF16),
        scratch_shapes=[pltpu.VMEM((GROUP, BQ, LANES), F32), pltpu.VMEM((GROUP, BQ, LANES), F32),
                        pltpu.VMEM((GROUP, BQ, hd), F32),
                        pltpu.VMEM((GROUP, T, hd), F32), pltpu.VMEM((GROUP, T, LANES), F32)],
        compiler_params=_params(("parallel", "parallel", "arbitrary")),
        name="b_prompt",
    )(*([qb] * len(q_specs)), kv, kv, band)


def _bs_kernel(q0_ref, q1_ref, q2_ref, kc_ref, vc_ref, kn_ref, vn_ref, tab_ref, o_ref):
    TS = q0_ref.shape[0]
    W = kc_ref.shape[0] // KV_HEADS
    rows = pl.ds(pl.program_id(1), W, stride=KV_HEADS)
    kc = kc_ref[rows, :].astype(BF16)
    vc = vc_ref[rows, :].astype(BF16)
    kn = kn_ref[...].astype(BF16)
    vn = vn_ref[...].astype(BF16)
    logits = []
    for n in range(len(B_GROUPS)):
        q_ref = (q0_ref, q1_ref, q2_ref)[n]
        q = (jnp.concatenate([q_ref[:, r * HEAD_DIM:(r + 1) * HEAD_DIM] for r in range(GROUP)],
                             axis=0) * QSCALE).astype(BF16)
        logits.append((_nt(q, kc) + tab_ref[n, :, :W], _nt(q, kn) + tab_ref[n, :, W:]))
    m = functools.reduce(jnp.maximum,
                         [jnp.max(x, axis=1, keepdims=True) for pair in logits for x in pair])
    l = jnp.zeros((GROUP * TS, 1), F32)
    acc = jnp.zeros((GROUP * TS, HEAD_DIM), F32)
    for sc_c, sc_n in logits:
        pc = jnp.exp2(sc_c - m)
        pn = jnp.exp2(sc_n - m)
        l = l + jnp.sum(pc, axis=1, keepdims=True) + jnp.sum(pn, axis=1, keepdims=True)
        acc = (acc + jnp.dot(pc.astype(BF16), vc, preferred_element_type=F32)
               + jnp.dot(pn.astype(BF16), vn, preferred_element_type=F32))
    res = acc / l
    for r in range(GROUP):
        o_ref[:, r * HEAD_DIM:(r + 1) * HEAD_DIM] = res[r * TS:(r + 1) * TS, :].astype(o_ref.dtype)


def _b_sample(qb_s, cache_k, cache_v, kv_new, tab):
    DB, TS, NQ = qb_s.shape
    W = cache_k.shape[1] // KV_HEADS
    hd = HEAD_DIM
    return pl.pallas_call(
        _bs_kernel,
        grid=(DB, KV_HEADS),
        in_specs=[pl.BlockSpec((None, TS, GROUP * hd), lambda b, g: (b, 0, g)),
                  pl.BlockSpec((None, TS, GROUP * hd), lambda b, g: (b, 0, KV_HEADS + g)),
                  pl.BlockSpec((None, TS, GROUP * hd), lambda b, g: (b, 0, 2 * KV_HEADS + g)),
                  pl.BlockSpec((None, W * KV_HEADS, hd), lambda b, g: (b, 0, 0)),
                  pl.BlockSpec((None, W * KV_HEADS, hd), lambda b, g: (b, 0, 0)),
                  pl.BlockSpec((None, LANES, hd), lambda b, g: (b, 0, g)),
                  pl.BlockSpec((None, LANES, hd), lambda b, g: (b, 0, KV_HEADS + g)),
                  pl.BlockSpec((len(B_GROUPS), None, GROUP * TS, W + LANES), lambda b, g: (0, g, 0, 0))],
        out_specs=pl.BlockSpec((None, TS, GROUP * hd), lambda b, g: (b, 0, g)),
        out_shape=jax.ShapeDtypeStruct((DB, TS, N_HEADS * hd), BF16),
        compiler_params=_params(("parallel", "arbitrary")),
        name="b_sample",
    )(qb_s, qb_s, qb_s, cache_k, cache_v, kv_new, kv_new, tab)


FFN_TF = 512
HALO = 16


def _ffn_up_kernel(xm_ref, xh_ref, wg_ref, wu_ref, cw_ref, cb_ref, act_ref, st_ref, xe_ref, *,
                   tiles_per_batch):
    i = pl.program_id(0)
    tm = xm_ref.shape[0]

    @pl.when(pl.program_id(1) == 0)
    def _():
        first = (i % tiles_per_batch) == 0
        xh = xh_ref[...]
        xe_ref[0:HALO, :] = jnp.where(first, jnp.zeros_like(xh), xh)
        xe_ref[HALO:, :] = xm_ref[...]

    gate = jnp.dot(xe_ref[...], wg_ref[...], preferred_element_type=F32)
    up = jnp.dot(xm_ref[...], wu_ref[...], preferred_element_type=F32)
    cw = cw_ref[...]
    conv = cb_ref[...] + ((gate[HALO - 2:HALO - 2 + tm] * cw[0:1] + gate[HALO - 1:HALO - 1 + tm] * cw[1:2])
                          + gate[HALO:] * cw[2:3])
    act_ref[...] = (jax.nn.silu(conv) * up).astype(act_ref.dtype)
    st_ref[...] = gate[HALO + tm - (CONV_WIDTH - 1):, :]


def _ffn_up(x_bf, w_g_bf, w_u_bf, conv_w, conv_b, layer, T):
    M, D = x_bf.shape
    F = conv_w.shape[2]
    tm = 1024
    assert T % tm == 0 and M % T == 0 and F % FFN_TF == 0 and tm % HALO == 0
    nf = F // FFN_TF
    hb = tm // HALO
    return pl.pallas_call(
        functools.partial(_ffn_up_kernel, tiles_per_batch=T // tm),
        grid=(M // tm, nf),
        in_specs=[pl.BlockSpec((tm, D), lambda i, j: (i, 0)),
                  pl.BlockSpec((HALO, D), lambda i, j: (jnp.maximum(i * hb - 1, 0), 0)),
                  pl.BlockSpec((None, D, FFN_TF), lambda i, j: (layer, 0, j)),
                  pl.BlockSpec((None, D, FFN_TF), lambda i, j: (layer, 0, j)),
                  pl.BlockSpec((None, CONV_WIDTH, FFN_TF), lambda i, j: (layer, 0, j)),
                  pl.BlockSpec((None, 1, FFN_TF), lambda i, j: (layer, 0, j))],
        out_specs=[pl.BlockSpec((tm, FFN_TF), lambda i, j: (i, j)),
                   pl.BlockSpec((None, CONV_WIDTH - 1, FFN_TF), lambda i, j: (i, 0, j))],
        out_shape=[jax.ShapeDtypeStruct((M, F), BF16),
                   jax.ShapeDtypeStruct((M // tm, CONV_WIDTH - 1, F), F32)],
        scratch_shapes=[pltpu.VMEM((HALO + tm, D), BF16)],
        compiler_params=_params(("parallel", "arbitrary")),
        name="ffn_up",
    )(x_bf, x_bf, w_g_bf, w_u_bf, conv_w, conv_b)


def _mm_ln_kernel(a_ref, w_ref, x_ref, g_ref, b_ref, h_ref, hb_ref, acc_ref, *, alpha):
    k = pl.program_id(1)

    @pl.when(k == 0)
    def _():
        acc_ref[...] = jnp.zeros_like(acc_ref)

    acc_ref[...] += jnp.dot(a_ref[...], w_ref[...], preferred_element_type=F32)

    @pl.when(k == pl.num_programs(1) - 1)
    def _():
        y = alpha * x_ref[...] + acc_ref[...]
        mu = jnp.mean(y, axis=-1, keepdims=True)
        d = y - mu
        var = jnp.mean(d * d, axis=-1, keepdims=True)
        h = d * lax.rsqrt(var + LN_EPS) * g_ref[...] + b_ref[...]
        h_ref[...] = h
        hb_ref[...] = h.astype(hb_ref.dtype)


def _mm_ln(a_bf, w_bf, layer, x, g, b, alpha):
    M, K = a_bf.shape
    D = w_bf.shape[2]
    tm = min(512, M)
    tk = K // 4 if K > 2048 else K // 2
    assert M % tm == 0 and K % tk == 0 and tk % LANES == 0
    return pl.pallas_call(
        functools.partial(_mm_ln_kernel, alpha=alpha),
        grid=(M // tm, K // tk),
        in_specs=[pl.BlockSpec((tm, tk), lambda i, k: (i, k)),
                  pl.BlockSpec((None, tk, D), lambda i, k: (layer, k, 0)),
                  pl.BlockSpec((tm, D), lambda i, k: (i, 0)),
                  pl.BlockSpec((1, D), lambda i, k: (0, 0)),
                  pl.BlockSpec((1, D), lambda i, k: (0, 0))],
        out_specs=[pl.BlockSpec((tm, D), lambda i, k: (i, 0)),
                   pl.BlockSpec((tm, D), lambda i, k: (i, 0))],
        out_shape=[jax.ShapeDtypeStruct((M, D), F32), jax.ShapeDtypeStruct((M, D), BF16)],
        scratch_shapes=[pltpu.VMEM((tm, D), F32)],
        compiler_params=_params(("parallel", "arbitrary")),
        name="mm_ln",
    )(a_bf, w_bf, x, g.reshape(1, D), b.reshape(1, D))


def _layer_norm(x, g, b):
    mu = x.mean(-1, keepdims=True)
    var = jnp.square(x - mu).mean(-1, keepdims=True)
    return (x - mu) * lax.rsqrt(var + LN_EPS) * g + b


def _pick_tm(M):
    for tm in (1024, 512, 256, 128, 64, 32, 16):
        if M % tm == 0:
            return tm
    raise ValueError(M)


def _pick_tn(N):
    for tn in (512, 384, 256, 128):
        if N % tn == 0:
            return tn
    raise ValueError(N)


def _proj(x_bf, w_bf, layer=0):
    return _mm(x_bf, w_bf, layer, _pick_tm(x_bf.shape[0]), _pick_tn(w_bf.shape[2]))


def _ffn_act_sample(hs_bf, state, w_g_bf, w_u_bf, conv_w, conv_b, layer, DB, TS):
    conv_w, conv_b = conv_w[layer], conv_b[layer]
    F = conv_w.shape[1]
    gate = _proj(hs_bf, w_g_bf, layer).reshape(DB, TS, F)
    up = _proj(hs_bf, w_u_bf, layer).reshape(DB, TS, F)
    ext = jnp.concatenate([state, gate], axis=1)
    conv = conv_b + sum(ext[:, j:j + TS] * conv_w[j] for j in range(CONV_WIDTH))
    act = (jax.nn.silu(conv) * up).astype(BF16).reshape(DB * TS, F)
    return act, ext[:, ext.shape[1] - (CONV_WIDTH - 1):]


def _table_indices(T, W, TS):
    U = SAT_DIST + TZ_OFF + QB
    assert T <= T5_MAX_DISTANCE
    u = np.arange(U)[:, None]
    c = np.arange(LANES)[None, :]
    tz_idx = _t5_bucket(jnp.asarray(np.maximum(u - c - TZ_OFF, 0), I32))
    a = np.arange(QB)[:, None]
    band = []
    for window, dil in B_GROUPS:
        for off in (QB, 0):
            d = a + off - c
            ok = (d >= 0) & (d <= window // dil)
            band.append(jnp.where(jnp.asarray(ok), _t5_bucket(jnp.asarray(np.maximum(d, 0) * dil, I32)),
                                  NUM_BUCKETS))
    band_idx = jnp.concatenate(band, axis=0)
    nkt = W // LANES + 1
    t = np.arange(TS)[None, :, None]
    key = (np.arange(nkt)[:, None, None] * LANES + np.arange(LANES)[None, None, :])
    d = W + t - key
    tabs = []
    for window, dil in B_GROUPS:
        ok = (d >= 0) & (d % dil == 0) & (d <= window)
        tabs.append(jnp.where(jnp.asarray(ok), _t5_bucket(jnp.asarray(np.maximum(d, 0), I32)), NUM_BUCKETS))
    samp_idx = jnp.concatenate(tabs, axis=0).reshape(-1, LANES)
    rows = [tz_idx, band_idx, samp_idx]
    total = sum(r.shape[0] for r in rows)
    pad = (-total) % 128
    if pad:
        rows.append(jnp.full((pad, LANES), NUM_BUCKETS, I32))
    return jnp.concatenate(rows, axis=0), U, nkt


def kernel(x_prompt, x_sample, cache_k_a, cache_v_a, cache_kidx_a, cache_k_b, cache_v_b, state_ffn, page_table, a_w_in, a_w_o, a_kn_g, a_kn_b, b_w_kv, b_w_q, b_w_o, ffn_w_up, ffn_conv_w, ffn_conv_b, ffn_w_down, ln_g, ln_b, rel_bias):
    B, T, D = x_prompt.shape
    DB, TS, _ = x_sample.shape
    depth = ffn_w_up.shape[0]
    n_a = a_w_in.shape[0]
    d_ff = ffn_w_down.shape[1]
    W = cache_k_b.shape[1]
    n_pages = page_table.shape[1]
    page = cache_k_a.shape[2]
    past = n_pages * page
    alpha = (2 * depth) ** 0.25
    kvw = KV_HEADS * HEAD_DIM
    a_q = N_HEADS * HEAD_DIM
    a_in = a_w_in.shape[2]
    np_cols = ((a_in + 511) // 512) * 512
    ki0 = a_q + 2 * kvw + IDX_HEADS * IDX_DIM

    idx_all, U, nkt = _table_indices(T, W, TS)
    tabs = _bias_tables(rel_bias * LOG2E, idx_all)
    tz = tabs[:, :U]
    band = tabs[:, U:U + 6 * QB].reshape(N_HEADS, 3, 2, QB, LANES)
    samp = tabs[:, U + 6 * QB:U + 6 * QB + 3 * nkt * TS].reshape(KV_HEADS, GROUP, 3, nkt, TS, LANES)
    samp = samp.transpose(2, 0, 1, 4, 3, 5).reshape(3, KV_HEADS, GROUP * TS, nkt * LANES)
    n_slab = SAT_DIST // LANES + 1
    tzs = tz[:, TZ_OFF:TZ_OFF + n_slab * LANES].reshape(N_HEADS, n_slab, LANES, LANES)[:, :, :TS, :]

    cache_k_a2 = cache_k_a.reshape(n_a, -1, page * KV_HEADS, HEAD_DIM)
    cache_v_a2 = cache_v_a.reshape(n_a, -1, page * KV_HEADS, HEAD_DIM)
    cache_k_b2 = cache_k_b.reshape(DB, W * KV_HEADS, HEAD_DIM)
    cache_v_b2 = cache_v_b.reshape(DB, W * KV_HEADS, HEAD_DIM)
    cache_kidx_t = jnp.swapaxes(cache_kidx_a, 2, 3)

    def pad_rows(x, n):
        return jnp.pad(x, ((0, 0), (0, n - x.shape[1]), (0, 0)))

    f_pad = ((d_ff + FFN_TF - 1) // FFN_TF) * FFN_TF
    pad_f = f_pad - d_ff
    w_g_all = jnp.pad(ffn_w_up[:, :, :d_ff], ((0, 0), (0, 0), (0, pad_f))).astype(BF16)
    w_u_all = jnp.pad(ffn_w_up[:, :, d_ff:], ((0, 0), (0, 0), (0, pad_f))).astype(BF16)
    w_down_all = jnp.pad(ffn_w_down, ((0, 0), (0, pad_f), (0, 0))).astype(BF16)
    conv_w_all = jnp.pad(ffn_conv_w, ((0, 0), (0, 0), (0, pad_f)))
    conv_b_all = jnp.pad(ffn_conv_b, ((0, 0), (0, pad_f))).reshape(depth, 1, f_pad)
    w_in_all = jnp.pad(a_w_in, ((0, 0), (0, 0), (0, np_cols - a_in))).astype(BF16)
    w_o_a, w_o_b = a_w_o.astype(BF16), b_w_o.astype(BF16)
    w_q_all = b_w_q.astype(BF16)
    w_kv_bf = b_w_kv.astype(BF16)[None]

    hp = x_prompt.reshape(B * T, D)
    hs = x_sample.reshape(DB * TS, D)
    hp_bf, hs_bf = hp.astype(BF16), hs.astype(BF16)
    ka_p, va_p, kia_p, ka_s, va_s, kia_s, ffn_p, ffn_s = [], [], [], [], [], [], [], []
    for layer in range(depth):
        if layer < n_a:
            a = layer
            w_o_bf, w_o_layer = w_o_a, a
            proj = _proj(hp_bf, w_in_all, a).reshape(B, T, np_cols)
            k = proj[..., a_q:a_q + kvw].reshape(B, T, KV_HEADS, HEAD_DIM)
            v = proj[..., a_q + kvw:a_q + 2 * kvw].reshape(B, T, KV_HEADS, HEAD_DIM)
            ki = _layer_norm(proj[..., ki0:ki0 + IDX_DIM], a_kn_g[a], a_kn_b[a])
            ka_p.append(k); va_p.append(v); kia_p.append(ki)
            mask = _a1_prompt(ki.astype(BF16), proj, min(TOPK_MAX, T // 4))
            o = _a2_prompt(proj, mask, tz).reshape(B * T, a_q)
            proj_s = _proj(hs_bf, w_in_all, a).reshape(DB, TS, np_cols)
            k_s = proj_s[..., a_q:a_q + kvw]
            v_s = proj_s[..., a_q + kvw:a_q + 2 * kvw]
            ki_s = _layer_norm(proj_s[..., ki0:ki0 + IDX_DIM], a_kn_g[a], a_kn_b[a])
            ka_s.append(k_s.reshape(DB, TS, KV_HEADS, HEAD_DIM))
            va_s.append(v_s.reshape(DB, TS, KV_HEADS, HEAD_DIM))
            kia_s.append(ki_s)
            mask_s = _a1_sample(page_table, proj_s, pad_rows(ki_s, LANES).astype(BF16), cache_kidx_t, a,
                                min(TOPK_MAX, (past + TS) // 4))
            o_s = _a2_sample(page_table, proj_s, mask_s, tzs, pad_rows(k_s, LANES), pad_rows(v_s, LANES),
                             cache_k_a2, cache_v_a2, a).reshape(DB * TS, a_q)
        else:
            if layer == n_a:
                kv_p = _proj(hp_bf, w_kv_bf).reshape(B, T, 2 * kvw)
                kv_s = _proj(hs_bf, w_kv_bf).reshape(DB, TS, 2 * kvw)
                kv_s_pad = pad_rows(kv_s, LANES)
            bl = layer - n_a
            w_o_bf, w_o_layer = w_o_b, bl
            o = _b_prompt(_proj(hp_bf, w_q_all, bl).reshape(B, T, -1), kv_p, band).reshape(B * T, a_q)
            o_s = _b_sample(_proj(hs_bf, w_q_all, bl).reshape(DB, TS, -1), cache_k_b2, cache_v_b2, kv_s_pad,
                            samp).reshape(DB * TS, a_q)
        hp, hp_bf = _mm_ln(o, w_o_bf, w_o_layer, hp, ln_g[layer, 0], ln_b[layer, 0], alpha)
        hs, hs_bf = _mm_ln(o_s, w_o_bf, w_o_layer, hs, ln_g[layer, 0], ln_b[layer, 0], alpha)
        act, st = _ffn_up(hp_bf, w_g_all, w_u_all, conv_w_all, conv_b_all, layer, T)
        ffn_p.append(st.reshape(B, -1, CONV_WIDTH - 1, f_pad)[:, -1, :, :d_ff])
        act_s, st_s = _ffn_act_sample(hs_bf, jnp.pad(state_ffn[layer], ((0, 0), (0, 0), (0, pad_f))),
                                      w_g_all, w_u_all, conv_w_all, conv_b_all, layer, DB, TS)
        ffn_s.append(st_s[..., :d_ff])
        hp, hp_bf = _mm_ln(act, w_down_all, layer, hp, ln_g[layer, 1], ln_b[layer, 1], alpha)
        hs, hs_bf = _mm_ln(act_s, w_down_all, layer, hs, ln_g[layer, 1], ln_b[layer, 1], alpha)
    keep = min(max(w for w, _ in B_GROUPS), T)
    kb_p = kv_p[..., :kvw].reshape(B, T, KV_HEADS, HEAD_DIM)
    vb_p = kv_p[..., kvw:].reshape(B, T, KV_HEADS, HEAD_DIM)
    kb_s = kv_s[..., :kvw].reshape(DB, TS, KV_HEADS, HEAD_DIM)
    vb_s = kv_s[..., kvw:].reshape(DB, TS, KV_HEADS, HEAD_DIM)
    return (hp.reshape(B, T, D), hs.reshape(DB, TS, D), jnp.stack(ka_p), jnp.stack(va_p), jnp.stack(kia_p),
            jnp.stack(ka_s), jnp.stack(va_s), jnp.stack(kia_s), kb_p[:, T - keep:], vb_p[:, T - keep:],
            kb_s, vb_s, jnp.stack(ffn_p), jnp.stack(ffn_s))
```

```python
import functools
import math

import numpy as np
import jax
import jax.numpy as jnp
from jax import lax
from jax.experimental import pallas as pl
from jax.experimental.pallas import tpu as pltpu

F32 = jnp.float32
BF16 = jnp.bfloat16
I32 = jnp.int32

HEAD_DIM = 128
N_HEADS = 16
KV_HEADS = 4
GROUP = N_HEADS // KV_HEADS
IDX_HEADS = 16
IDX_DIM = 64
TOPK_MAX = 256
B_GROUPS = ((128, 1), (512, 4), (2048, 16))
NUM_BUCKETS = 32
T5_MAX_DISTANCE = 2048
LN_EPS = 1e-5
NEG = -1e30
CONV_WIDTH = 3

LANES = 128
QB = 128
KC = 512
TZ_OFF = KC - QB
SAT_DIST = T5_MAX_DISTANCE + QB
INT_MIN = -2 ** 31
KEY_NEG_INF = INT_MIN + 0x7FFFFF
VMEM_LIMIT = 56 * 1024 * 1024

LOG2E = 1.4426950408889634
QSCALE = HEAD_DIM ** -0.5 * LOG2E

NT_DIMS = (((1,), (1,)), ((), ()))


def _nt(a, b):
    return lax.dot_general(a, b, NT_DIMS, preferred_element_type=F32)


def _params(sem):
    return pltpu.CompilerParams(dimension_semantics=sem, vmem_limit_bytes=VMEM_LIMIT)


def _mm_kernel(x_ref, w_ref, o_ref):
    o_ref[...] = jnp.dot(x_ref[...], w_ref[...], preferred_element_type=F32).astype(o_ref.dtype)


def _mm(x, w, layer, tm, tn, out_dtype=F32):
    M, K = x.shape
    N = w.shape[2]
    assert M % tm == 0 and N % tn == 0, (M, N, tm, tn)
    return pl.pallas_call(
        _mm_kernel,
        grid=(M // tm, N // tn),
        in_specs=[pl.BlockSpec((tm, K), lambda i, j: (i, 0)),
                  pl.BlockSpec((None, K, tn), lambda i, j: (layer, 0, j))],
        out_specs=pl.BlockSpec((tm, tn), lambda i, j: (i, j)),
        out_shape=jax.ShapeDtypeStruct((M, N), out_dtype),
        compiler_params=_params(("parallel", "arbitrary")),
        name="mm",
    )(x, w)


def _t5_bucket(dist):
    dist = jnp.maximum(dist, 0)
    exact = NUM_BUCKETS // 2
    far = exact + (jnp.log(jnp.maximum(dist, 1).astype(F32) / exact)
                   / math.log(T5_MAX_DISTANCE / exact) * (NUM_BUCKETS - exact)).astype(I32)
    return jnp.where(dist < exact, dist, jnp.minimum(far, NUM_BUCKETS - 1))


def _table_kernel(rb_ref, idx_ref, o_ref):
    slab = 16

    def body(i, carry):
        r0 = pl.multiple_of(i * slab, slab)
        idx = idx_ref[pl.ds(r0, slab), :]
        accs = [jnp.full(idx.shape, NEG, F32)] * N_HEADS
        for k in range(NUM_BUCKETS):
            hit = idx == k
            accs = [jnp.where(hit, rb_ref[k, h], accs[h]) for h in range(N_HEADS)]
        for h in range(N_HEADS):
            o_ref[h, pl.ds(r0, slab), :] = accs[h]
        return carry

    lax.fori_loop(0, idx_ref.shape[0] // slab, body, 0)


def _bias_tables(rel_bias, idx):
    R = idx.shape[0]
    tr = 128
    assert R % tr == 0
    return pl.pallas_call(
        _table_kernel,
        grid=(R // tr,),
        in_specs=[pl.BlockSpec(memory_space=pltpu.SMEM),
                  pl.BlockSpec((tr, LANES), lambda i: (i, 0))],
        out_specs=pl.BlockSpec((N_HEADS, tr, LANES), lambda i: (0, i, 0)),
        out_shape=jax.ShapeDtypeStruct((N_HEADS, R, LANES), F32),
        compiler_params=_params(("arbitrary",)),
        name="bias_tables",
    )(rel_bias, idx)


def _sortable(x):
    bits = lax.bitcast_convert_type(x, I32)
    return bits ^ ((bits >> 31) & jnp.int32(0x7FFFFFFF))


def _kth_largest(count_ge, shape, k):
    def body(t, ans):
        cand_u = ans | (jnp.int32(1) << (31 - t))
        cnt = count_ge(cand_u ^ jnp.int32(INT_MIN))
        return jnp.where(cnt >= k, cand_u, ans)
    ans = lax.fori_loop(0, 32, body, jnp.zeros(shape, I32))
    return ans ^ jnp.int32(INT_MIN)


def _tie_bound(count_eq_below, shape, need, nbits):
    def body(t, ans):
        cand = ans | (jnp.int32(1) << (nbits - 1 - t))
        return jnp.where(count_eq_below(cand) < need, cand, ans)
    return lax.fori_loop(0, nbits, body, jnp.zeros(shape, I32))


A1_TILES = 4


def _a1p_kernel(kin_ref, qi_ref, tail_ref, mask_ref, st_ref, jb_ref, *, topk):
    i = pl.program_id(1)
    T = kin_ref.shape[0]
    NQ = A1_TILES * QB
    KS = KC // A1_TILES
    nchunk = T // KC
    nbits = (T - 1).bit_length()
    tail_t = jnp.concatenate([tail_ref[t * QB:(t + 1) * QB, :].T for t in range(A1_TILES)], axis=1)
    wi_t = tail_t[IDX_DIM:IDX_DIM + IDX_HEADS, :] * (IDX_HEADS ** -0.5 * IDX_DIM ** -0.5)
    qpos = i * NQ + lax.broadcasted_iota(I32, (KC, NQ), 1)
    row = lax.broadcasted_iota(I32, (KC, NQ), 0)
    qpos_s = i * NQ + lax.broadcasted_iota(I32, (KS, NQ), 1)
    row_s = lax.broadcasted_iota(I32, (KS, NQ), 0)

    nj = (i * NQ) // KC + 1

    for c in range(nchunk):
        @pl.when(c < nj)
        def _():
            for sub in range(KC // KS):
                r0 = c * KC + sub * KS
                kc = kin_ref[r0:r0 + KS, :]
                acc = jnp.zeros((KS, NQ), F32)
                for h in range(IDX_HEADS):
                    qh = qi_ref[:, h * IDX_DIM:(h + 1) * IDX_DIM].astype(BF16)
                    acc = acc + jnp.maximum(_nt(kc, qh), 0.0) * wi_t[h:h + 1, :]
                acc = jnp.where(r0 + row_s <= qpos_s, acc, -jnp.inf)
                st_ref[r0:r0 + KS, :] = _sortable(acc)

    def select(nvis):
        def count(pred):
            cnt = jnp.zeros((1, NQ), F32)
            for c in range(nvis):
                blk = st_ref[c * KC:(c + 1) * KC, :]
                cnt = cnt + jnp.sum(jnp.where(pred(blk, c * KC + row), 1.0, 0.0), axis=0, keepdims=True)
            return cnt

        thr = _kth_largest(lambda cand: count(lambda blk, _: blk >= cand), (1, NQ), float(topk))
        need = float(topk) - count(lambda blk, _: blk > thr)
        n_eq = count(lambda blk, _: blk == thr)
        excess = jnp.where((n_eq > need) & (thr != KEY_NEG_INF), 1.0, 0.0)
        jb_ref[...] = jnp.full((1, NQ), T, I32)

        @pl.when(jnp.max(excess) > 0.0)
        def _():
            jb_ref[...] = _tie_bound(
                lambda cand: count(lambda blk, kpos: jnp.where(blk == thr, kpos, T) < cand),
                (1, NQ), need, nbits)

        jb = jb_ref[...]
        for c in range(nchunk):
            if c < nvis:
                blk = st_ref[c * KC:(c + 1) * KC, :]
                kpos = c * KC + row
                rank_pos = jnp.where(blk == thr, kpos, jnp.where(blk > thr, -1, T + 1))
                sel = jnp.where(kpos <= qpos, rank_pos, T + 1) <= jb
                m_t = jnp.where(sel, 0.0, NEG)
                for t in range(A1_TILES):
                    for s4 in range(KC // QB):
                        mask_ref[t, c, :, s4 * QB:(s4 + 1) * QB] = (
                            m_t[s4 * QB:(s4 + 1) * QB, t * QB:(t + 1) * QB].T.astype(mask_ref.dtype))
            else:
                for t in range(A1_TILES):
                    mask_ref[t, c] = jnp.full((QB, KC), NEG, mask_ref.dtype)

    for nvis in range(1, nchunk + 1):
        pl.when(nj == nvis)(functools.partial(select, nvis))


def _a1_prompt(kin_bf, proj, topk):
    B, T, _ = proj.shape
    nq, nc = T // QB, T // KC
    assert topk <= KC
    assert (KC // QB) % A1_TILES == 0 and nq % A1_TILES == 0
    nqs = A1_TILES * QB
    qi_blk = (N_HEADS * HEAD_DIM + 2 * KV_HEADS * HEAD_DIM) // (IDX_HEADS * IDX_DIM)
    tail_blk = (N_HEADS * HEAD_DIM + 2 * KV_HEADS * HEAD_DIM + IDX_HEADS * IDX_DIM) // LANES
    return pl.pallas_call(
        functools.partial(_a1p_kernel, topk=topk),
        grid=(B, nq // A1_TILES),
        in_specs=[pl.BlockSpec((None, T, IDX_DIM), lambda b, i: (b, 0, 0)),
                  pl.BlockSpec((None, nqs, IDX_HEADS * IDX_DIM), lambda b, i: (b, i, qi_blk)),
                  pl.BlockSpec((None, nqs, LANES), lambda b, i: (b, i, tail_blk))],
        out_specs=pl.BlockSpec((None, A1_TILES, nc, QB, KC), lambda b, i: (b, i, 0, 0, 0)),
        out_shape=jax.ShapeDtypeStruct((B, nq, nc, QB, KC), BF16),
        scratch_shapes=[pltpu.VMEM((T, nqs), I32), pltpu.VMEM((1, nqs), I32)],
        compiler_params=_params(("parallel", "arbitrary")),
        name="a1_prompt",
    )(kin_bf, proj, proj)


def _a2p_kernel(q_ref, k_ref, v_ref, mask_ref, tz_ref, o_ref):
    i = pl.program_id(2)
    nj = (i * QB) // KC + 1
    HP = GROUP
    R = HP * QB
    heads = [list(range(c * HP, (c + 1) * HP)) for c in range(GROUP // HP)]
    qs = [(jnp.concatenate([q_ref[:, r * HEAD_DIM:(r + 1) * HEAD_DIM] for r in hs], axis=0)
           * QSCALE).astype(BF16) for hs in heads]

    def body(j, carry):
        k0 = pl.multiple_of(j * KC, KC)
        kt = k_ref[pl.ds(k0, KC), :].astype(BF16)
        vt = v_ref[pl.ds(k0, KC), :].astype(BF16)
        base = i * QB - j * KC + TZ_OFF
        mk = mask_ref[j].astype(F32)
        mk = jnp.concatenate([mk] * HP, axis=0)
        out = []
        for hs, q, (m, l, acc) in zip(heads, qs, carry):
            bias = jnp.concatenate(
                [jnp.concatenate(
                    [tz_ref[r, pl.ds(pl.multiple_of(base - QB * s4, QB), QB), :] for s4 in range(KC // QB)],
                    axis=1) for r in hs], axis=0)
            s = _nt(q, kt) + bias + mk
            m_new = jnp.maximum(m, jnp.max(s, axis=1, keepdims=True))
            alpha = jnp.exp2(m - m_new)
            p = jnp.exp2(s - m_new)
            l = alpha * l + jnp.sum(p, axis=1, keepdims=True)
            acc = alpha * acc + jnp.dot(p.astype(BF16), vt, preferred_element_type=F32)
            out.append((m_new, l, acc))
        return tuple(out)

    init = tuple((jnp.full((R, 1), NEG, F32), jnp.zeros((R, 1), F32), jnp.zeros((R, HEAD_DIM), F32))
                 for _ in heads)
    final = lax.fori_loop(0, nj, body, init)
    for hs, (m, l, acc) in zip(heads, final):
        res = acc / l
        for n, r in enumerate(hs):
            o_ref[:, r * HEAD_DIM:(r + 1) * HEAD_DIM] = res[n * QB:(n + 1) * QB, :].astype(o_ref.dtype)


def _a2_prompt(proj, mask, tz):
    B, T, _ = proj.shape
    nq, nc = T // QB, T // KC
    gw = GROUP * HEAD_DIM
    k_blk0 = N_HEADS * HEAD_DIM // HEAD_DIM
    v_blk0 = k_blk0 + KV_HEADS
    U = tz.shape[1]
    return pl.pallas_call(
        _a2p_kernel,
        grid=(KV_HEADS, B, nq),
        in_specs=[pl.BlockSpec((None, QB, gw), lambda g, b, i: (b, i, g)),
                  pl.BlockSpec((None, T, HEAD_DIM), lambda g, b, i: (b, 0, k_blk0 + g)),
                  pl.BlockSpec((None, T, HEAD_DIM), lambda g, b, i: (b, 0, v_blk0 + g)),
                  pl.BlockSpec((None, None, nc, QB, KC), lambda g, b, i: (b, i, 0, 0, 0)),
                  pl.BlockSpec((GROUP, U, LANES), lambda g, b, i: (g, 0, 0))],
        out_specs=pl.BlockSpec((None, QB, gw), lambda g, b, i: (b, i, g)),
        out_shape=jax.ShapeDtypeStruct((B, T, N_HEADS * HEAD_DIM), BF16),
        compiler_params=_params(("parallel", "parallel", "arbitrary")),
        name="a2_prompt",
    )(proj, proj, proj, mask, tz)


PAGES_PER_STEP = 32


def _a1s_kernel(pt_ref, qi_ref, tail_ref, knew_ref, *rest, topk, past):
    page_refs = rest[:PAGES_PER_STEP]
    mask_ref, st_ref, jb_ref = rest[PAGES_PER_STEP:]
    s = pl.program_id(1)
    nt, TS, _ = st_ref.shape
    L = nt * LANES
    nbits = (L - 1).bit_length()
    qi = jnp.concatenate([qi_ref[:, h * IDX_DIM:(h + 1) * IDX_DIM] for h in range(IDX_HEADS)],
                         axis=0).astype(BF16)
    w_col = jnp.concatenate([tail_ref[:, IDX_DIM + h:IDX_DIM + h + 1] for h in range(IDX_HEADS)],
                            axis=0) * (IDX_HEADS ** -0.5 * IDX_DIM ** -0.5)

    def scores(qk):
        sc = jnp.maximum(qk, 0.0) * w_col
        return jnp.sum(sc.reshape(IDX_HEADS, TS, LANES), axis=0)

    for p in range(PAGES_PER_STEP):
        qk = jnp.dot(qi, page_refs[p][...].astype(BF16), preferred_element_type=F32)
        st_ref[s * PAGES_PER_STEP + p] = _sortable(scores(qk))

    @pl.when(s == pl.num_programs(1) - 1)
    def _():
        t = lax.broadcasted_iota(I32, (TS, LANES), 0)
        c = lax.broadcasted_iota(I32, (TS, LANES), 1)
        sc = jnp.where(c <= t, scores(_nt(qi, knew_ref[...])), -jnp.inf)
        st_ref[nt - 1] = _sortable(sc)

        kpos = (lax.broadcasted_iota(I32, (nt, TS, LANES), 0) * LANES
                + lax.broadcasted_iota(I32, (nt, TS, LANES), 2))
        qpos = past + lax.broadcasted_iota(I32, (nt, TS, LANES), 1)

        def count(pred):
            per_lane = jnp.sum(jnp.where(pred(st_ref[...]), 1.0, 0.0), axis=0)
            return jnp.sum(per_lane, axis=1, keepdims=True)

        thr = _kth_largest(lambda cand: count(lambda k: k >= cand[None]), (TS, 1), float(topk))
        thr3 = thr[None]
        need = float(topk) - count(lambda k: k > thr3)
        n_eq = count(lambda k: k == thr3)
        excess = jnp.where((n_eq > need) & (thr != KEY_NEG_INF), 1.0, 0.0)
        jb_ref[...] = jnp.full((TS, 1), L, I32)

        @pl.when(jnp.max(excess) > 0.0)
        def _():
            jb_ref[...] = _tie_bound(
                lambda cand: count(lambda k: jnp.where(k == thr3, kpos, L) < cand[None]),
                (TS, 1), need, nbits)

        keys = st_ref[...]
        rank_pos = jnp.where(keys == thr3, kpos, jnp.where(keys > thr3, -1, L + 1))
        sel = jnp.where(kpos <= qpos, rank_pos, L + 1) <= jb_ref[...][None]
        mask_ref[...] = jnp.where(sel, 0.0, NEG)


def _a1_sample(page_table, proj_s, knew_bf, cache_kidx, layer, topk):
    DB, TS, _ = proj_s.shape
    n_pages = page_table.shape[1]
    page = cache_kidx.shape[3]
    assert page == LANES and n_pages % PAGES_PER_STEP == 0 and topk <= n_pages * page
    nt = n_pages + 1
    qi_blk = (N_HEADS * HEAD_DIM + 2 * KV_HEADS * HEAD_DIM) // (IDX_HEADS * IDX_DIM)
    tail_blk = (N_HEADS * HEAD_DIM + 2 * KV_HEADS * HEAD_DIM + IDX_HEADS * IDX_DIM) // LANES

    def page_spec(p):
        return pl.BlockSpec((None, None, IDX_DIM, page),
                            lambda b, s, pt: (layer, pt[b, s * PAGES_PER_STEP + p], 0, 0))

    grid_spec = pltpu.PrefetchScalarGridSpec(
        num_scalar_prefetch=1,
        grid=(DB, n_pages // PAGES_PER_STEP),
        in_specs=[pl.BlockSpec((None, TS, IDX_HEADS * IDX_DIM), lambda b, s, pt: (b, 0, qi_blk)),
                  pl.BlockSpec((None, TS, LANES), lambda b, s, pt: (b, 0, tail_blk)),
                  pl.BlockSpec((None, LANES, IDX_DIM), lambda b, s, pt: (b, 0, 0))]
                 + [page_spec(p) for p in range(PAGES_PER_STEP)],
        out_specs=pl.BlockSpec((None, nt, TS, LANES), lambda b, s, pt: (b, 0, 0, 0)),
        scratch_shapes=[pltpu.VMEM((nt, TS, LANES), I32), pltpu.VMEM((TS, 1), I32)])
    return pl.pallas_call(
        functools.partial(_a1s_kernel, topk=topk, past=n_pages * page),
        grid_spec=grid_spec,
        out_shape=jax.ShapeDtypeStruct((DB, nt, TS, LANES), F32),
        compiler_params=_params(("parallel", "arbitrary")),
        name="a1_sample",
    )(page_table, proj_s, proj_s, knew_bf, *([cache_kidx] * PAGES_PER_STEP))


def _a2s_kernel(pt_ref, q_ref, mask_ref, tzs_ref, knew_ref, vnew_ref, *rest, past):
    k_pages = rest[:PAGES_PER_STEP]
    v_pages = rest[PAGES_PER_STEP:2 * PAGES_PER_STEP]
    o_ref, m_ref, l_ref, acc_ref = rest[2 * PAGES_PER_STEP:]
    s = pl.program_id(1)
    TS = q_ref.shape[0]
    n_slab = tzs_ref.shape[1]

    @pl.when(s == 0)
    def _():
        m_ref[...] = jnp.full(m_ref.shape, NEG, F32)
        l_ref[...] = jnp.zeros(l_ref.shape, F32)
        acc_ref[...] = jnp.zeros(acc_ref.shape, F32)

    qg = [(jnp.concatenate([q_ref[:, (g * GROUP + r) * HEAD_DIM:(g * GROUP + r + 1) * HEAD_DIM]
                            for r in range(GROUP)], axis=0) * QSCALE).astype(BF16)
          for g in range(KV_HEADS)]

    def update(g, kt, vt, tiles):
        slabs = [jnp.minimum((past // LANES) - t, n_slab - 1) for t in tiles]
        mk = jnp.concatenate([mask_ref[t] for t in tiles], axis=1)
        mk = jnp.concatenate([mk] * GROUP, axis=0)
        bias = jnp.concatenate(
            [jnp.concatenate([tzs_ref[g * GROUP + r, sl] for r in range(GROUP)], axis=0) for sl in slabs],
            axis=1)
        sc = _nt(qg[g], kt) + bias + mk
        m_old = m_ref[g]
        m_new = jnp.maximum(m_old, jnp.max(sc, axis=1, keepdims=True))
        alpha = jnp.exp2(m_old - m_new)
        p = jnp.exp2(sc - m_new)
        l_ref[g] = alpha * l_ref[g] + jnp.sum(p, axis=1, keepdims=True)
        acc_ref[g] = alpha * acc_ref[g] + jnp.dot(p.astype(BF16), vt, preferred_element_type=F32)
        m_ref[g] = m_new

    tiles = [s * PAGES_PER_STEP + p for p in range(PAGES_PER_STEP)]
    for g in range(KV_HEADS):
        rows = pl.ds(g, LANES, stride=KV_HEADS)
        kt = jnp.concatenate([kp[rows, :] for kp in k_pages], axis=0).astype(BF16)
        vt = jnp.concatenate([vp[rows, :] for vp in v_pages], axis=0).astype(BF16)
        update(g, kt, vt, tiles)

    @pl.when(s == pl.num_programs(1) - 1)
    def _():
        for g in range(KV_HEADS):
            update(g, knew_ref[:, g * HEAD_DIM:(g + 1) * HEAD_DIM].astype(BF16),
                   vnew_ref[:, g * HEAD_DIM:(g + 1) * HEAD_DIM].astype(BF16), [past // LANES])
        for g in range(KV_HEADS):
            res = acc_ref[g] / l_ref[g]
            for r in range(GROUP):
                h = g * GROUP + r
                o_ref[:, h * HEAD_DIM:(h + 1) * HEAD_DIM] = res[r * TS:(r + 1) * TS, :].astype(o_ref.dtype)


def _a2_sample(page_table, proj_s, mask_s, tzs, knew, vnew, cache_k, cache_v, layer):
    DB, TS, _ = proj_s.shape
    n_pages = page_table.shape[1]
    page = cache_k.shape[2] // KV_HEADS
    kvw = KV_HEADS * HEAD_DIM
    nt = n_pages + 1

    def page_spec(p):
        return pl.BlockSpec((None, None, page * KV_HEADS, HEAD_DIM),
                            lambda b, s, pt: (layer, pt[b, s * PAGES_PER_STEP + p], 0, 0))

    grid_spec = pltpu.PrefetchScalarGridSpec(
        num_scalar_prefetch=1,
        grid=(DB, n_pages // PAGES_PER_STEP),
        in_specs=[pl.BlockSpec((None, TS, N_HEADS * HEAD_DIM), lambda b, s, pt: (b, 0, 0)),
                  pl.BlockSpec((None, nt, TS, LANES), lambda b, s, pt: (b, 0, 0, 0)),
                  pl.BlockSpec(tzs.shape, lambda b, s, pt: (0, 0, 0, 0)),
                  pl.BlockSpec((None, LANES, kvw), lambda b, s, pt: (b, 0, 0)),
                  pl.BlockSpec((None, LANES, kvw), lambda b, s, pt: (b, 0, 0))]
                 + [page_spec(p) for p in range(PAGES_PER_STEP)] * 2,
        out_specs=pl.BlockSpec((None, TS, N_HEADS * HEAD_DIM), lambda b, s, pt: (b, 0, 0)),
        scratch_shapes=[pltpu.VMEM((KV_HEADS, GROUP * TS, 1), F32),
                        pltpu.VMEM((KV_HEADS, GROUP * TS, 1), F32),
                        pltpu.VMEM((KV_HEADS, GROUP * TS, HEAD_DIM), F32)])
    return pl.pallas_call(
        functools.partial(_a2s_kernel, past=n_pages * page),
        grid_spec=grid_spec,
        out_shape=jax.ShapeDtypeStruct((DB, TS, N_HEADS * HEAD_DIM), BF16),
        compiler_params=_params(("parallel", "arbitrary")),
        name="a2_sample",
    )(page_table, proj_s, mask_s, tzs, knew, vnew,
      *([cache_k] * PAGES_PER_STEP), *([cache_v] * PAGES_PER_STEP))


BQ = 512


def _bp_kernel(*refs):
    n_q = len(B_GROUPS) * GROUP
    q_refs = refs[:n_q]
    k_ref, v_ref, band_ref, o_ref, m_ref, l_ref, acc_ref, og_ref, lseg_ref = refs[n_q:]
    step = pl.program_id(2)
    t0 = step * BQ
    first = t0 == 0
    whole = [n for n, (_, dil) in enumerate(B_GROUPS) if dil * QB > BQ]
    assert len(whole) == 1 and B_GROUPS[whole[0]][1] * QB == k_ref.shape[0]

    def band(n, half):
        return jnp.concatenate([band_ref[r, n, half] for r in range(GROUP)], axis=0)

    @pl.when(first)
    def _():
        n = whole[0]
        dil = B_GROUPS[n][1]
        for rho in range(dil):
            cls = pl.ds(rho, QB, stride=dil)
            qs = (jnp.concatenate([qr[cls, :] for qr in q_refs[n * GROUP:(n + 1) * GROUP]], axis=0)
                  * QSCALE).astype(BF16)
            s = _nt(qs, k_ref[cls, :].astype(BF16)) + band(n, 1)
            m = jnp.max(s, axis=1, keepdims=True)
            p = jnp.exp2(s - m)
            l = jnp.sum(p, axis=1, keepdims=True)
            out = jnp.dot(p.astype(BF16), v_ref[cls, :].astype(BF16), preferred_element_type=F32) / l
            lse = jnp.broadcast_to(m + jnp.log2(l), out.shape)
            for r in range(GROUP):
                og_ref[r, cls, :] = out[r * QB:(r + 1) * QB]
                lseg_ref[r, cls, :] = lse[r * QB:(r + 1) * QB]

    rows_step = pl.ds(pl.multiple_of(t0, BQ), BQ)
    for r in range(GROUP):
        m_ref[r] = lseg_ref[r, rows_step, :]
        l_ref[r] = jnp.ones(l_ref.shape[1:], F32)
        acc_ref[r] = og_ref[r, rows_step, :]

    def stacked(ref, rows):
        return jnp.concatenate([ref[r, rows, :] for r in range(GROUP)], axis=0)

    def update(q_ref, rows_q, nrow, key_rows, biases):
        qs = (jnp.concatenate([qr[rows_q, :] for qr in q_ref], axis=0) * QSCALE).astype(BF16)
        kt = jnp.concatenate([k_ref[kr, :] for kr in key_rows], axis=0).astype(BF16)
        vt = jnp.concatenate([v_ref[kr, :] for kr in key_rows], axis=0).astype(BF16)
        s = _nt(qs, kt) + jnp.concatenate(biases, axis=1)
        m_old = stacked(m_ref, rows_q)
        m_new = jnp.maximum(m_old, jnp.max(s, axis=1, keepdims=True))
        alpha = jnp.exp2(m_old - m_new)
        p = jnp.exp2(s - jnp.concatenate([m_new] * len(key_rows), axis=1))
        l_new = alpha * stacked(l_ref, rows_q) + jnp.sum(p, axis=1, keepdims=True)
        acc = alpha * stacked(acc_ref, rows_q) + jnp.dot(p.astype(BF16), vt, preferred_element_type=F32)
        for r in range(GROUP):
            m_ref[r, rows_q, :] = m_new[r * nrow:(r + 1) * nrow]
            l_ref[r, rows_q, :] = l_new[r * nrow:(r + 1) * nrow]
            acc_ref[r, rows_q, :] = acc[r * nrow:(r + 1) * nrow]

    for n, (window, dil) in enumerate(B_GROUPS):
        assert window // dil == QB
        span = dil * QB
        if n in whole:
            continue
        q_ref = q_refs[n * GROUP:(n + 1) * GROUP]
        for rho in range(dil):
            for mt in range(BQ // span):
                def rows(start):
                    return pl.ds(start, QB, stride=dil) if dil > 1 else pl.ds(start, QB)
                off = rho + span * mt
                prev = jnp.maximum(t0 + off - span, rho) if mt == 0 else t0 + off - span
                bias_a = band(n, 0)
                if mt == 0:
                    bias_a = bias_a + jnp.where(first, NEG, 0.0)
                update(q_ref, rows(off), QB, [rows(prev), rows(t0 + off)], [bias_a, band(n, 1)])

    for r in range(GROUP):
        o_ref[:, r * HEAD_DIM:(r + 1) * HEAD_DIM] = (acc_ref[r] / l_ref[r]).astype(o_ref.dtype)


def _b_prompt(qb, kv, band):
    B, T, _ = qb.shape
    nh = N_HEADS
    hd = HEAD_DIM
    gw = GROUP * hd
    assert T % BQ == 0

    def q_spec(n, r):
        whole =B_GROUPS[n][1] * QB > BQ
        if whole:
            return pl.BlockSpec((None, T, hd), lambda b, g, c: (b, 0, (n * KV_HEADS + g) * GROUP + r))
        return pl.BlockSpec((None, BQ, hd), lambda b, g, c: (b, c, (n * KV_HEADS + g) * GROUP + r))

    q_specs = [q_spec(n, r) for n in range(len(B_GROUPS)) for r in range(GROUP)]
    return pl.pallas_call(
        _bp_kernel,
        grid=(B, KV_HEADS, T // BQ),
        in_specs=q_specs + [
                  pl.BlockSpec((None, T, hd), lambda b, g, c: (b, 0, g)),
                  pl.BlockSpec((None, T, hd), lambda b, g, c: (b, 0, KV_HEADS + g)),
                  pl.BlockSpec((GROUP, 3, 2, QB, LANES), lambda b, g, c: (g, 0, 0, 0, 0))],
        out_specs=pl.BlockSpec((None, BQ, gw), lambda b, g, c: (b, c, g)),
        out_shape=jax.ShapeDtypeStruct((B, T, nh * hd), BF16),
        scratch_shapes=[pltpu.VMEM((GROUP, BQ, LANES), F32), pltpu.VMEM((GROUP, BQ, LANES), F32),
                        pltpu.VMEM((GROUP, BQ, hd), F32),
                        pltpu.VMEM((GROUP, T, hd), F32), pltpu.VMEM((GROUP, T, LANES), F32)],
        compiler_params=_params(("parallel", "parallel", "arbitrary")),
        name="b_prompt",
    )(*([qb] * len(q_specs)), kv, kv, band)


def _bs_kernel(q0_ref, q1_ref, q2_ref, kc_ref, vc_ref, kn_ref, vn_ref, tab_ref, o_ref):
    TS = q0_ref.shape[0]
    W = kc_ref.shape[0] // KV_HEADS
    rows = pl.ds(pl.program_id(1), W, stride=KV_HEADS)
    kc = kc_ref[rows, :].astype(BF16)
    vc = vc_ref[rows, :].astype(BF16)
    kn = kn_ref[...].astype(BF16)
    vn = vn_ref[...].astype(BF16)
    logits = []
    for n in range(len(B_GROUPS)):
        q_ref = (q0_ref, q1_ref, q2_ref)[n]
        q = (jnp.concatenate([q_ref[:, r * HEAD_DIM:(r + 1) * HEAD_DIM] for r in range(GROUP)],
                             axis=0) * QSCALE).astype(BF16)
        logits.append((_nt(q, kc) + tab_ref[n, :, :W], _nt(q, kn) + tab_ref[n, :, W:]))
    m = functools.reduce(jnp.maximum,
                         [jnp.max(x, axis=1, keepdims=True) for pair in logits for x in pair])
    l = jnp.zeros((GROUP * TS, 1), F32)
    acc = jnp.zeros((GROUP * TS, HEAD_DIM), F32)
    for sc_c, sc_n in logits:
        pc = jnp.exp2(sc_c - m)
        pn = jnp.exp2(sc_n - m)
        l = l + jnp.sum(pc, axis=1, keepdims=True) + jnp.sum(pn, axis=1, keepdims=True)
        acc = (acc + jnp.dot(pc.astype(BF16), vc, preferred_element_type=F32)
               + jnp.dot(pn.astype(BF16), vn, preferred_element_type=F32))
    res = acc / l
    for r in range(GROUP):
        o_ref[:, r * HEAD_DIM:(r + 1) * HEAD_DIM] = res[r * TS:(r + 1) * TS, :].astype(o_ref.dtype)


def _b_sample(qb_s, cache_k, cache_v, kv_new, tab):
    DB, TS, NQ = qb_s.shape
    W = cache_k.shape[1] // KV_HEADS
    hd = HEAD_DIM
    return pl.pallas_call(
        _bs_kernel,
        grid=(DB, KV_HEADS),
        in_specs=[pl.BlockSpec((None, TS, GROUP * hd), lambda b, g: (b, 0, g)),
                  pl.BlockSpec((None, TS, GROUP * hd), lambda b, g: (b, 0, KV_HEADS + g)),
                  pl.BlockSpec((None, TS, GROUP * hd), lambda b, g: (b, 0, 2 * KV_HEADS + g)),
                  pl.BlockSpec((None, W * KV_HEADS, hd), lambda b, g: (b, 0, 0)),
                  pl.BlockSpec((None, W * KV_HEADS, hd), lambda b, g: (b, 0, 0)),
                  pl.BlockSpec((None, LANES, hd), lambda b, g: (b, 0, g)),
                  pl.BlockSpec((None, LANES, hd), lambda b, g: (b, 0, KV_HEADS + g)),
                  pl.BlockSpec((len(B_GROUPS), None, GROUP * TS, W + LANES), lambda b, g: (0, g, 0, 0))],
        out_specs=pl.BlockSpec((None, TS, GROUP * hd), lambda b, g: (b, 0, g)),
        out_shape=jax.ShapeDtypeStruct((DB, TS, N_HEADS * hd), BF16),
        compiler_params=_params(("parallel", "arbitrary")),
        name="b_sample",
    )(qb_s, qb_s, qb_s, cache_k, cache_v, kv_new, kv_new, tab)


FFN_TF = 512
HALO = 16


def _ffn_up_kernel(xm_ref, xh_ref, wg_ref, wu_ref, cw_ref, cb_ref, act_ref, st_ref, xe_ref, *,
                   tiles_per_batch):
    i = pl.program_id(0)
    tm = xm_ref.shape[0]

    @pl.when(pl.program_id(1) == 0)
    def _():
        first = (i % tiles_per_batch) == 0
        xh = xh_ref[...]
        xe_ref[0:HALO, :] = jnp.where(first, jnp.zeros_like(xh), xh)
        xe_ref[HALO:, :] = xm_ref[...]

    gate = jnp.dot(xe_ref[...], wg_ref[...], preferred_element_type=F32)
    up = jnp.dot(xm_ref[...], wu_ref[...], preferred_element_type=F32)
    cw = cw_ref[...]
    conv = cb_ref[...] + ((gate[HALO - 2:HALO - 2 + tm] * cw[0:1] + gate[HALO - 1:HALO - 1 + tm] * cw[1:2])
                          + gate[HALO:] * cw[2:3])
    act_ref[...] = (jax.nn.silu(conv) * up).astype(act_ref.dtype)
    st_ref[...] = gate[HALO + tm - (CONV_WIDTH - 1):, :]


def _ffn_up(x_bf, w_g_bf, w_u_bf, conv_w, conv_b, layer, T):
    M, D = x_bf.shape
    F = conv_w.shape[2]
    tm = 1024
    assert T % tm == 0 and M % T == 0 and F % FFN_TF == 0 and tm % HALO == 0
    nf = F // FFN_TF
    hb = tm // HALO
    return pl.pallas_call(
        functools.partial(_ffn_up_kernel, tiles_per_batch=T // tm),
        grid=(M // tm, nf),
        in_specs=[pl.BlockSpec((tm, D), lambda i, j: (i, 0)),
                  pl.BlockSpec((HALO, D), lambda i, j: (jnp.maximum(i * hb - 1, 0), 0)),
                  pl.BlockSpec((None, D, FFN_TF), lambda i, j: (layer, 0, j)),
                  pl.BlockSpec((None, D, FFN_TF), lambda i, j: (layer, 0, j)),
                  pl.BlockSpec((None, CONV_WIDTH, FFN_TF), lambda i, j: (layer, 0, j)),
                  pl.BlockSpec((None, 1, FFN_TF), lambda i, j: (layer, 0, j))],
        out_specs=[pl.BlockSpec((tm, FFN_TF), lambda i, j: (i, j)),
                   pl.BlockSpec((None, CONV_WIDTH - 1, FFN_TF), lambda i, j: (i, 0, j))],
        out_shape=[jax.ShapeDtypeStruct((M, F), BF16),
                   jax.ShapeDtypeStruct((M // tm, CONV_WIDTH - 1, F), F32)],
        scratch_shapes=[pltpu.VMEM((HALO + tm, D), BF16)],
        compiler_params=_params(("parallel", "arbitrary")),
        name="ffn_up",
    )(x_bf, x_bf, w_g_bf, w_u_bf, conv_w, conv_b)


def _mm_ln_kernel(a_ref, w_ref, x_ref, g_ref, b_ref, h_ref, hb_ref, acc_ref, *, alpha):
    k = pl.program_id(1)

    @pl.when(k == 0)
    def _():
        acc_ref[...] = jnp.zeros_like(acc_ref)

    acc_ref[...] += jnp.dot(a_ref[...], w_ref[...], preferred_element_type=F32)

    @pl.when(k == pl.num_programs(1) - 1)
    def _():
        y = alpha * x_ref[...] + acc_ref[...]
        mu = jnp.mean(y, axis=-1, keepdims=True)
        d = y - mu
        var = jnp.mean(d * d, axis=-1, keepdims=True)
        h = d * lax.rsqrt(var + LN_EPS) * g_ref[...] + b_ref[...]
        h_ref[...] = h
        hb_ref[...] = h.astype(hb_ref.dtype)


def _mm_ln(a_bf, w_bf, layer, x, g, b, alpha):
    M, K = a_bf.shape
    D = w_bf.shape[2]
    tm = min(512, M)
    tk = K // 4 if K > 2048 else K // 2
    assert M % tm == 0 and K % tk == 0 and tk % LANES == 0
    return pl.pallas_call(
        functools.partial(_mm_ln_kernel, alpha=alpha),
        grid=(M // tm, K // tk),
        in_specs=[pl.BlockSpec((tm, tk), lambda i, k: (i, k)),
                  pl.BlockSpec((None, tk, D), lambda i, k: (layer, k, 0)),
                  pl.BlockSpec((tm, D), lambda i, k: (i, 0)),
                  pl.BlockSpec((1, D), lambda i, k: (0, 0)),
                  pl.BlockSpec((1, D), lambda i, k: (0, 0))],
        out_specs=[pl.BlockSpec((tm, D), lambda i, k: (i, 0)),
                   pl.BlockSpec((tm, D), lambda i, k: (i, 0))],
        out_shape=[jax.ShapeDtypeStruct((M, D), F32), jax.ShapeDtypeStruct((M, D), BF16)],
        scratch_shapes=[pltpu.VMEM((tm, D), F32)],
        compiler_params=_params(("parallel", "arbitrary")),
        name="mm_ln",
    )(a_bf, w_bf, x, g.reshape(1, D), b.reshape(1, D))


def _layer_norm(x, g, b):
    mu = x.mean(-1, keepdims=True)
    var = jnp.square(x - mu).mean(-1, keepdims=True)
    return (x - mu) * lax.rsqrt(var + LN_EPS) * g + b


def _pick_tm(M):
    for tm in (1024, 512, 256, 128, 64, 32, 16):
        if M % tm == 0:
            return tm
    raise ValueError(M)


def _pick_tn(N):
    for tn in (512, 384, 256, 128):
        if N % tn == 0:
            return tn
    raise ValueError(N)


def _proj(x_bf, w_bf, layer=0):
    return _mm(x_bf, w_bf, layer, _pick_tm(x_bf.shape[0]), _pick_tn(w_bf.shape[2]))


def _ffn_act_sample(hs_bf, state, w_g_bf, w_u_bf, conv_w, conv_b, layer, DB, TS):
    conv_w, conv_b = conv_w[layer], conv_b[layer]
    F = conv_w.shape[1]
    gate = _proj(hs_bf, w_g_bf, layer).reshape(DB, TS, F)
    up = _proj(hs_bf, w_u_bf, layer).reshape(DB, TS, F)
    ext = jnp.concatenate([state, gate], axis=1)
    conv = conv_b + sum(ext[:, j:j + TS] * conv_w[j] for j in range(CONV_WIDTH))
    act = (jax.nn.silu(conv) * up).astype(BF16).reshape(DB * TS, F)
    return act, ext[:, ext.shape[1] - (CONV_WIDTH - 1):]


def _table_indices(T, W, TS):
    U = SAT_DIST + TZ_OFF + QB
    assert T <= T5_MAX_DISTANCE
    u = np.arange(U)[:, None]
    c = np.arange(LANES)[None, :]
    tz_idx = _t5_bucket(jnp.asarray(np.maximum(u - c - TZ_OFF, 0), I32))
    a = np.arange(QB)[:, None]
    band = []
    for window, dil in B_GROUPS:
        for off in (QB, 0):
            d = a + off - c
            ok = (d >= 0) & (d <= window // dil)
            band.append(jnp.where(jnp.asarray(ok), _t5_bucket(jnp.asarray(np.maximum(d, 0) * dil, I32)),
                                  NUM_BUCKETS))
    band_idx = jnp.concatenate(band, axis=0)
    nkt = W // LANES + 1
    t = np.arange(TS)[None, :, None]
    key = (np.arange(nkt)[:, None, None] * LANES + np.arange(LANES)[None, None, :])
    d = W + t - key
    tabs = []
    for window, dil in B_GROUPS:
        ok = (d >= 0) & (d % dil == 0) & (d <= window)
        tabs.append(jnp.where(jnp.asarray(ok), _t5_bucket(jnp.asarray(np.maximum(d, 0), I32)), NUM_BUCKETS))
    samp_idx = jnp.concatenate(tabs, axis=0).reshape(-1, LANES)
    rows = [tz_idx, band_idx, samp_idx]
    total = sum(r.shape[0] for r in rows)
    pad = (-total) % 128
    if pad:
        rows.append(jnp.full((pad, LANES), NUM_BUCKETS, I32))
    return jnp.concatenate(rows, axis=0), U, nkt


def kernel(x_prompt, x_sample, cache_k_a, cache_v_a, cache_kidx_a, cache_k_b, cache_v_b, state_ffn, page_table, a_w_in, a_w_o, a_kn_g, a_kn_b, b_w_kv, b_w_q, b_w_o, ffn_w_up, ffn_conv_w, ffn_conv_b, ffn_w_down, ln_g, ln_b, rel_bias):
    B, T, D = x_prompt.shape
    DB, TS, _ = x_sample.shape
    depth = ffn_w_up.shape[0]
    n_a = a_w_in.shape[0]
    d_ff = ffn_w_down.shape[1]
    W = cache_k_b.shape[1]
    n_pages = page_table.shape[1]
    page = cache_k_a.shape[2]
    past = n_pages * page
    alpha = (2 * depth) ** 0.25
    kvw = KV_HEADS * HEAD_DIM
    a_q = N_HEADS * HEAD_DIM
    a_in = a_w_in.shape[2]
    np_cols = ((a_in + 511) // 512) * 512
    ki0 = a_q + 2 * kvw + IDX_HEADS * IDX_DIM

    idx_all, U, nkt = _table_indices(T, W, TS)
    tabs = _bias_tables(rel_bias * LOG2E, idx_all)
    tz = tabs[:, :U]
    band = tabs[:, U:U + 6 * QB].reshape(N_HEADS, 3, 2, QB, LANES)
    samp = tabs[:, U + 6 * QB:U + 6 * QB + 3 * nkt * TS].reshape(KV_HEADS, GROUP, 3, nkt, TS, LANES)
    samp = samp.transpose(2, 0, 1, 4, 3, 5).reshape(3, KV_HEADS, GROUP * TS, nkt * LANES)
    n_slab = SAT_DIST // LANES + 1
    tzs = tz[:, TZ_OFF:TZ_OFF + n_slab * LANES].reshape(N_HEADS, n_slab, LANES, LANES)[:, :, :TS, :]

    cache_k_a2 = cache_k_a.reshape(n_a, -1, page * KV_HEADS, HEAD_DIM)
    cache_v_a2 = cache_v_a.reshape(n_a, -1, page * KV_HEADS, HEAD_DIM)
    cache_k_b2 = cache_k_b.reshape(DB, W * KV_HEADS, HEAD_DIM)
    cache_v_b2 = cache_v_b.reshape(DB, W * KV_HEADS, HEAD_DIM)
    cache_kidx_t = jnp.swapaxes(cache_kidx_a, 2, 3)

    def pad_rows(x, n):
        return jnp.pad(x, ((0, 0), (0, n - x.shape[1]), (0, 0)))

    f_pad = ((d_ff + FFN_TF - 1) // FFN_TF) * FFN_TF
    pad_f = f_pad - d_ff
    w_g_all = jnp.pad(ffn_w_up[:, :, :d_ff], ((0, 0), (0, 0), (0, pad_f))).astype(BF16)
    w_u_all = jnp.pad(ffn_w_up[:, :, d_ff:], ((0, 0), (0, 0), (0, pad_f))).astype(BF16)
    w_down_all = jnp.pad(ffn_w_down, ((0, 0), (0, pad_f), (0, 0))).astype(BF16)
    conv_w_all = jnp.pad(ffn_conv_w, ((0, 0), (0, 0), (0, pad_f)))
    conv_b_all = jnp.pad(ffn_conv_b, ((0, 0), (0, pad_f))).reshape(depth, 1, f_pad)
    w_in_all = jnp.pad(a_w_in, ((0, 0), (0, 0), (0, np_cols - a_in))).astype(BF16)
    w_o_a, w_o_b = a_w_o.astype(BF16), b_w_o.astype(BF16)
    w_q_all = b_w_q.astype(BF16)
    w_kv_bf = b_w_kv.astype(BF16)[None]

    hp = x_prompt.reshape(B * T, D)
    hs = x_sample.reshape(DB * TS, D)
    hp_bf, hs_bf = hp.astype(BF16), hs.astype(BF16)
    ka_p, va_p, kia_p, ka_s, va_s, kia_s, ffn_p, ffn_s = [], [], [], [], [], [], [], []
    for layer in range(depth):
        if layer < n_a:
            a = layer
            w_o_bf, w_o_layer = w_o_a, a
            proj = _proj(hp_bf, w_in_all, a).reshape(B, T, np_cols)
            k = proj[..., a_q:a_q + kvw].reshape(B, T, KV_HEADS, HEAD_DIM)
            v = proj[..., a_q + kvw:a_q + 2 * kvw].reshape(B, T, KV_HEADS, HEAD_DIM)
            ki = _layer_norm(proj[..., ki0:ki0 + IDX_DIM], a_kn_g[a], a_kn_b[a])
            ka_p.append(k); va_p.append(v); kia_p.append(ki)
            mask = _a1_prompt(ki.astype(BF16), proj, min(TOPK_MAX, T // 4))
            o = _a2_prompt(proj, mask, tz).reshape(B * T, a_q)
            proj_s = _proj(hs_bf, w_in_all, a).reshape(DB, TS, np_cols)
            k_s = proj_s[..., a_q:a_q + kvw]
            v_s = proj_s[..., a_q + kvw:a_q + 2 * kvw]
            ki_s = _layer_norm(proj_s[..., ki0:ki0 + IDX_DIM], a_kn_g[a], a_kn_b[a])
            ka_s.append(k_s.reshape(DB, TS, KV_HEADS, HEAD_DIM))
            va_s.append(v_s.reshape(DB, TS, KV_HEADS, HEAD_DIM))
            kia_s.append(ki_s)
            mask_s = _a1_sample(page_table, proj_s, pad_rows(ki_s, LANES).astype(BF16), cache_kidx_t, a,
                                min(TOPK_MAX, (past + TS) // 4))
            o_s = _a2_sample(page_table, proj_s, mask_s, tzs, pad_rows(k_s, LANES), pad_rows(v_s, LANES),
                             cache_k_a2, cache_v_a2, a).reshape(DB * TS, a_q)
        else:
            if layer == n_a:
                kv_p = _proj(hp_bf, w_kv_bf).reshape(B, T, 2 * kvw)
                kv_s = _proj(hs_bf, w_kv_bf).reshape(DB, TS, 2 * kvw)
                kv_s_pad = pad_rows(kv_s, LANES)
            bl = layer - n_a
            w_o_bf, w_o_layer = w_o_b, bl
            o = _b_prompt(_proj(hp_bf, w_q_all, bl).reshape(B, T, -1), kv_p, band).reshape(B * T, a_q)
            o_s = _b_sample(_proj(hs_bf, w_q_all, bl).reshape(DB, TS, -1), cache_k_b2, cache_v_b2, kv_s_pad,
                            samp).reshape(DB * TS, a_q)
        hp, hp_bf = _mm_ln(o, w_o_bf, w_o_layer, hp, ln_g[layer, 0], ln_b[layer, 0], alpha)
        hs, hs_bf = _mm_ln(o_s, w_o_bf, w_o_layer, hs, ln_g[layer, 0], ln_b[layer, 0], alpha)
        act, st = _ffn_up(hp_bf, w_g_all, w_u_all, conv_w_all, conv_b_all, layer, T)
        ffn_p.append(st.reshape(B, -1, CONV_WIDTH - 1, f_pad)[:, -1, :, :d_ff])
        act_s, st_s = _ffn_act_sample(hs_bf, jnp.pad(state_ffn[layer], ((0, 0), (0, 0), (0, pad_f))),
                                      w_g_all, w_u_all, conv_w_all, conv_b_all, layer, DB, TS)
        ffn_s.append(st_s[..., :d_ff])
        hp, hp_bf = _mm_ln(act, w_down_all, layer, hp, ln_g[layer, 1], ln_b[layer, 1], alpha)
        hs, hs_bf = _mm_ln(act_s, w_down_all, layer, hs, ln_g[layer, 1], ln_b[layer, 1], alpha)
    keep = min(max(w for w, _ in B_GROUPS), T)
    kb_p = kv_p[..., :kvw].reshape(B, T, KV_HEADS, HEAD_DIM)
    vb_p = kv_p[..., kvw:].reshape(B, T, KV_HEADS, HEAD_DIM)
    kb_s = kv_s[..., :kvw].reshape(DB, TS, KV_HEADS, HEAD_DIM)
    vb_s = kv_s[..., kvw:].reshape(DB, TS, KV_HEADS, HEAD_DIM)
    return (hp.reshape(B, T, D), hs.reshape(DB, TS, D), jnp.stack(ka_p), jnp.stack(va_p), jnp.stack(kia_p),
            jnp.stack(ka_s), jnp.stack(va_s), jnp.stack(kia_s), kb_p[:, T - keep:], vb_p[:, T - keep:],
            kb_s, vb_s, jnp.stack(ffn_p), jnp.stack(ffn_s))
```

```python
import functools
import math

import numpy as np
import jax
import jax.numpy as jnp
from jax import lax
from jax.experimental import pallas as pl
from jax.experimental.pallas import tpu as pltpu

F32 = jnp.float32
BF16 = jnp.bfloat16
I32 = jnp.int32

HEAD_DIM = 128
N_HEADS = 16
KV_HEADS = 4
GROUP = N_HEADS // KV_HEADS
IDX_HEADS = 16
IDX_DIM = 64
TOPK_MAX = 256
B_GROUPS = ((128, 1), (512, 4), (2048, 16))
NUM_BUCKETS = 32
T5_MAX_DISTANCE = 2048
LN_EPS = 1e-5
NEG = -1e30
CONV_WIDTH = 3

LANES = 128
QB = 128
KC = 512
A2_KC = KC
TZ_OFF = A2_KC - QB
SAT_DIST = T5_MAX_DISTANCE + QB
INT_MIN = -2 ** 31
KEY_NEG_INF = INT_MIN + 0x7FFFFF
VMEM_LIMIT = 56 * 1024 * 1024

LOG2E = 1.4426950408889634
QSCALE = HEAD_DIM ** -0.5 * LOG2E

NT_DIMS = (((1,), (1,)), ((), ()))


def _nt(a, b):
    return lax.dot_general(a, b, NT_DIMS, preferred_element_type=F32)


def _params(sem):
    return pltpu.CompilerParams(dimension_semantics=sem, vmem_limit_bytes=VMEM_LIMIT)


def _mm_kernel(x_ref, w_ref, o_ref):
    o_ref[...] = jnp.dot(x_ref[...], w_ref[...], preferred_element_type=F32).astype(o_ref.dtype)


def _mm(x, w, layer, tm, tn, out_dtype=F32):
    M, K = x.shape
    N = w.shape[2]
    assert M % tm == 0 and N % tn == 0, (M, N, tm, tn)
    return pl.pallas_call(
        _mm_kernel,
        grid=(M // tm, N // tn),
        in_specs=[pl.BlockSpec((tm, K), lambda i, j: (i, 0)),
                  pl.BlockSpec((None, K, tn), lambda i, j: (layer, 0, j))],
        out_specs=pl.BlockSpec((tm, tn), lambda i, j: (i, j)),
        out_shape=jax.ShapeDtypeStruct((M, N), out_dtype),
        compiler_params=_params(("parallel", "arbitrary")),
        name="mm",
    )(x, w)


def _t5_bucket(dist):
    dist = jnp.maximum(dist, 0)
    exact = NUM_BUCKETS // 2
    far = exact + (jnp.log(jnp.maximum(dist, 1).astype(F32) / exact)
                   / math.log(T5_MAX_DISTANCE / exact) * (NUM_BUCKETS - exact)).astype(I32)
    return jnp.where(dist < exact, dist, jnp.minimum(far, NUM_BUCKETS - 1))


def _table_kernel(rb_ref, idx_ref, o_ref):
    slab = 16

    def body(i, carry):
        r0 = pl.multiple_of(i * slab, slab)
        idx = idx_ref[pl.ds(r0, slab), :]
        accs = [jnp.full(idx.shape, NEG, F32)] * N_HEADS
        for k in range(NUM_BUCKETS):
            hit = idx == k
            accs = [jnp.where(hit, rb_ref[k, h], accs[h]) for h in range(N_HEADS)]
        for h in range(N_HEADS):
            o_ref[h, pl.ds(r0, slab), :] = accs[h]
        return carry

    lax.fori_loop(0, idx_ref.shape[0] // slab, body, 0)


def _bias_tables(rel_bias, idx):
    R = idx.shape[0]
    tr = 128
    assert R % tr == 0
    return pl.pallas_call(
        _table_kernel,
        grid=(R // tr,),
        in_specs=[pl.BlockSpec(memory_space=pltpu.SMEM),
                  pl.BlockSpec((tr, LANES), lambda i: (i, 0))],
        out_specs=pl.BlockSpec((N_HEADS, tr, LANES), lambda i: (0, i, 0)),
        out_shape=jax.ShapeDtypeStruct((N_HEADS, R, LANES), F32),
        compiler_params=_params(("arbitrary",)),
        name="bias_tables",
    )(rel_bias, idx)


def _sortable(x):
    bits = lax.bitcast_convert_type(x, I32)
    return bits ^ ((bits >> 31) & jnp.int32(0x7FFFFFFF))


def _kth_largest(count_ge, shape, k):
    def body(t, ans):
        cand_u = ans | (jnp.int32(1) << (31 - t))
        cnt = count_ge(cand_u ^ jnp.int32(INT_MIN))
        return jnp.where(cnt >= k, cand_u, ans)
    ans = lax.fori_loop(0, 32, body, jnp.zeros(shape, I32))
    return ans ^ jnp.int32(INT_MIN)


def _tie_bound(count_eq_below, shape, need, nbits):
    def body(t, ans):
        cand = ans | (jnp.int32(1) << (nbits - 1 - t))
        return jnp.where(count_eq_below(cand) < need, cand, ans)
    return lax.fori_loop(0, nbits, body, jnp.zeros(shape, I32))


A1_TILES = 4


def _a1p_kernel(kin_ref, qi_ref, tail_ref, mask_ref, st_ref, jb_ref, *, topk):
    i = pl.program_id(1)
    T = kin_ref.shape[0]
    NQ = A1_TILES * QB
    KS = KC // A1_TILES
    nchunk = T // KC
    nbits = (T - 1).bit_length()
    tail_t = jnp.concatenate([tail_ref[t * QB:(t + 1) * QB, :].T for t in range(A1_TILES)], axis=1)
    wi_t = tail_t[IDX_DIM:IDX_DIM + IDX_HEADS, :] * (IDX_HEADS ** -0.5 * IDX_DIM ** -0.5)
    qpos = i * NQ + lax.broadcasted_iota(I32, (KC, NQ), 1)
    row = lax.broadcasted_iota(I32, (KC, NQ), 0)
    qpos_s = i * NQ + lax.broadcasted_iota(I32, (KS, NQ), 1)
    row_s = lax.broadcasted_iota(I32, (KS, NQ), 0)

    nj = (i * NQ) // KC + 1

    for c in range(nchunk):
        @pl.when(c < nj)
        def _():
            for sub in range(KC // KS):
                r0 = c * KC + sub * KS
                kc = kin_ref[r0:r0 + KS, :]
                acc = jnp.zeros((KS, NQ), F32)
                for h in range(IDX_HEADS):
                    qh = qi_ref[:, h * IDX_DIM:(h + 1) * IDX_DIM].astype(BF16)
                    acc = acc + jnp.maximum(_nt(kc, qh), 0.0) * wi_t[h:h + 1, :]
                acc = jnp.where(r0 + row_s <= qpos_s, acc, -jnp.inf)
                st_ref[r0:r0 + KS, :] = _sortable(acc)

    def select(nvis):
        def count(pred):
            cnt = jnp.zeros((1, NQ), F32)
            for c in range(nvis):
                blk = st_ref[c * KC:(c + 1) * KC, :]
                cnt = cnt + jnp.sum(jnp.where(pred(blk, c * KC + row), 1.0, 0.0), axis=0, keepdims=True)
            return cnt

        thr = _kth_largest(lambda cand: count(lambda blk, _: blk >= cand), (1, NQ), float(topk))
        need = float(topk) - count(lambda blk, _: blk > thr)
        n_eq = count(lambda blk, _: blk == thr)
        excess = jnp.where((n_eq > need) & (thr != KEY_NEG_INF), 1.0, 0.0)
        jb_ref[...] = jnp.full((1, NQ), T, I32)

        @pl.when(jnp.max(excess) > 0.0)
        def _():
            jb_ref[...] = _tie_bound(
                lambda cand: count(lambda blk, kpos: jnp.where(blk == thr, kpos, T) < cand),
                (1, NQ), need, nbits)

        jb = jb_ref[...]
        for c in range(nchunk):
            if c < nvis:
                blk = st_ref[c * KC:(c + 1) * KC, :]
                kpos = c * KC + row
                rank_pos = jnp.where(blk == thr, kpos, jnp.where(blk > thr, -1, T + 1))
                sel = jnp.where(kpos <= qpos, rank_pos, T + 1) <= jb
                m_t = jnp.where(sel, 0.0, NEG)
                for t in range(A1_TILES):
                    for s4 in range(KC // QB):
                        mask_ref[t, c, :, s4 * QB:(s4 + 1) * QB] = (
                            m_t[s4 * QB:(s4 + 1) * QB, t * QB:(t + 1) * QB].T.astype(mask_ref.dtype))
            else:
                for t in range(A1_TILES):
                    mask_ref[t, c] = jnp.full((QB, KC), NEG, mask_ref.dtype)

    for nvis in range(1, nchunk + 1):
        pl.when(nj == nvis)(functools.partial(select, nvis))


def _a1_prompt(kin_bf, proj, topk):
    B, T, _ = proj.shape
    nq, nc = T // QB, T // KC
    assert topk <= KC
    assert (KC // QB) % A1_TILES == 0 and nq % A1_TILES == 0
    nqs = A1_TILES * QB
    qi_blk = (N_HEADS * HEAD_DIM + 2 * KV_HEADS * HEAD_DIM) // (IDX_HEADS * IDX_DIM)
    tail_blk = (N_HEADS * HEAD_DIM + 2 * KV_HEADS * HEAD_DIM + IDX_HEADS * IDX_DIM) // LANES
    return pl.pallas_call(
        functools.partial(_a1p_kernel, topk=topk),
        grid=(B, nq // A1_TILES),
        in_specs=[pl.BlockSpec((None, T, IDX_DIM), lambda b, i: (b, 0, 0)),
                  pl.BlockSpec((None, nqs, IDX_HEADS * IDX_DIM), lambda b, i: (b, i, qi_blk)),
                  pl.BlockSpec((None, nqs, LANES), lambda b, i: (b, i, tail_blk))],
        out_specs=pl.BlockSpec((None, A1_TILES, nc, QB, KC), lambda b, i: (b, i, 0, 0, 0)),
        out_shape=jax.ShapeDtypeStruct((B, nq, nc, QB, KC), BF16),
        scratch_shapes=[pltpu.VMEM((T, nqs), I32), pltpu.VMEM((1, nqs), I32)],
        compiler_params=_params(("parallel", "arbitrary")),
        name="a1_prompt",
    )(kin_bf, proj, proj)


def _a2p_kernel(q_ref, k_ref, v_ref, mask_ref, tz_ref, o_ref):
    i = pl.program_id(2)
    nj = (i * QB) // A2_KC + 1
    HP = GROUP
    R = HP * QB
    heads = [list(range(c * HP, (c + 1) * HP)) for c in range(GROUP // HP)]
    qs = [(jnp.concatenate([q_ref[:, r * HEAD_DIM:(r + 1) * HEAD_DIM] for r in hs], axis=0)
           * QSCALE).astype(BF16) for hs in heads]

    def body(j, carry):
        k0 = pl.multiple_of(j * A2_KC, A2_KC)
        kt = k_ref[pl.ds(k0, A2_KC), :].astype(BF16)
        vt = v_ref[pl.ds(k0, A2_KC), :].astype(BF16)
        base = i * QB - j * A2_KC + TZ_OFF
        mk = jnp.concatenate([mask_ref[j * (A2_KC // KC) + c] for c in range(A2_KC // KC)],
                             axis=1).astype(F32)
        mk = jnp.concatenate([mk] * HP, axis=0)
        out = []
        for hs, q, (m, l, acc) in zip(heads, qs, carry):
            bias = jnp.concatenate(
                [jnp.concatenate(
                    [tz_ref[r, pl.ds(pl.multiple_of(base - QB * s4, QB), QB), :]
                     for s4 in range(A2_KC // QB)], axis=1) for r in hs], axis=0)
            s = _nt(q, kt) + bias + mk
            m_new = jnp.maximum(m, jnp.max(s, axis=1, keepdims=True))
            alpha = jnp.exp2(m - m_new)
            p = jnp.exp2(s - m_new)
            l = alpha * l + jnp.sum(p, axis=1, keepdims=True)
            acc = alpha * acc + jnp.dot(p.astype(BF16), vt, preferred_element_type=F32)
            out.append((m_new, l, acc))
        return tuple(out)

    init = tuple((jnp.full((R, 1), NEG, F32), jnp.zeros((R, 1), F32), jnp.zeros((R, HEAD_DIM), F32))
                 for _ in heads)
    final = lax.fori_loop(0, nj, body, init)
    for hs, (m, l, acc) in zip(heads, final):
        res = acc / l
        for n, r in enumerate(hs):
            o_ref[:, r * HEAD_DIM:(r + 1) * HEAD_DIM] = res[n * QB:(n + 1) * QB, :].astype(o_ref.dtype)


def _a2_prompt(proj, mask, tz):
    B, T, _ = proj.shape
    nq, nc = T // QB, T // KC
    gw = GROUP * HEAD_DIM
    k_blk0 = N_HEADS * HEAD_DIM // HEAD_DIM
    v_blk0 = k_blk0 + KV_HEADS
    U = tz.shape[1]
    return pl.pallas_call(
        _a2p_kernel,
        grid=(KV_HEADS, B, nq),
        in_specs=[pl.BlockSpec((None, QB, gw), lambda g, b, i: (b, i, g)),
                  pl.BlockSpec((None, T, HEAD_DIM), lambda g, b, i: (b, 0, k_blk0 + g)),
                  pl.BlockSpec((None, T, HEAD_DIM), lambda g, b, i: (b, 0, v_blk0 + g)),
                  pl.BlockSpec((None, None, nc, QB, KC), lambda g, b, i: (b, i, 0, 0, 0)),
                  pl.BlockSpec((GROUP, U, LANES), lambda g, b, i: (g, 0, 0))],
        out_specs=pl.BlockSpec((None, QB, gw), lambda g, b, i: (b, i, g)),
        out_shape=jax.ShapeDtypeStruct((B, T, N_HEADS * HEAD_DIM), BF16),
        compiler_params=_params(("parallel", "parallel", "arbitrary")),
        name="a2_prompt",
    )(proj, proj, proj, mask, tz)


PAGES_PER_STEP = 32


def _a1s_kernel(pt_ref, qi_ref, tail_ref, knew_ref, *rest, topk, past):
    page_refs = rest[:PAGES_PER_STEP]
    mask_ref, st_ref, jb_ref = rest[PAGES_PER_STEP:]
    s = pl.program_id(1)
    nt, TS, _ = st_ref.shape
    L = nt * LANES
    nbits = (L - 1).bit_length()
    qi = jnp.concatenate([qi_ref[:, h * IDX_DIM:(h + 1) * IDX_DIM] for h in range(IDX_HEADS)],
                         axis=0).astype(BF16)
    w_col = jnp.concatenate([tail_ref[:, IDX_DIM + h:IDX_DIM + h + 1] for h in range(IDX_HEADS)],
                            axis=0) * (IDX_HEADS ** -0.5 * IDX_DIM ** -0.5)

    def scores(qk):
        sc = jnp.maximum(qk, 0.0) * w_col
        return jnp.sum(sc.reshape(IDX_HEADS, TS, qk.shape[1]), axis=0)

    half = PAGES_PER_STEP // 2
    for part in range(2):
        kt = jnp.concatenate([page_refs[part * half + p][...] for p in range(half)], axis=1).astype(BF16)
        keys = _sortable(scores(jnp.dot(qi, kt, preferred_element_type=F32)))
        for p in range(half):
            st_ref[s * PAGES_PER_STEP + part * half + p] = keys[:, p * LANES:(p + 1) * LANES]

    @pl.when(s == pl.num_programs(1) - 1)
    def _():
        t = lax.broadcasted_iota(I32, (TS, LANES), 0)
        c = lax.broadcasted_iota(I32, (TS, LANES), 1)
        sc = jnp.where(c <= t, scores(_nt(qi, knew_ref[...])), -jnp.inf)
        st_ref[nt - 1] = _sortable(sc)

        kpos = (lax.broadcasted_iota(I32, (nt, TS, LANES), 0) * LANES
                + lax.broadcasted_iota(I32, (nt, TS, LANES), 2))
        qpos = past + lax.broadcasted_iota(I32, (nt, TS, LANES), 1)

        def count(pred):
            per_lane = jnp.sum(jnp.where(pred(st_ref[...]), 1.0, 0.0), axis=0)
            return jnp.sum(per_lane, axis=1, keepdims=True)

        thr = _kth_largest(lambda cand: count(lambda k: k >= cand[None]), (TS, 1), float(topk))
        thr3 = thr[None]
        need = float(topk) - count(lambda k: k > thr3)
        n_eq = count(lambda k: k == thr3)
        excess = jnp.where((n_eq > need) & (thr != KEY_NEG_INF), 1.0, 0.0)
        jb_ref[...] = jnp.full((TS, 1), L, I32)

        @pl.when(jnp.max(excess) > 0.0)
        def _():
            jb_ref[...] = _tie_bound(
                lambda cand: count(lambda k: jnp.where(k == thr3, kpos, L) < cand[None]),
                (TS, 1), need, nbits)

        keys = st_ref[...]
        rank_pos = jnp.where(keys == thr3, kpos, jnp.where(keys > thr3, -1, L + 1))
        sel = jnp.where(kpos <= qpos, rank_pos, L + 1) <= jb_ref[...][None]
        mask_ref[...] = jnp.where(sel, 0.0, NEG)


def _a1_sample(page_table, proj_s, knew_bf, cache_kidx, layer, topk):
    DB, TS, _ = proj_s.shape
    n_pages = page_table.shape[1]
    page = cache_kidx.shape[3]
    assert page == LANES and n_pages % PAGES_PER_STEP == 0 and topk <= n_pages * page
    nt = n_pages + 1
    qi_blk = (N_HEADS * HEAD_DIM + 2 * KV_HEADS * HEAD_DIM) // (IDX_HEADS * IDX_DIM)
    tail_blk = (N_HEADS * HEAD_DIM + 2 * KV_HEADS * HEAD_DIM + IDX_HEADS * IDX_DIM) // LANES

    def page_spec(p):
        return pl.BlockSpec((None, None, IDX_DIM, page),
                            lambda b, s, pt: (layer, pt[b, s * PAGES_PER_STEP + p], 0, 0))

    grid_spec = pltpu.PrefetchScalarGridSpec(
        num_scalar_prefetch=1,
        grid=(DB, n_pages // PAGES_PER_STEP),
        in_specs=[pl.BlockSpec((None, TS, IDX_HEADS * IDX_DIM), lambda b, s, pt: (b, 0, qi_blk)),
                  pl.BlockSpec((None, TS, LANES), lambda b, s, pt: (b, 0, tail_blk)),
                  pl.BlockSpec((None, LANES, IDX_DIM), lambda b, s, pt: (b, 0, 0))]
                 + [page_spec(p) for p in range(PAGES_PER_STEP)],
        out_specs=pl.BlockSpec((None, nt, TS, LANES), lambda b, s, pt: (b, 0, 0, 0)),
        scratch_shapes=[pltpu.VMEM((nt, TS, LANES), I32), pltpu.VMEM((TS, 1), I32)])
    return pl.pallas_call(
        functools.partial(_a1s_kernel, topk=topk, past=n_pages * page),
        grid_spec=grid_spec,
        out_shape=jax.ShapeDtypeStruct((DB, nt, TS, LANES), F32),
        compiler_params=_params(("parallel", "arbitrary")),
        name="a1_sample",
    )(page_table, proj_s, proj_s, knew_bf, *([cache_kidx] * PAGES_PER_STEP))


def _a2s_kernel(pt_ref, q_ref, mask_ref, tzs_ref, knew_ref, vnew_ref, *rest, past):
    k_pages = rest[:PAGES_PER_STEP]
    v_pages = rest[PAGES_PER_STEP:2 * PAGES_PER_STEP]
    o_ref, m_ref, l_ref, acc_ref = rest[2 * PAGES_PER_STEP:]
    s = pl.program_id(1)
    TS = q_ref.shape[0]
    n_slab = tzs_ref.shape[1]

    @pl.when(s == 0)
    def _():
        m_ref[...] = jnp.full(m_ref.shape, NEG, F32)
        l_ref[...] = jnp.zeros(l_ref.shape, F32)
        acc_ref[...] = jnp.zeros(acc_ref.shape, F32)

    qg = [(jnp.concatenate([q_ref[:, (g * GROUP + r) * HEAD_DIM:(g * GROUP + r + 1) * HEAD_DIM]
                            for r in range(GROUP)], axis=0) * QSCALE).astype(BF16)
          for g in range(KV_HEADS)]

    def update(g, kt, vt, tiles):
        slabs = [jnp.minimum((past // LANES) - t, n_slab - 1) for t in tiles]
        mk = jnp.concatenate([mask_ref[t] for t in tiles], axis=1)
        mk = jnp.concatenate([mk] * GROUP, axis=0)
        bias = jnp.concatenate(
            [jnp.concatenate([tzs_ref[g * GROUP + r, sl] for r in range(GROUP)], axis=0) for sl in slabs],
            axis=1)
        sc = _nt(qg[g], kt) + bias + mk
        m_old = m_ref[g]
        m_new = jnp.maximum(m_old, jnp.max(sc, axis=1, keepdims=True))
        alpha = jnp.exp2(m_old - m_new)
        p = jnp.exp2(sc - m_new)
        l_ref[g] = alpha * l_ref[g] + jnp.sum(p, axis=1, keepdims=True)
        acc_ref[g] = alpha * acc_ref[g] + jnp.dot(p.astype(BF16), vt, preferred_element_type=F32)
        m_ref[g] = m_new

    tiles = [s * PAGES_PER_STEP + p for p in range(PAGES_PER_STEP)]
    for g in range(KV_HEADS):
        rows = pl.ds(g, LANES, stride=KV_HEADS)
        kt = jnp.concatenate([kp[rows, :] for kp in k_pages], axis=0).astype(BF16)
        vt = jnp.concatenate([vp[rows, :] for vp in v_pages], axis=0).astype(BF16)
        update(g, kt, vt, tiles)

    @pl.when(s == pl.num_programs(1) - 1)
    def _():
        for g in range(KV_HEADS):
            update(g, knew_ref[:, g * HEAD_DIM:(g + 1) * HEAD_DIM].astype(BF16),
                   vnew_ref[:, g * HEAD_DIM:(g + 1) * HEAD_DIM].astype(BF16), [past // LANES])
        for g in range(KV_HEADS):
            res = acc_ref[g] / l_ref[g]
            for r in range(GROUP):
                h = g * GROUP + r
                o_ref[:, h * HEAD_DIM:(h + 1) * HEAD_DIM] = res[r * TS:(r + 1) * TS, :].astype(o_ref.dtype)


def _a2_sample(page_table, proj_s, mask_s, tzs, knew, vnew, cache_k, cache_v, layer):
    DB, TS, _ = proj_s.shape
    n_pages = page_table.shape[1]
    page = cache_k.shape[2] // KV_HEADS
    kvw = KV_HEADS * HEAD_DIM
    nt = n_pages + 1

    def page_spec(p):
        return pl.BlockSpec((None, None, page * KV_HEADS, HEAD_DIM),
                            lambda b, s, pt: (layer, pt[b, s * PAGES_PER_STEP + p], 0, 0))

    grid_spec = pltpu.PrefetchScalarGridSpec(
        num_scalar_prefetch=1,
        grid=(DB, n_pages // PAGES_PER_STEP),
        in_specs=[pl.BlockSpec((None, TS, N_HEADS * HEAD_DIM), lambda b, s, pt: (b, 0, 0)),
                  pl.BlockSpec((None, nt, TS, LANES), lambda b, s, pt: (b, 0, 0, 0)),
                  pl.BlockSpec(tzs.shape, lambda b, s, pt: (0, 0, 0, 0)),
                  pl.BlockSpec((None, LANES, kvw), lambda b, s, pt: (b, 0, 0)),
                  pl.BlockSpec((None, LANES, kvw), lambda b, s, pt: (b, 0, 0))]
                 + [page_spec(p) for p in range(PAGES_PER_STEP)] * 2,
        out_specs=pl.BlockSpec((None, TS, N_HEADS * HEAD_DIM), lambda b, s, pt: (b, 0, 0)),
        scratch_shapes=[pltpu.VMEM((KV_HEADS, GROUP * TS, 1), F32),
                        pltpu.VMEM((KV_HEADS, GROUP * TS, 1), F32),
                        pltpu.VMEM((KV_HEADS, GROUP * TS, HEAD_DIM), F32)])
    return pl.pallas_call(
        functools.partial(_a2s_kernel, past=n_pages * page),
        grid_spec=grid_spec,
        out_shape=jax.ShapeDtypeStruct((DB, TS, N_HEADS * HEAD_DIM), BF16),
        compiler_params=_params(("parallel", "arbitrary")),
        name="a2_sample",
    )(page_table, proj_s, mask_s, tzs, knew, vnew,
      *([cache_k] * PAGES_PER_STEP), *([cache_v] * PAGES_PER_STEP))


BQ = 512


def _bp_kernel(*refs):
    n_q = len(B_GROUPS) * GROUP
    q_refs = refs[:n_q]
    k_ref, v_ref, band_ref, o_ref, m_ref, l_ref, acc_ref, og_ref, lseg_ref = refs[n_q:]
    step = pl.program_id(2)
    t0 = step * BQ
    first = t0 == 0
    whole = [n for n, (_, dil) in enumerate(B_GROUPS) if dil * QB > BQ]
    assert len(whole) == 1 and B_GROUPS[whole[0]][1] * QB == k_ref.shape[0]

    def band(n, half):
        return jnp.concatenate([band_ref[r, n, half] for r in range(GROUP)], axis=0)

    @pl.when(first)
    def _():
        n = whole[0]
        dil = B_GROUPS[n][1]
        for rho in range(dil):
            cls = pl.ds(rho, QB, stride=dil)
            qs = (jnp.concatenate([qr[cls, :] for qr in q_refs[n * GROUP:(n + 1) * GROUP]], axis=0)
                  * QSCALE).astype(BF16)
            s = _nt(qs, k_ref[cls, :].astype(BF16)) + band(n, 1)
            m = jnp.max(s, axis=1, keepdims=True)
            p = jnp.exp2(s - m)
            l = jnp.sum(p, axis=1, keepdims=True)
            out = jnp.dot(p.astype(BF16), v_ref[cls, :].astype(BF16), preferred_element_type=F32) / l
            lse = jnp.broadcast_to(m + jnp.log2(l), out.shape)
            for r in range(GROUP):
                og_ref[r, cls, :] = out[r * QB:(r + 1) * QB]
                lseg_ref[r, cls, :] = lse[r * QB:(r + 1) * QB]

    rows_step = pl.ds(pl.multiple_of(t0, BQ), BQ)
    for r in range(GROUP):
        m_ref[r] = lseg_ref[r, rows_step, :]
        l_ref[r] = jnp.ones(l_ref.shape[1:], F32)
        acc_ref[r] = og_ref[r, rows_step, :]

    def stacked(ref, rows):
        return jnp.concatenate([ref[r, rows, :] for r in range(GROUP)], axis=0)

    def update(q_ref, rows_q, nrow, key_rows, biases):
        qs = (jnp.concatenate([qr[rows_q, :] for qr in q_ref], axis=0) * QSCALE).astype(BF16)
        kt = jnp.concatenate([k_ref[kr, :] for kr in key_rows], axis=0).astype(BF16)
        vt = jnp.concatenate([v_ref[kr, :] for kr in key_rows], axis=0).astype(BF16)
        s = _nt(qs, kt) + jnp.concatenate(biases, axis=1)
        m_old = stacked(m_ref, rows_q)
        m_new = jnp.maximum(m_old, jnp.max(s, axis=1, keepdims=True))
        alpha = jnp.exp2(m_old - m_new)
        p = jnp.exp2(s - jnp.concatenate([m_new] * len(key_rows), axis=1))
        l_new = alpha * stacked(l_ref, rows_q) + jnp.sum(p, axis=1, keepdims=True)
        acc = alpha * stacked(acc_ref, rows_q) + jnp.dot(p.astype(BF16), vt, preferred_element_type=F32)
        for r in range(GROUP):
            m_ref[r, rows_q, :] = m_new[r * nrow:(r + 1) * nrow]
            l_ref[r, rows_q, :] = l_new[r * nrow:(r + 1) * nrow]
            acc_ref[r, rows_q, :] = acc[r * nrow:(r + 1) * nrow]

    for n, (window, dil) in enumerate(B_GROUPS):
        assert window // dil == QB
        span = dil * QB
        if n in whole:
            continue
        q_ref = q_refs[n * GROUP:(n + 1) * GROUP]
        for rho in range(dil):
            for mt in range(BQ // span):
                def rows(start):
                    return pl.ds(start, QB, stride=dil) if dil > 1 else pl.ds(start, QB)
                off = rho + span * mt
                prev = jnp.maximum(t0 + off - span, rho) if mt == 0 else t0 + off - span
                bias_a = band(n, 0)
                if mt == 0:
                    bias_a = bias_a + jnp.where(first, NEG, 0.0)
                update(q_ref, rows(off), QB, [rows(prev), rows(t0 + off)], [bias_a, band(n, 1)])

    for r in range(GROUP):
        o_ref[:, r * HEAD_DIM:(r + 1) * HEAD_DIM] = (acc_ref[r] / l_ref[r]).astype(o_ref.dtype)


def _b_prompt(qb, kv, band):
    B, T, _ = qb.shape
    nh = N_HEADS
    hd = HEAD_DIM
    gw = GROUP * hd
    assert T % BQ == 0

    def q_spec(n, r):
        whole =B_GROUPS[n][1] * QB > BQ
        if whole:
            return pl.BlockSpec((None, T, hd), lambda b, g, c: (b, 0, (n * KV_HEADS + g) * GROUP + r))
        return pl.BlockSpec((None, BQ, hd), lambda b, g, c: (b, c, (n * KV_HEADS + g) * GROUP + r))

    q_specs = [q_spec(n, r) for n in range(len(B_GROUPS)) for r in range(GROUP)]
    return pl.pallas_call(
        _bp_kernel,
        grid=(B, KV_HEADS, T // BQ),
        in_specs=q_specs + [
                  pl.BlockSpec((None, T, hd), lambda b, g, c: (b, 0, g)),
                  pl.BlockSpec((None, T, hd), lambda b, g, c: (b, 0, KV_HEADS + g)),
                  pl.BlockSpec((GROUP, 3, 2, QB, LANES), lambda b, g, c: (g, 0, 0, 0, 0))],
        out_specs=pl.BlockSpec((None, BQ, gw), lambda b, g, c: (b, c, g)),
        out_shape=jax.ShapeDtypeStruct((B, T, nh * hd), BF16),
        scratch_shapes=[pltpu.VMEM((GROUP, BQ, LANES), F32), pltpu.VMEM((GROUP, BQ, LANES), F32),
                        pltpu.VMEM((GROUP, BQ, hd), F32),
                        pltpu.VMEM((GROUP, T, hd), F32), pltpu.VMEM((GROUP, T, LANES), F32)],
        compiler_params=_params(("parallel", "parallel", "arbitrary")),
        name="b_prompt",
    )(*([qb] * len(q_specs)), kv, kv, band)


def _bs_kernel(q0_ref, q1_ref, q2_ref, kc_ref, vc_ref, kn_ref, vn_ref, tab_ref, o_ref):
    TS = q0_ref.shape[0]
    W = kc_ref.shape[0] // KV_HEADS
    rows = pl.ds(pl.program_id(1), W, stride=KV_HEADS)
    kc = kc_ref[rows, :].astype(BF16)
    vc = vc_ref[rows, :].astype(BF16)
    kn = kn_ref[...].astype(BF16)
    vn = vn_ref[...].astype(BF16)
    ng = len(B_GROUPS)
    R = GROUP * TS
    q = (jnp.concatenate([q_ref[:, r * HEAD_DIM:(r + 1) * HEAD_DIM]
                          for q_ref in (q0_ref, q1_ref, q2_ref) for r in range(GROUP)], axis=0)
         * QSCALE).astype(BF16)
    tab = jnp.concatenate([tab_ref[n] for n in range(ng)], axis=0)
    sc_c = _nt(q, kc) + tab[:, :W]
    sc_n = _nt(q, kn) + tab[:, W:]
    m_rows = jnp.maximum(jnp.max(sc_c, axis=1, keepdims=True), jnp.max(sc_n, axis=1, keepdims=True))
    m = functools.reduce(jnp.maximum, [m_rows[n * R:(n + 1) * R] for n in range(ng)])
    m = jnp.concatenate([m] * ng, axis=0)
    pc = jnp.exp2(sc_c - m)
    pn = jnp.exp2(sc_n - m)
    l_rows = jnp.sum(pc, axis=1, keepdims=True) + jnp.sum(pn, axis=1, keepdims=True)
    acc_rows = (jnp.dot(pc.astype(BF16), vc, preferred_element_type=F32)
                + jnp.dot(pn.astype(BF16), vn, preferred_element_type=F32))
    l = sum(l_rows[n * R:(n + 1) * R] for n in range(ng))
    acc = sum(acc_rows[n * R:(n + 1) * R] for n in range(ng))
    res = acc / l
    for r in range(GROUP):
        o_ref[:, r * HEAD_DIM:(r + 1) * HEAD_DIM] = res[r * TS:(r + 1) * TS, :].astype(o_ref.dtype)


def _b_sample(qb_s, cache_k, cache_v, kv_new, tab):
    DB, TS, NQ = qb_s.shape
    W = cache_k.shape[1] // KV_HEADS
    hd = HEAD_DIM
    return pl.pallas_call(
        _bs_kernel,
        grid=(DB, KV_HEADS),
        in_specs=[pl.BlockSpec((None, TS, GROUP * hd), lambda b, g: (b, 0, g)),
                  pl.BlockSpec((None, TS, GROUP * hd), lambda b, g: (b, 0, KV_HEADS + g)),
                  pl.BlockSpec((None, TS, GROUP * hd), lambda b, g: (b, 0, 2 * KV_HEADS + g)),
                  pl.BlockSpec((None, W * KV_HEADS, hd), lambda b, g: (b, 0, 0)),
                  pl.BlockSpec((None, W * KV_HEADS, hd), lambda b, g: (b, 0, 0)),
                  pl.BlockSpec((None, LANES, hd), lambda b, g: (b, 0, g)),
                  pl.BlockSpec((None, LANES, hd), lambda b, g: (b, 0, KV_HEADS + g)),
                  pl.BlockSpec((len(B_GROUPS), None, GROUP * TS, W + LANES), lambda b, g: (0, g, 0, 0))],
        out_specs=pl.BlockSpec((None, TS, GROUP * hd), lambda b, g: (b, 0, g)),
        out_shape=jax.ShapeDtypeStruct((DB, TS, N_HEADS * hd), BF16),
        compiler_params=_params(("parallel", "arbitrary")),
        name="b_sample",
    )(qb_s, qb_s, qb_s, cache_k, cache_v, kv_new, kv_new, tab)


FFN_TF = 512
HALO = 16


def _ffn_up_kernel(xm_ref, xh_ref, wg_ref, wu_ref, cw_ref, cb_ref, act_ref, st_ref, xe_ref, *,
                   tiles_per_batch):
    i = pl.program_id(0)
    tm = xm_ref.shape[0]

    @pl.when(pl.program_id(1) == 0)
    def _():
        first = (i % tiles_per_batch) == 0
        xh = xh_ref[...]
        xe_ref[0:HALO, :] = jnp.where(first, jnp.zeros_like(xh), xh)
        xe_ref[HALO:, :] = xm_ref[...]

    gate = jnp.dot(xe_ref[...], wg_ref[...], preferred_element_type=F32)
    up = jnp.dot(xm_ref[...], wu_ref[...], preferred_element_type=F32)
    cw = cw_ref[...]
    conv = cb_ref[...] + ((gate[HALO - 2:HALO - 2 + tm] * cw[0:1] + gate[HALO - 1:HALO - 1 + tm] * cw[1:2])
                          + gate[HALO:] * cw[2:3])
    act_ref[...] = (jax.nn.silu(conv) * up).astype(act_ref.dtype)
    st_ref[...] = gate[HALO + tm - (CONV_WIDTH - 1):, :]


def _ffn_up(x_bf, w_g_bf, w_u_bf, conv_w, conv_b, layer, T):
    M, D = x_bf.shape
    F = conv_w.shape[2]
    tm = 1024
    assert T % tm == 0 and M % T == 0 and F % FFN_TF == 0 and tm % HALO == 0
    nf = F // FFN_TF
    hb = tm // HALO
    return pl.pallas_call(
        functools.partial(_ffn_up_kernel, tiles_per_batch=T // tm),
        grid=(M // tm, nf),
        in_specs=[pl.BlockSpec((tm, D), lambda i, j: (i, 0)),
                  pl.BlockSpec((HALO, D), lambda i, j: (jnp.maximum(i * hb - 1, 0), 0)),
                  pl.BlockSpec((None, D, FFN_TF), lambda i, j: (layer, 0, j)),
                  pl.BlockSpec((None, D, FFN_TF), lambda i, j: (layer, 0, j)),
                  pl.BlockSpec((None, CONV_WIDTH, FFN_TF), lambda i, j: (layer, 0, j)),
                  pl.BlockSpec((None, 1, FFN_TF), lambda i, j: (layer, 0, j))],
        out_specs=[pl.BlockSpec((tm, FFN_TF), lambda i, j: (i, j)),
                   pl.BlockSpec((None, CONV_WIDTH - 1, FFN_TF), lambda i, j: (i, 0, j))],
        out_shape=[jax.ShapeDtypeStruct((M, F), BF16),
                   jax.ShapeDtypeStruct((M // tm, CONV_WIDTH - 1, F), F32)],
        scratch_shapes=[pltpu.VMEM((HALO + tm, D), BF16)],
        compiler_params=_params(("parallel", "arbitrary")),
        name="ffn_up",
    )(x_bf, x_bf, w_g_bf, w_u_bf, conv_w, conv_b)


def _mm_ln_kernel(a_ref, w_ref, x_ref, g_ref, b_ref, h_ref, hb_ref, acc_ref, *, alpha):
    k = pl.program_id(1)

    @pl.when(k == 0)
    def _():
        acc_ref[...] = jnp.zeros_like(acc_ref)

    acc_ref[...] += jnp.dot(a_ref[...], w_ref[...], preferred_element_type=F32)

    @pl.when(k == pl.num_programs(1) - 1)
    def _():
        y = alpha * x_ref[...] + acc_ref[...]
        mu = jnp.mean(y, axis=-1, keepdims=True)
        d = y - mu
        var = jnp.mean(d * d, axis=-1, keepdims=True)
        h = d * lax.rsqrt(var + LN_EPS) * g_ref[...] + b_ref[...]
        h_ref[...] = h
        hb_ref[...] = h.astype(hb_ref.dtype)


def _mm_ln(a_bf, w_bf, layer, x, g, b, alpha):
    M, K = a_bf.shape
    D = w_bf.shape[2]
    tm = min(512, M)
    tk = K // 4 if K > 2048 else K // 2
    assert M % tm == 0 and K % tk == 0 and tk % LANES == 0
    return pl.pallas_call(
        functools.partial(_mm_ln_kernel, alpha=alpha),
        grid=(M // tm, K // tk),
        in_specs=[pl.BlockSpec((tm, tk), lambda i, k: (i, k)),
                  pl.BlockSpec((None, tk, D), lambda i, k: (layer, k, 0)),
                  pl.BlockSpec((tm, D), lambda i, k: (i, 0)),
                  pl.BlockSpec((1, D), lambda i, k: (0, 0)),
                  pl.BlockSpec((1, D), lambda i, k: (0, 0))],
        out_specs=[pl.BlockSpec((tm, D), lambda i, k: (i, 0)),
                   pl.BlockSpec((tm, D), lambda i, k: (i, 0))],
        out_shape=[jax.ShapeDtypeStruct((M, D), F32), jax.ShapeDtypeStruct((M, D), BF16)],
        scratch_shapes=[pltpu.VMEM((tm, D), F32)],
        compiler_params=_params(("parallel", "arbitrary")),
        name="mm_ln",
    )(a_bf, w_bf, x, g.reshape(1, D), b.reshape(1, D))


def _layer_norm(x, g, b):
    mu = x.mean(-1, keepdims=True)
    var = jnp.square(x - mu).mean(-1, keepdims=True)
    return (x - mu) * lax.rsqrt(var + LN_EPS) * g + b


def _pick_tm(M):
    for tm in (1024, 512, 256, 128, 64, 32, 16):
        if M % tm == 0:
            return tm
    raise ValueError(M)


def _pick_tn(N):
    for tn in (512, 384, 256, 128):
        if N % tn == 0:
            return tn
    raise ValueError(N)


def _proj(x_bf, w_bf, layer=0):
    return _mm(x_bf, w_bf, layer, _pick_tm(x_bf.shape[0]), _pick_tn(w_bf.shape[2]))


def _ffn_act_sample(hs_bf, state, w_g_bf, w_u_bf, conv_w, conv_b, layer, DB, TS):
    conv_w, conv_b = conv_w[layer], conv_b[layer]
    F = conv_w.shape[1]
    gate = _proj(hs_bf, w_g_bf, layer).reshape(DB, TS, F)
    up = _proj(hs_bf, w_u_bf, layer).reshape(DB, TS, F)
    ext = jnp.concatenate([state, gate], axis=1)
    conv = conv_b + sum(ext[:, j:j + TS] * conv_w[j] for j in range(CONV_WIDTH))
    act = (jax.nn.silu(conv) * up).astype(BF16).reshape(DB * TS, F)
    return act, ext[:, ext.shape[1] - (CONV_WIDTH - 1):]


def _table_indices(T, W, TS):
    U = SAT_DIST + TZ_OFF + QB
    assert T <= T5_MAX_DISTANCE
    u = np.arange(U)[:, None]
    c = np.arange(LANES)[None, :]
    tz_idx = _t5_bucket(jnp.asarray(np.maximum(u - c - TZ_OFF, 0), I32))
    a = np.arange(QB)[:, None]
    band = []
    for window, dil in B_GROUPS:
        for off in (QB, 0):
            d = a + off - c
            ok = (d >= 0) & (d <= window // dil)
            band.append(jnp.where(jnp.asarray(ok), _t5_bucket(jnp.asarray(np.maximum(d, 0) * dil, I32)),
                                  NUM_BUCKETS))
    band_idx = jnp.concatenate(band, axis=0)
    nkt = W // LANES + 1
    t = np.arange(TS)[None, :, None]
    key = (np.arange(nkt)[:, None, None] * LANES + np.arange(LANES)[None, None, :])
    d = W + t - key
    tabs = []
    for window, dil in B_GROUPS:
        ok = (d >= 0) & (d % dil == 0) & (d <= window)
        tabs.append(jnp.where(jnp.asarray(ok), _t5_bucket(jnp.asarray(np.maximum(d, 0), I32)), NUM_BUCKETS))
    samp_idx = jnp.concatenate(tabs, axis=0).reshape(-1, LANES)
    rows = [tz_idx, band_idx, samp_idx]
    total = sum(r.shape[0] for r in rows)
    pad = (-total) % 128
    if pad:
        rows.append(jnp.full((pad, LANES), NUM_BUCKETS, I32))
    return jnp.concatenate(rows, axis=0), U, nkt


def kernel(x_prompt, x_sample, cache_k_a, cache_v_a, cache_kidx_a, cache_k_b, cache_v_b, state_ffn, page_table, a_w_in, a_w_o, a_kn_g, a_kn_b, b_w_kv, b_w_q, b_w_o, ffn_w_up, ffn_conv_w, ffn_conv_b, ffn_w_down, ln_g, ln_b, rel_bias):
    B, T, D = x_prompt.shape
    DB, TS, _ = x_sample.shape
    depth = ffn_w_up.shape[0]
    n_a = a_w_in.shape[0]
    d_ff = ffn_w_down.shape[1]
    W = cache_k_b.shape[1]
    n_pages = page_table.shape[1]
    page = cache_k_a.shape[2]
    past = n_pages * page
    alpha = (2 * depth) ** 0.25
    kvw = KV_HEADS * HEAD_DIM
    a_q = N_HEADS * HEAD_DIM
    a_in = a_w_in.shape[2]
    np_cols = ((a_in + 511) // 512) * 512
    ki0 = a_q + 2 * kvw + IDX_HEADS * IDX_DIM

    idx_all, U, nkt = _table_indices(T, W, TS)
    tabs = _bias_tables(rel_bias * LOG2E, idx_all)
    tz = tabs[:, :U]
    band = tabs[:, U:U + 6 * QB].reshape(N_HEADS, 3, 2, QB, LANES)
    samp = tabs[:, U + 6 * QB:U + 6 * QB + 3 * nkt * TS].reshape(KV_HEADS, GROUP, 3, nkt, TS, LANES)
    samp = samp.transpose(2, 0, 1, 4, 3, 5).reshape(3, KV_HEADS, GROUP * TS, nkt * LANES)
    n_slab = SAT_DIST // LANES + 1
    tzs = tz[:, TZ_OFF:TZ_OFF + n_slab * LANES].reshape(N_HEADS, n_slab, LANES, LANES)[:, :, :TS, :]

    cache_k_a2 = cache_k_a.reshape(n_a, -1, page * KV_HEADS, HEAD_DIM)
    cache_v_a2 = cache_v_a.reshape(n_a, -1, page * KV_HEADS, HEAD_DIM)
    cache_k_b2 = cache_k_b.reshape(DB, W * KV_HEADS, HEAD_DIM)
    cache_v_b2 = cache_v_b.reshape(DB, W * KV_HEADS, HEAD_DIM)
    cache_kidx_t = jnp.swapaxes(cache_kidx_a, 2, 3)

    def pad_rows(x, n):
        return jnp.pad(x, ((0, 0), (0, n - x.shape[1]), (0, 0)))

    f_pad = ((d_ff + FFN_TF - 1) // FFN_TF) * FFN_TF
    pad_f = f_pad - d_ff
    w_g_all = jnp.pad(ffn_w_up[:, :, :d_ff], ((0, 0), (0, 0), (0, pad_f))).astype(BF16)
    w_u_all = jnp.pad(ffn_w_up[:, :, d_ff:], ((0, 0), (0, 0), (0, pad_f))).astype(BF16)
    w_down_all = jnp.pad(ffn_w_down, ((0, 0), (0, pad_f), (0, 0))).astype(BF16)
    conv_w_all = jnp.pad(ffn_conv_w, ((0, 0), (0, 0), (0, pad_f)))
    conv_b_all = jnp.pad(ffn_conv_b, ((0, 0), (0, pad_f))).reshape(depth, 1, f_pad)
    w_in_all = jnp.pad(a_w_in, ((0, 0), (0, 0), (0, np_cols - a_in))).astype(BF16)
    w_o_a, w_o_b = a_w_o.astype(BF16), b_w_o.astype(BF16)
    w_q_all = b_w_q.astype(BF16)
    w_kv_bf = b_w_kv.astype(BF16)[None]

    hp = x_prompt.reshape(B * T, D)
    hs = x_sample.reshape(DB * TS, D)
    hp_bf, hs_bf = hp.astype(BF16), hs.astype(BF16)
    ka_p, va_p, kia_p, ka_s, va_s, kia_s, ffn_p, ffn_s = [], [], [], [], [], [], [], []
    for layer in range(depth):
        if layer < n_a:
            a = layer
            w_o_bf, w_o_layer = w_o_a, a
            proj = _proj(hp_bf, w_in_all, a).reshape(B, T, np_cols)
            k = proj[..., a_q:a_q + kvw].reshape(B, T, KV_HEADS, HEAD_DIM)
            v = proj[..., a_q + kvw:a_q + 2 * kvw].reshape(B, T, KV_HEADS, HEAD_DIM)
            ki = _layer_norm(proj[..., ki0:ki0 + IDX_DIM], a_kn_g[a], a_kn_b[a])
            ka_p.append(k); va_p.append(v); kia_p.append(ki)
            mask = _a1_prompt(ki.astype(BF16), proj, min(TOPK_MAX, T // 4))
            o = _a2_prompt(proj, mask, tz).reshape(B * T, a_q)
            proj_s = _proj(hs_bf, w_in_all, a).reshape(DB, TS, np_cols)
            k_s = proj_s[..., a_q:a_q + kvw]
            v_s = proj_s[..., a_q + kvw:a_q + 2 * kvw]
            ki_s = _layer_norm(proj_s[..., ki0:ki0 + IDX_DIM], a_kn_g[a], a_kn_b[a])
            ka_s.append(k_s.reshape(DB, TS, KV_HEADS, HEAD_DIM))
            va_s.append(v_s.reshape(DB, TS, KV_HEADS, HEAD_DIM))
            kia_s.append(ki_s)
            mask_s = _a1_sample(page_table, proj_s, pad_rows(ki_s, LANES).astype(BF16), cache_kidx_t, a,
                                min(TOPK_MAX, (past + TS) // 4))
            o_s = _a2_sample(page_table, proj_s, mask_s, tzs, pad_rows(k_s, LANES), pad_rows(v_s, LANES),
                             cache_k_a2, cache_v_a2, a).reshape(DB * TS, a_q)
        else:
            if layer == n_a:
                kv_p = _proj(hp_bf, w_kv_bf).reshape(B, T, 2 * kvw)
                kv_s = _proj(hs_bf, w_kv_bf).reshape(DB, TS, 2 * kvw)
                kv_s_pad = pad_rows(kv_s, LANES)
            bl = layer - n_a
            w_o_bf, w_o_layer = w_o_b, bl
            o = _b_prompt(_proj(hp_bf, w_q_all, bl).reshape(B, T, -1), kv_p, band).reshape(B * T, a_q)
            o_s = _b_sample(_proj(hs_bf, w_q_all, bl).reshape(DB, TS, -1), cache_k_b2, cache_v_b2, kv_s_pad,
                            samp).reshape(DB * TS, a_q)
        hp, hp_bf = _mm_ln(o, w_o_bf, w_o_layer, hp, ln_g[layer, 0], ln_b[layer, 0], alpha)
        hs, hs_bf = _mm_ln(o_s, w_o_bf, w_o_layer, hs, ln_g[layer, 0], ln_b[layer, 0], alpha)
        act, st = _ffn_up(hp_bf, w_g_all, w_u_all, conv_w_all, conv_b_all, layer, T)
        ffn_p.append(st.reshape(B, -1, CONV_WIDTH - 1, f_pad)[:, -1, :, :d_ff])
        act_s, st_s = _ffn_act_sample(hs_bf, jnp.pad(state_ffn[layer], ((0, 0), (0, 0), (0, pad_f))),
                                      w_g_all, w_u_all, conv_w_all, conv_b_all, layer, DB, TS)
        ffn_s.append(st_s[..., :d_ff])
        hp, hp_bf = _mm_ln(act, w_down_all, layer, hp, ln_g[layer, 1], ln_b[layer, 1], alpha)
        hs, hs_bf = _mm_ln(act_s, w_down_all, layer, hs, ln_g[layer, 1], ln_b[layer, 1], alpha)
    keep = min(max(w for w, _ in B_GROUPS), T)
    kb_p = kv_p[..., :kvw].reshape(B, T, KV_HEADS, HEAD_DIM)
    vb_p = kv_p[..., kvw:].reshape(B, T, KV_HEADS, HEAD_DIM)
    kb_s = kv_s[..., :kvw].reshape(DB, TS, KV_HEADS, HEAD_DIM)
    vb_s = kv_s[..., kvw:].reshape(DB, TS, KV_HEADS, HEAD_DIM)
    return (hp.reshape(B, T, D), hs.reshape(DB, TS, D), jnp.stack(ka_p), jnp.stack(va_p), jnp.stack(kia_p),
            jnp.stack(ka_s), jnp.stack(va_s), jnp.stack(kia_s), kb_p[:, T - keep:], vb_p[:, T - keep:],
            kb_s, vb_s, jnp.stack(ffn_p), jnp.stack(ffn_s))
```

```python
import functools
import math

import numpy as np
import jax
import jax.numpy as jnp
from jax import lax
from jax.experimental import pallas as pl
from jax.experimental.pallas import tpu as pltpu

F32 = jnp.float32
BF16 = jnp.bfloat16
I32 = jnp.int32

HEAD_DIM = 128
N_HEADS = 16
KV_HEADS = 4
GROUP = N_HEADS // KV_HEADS
IDX_HEADS = 16
IDX_DIM = 64
TOPK_MAX = 256
B_GROUPS = ((128, 1), (512, 4), (2048, 16))
NUM_BUCKETS = 32
T5_MAX_DISTANCE = 2048
LN_EPS = 1e-5
NEG = -1e30
CONV_WIDTH = 3

LANES = 128
QB = 128
KC = 512
A2_KC = KC
TZ_OFF = A2_KC - QB
SAT_DIST = T5_MAX_DISTANCE + QB
INT_MIN = -2 ** 31
KEY_NEG_INF = INT_MIN + 0x7FFFFF
VMEM_LIMIT = 56 * 1024 * 1024

LOG2E = 1.4426950408889634
QSCALE = HEAD_DIM ** -0.5 * LOG2E

NT_DIMS = (((1,), (1,)), ((), ()))


def _nt(a, b):
    return lax.dot_general(a, b, NT_DIMS, preferred_element_type=F32)


def _params(sem):
    return pltpu.CompilerParams(dimension_semantics=sem, vmem_limit_bytes=VMEM_LIMIT)


def _mm_kernel(x_ref, w_ref, o_ref):
    o_ref[...] = jnp.dot(x_ref[...], w_ref[...], preferred_element_type=F32).astype(o_ref.dtype)


def _mm(x, w, layer, tm, tn, out_dtype=F32):
    M, K = x.shape
    N = w.shape[2]
    assert M % tm == 0 and N % tn == 0, (M, N, tm, tn)
    return pl.pallas_call(
        _mm_kernel,
        grid=(M // tm, N // tn),
        in_specs=[pl.BlockSpec((tm, K), lambda i, j: (i, 0)),
                  pl.BlockSpec((None, K, tn), lambda i, j: (layer, 0, j))],
        out_specs=pl.BlockSpec((tm, tn), lambda i, j: (i, j)),
        out_shape=jax.ShapeDtypeStruct((M, N), out_dtype),
        compiler_params=_params(("parallel", "arbitrary")),
        name="mm",
    )(x, w)


def _t5_bucket(dist):
    dist = jnp.maximum(dist, 0)
    exact = NUM_BUCKETS // 2
    far = exact + (jnp.log(jnp.maximum(dist, 1).astype(F32) / exact)
                   / math.log(T5_MAX_DISTANCE / exact) * (NUM_BUCKETS - exact)).astype(I32)
    return jnp.where(dist < exact, dist, jnp.minimum(far, NUM_BUCKETS - 1))


def _table_kernel(rb_ref, idx_ref, o_ref):
    slab = 16

    def body(i, carry):
        r0 = pl.multiple_of(i * slab, slab)
        idx = idx_ref[pl.ds(r0, slab), :]
        accs = [jnp.full(idx.shape, NEG, F32)] * N_HEADS
        for k in range(NUM_BUCKETS):
            hit = idx == k
            accs = [jnp.where(hit, rb_ref[k, h], accs[h]) for h in range(N_HEADS)]
        for h in range(N_HEADS):
            o_ref[h, pl.ds(r0, slab), :] = accs[h]
        return carry

    lax.fori_loop(0, idx_ref.shape[0] // slab, body, 0)


def _bias_tables(rel_bias, idx):
    R = idx.shape[0]
    tr = 128
    assert R % tr == 0
    return pl.pallas_call(
        _table_kernel,
        grid=(R // tr,),
        in_specs=[pl.BlockSpec(memory_space=pltpu.SMEM),
                  pl.BlockSpec((tr, LANES), lambda i: (i, 0))],
        out_specs=pl.BlockSpec((N_HEADS, tr, LANES), lambda i: (0, i, 0)),
        out_shape=jax.ShapeDtypeStruct((N_HEADS, R, LANES), F32),
        compiler_params=_params(("arbitrary",)),
        name="bias_tables",
    )(rel_bias, idx)


def _sortable(x):
    bits = lax.bitcast_convert_type(x, I32)
    return bits ^ ((bits >> 31) & jnp.int32(0x7FFFFFFF))


def _kth_largest(count_ge, shape, k):
    def body(t, ans):
        cand_u = ans | (jnp.int32(1) << (31 - t))
        cnt = count_ge(cand_u ^ jnp.int32(INT_MIN))
        return jnp.where(cnt >= k, cand_u, ans)
    ans = lax.fori_loop(0, 32, body, jnp.zeros(shape, I32))
    return ans ^ jnp.int32(INT_MIN)


def _tie_bound(count_eq_below, shape, need, nbits):
    def body(t, ans):
        cand = ans | (jnp.int32(1) << (nbits - 1 - t))
        return jnp.where(count_eq_below(cand) < need, cand, ans)
    return lax.fori_loop(0, nbits, body, jnp.zeros(shape, I32))


A1_TILES = 4


def _a1p_kernel(kin_ref, qi_ref, tail_ref, mask_ref, st_ref, jb_ref, *, topk):
    i = pl.program_id(1)
    T = kin_ref.shape[0]
    NQ = A1_TILES * QB
    KS = KC // A1_TILES
    nchunk = T // KC
    nbits = (T - 1).bit_length()
    tail_t = jnp.concatenate([tail_ref[t * QB:(t + 1) * QB, :].T for t in range(A1_TILES)], axis=1)
    wi_t = tail_t[IDX_DIM:IDX_DIM + IDX_HEADS, :] * (IDX_HEADS ** -0.5 * IDX_DIM ** -0.5)
    qpos = i * NQ + lax.broadcasted_iota(I32, (KC, NQ), 1)
    row = lax.broadcasted_iota(I32, (KC, NQ), 0)
    qpos_s = i * NQ + lax.broadcasted_iota(I32, (KS, NQ), 1)
    row_s = lax.broadcasted_iota(I32, (KS, NQ), 0)

    nj = (i * NQ) // KC + 1

    for c in range(nchunk):
        @pl.when(c < nj)
        def _():
            for sub in range(KC // KS):
                r0 = c * KC + sub * KS
                kc = kin_ref[r0:r0 + KS, :]
                acc = jnp.zeros((KS, NQ), F32)
                for h in range(IDX_HEADS):
                    qh = qi_ref[:, h * IDX_DIM:(h + 1) * IDX_DIM].astype(BF16)
                    acc = acc + jnp.maximum(_nt(kc, qh), 0.0) * wi_t[h:h + 1, :]
                acc = jnp.where(r0 + row_s <= qpos_s, acc, -jnp.inf)
                st_ref[r0:r0 + KS, :] = _sortable(acc)

    def select(nvis):
        def count(pred):
            cnt = jnp.zeros((1, NQ), F32)
            for c in range(nvis):
                blk = st_ref[c * KC:(c + 1) * KC, :]
                cnt = cnt + jnp.sum(jnp.where(pred(blk, c * KC + row), 1.0, 0.0), axis=0, keepdims=True)
            return cnt

        thr = _kth_largest(lambda cand: count(lambda blk, _: blk >= cand), (1, NQ), float(topk))
        need = float(topk) - count(lambda blk, _: blk > thr)
        n_eq = count(lambda blk, _: blk == thr)
        excess = jnp.where((n_eq > need) & (thr != KEY_NEG_INF), 1.0, 0.0)
        jb_ref[...] = jnp.full((1, NQ), T, I32)

        @pl.when(jnp.max(excess) > 0.0)
        def _():
            jb_ref[...] = _tie_bound(
                lambda cand: count(lambda blk, kpos: jnp.where(blk == thr, kpos, T) < cand),
                (1, NQ), need, nbits)

        jb = jb_ref[...]
        for c in range(nchunk):
            if c < nvis:
                blk = st_ref[c * KC:(c + 1) * KC, :]
                kpos = c * KC + row
                rank_pos = jnp.where(blk == thr, kpos, jnp.where(blk > thr, -1, T + 1))
                sel = jnp.where(kpos <= qpos, rank_pos, T + 1) <= jb
                m_t = jnp.where(sel, 0.0, NEG)
                for t in range(A1_TILES):
                    for s4 in range(KC // QB):
                        mask_ref[t, c, :, s4 * QB:(s4 + 1) * QB] = (
                            m_t[s4 * QB:(s4 + 1) * QB, t * QB:(t + 1) * QB].T.astype(mask_ref.dtype))
            else:
                for t in range(A1_TILES):
                    mask_ref[t, c] = jnp.full((QB, KC), NEG, mask_ref.dtype)

    for nvis in range(1, nchunk + 1):
        pl.when(nj == nvis)(functools.partial(select, nvis))


def _a1_prompt(kin_bf, proj, topk):
    B, T, _ = proj.shape
    nq, nc = T // QB, T // KC
    assert topk <= KC
    assert (KC // QB) % A1_TILES == 0 and nq % A1_TILES == 0
    nqs = A1_TILES * QB
    qi_blk = (N_HEADS * HEAD_DIM + 2 * KV_HEADS * HEAD_DIM) // (IDX_HEADS * IDX_DIM)
    tail_blk = (N_HEADS * HEAD_DIM + 2 * KV_HEADS * HEAD_DIM + IDX_HEADS * IDX_DIM) // LANES
    return pl.pallas_call(
        functools.partial(_a1p_kernel, topk=topk),
        grid=(B, nq // A1_TILES),
        in_specs=[pl.BlockSpec((None, T, IDX_DIM), lambda b, i: (b, 0, 0)),
                  pl.BlockSpec((None, nqs, IDX_HEADS * IDX_DIM), lambda b, i: (b, i, qi_blk)),
                  pl.BlockSpec((None, nqs, LANES), lambda b, i: (b, i, tail_blk))],
        out_specs=pl.BlockSpec((None, A1_TILES, nc, QB, KC), lambda b, i: (b, i, 0, 0, 0)),
        out_shape=jax.ShapeDtypeStruct((B, nq, nc, QB, KC), BF16),
        scratch_shapes=[pltpu.VMEM((T, nqs), I32), pltpu.VMEM((1, nqs), I32)],
        compiler_params=_params(("parallel", "arbitrary")),
        name="a1_prompt",
    )(kin_bf, proj, proj)


A2_TILES = 2


def _a2p_kernel(q_ref, k_ref, v_ref, mask_ref, tz_ref, o_ref):
    for t in range(A2_TILES):
        _a2p_tile(pl.program_id(2) * A2_TILES + t, q_ref.at[t * QB:(t + 1) * QB], k_ref, v_ref,
                  mask_ref.at[t], tz_ref, o_ref.at[t * QB:(t + 1) * QB])


def _a2p_tile(i, q_ref, k_ref, v_ref, mask_ref, tz_ref, o_ref):
    nj = (i * QB) // A2_KC + 1
    HP = GROUP
    R = HP * QB
    heads = [list(range(c * HP, (c + 1) * HP)) for c in range(GROUP // HP)]
    qs = [(jnp.concatenate([q_ref[:, r * HEAD_DIM:(r + 1) * HEAD_DIM] for r in hs], axis=0)
           * QSCALE).astype(BF16) for hs in heads]

    def body(j, carry):
        k0 = pl.multiple_of(j * A2_KC, A2_KC)
        kt = k_ref[pl.ds(k0, A2_KC), :].astype(BF16)
        vt = v_ref[pl.ds(k0, A2_KC), :].astype(BF16)
        base = i * QB - j * A2_KC + TZ_OFF
        mk = jnp.concatenate([mask_ref[j * (A2_KC // KC) + c] for c in range(A2_KC // KC)],
                             axis=1).astype(F32)
        mk = jnp.concatenate([mk] * HP, axis=0)
        out = []
        for hs, q, (m, l, acc) in zip(heads, qs, carry):
            bias = jnp.concatenate(
                [jnp.concatenate(
                    [tz_ref[r, pl.ds(pl.multiple_of(base - QB * s4, QB), QB), :]
                     for s4 in range(A2_KC // QB)], axis=1) for r in hs], axis=0)
            s = _nt(q, kt) + bias + mk
            m_new = jnp.maximum(m, jnp.max(s, axis=1, keepdims=True))
            alpha = jnp.exp2(m - m_new)
            p = jnp.exp2(s - m_new)
            l = alpha * l + jnp.sum(p, axis=1, keepdims=True)
            acc = alpha * acc + jnp.dot(p.astype(BF16), vt, preferred_element_type=F32)
            out.append((m_new, l, acc))
        return tuple(out)

    init = tuple((jnp.full((R, 1), NEG, F32), jnp.zeros((R, 1), F32), jnp.zeros((R, HEAD_DIM), F32))
                 for _ in heads)
    final = lax.fori_loop(0, nj, body, init)
    for hs, (m, l, acc) in zip(heads, final):
        res = acc / l
        for n, r in enumerate(hs):
            o_ref[:, r * HEAD_DIM:(r + 1) * HEAD_DIM] = res[n * QB:(n + 1) * QB, :].astype(o_ref.dtype)


def _a2_prompt(proj, mask, tz):
    B, T, _ = proj.shape
    nq, nc = T // QB, T // KC
    gw = GROUP * HEAD_DIM
    k_blk0 = N_HEADS * HEAD_DIM // HEAD_DIM
    v_blk0 = k_blk0 + KV_HEADS
    U = tz.shape[1]
    assert nq % A2_TILES == 0
    qr = A2_TILES * QB
    return pl.pallas_call(
        _a2p_kernel,
        grid=(KV_HEADS, B, nq // A2_TILES),
        in_specs=[pl.BlockSpec((None, qr, gw), lambda g, b, i: (b, i, g)),
                  pl.BlockSpec((None, T, HEAD_DIM), lambda g, b, i: (b, 0, k_blk0 + g)),
                  pl.BlockSpec((None, T, HEAD_DIM), lambda g, b, i: (b, 0, v_blk0 + g)),
                  pl.BlockSpec((None, A2_TILES, nc, QB, KC), lambda g, b, i: (b, i, 0, 0, 0)),
                  pl.BlockSpec((GROUP, U, LANES), lambda g, b, i: (g, 0, 0))],
        out_specs=pl.BlockSpec((None, qr, gw), lambda g, b, i: (b, i, g)),
        out_shape=jax.ShapeDtypeStruct((B, T, N_HEADS * HEAD_DIM), BF16),
        compiler_params=_params(("parallel", "parallel", "arbitrary")),
        name="a2_prompt",
    )(proj, proj, proj, mask, tz)


PAGES_PER_STEP = 32


def _a1s_kernel(pt_ref, qi_ref, tail_ref, knew_ref, *rest, topk, past):
    page_refs = rest[:PAGES_PER_STEP]
    mask_ref, st_ref, jb_ref = rest[PAGES_PER_STEP:]
    s = pl.program_id(1)
    nt, TS, _ = st_ref.shape
    L = nt * LANES
    nbits = (L - 1).bit_length()
    qi = jnp.concatenate([qi_ref[:, h * IDX_DIM:(h + 1) * IDX_DIM] for h in range(IDX_HEADS)],
                         axis=0).astype(BF16)
    w_col = jnp.concatenate([tail_ref[:, IDX_DIM + h:IDX_DIM + h + 1] for h in range(IDX_HEADS)],
                            axis=0) * (IDX_HEADS ** -0.5 * IDX_DIM ** -0.5)

    def scores(qk):
        sc = jnp.maximum(qk, 0.0) * w_col
        return jnp.sum(sc.reshape(IDX_HEADS, TS, qk.shape[1]), axis=0)

    half = PAGES_PER_STEP // 2
    for part in range(2):
        kt = jnp.concatenate([page_refs[part * half + p][...] for p in range(half)], axis=1).astype(BF16)
        keys = _sortable(scores(jnp.dot(qi, kt, preferred_element_type=F32)))
        for p in range(half):
            st_ref[s * PAGES_PER_STEP + part * half + p] = keys[:, p * LANES:(p + 1) * LANES]

    @pl.when(s == pl.num_programs(1) - 1)
    def _():
        t = lax.broadcasted_iota(I32, (TS, LANES), 0)
        c = lax.broadcasted_iota(I32, (TS, LANES), 1)
        sc = jnp.where(c <= t, scores(_nt(qi, knew_ref[...])), -jnp.inf)
        st_ref[nt - 1] = _sortable(sc)

        kpos = (lax.broadcasted_iota(I32, (nt, TS, LANES), 0) * LANES
                + lax.broadcasted_iota(I32, (nt, TS, LANES), 2))
        qpos = past + lax.broadcasted_iota(I32, (nt, TS, LANES), 1)

        def count(pred):
            per_lane = jnp.sum(jnp.where(pred(st_ref[...]), 1.0, 0.0), axis=0)
            return jnp.sum(per_lane, axis=1, keepdims=True)

        thr = _kth_largest(lambda cand: count(lambda k: k >= cand[None]), (TS, 1), float(topk))
        thr3 = thr[None]
        need = float(topk) - count(lambda k: k > thr3)
        n_eq = count(lambda k: k == thr3)
        excess = jnp.where((n_eq > need) & (thr != KEY_NEG_INF), 1.0, 0.0)
        jb_ref[...] = jnp.full((TS, 1), L, I32)

        @pl.when(jnp.max(excess) > 0.0)
        def _():
            jb_ref[...] = _tie_bound(
                lambda cand: count(lambda k: jnp.where(k == thr3, kpos, L) < cand[None]),
                (TS, 1), need, nbits)

        keys = st_ref[...]
        rank_pos = jnp.where(keys == thr3, kpos, jnp.where(keys > thr3, -1, L + 1))
        sel = jnp.where(kpos <= qpos, rank_pos, L + 1) <= jb_ref[...][None]
        mask_ref[...] = jnp.where(sel, 0.0, NEG)


def _a1_sample(page_table, proj_s, knew_bf, cache_kidx, layer, topk):
    DB, TS, _ = proj_s.shape
    n_pages = page_table.shape[1]
    page = cache_kidx.shape[3]
    assert page == LANES and n_pages % PAGES_PER_STEP == 0 and topk <= n_pages * page
    nt = n_pages + 1
    qi_blk = (N_HEADS * HEAD_DIM + 2 * KV_HEADS * HEAD_DIM) // (IDX_HEADS * IDX_DIM)
    tail_blk = (N_HEADS * HEAD_DIM + 2 * KV_HEADS * HEAD_DIM + IDX_HEADS * IDX_DIM) // LANES

    def page_spec(p):
        return pl.BlockSpec((None, None, IDX_DIM, page),
                            lambda b, s, pt: (layer, pt[b, s * PAGES_PER_STEP + p], 0, 0))

    grid_spec = pltpu.PrefetchScalarGridSpec(
        num_scalar_prefetch=1,
        grid=(DB, n_pages // PAGES_PER_STEP),
        in_specs=[pl.BlockSpec((None, TS, IDX_HEADS * IDX_DIM), lambda b, s, pt: (b, 0, qi_blk)),
                  pl.BlockSpec((None, TS, LANES), lambda b, s, pt: (b, 0, tail_blk)),
                  pl.BlockSpec((None, LANES, IDX_DIM), lambda b, s, pt: (b, 0, 0))]
                 + [page_spec(p) for p in range(PAGES_PER_STEP)],
        out_specs=pl.BlockSpec((None, nt, TS, LANES), lambda b, s, pt: (b, 0, 0, 0)),
        scratch_shapes=[pltpu.VMEM((nt, TS, LANES), I32), pltpu.VMEM((TS, 1), I32)])
    return pl.pallas_call(
        functools.partial(_a1s_kernel, topk=topk, past=n_pages * page),
        grid_spec=grid_spec,
        out_shape=jax.ShapeDtypeStruct((DB, nt, TS, LANES), F32),
        compiler_params=_params(("parallel", "arbitrary")),
        name="a1_sample",
    )(page_table, proj_s, proj_s, knew_bf, *([cache_kidx] * PAGES_PER_STEP))


def _a2s_kernel(pt_ref, q_ref, mask_ref, tzs_ref, knew_ref, vnew_ref, *rest, past):
    k_pages = rest[:PAGES_PER_STEP]
    v_pages = rest[PAGES_PER_STEP:2 * PAGES_PER_STEP]
    o_ref, m_ref, l_ref, acc_ref = rest[2 * PAGES_PER_STEP:]
    s = pl.program_id(1)
    TS = q_ref.shape[0]
    n_slab = tzs_ref.shape[1]

    @pl.when(s == 0)
    def _():
        m_ref[...] = jnp.full(m_ref.shape, NEG, F32)
        l_ref[...] = jnp.zeros(l_ref.shape, F32)
        acc_ref[...] = jnp.zeros(acc_ref.shape, F32)

    qg = [(jnp.concatenate([q_ref[:, (g * GROUP + r) * HEAD_DIM:(g * GROUP + r + 1) * HEAD_DIM]
                            for r in range(GROUP)], axis=0) * QSCALE).astype(BF16)
          for g in range(KV_HEADS)]

    def update(g, kt, vt, tiles):
        slabs = [jnp.minimum((past // LANES) - t, n_slab - 1) for t in tiles]
        mk = jnp.concatenate([mask_ref[t] for t in tiles], axis=1)
        mk = jnp.concatenate([mk] * GROUP, axis=0)
        bias = jnp.concatenate(
            [jnp.concatenate([tzs_ref[g * GROUP + r, sl] for r in range(GROUP)], axis=0) for sl in slabs],
            axis=1)
        sc = _nt(qg[g], kt) + bias + mk
        m_old = m_ref[g]
        m_new = jnp.maximum(m_old, jnp.max(sc, axis=1, keepdims=True))
        alpha = jnp.exp2(m_old - m_new)
        p = jnp.exp2(sc - m_new)
        l_ref[g] = alpha * l_ref[g] + jnp.sum(p, axis=1, keepdims=True)
        acc_ref[g] = alpha * acc_ref[g] + jnp.dot(p.astype(BF16), vt, preferred_element_type=F32)
        m_ref[g] = m_new

    tiles = [s * PAGES_PER_STEP + p for p in range(PAGES_PER_STEP)]
    for g in range(KV_HEADS):
        rows = pl.ds(g, LANES, stride=KV_HEADS)
        kt = jnp.concatenate([kp[rows, :] for kp in k_pages], axis=0).astype(BF16)
        vt = jnp.concatenate([vp[rows, :] for vp in v_pages], axis=0).astype(BF16)
        update(g, kt, vt, tiles)

    @pl.when(s == pl.num_programs(1) - 1)
    def _():
        for g in range(KV_HEADS):
            update(g, knew_ref[:, g * HEAD_DIM:(g + 1) * HEAD_DIM].astype(BF16),
                   vnew_ref[:, g * HEAD_DIM:(g + 1) * HEAD_DIM].astype(BF16), [past // LANES])
        for g in range(KV_HEADS):
            res = acc_ref[g] / l_ref[g]
            for r in range(GROUP):
                h = g * GROUP + r
                o_ref[:, h * HEAD_DIM:(h + 1) * HEAD_DIM] = res[r * TS:(r + 1) * TS, :].astype(o_ref.dtype)


def _a2_sample(page_table, proj_s, mask_s, tzs, knew, vnew, cache_k, cache_v, layer):
    DB, TS, _ = proj_s.shape
    n_pages = page_table.shape[1]
    page = cache_k.shape[2] // KV_HEADS
    kvw = KV_HEADS * HEAD_DIM
    nt = n_pages + 1

    def page_spec(p):
        return pl.BlockSpec((None, None, page * KV_HEADS, HEAD_DIM),
                            lambda b, s, pt: (layer, pt[b, s * PAGES_PER_STEP + p], 0, 0))

    grid_spec = pltpu.PrefetchScalarGridSpec(
        num_scalar_prefetch=1,
        grid=(DB, n_pages // PAGES_PER_STEP),
        in_specs=[pl.BlockSpec((None, TS, N_HEADS * HEAD_DIM), lambda b, s, pt: (b, 0, 0)),
                  pl.BlockSpec((None, nt, TS, LANES), lambda b, s, pt: (b, 0, 0, 0)),
                  pl.BlockSpec(tzs.shape, lambda b, s, pt: (0, 0, 0, 0)),
                  pl.BlockSpec((None, LANES, kvw), lambda b, s, pt: (b, 0, 0)),
                  pl.BlockSpec((None, LANES, kvw), lambda b, s, pt: (b, 0, 0))]
                 + [page_spec(p) for p in range(PAGES_PER_STEP)] * 2,
        out_specs=pl.BlockSpec((None, TS, N_HEADS * HEAD_DIM), lambda b, s, pt: (b, 0, 0)),
        scratch_shapes=[pltpu.VMEM((KV_HEADS, GROUP * TS, 1), F32),
                        pltpu.VMEM((KV_HEADS, GROUP * TS, 1), F32),
                        pltpu.VMEM((KV_HEADS, GROUP * TS, HEAD_DIM), F32)])
    return pl.pallas_call(
        functools.partial(_a2s_kernel, past=n_pages * page),
        grid_spec=grid_spec,
        out_shape=jax.ShapeDtypeStruct((DB, TS, N_HEADS * HEAD_DIM), BF16),
        compiler_params=_params(("parallel", "arbitrary")),
        name="a2_sample",
    )(page_table, proj_s, mask_s, tzs, knew, vnew,
      *([cache_k] * PAGES_PER_STEP), *([cache_v] * PAGES_PER_STEP))


BQ = 512


def _bp_kernel(*refs):
    n_q = len(B_GROUPS) * GROUP
    q_refs = refs[:n_q]
    k_ref, v_ref, band_ref, o_ref, m_ref, l_ref, acc_ref, og_ref, lseg_ref = refs[n_q:]
    step = pl.program_id(2)
    t0 = step * BQ
    first = t0 == 0
    whole = [n for n, (_, dil) in enumerate(B_GROUPS) if dil * QB > BQ]
    assert len(whole) == 1 and B_GROUPS[whole[0]][1] * QB == k_ref.shape[0]

    def band(n, half):
        return jnp.concatenate([band_ref[r, n, half] for r in range(GROUP)], axis=0)

    @pl.when(first)
    def _():
        n = whole[0]
        dil = B_GROUPS[n][1]
        for rho in range(dil):
            cls = pl.ds(rho, QB, stride=dil)
            qs = (jnp.concatenate([qr[cls, :] for qr in q_refs[n * GROUP:(n + 1) * GROUP]], axis=0)
                  * QSCALE).astype(BF16)
            s = _nt(qs, k_ref[cls, :].astype(BF16)) + band(n, 1)
            m = jnp.max(s, axis=1, keepdims=True)
            p = jnp.exp2(s - m)
            l = jnp.sum(p, axis=1, keepdims=True)
            out = jnp.dot(p.astype(BF16), v_ref[cls, :].astype(BF16), preferred_element_type=F32) / l
            lse = jnp.broadcast_to(m + jnp.log2(l), out.shape)
            for r in range(GROUP):
                og_ref[r, cls, :] = out[r * QB:(r + 1) * QB]
                lseg_ref[r, cls, :] = lse[r * QB:(r + 1) * QB]

    rows_step = pl.ds(pl.multiple_of(t0, BQ), BQ)
    for r in range(GROUP):
        m_ref[r] = lseg_ref[r, rows_step, :]
        l_ref[r] = jnp.ones(l_ref.shape[1:], F32)
        acc_ref[r] = og_ref[r, rows_step, :]

    def stacked(ref, rows):
        return jnp.concatenate([ref[r, rows, :] for r in range(GROUP)], axis=0)

    def update(q_ref, rows_q, nrow, key_rows, biases):
        qs = (jnp.concatenate([qr[rows_q, :] for qr in q_ref], axis=0) * QSCALE).astype(BF16)
        kt = jnp.concatenate([k_ref[kr, :] for kr in key_rows], axis=0).astype(BF16)
        vt = jnp.concatenate([v_ref[kr, :] for kr in key_rows], axis=0).astype(BF16)
        s = _nt(qs, kt) + jnp.concatenate(biases, axis=1)
        m_old = stacked(m_ref, rows_q)
        m_new = jnp.maximum(m_old, jnp.max(s, axis=1, keepdims=True))
        alpha = jnp.exp2(m_old - m_new)
        p = jnp.exp2(s - jnp.concatenate([m_new] * len(key_rows), axis=1))
        l_new = alpha * stacked(l_ref, rows_q) + jnp.sum(p, axis=1, keepdims=True)
        acc = alpha * stacked(acc_ref, rows_q) + jnp.dot(p.astype(BF16), vt, preferred_element_type=F32)
        for r in range(GROUP):
            m_ref[r, rows_q, :] = m_new[r * nrow:(r + 1) * nrow]
            l_ref[r, rows_q, :] = l_new[r * nrow:(r + 1) * nrow]
            acc_ref[r, rows_q, :] = acc[r * nrow:(r + 1) * nrow]

    for n, (window, dil) in enumerate(B_GROUPS):
        assert window // dil == QB
        span = dil * QB
        if n in whole:
            continue
        q_ref = q_refs[n * GROUP:(n + 1) * GROUP]
        for rho in range(dil):
            for mt in range(BQ // span):
                def rows(start):
                    return pl.ds(start, QB, stride=dil) if dil > 1 else pl.ds(start, QB)
                off = rho + span * mt
                prev = jnp.maximum(t0 + off - span, rho) if mt == 0 else t0 + off - span
                bias_a = band(n, 0)
                if mt == 0:
                    bias_a = bias_a + jnp.where(first, NEG, 0.0)
                update(q_ref, rows(off), QB, [rows(prev), rows(t0 + off)], [bias_a, band(n, 1)])

    for r in range(GROUP):
        o_ref[:, r * HEAD_DIM:(r + 1) * HEAD_DIM] = (acc_ref[r] / l_ref[r]).astype(o_ref.dtype)


def _b_prompt(qb, kv, band):
    B, T, _ = qb.shape
    nh = N_HEADS
    hd = HEAD_DIM
    gw = GROUP * hd
    assert T % BQ == 0

    def q_spec(n, r):
        whole =B_GROUPS[n][1] * QB > BQ
        if whole:
            return pl.BlockSpec((None, T, hd), lambda b, g, c: (b, 0, (n * KV_HEADS + g) * GROUP + r))
        return pl.BlockSpec((None, BQ, hd), lambda b, g, c: (b, c, (n * KV_HEADS + g) * GROUP + r))

    q_specs = [q_spec(n, r) for n in range(len(B_GROUPS)) for r in range(GROUP)]
    return pl.pallas_call(
        _bp_kernel,
        grid=(B, KV_HEADS, T // BQ),
        in_specs=q_specs + [
                  pl.BlockSpec((None, T, hd), lambda b, g, c: (b, 0, g)),
                  pl.BlockSpec((None, T, hd), lambda b, g, c: (b, 0, KV_HEADS + g)),
                  pl.BlockSpec((GROUP, 3, 2, QB, LANES), lambda b, g, c: (g, 0, 0, 0, 0))],
        out_specs=pl.BlockSpec((None, BQ, gw), lambda b, g, c: (b, c, g)),
        out_shape=jax.ShapeDtypeStruct((B, T, nh * hd), BF16),
        scratch_shapes=[pltpu.VMEM((GROUP, BQ, LANES), F32), pltpu.VMEM((GROUP, BQ, LANES), F32),
                        pltpu.VMEM((GROUP, BQ, hd), F32),
                        pltpu.VMEM((GROUP, T, hd), F32), pltpu.VMEM((GROUP, T, LANES), F32)],
        compiler_params=_params(("parallel", "parallel", "arbitrary")),
        name="b_prompt",
    )(*([qb] * len(q_specs)), kv, kv, band)


def _bs_kernel(q0_ref, q1_ref, q2_ref, kc_ref, vc_ref, kn_ref, vn_ref, tab_ref, o_ref):
    TS = q0_ref.shape[0]
    W = kc_ref.shape[0] // KV_HEADS
    rows = pl.ds(pl.program_id(1), W, stride=KV_HEADS)
    kc = kc_ref[rows, :].astype(BF16)
    vc = vc_ref[rows, :].astype(BF16)
    kn = kn_ref[...].astype(BF16)
    vn = vn_ref[...].astype(BF16)
    ng = len(B_GROUPS)
    R = GROUP * TS
    q = (jnp.concatenate([q_ref[:, r * HEAD_DIM:(r + 1) * HEAD_DIM]
                          for q_ref in (q0_ref, q1_ref, q2_ref) for r in range(GROUP)], axis=0)
         * QSCALE).astype(BF16)
    tab = jnp.concatenate([tab_ref[n] for n in range(ng)], axis=0)
    sc_c = _nt(q, kc) + tab[:, :W]
    sc_n = _nt(q, kn) + tab[:, W:]
    m_rows = jnp.maximum(jnp.max(sc_c, axis=1, keepdims=True), jnp.max(sc_n, axis=1, keepdims=True))
    m = functools.reduce(jnp.maximum, [m_rows[n * R:(n + 1) * R] for n in range(ng)])
    m = jnp.concatenate([m] * ng, axis=0)
    pc = jnp.exp2(sc_c - m)
    pn = jnp.exp2(sc_n - m)
    l_rows = jnp.sum(pc, axis=1, keepdims=True) + jnp.sum(pn, axis=1, keepdims=True)
    acc_rows = (jnp.dot(pc.astype(BF16), vc, preferred_element_type=F32)
                + jnp.dot(pn.astype(BF16), vn, preferred_element_type=F32))
    l = sum(l_rows[n * R:(n + 1) * R] for n in range(ng))
    acc = sum(acc_rows[n * R:(n + 1) * R] for n in range(ng))
    res = acc / l
    for r in range(GROUP):
        o_ref[:, r * HEAD_DIM:(r + 1) * HEAD_DIM] = res[r * TS:(r + 1) * TS, :].astype(o_ref.dtype)


def _b_sample(qb_s, cache_k, cache_v, kv_new, tab):
    DB, TS, NQ = qb_s.shape
    W = cache_k.shape[1] // KV_HEADS
    hd = HEAD_DIM
    return pl.pallas_call(
        _bs_kernel,
        grid=(DB, KV_HEADS),
        in_specs=[pl.BlockSpec((None, TS, GROUP * hd), lambda b, g: (b, 0, g)),
                  pl.BlockSpec((None, TS, GROUP * hd), lambda b, g: (b, 0, KV_HEADS + g)),
                  pl.BlockSpec((None, TS, GROUP * hd), lambda b, g: (b, 0, 2 * KV_HEADS + g)),
                  pl.BlockSpec((None, W * KV_HEADS, hd), lambda b, g: (b, 0, 0)),
                  pl.BlockSpec((None, W * KV_HEADS, hd), lambda b, g: (b, 0, 0)),
                  pl.BlockSpec((None, LANES, hd), lambda b, g: (b, 0, g)),
                  pl.BlockSpec((None, LANES, hd), lambda b, g: (b, 0, KV_HEADS + g)),
                  pl.BlockSpec((len(B_GROUPS), None, GROUP * TS, W + LANES), lambda b, g: (0, g, 0, 0))],
        out_specs=pl.BlockSpec((None, TS, GROUP * hd), lambda b, g: (b, 0, g)),
        out_shape=jax.ShapeDtypeStruct((DB, TS, N_HEADS * hd), BF16),
        compiler_params=_params(("parallel", "arbitrary")),
        name="b_sample",
    )(qb_s, qb_s, qb_s, cache_k, cache_v, kv_new, kv_new, tab)


FFN_TF = 512
HALO = 16


def _ffn_up_kernel(xm_ref, xh_ref, wg_ref, wu_ref, cw_ref, cb_ref, act_ref, st_ref, xe_ref, *,
                   tiles_per_batch):
    i = pl.program_id(0)
    tm = xm_ref.shape[0]

    @pl.when(pl.program_id(1) == 0)
    def _():
        first = (i % tiles_per_batch) == 0
        xh = xh_ref[...]
        xe_ref[0:HALO, :] = jnp.where(first, jnp.zeros_like(xh), xh)
        xe_ref[HALO:, :] = xm_ref[...]

    gate = jnp.dot(xe_ref[...], wg_ref[...], preferred_element_type=F32)
    up = jnp.dot(xm_ref[...], wu_ref[...], preferred_element_type=F32)
    cw = cw_ref[...]
    conv = cb_ref[...] + ((gate[HALO - 2:HALO - 2 + tm] * cw[0:1] + gate[HALO - 1:HALO - 1 + tm] * cw[1:2])
                          + gate[HALO:] * cw[2:3])
    act_ref[...] = (jax.nn.silu(conv) * up).astype(act_ref.dtype)
    st_ref[...] = gate[HALO + tm - (CONV_WIDTH - 1):, :]


def _ffn_up(x_bf, w_g_bf, w_u_bf, conv_w, conv_b, layer, T):
    M, D = x_bf.shape
    F = conv_w.shape[2]
    tm = 1024
    assert T % tm == 0 and M % T == 0 and F % FFN_TF == 0 and tm % HALO == 0
    nf = F // FFN_TF
    hb = tm // HALO
    return pl.pallas_call(
        functools.partial(_ffn_up_kernel, tiles_per_batch=T // tm),
        grid=(M // tm, nf),
        in_specs=[pl.BlockSpec((tm, D), lambda i, j: (i, 0)),
                  pl.BlockSpec((HALO, D), lambda i, j: (jnp.maximum(i * hb - 1, 0), 0)),
                  pl.BlockSpec((None, D, FFN_TF), lambda i, j: (layer, 0, j)),
                  pl.BlockSpec((None, D, FFN_TF), lambda i, j: (layer, 0, j)),
                  pl.BlockSpec((None, CONV_WIDTH, FFN_TF), lambda i, j: (layer, 0, j)),
                  pl.BlockSpec((None, 1, FFN_TF), lambda i, j: (layer, 0, j))],
        out_specs=[pl.BlockSpec((tm, FFN_TF), lambda i, j: (i, j)),
                   pl.BlockSpec((None, CONV_WIDTH - 1, FFN_TF), lambda i, j: (i, 0, j))],
        out_shape=[jax.ShapeDtypeStruct((M, F), BF16),
                   jax.ShapeDtypeStruct((M // tm, CONV_WIDTH - 1, F), F32)],
        scratch_shapes=[pltpu.VMEM((HALO + tm, D), BF16)],
        compiler_params=_params(("parallel", "arbitrary")),
        name="ffn_up",
    )(x_bf, x_bf, w_g_bf, w_u_bf, conv_w, conv_b)


def _mm_ln_kernel(a_ref, w_ref, x_ref, g_ref, b_ref, h_ref, hb_ref, acc_ref, *, alpha):
    k = pl.program_id(1)

    @pl.when(k == 0)
    def _():
        acc_ref[...] = jnp.zeros_like(acc_ref)

    acc_ref[...] += jnp.dot(a_ref[...], w_ref[...], preferred_element_type=F32)

    @pl.when(k == pl.num_programs(1) - 1)
    def _():
        y = alpha * x_ref[...] + acc_ref[...]
        mu = jnp.mean(y, axis=-1, keepdims=True)
        d = y - mu
        var = jnp.mean(d * d, axis=-1, keepdims=True)
        h = d * lax.rsqrt(var + LN_EPS) * g_ref[...] + b_ref[...]
        h_ref[...] = h
        hb_ref[...] = h.astype(hb_ref.dtype)


def _mm_ln(a_bf, w_bf, layer, x, g, b, alpha):
    M, K = a_bf.shape
    D = w_bf.shape[2]
    tm = min(512, M)
    tk = K // 4 if K > 2048 else K
    assert M % tm == 0 and K % tk == 0 and tk % LANES == 0
    return pl.pallas_call(
        functools.partial(_mm_ln_kernel, alpha=alpha),
        grid=(M // tm, K // tk),
        in_specs=[pl.BlockSpec((tm, tk), lambda i, k: (i, k)),
                  pl.BlockSpec((None, tk, D), lambda i, k: (layer, k, 0)),
                  pl.BlockSpec((tm, D), lambda i, k: (i, 0)),
                  pl.BlockSpec((1, D), lambda i, k: (0, 0)),
                  pl.BlockSpec((1, D), lambda i, k: (0, 0))],
        out_specs=[pl.BlockSpec((tm, D), lambda i, k: (i, 0)),
                   pl.BlockSpec((tm, D), lambda i, k: (i, 0))],
        out_shape=[jax.ShapeDtypeStruct((M, D), F32), jax.ShapeDtypeStruct((M, D), BF16)],
        scratch_shapes=[pltpu.VMEM((tm, D), F32)],
        compiler_params=_params(("parallel", "arbitrary")),
        name="mm_ln",
    )(a_bf, w_bf, x, g.reshape(1, D), b.reshape(1, D))


def _layer_norm(x, g, b):
    mu = x.mean(-1, keepdims=True)
    var = jnp.square(x - mu).mean(-1, keepdims=True)
    return (x - mu) * lax.rsqrt(var + LN_EPS) * g + b


def _pick_tm(M):
    for tm in (1024, 512, 256, 128, 64, 32, 16):
        if M % tm == 0:
            return tm
    raise ValueError(M)


def _pick_tn(N):
    for tn in (1536, 1024, 512, 384, 256, 128):
        if N % tn == 0:
            return tn
    raise ValueError(N)


def _proj(x_bf, w_bf, layer=0):
    return _mm(x_bf, w_bf, layer, _pick_tm(x_bf.shape[0]), _pick_tn(w_bf.shape[2]))


def _ffn_act_sample(hs_bf, state, w_g_bf, w_u_bf, conv_w, conv_b, layer, DB, TS):
    conv_w, conv_b = conv_w[layer], conv_b[layer]
    F = conv_w.shape[1]
    gate = _proj(hs_bf, w_g_bf, layer).reshape(DB, TS, F)
    up = _proj(hs_bf, w_u_bf, layer).reshape(DB, TS, F)
    ext = jnp.concatenate([state, gate], axis=1)
    conv = conv_b + sum(ext[:, j:j + TS] * conv_w[j] for j in range(CONV_WIDTH))
    act = (jax.nn.silu(conv) * up).astype(BF16).reshape(DB * TS, F)
    return act, ext[:, ext.shape[1] - (CONV_WIDTH - 1):]


def _table_indices(T, W, TS):
    U = SAT_DIST + TZ_OFF + QB
    assert T <= T5_MAX_DISTANCE
    u = np.arange(U)[:, None]
    c = np.arange(LANES)[None, :]
    tz_idx = _t5_bucket(jnp.asarray(np.maximum(u - c - TZ_OFF, 0), I32))
    a = np.arange(QB)[:, None]
    band = []
    for window, dil in B_GROUPS:
        for off in (QB, 0):
            d = a + off - c
            ok = (d >= 0) & (d <= window // dil)
            band.append(jnp.where(jnp.asarray(ok), _t5_bucket(jnp.asarray(np.maximum(d, 0) * dil, I32)),
                                  NUM_BUCKETS))
    band_idx = jnp.concatenate(band, axis=0)
    nkt = W // LANES + 1
    t = np.arange(TS)[None, :, None]
    key = (np.arange(nkt)[:, None, None] * LANES + np.arange(LANES)[None, None, :])
    d = W + t - key
    tabs = []
    for window, dil in B_GROUPS:
        ok = (d >= 0) & (d % dil == 0) & (d <= window)
        tabs.append(jnp.where(jnp.asarray(ok), _t5_bucket(jnp.asarray(np.maximum(d, 0), I32)), NUM_BUCKETS))
    samp_idx = jnp.concatenate(tabs, axis=0).reshape(-1, LANES)
    rows = [tz_idx, band_idx, samp_idx]
    total = sum(r.shape[0] for r in rows)
    pad = (-total) % 128
    if pad:
        rows.append(jnp.full((pad, LANES), NUM_BUCKETS, I32))
    return jnp.concatenate(rows, axis=0), U, nkt


def kernel(x_prompt, x_sample, cache_k_a, cache_v_a, cache_kidx_a, cache_k_b, cache_v_b, state_ffn, page_table, a_w_in, a_w_o, a_kn_g, a_kn_b, b_w_kv, b_w_q, b_w_o, ffn_w_up, ffn_conv_w, ffn_conv_b, ffn_w_down, ln_g, ln_b, rel_bias):
    B, T, D = x_prompt.shape
    DB, TS, _ = x_sample.shape
    depth = ffn_w_up.shape[0]
    n_a = a_w_in.shape[0]
    d_ff = ffn_w_down.shape[1]
    W = cache_k_b.shape[1]
    n_pages = page_table.shape[1]
    page = cache_k_a.shape[2]
    past = n_pages * page
    alpha = (2 * depth) ** 0.25
    kvw = KV_HEADS * HEAD_DIM
    a_q = N_HEADS * HEAD_DIM
    a_in = a_w_in.shape[2]
    np_cols = ((a_in + 511) // 512) * 512
    ki0 = a_q + 2 * kvw + IDX_HEADS * IDX_DIM

    idx_all, U, nkt = _table_indices(T, W, TS)
    tabs = _bias_tables(rel_bias * LOG2E, idx_all)
    tz = tabs[:, :U]
    band = tabs[:, U:U + 6 * QB].reshape(N_HEADS, 3, 2, QB, LANES)
    samp = tabs[:, U + 6 * QB:U + 6 * QB + 3 * nkt * TS].reshape(KV_HEADS, GROUP, 3, nkt, TS, LANES)
    samp = samp.transpose(2, 0, 1, 4, 3, 5).reshape(3, KV_HEADS, GROUP * TS, nkt * LANES)
    n_slab = SAT_DIST // LANES + 1
    tzs = tz[:, TZ_OFF:TZ_OFF + n_slab * LANES].reshape(N_HEADS, n_slab, LANES, LANES)[:, :, :TS, :]

    cache_k_a2 = cache_k_a.reshape(n_a, -1, page * KV_HEADS, HEAD_DIM)
    cache_v_a2 = cache_v_a.reshape(n_a, -1, page * KV_HEADS, HEAD_DIM)
    cache_k_b2 = cache_k_b.reshape(DB, W * KV_HEADS, HEAD_DIM)
    cache_v_b2 = cache_v_b.reshape(DB, W * KV_HEADS, HEAD_DIM)
    cache_kidx_t = jnp.swapaxes(cache_kidx_a, 2, 3)

    def pad_rows(x, n):
        return jnp.pad(x, ((0, 0), (0, n - x.shape[1]), (0, 0)))

    f_pad = ((d_ff + FFN_TF - 1) // FFN_TF) * FFN_TF
    pad_f = f_pad - d_ff
    w_g_all = jnp.pad(ffn_w_up[:, :, :d_ff], ((0, 0), (0, 0), (0, pad_f))).astype(BF16)
    w_u_all = jnp.pad(ffn_w_up[:, :, d_ff:], ((0, 0), (0, 0), (0, pad_f))).astype(BF16)
    w_down_all = jnp.pad(ffn_w_down, ((0, 0), (0, pad_f), (0, 0))).astype(BF16)
    conv_w_all = jnp.pad(ffn_conv_w, ((0, 0), (0, 0), (0, pad_f)))
    conv_b_all = jnp.pad(ffn_conv_b, ((0, 0), (0, pad_f))).reshape(depth, 1, f_pad)
    w_in_all = jnp.pad(a_w_in, ((0, 0), (0, 0), (0, np_cols - a_in))).astype(BF16)
    w_o_a, w_o_b = a_w_o.astype(BF16), b_w_o.astype(BF16)
    w_q_all = b_w_q.astype(BF16)
    w_kv_bf = b_w_kv.astype(BF16)[None]

    hp = x_prompt.reshape(B * T, D)
    hs = x_sample.reshape(DB * TS, D)
    hp_bf, hs_bf = hp.astype(BF16), hs.astype(BF16)
    ka_p, va_p, kia_p, ka_s, va_s, kia_s, ffn_p, ffn_s = [], [], [], [], [], [], [], []
    for layer in range(depth):
        if layer < n_a:
            a = layer
            w_o_bf, w_o_layer = w_o_a, a
            proj = _proj(hp_bf, w_in_all, a).reshape(B, T, np_cols)
            k = proj[..., a_q:a_q + kvw].reshape(B, T, KV_HEADS, HEAD_DIM)
            v = proj[..., a_q + kvw:a_q + 2 * kvw].reshape(B, T, KV_HEADS, HEAD_DIM)
            ki = _layer_norm(proj[..., ki0:ki0 + IDX_DIM], a_kn_g[a], a_kn_b[a])
            ka_p.append(k); va_p.append(v); kia_p.append(ki)
            mask = _a1_prompt(ki.astype(BF16), proj, min(TOPK_MAX, T // 4))
            o = _a2_prompt(proj, mask, tz).reshape(B * T, a_q)
            proj_s = _proj(hs_bf, w_in_all, a).reshape(DB, TS, np_cols)
            k_s = proj_s[..., a_q:a_q + kvw]
            v_s = proj_s[..., a_q + kvw:a_q + 2 * kvw]
            ki_s = _layer_norm(proj_s[..., ki0:ki0 + IDX_DIM], a_kn_g[a], a_kn_b[a])
            ka_s.append(k_s.reshape(DB, TS, KV_HEADS, HEAD_DIM))
            va_s.append(v_s.reshape(DB, TS, KV_HEADS, HEAD_DIM))
            kia_s.append(ki_s)
            mask_s = _a1_sample(page_table, proj_s, pad_rows(ki_s, LANES).astype(BF16), cache_kidx_t, a,
                                min(TOPK_MAX, (past + TS) // 4))
            o_s = _a2_sample(page_table, proj_s, mask_s, tzs, pad_rows(k_s, LANES), pad_rows(v_s, LANES),
                             cache_k_a2, cache_v_a2, a).reshape(DB * TS, a_q)
        else:
            if layer == n_a:
                kv_p = _proj(hp_bf, w_kv_bf).reshape(B, T, 2 * kvw)
                kv_s = _proj(hs_bf, w_kv_bf).reshape(DB, TS, 2 * kvw)
                kv_s_pad = pad_rows(kv_s, LANES)
            bl = layer - n_a
            w_o_bf, w_o_layer = w_o_b, bl
            o = _b_prompt(_proj(hp_bf, w_q_all, bl).reshape(B, T, -1), kv_p, band).reshape(B * T, a_q)
            o_s = _b_sample(_proj(hs_bf, w_q_all, bl).reshape(DB, TS, -1), cache_k_b2, cache_v_b2, kv_s_pad,
                            samp).reshape(DB * TS, a_q)
        hp, hp_bf = _mm_ln(o, w_o_bf, w_o_layer, hp, ln_g[layer, 0], ln_b[layer, 0], alpha)
        hs, hs_bf = _mm_ln(o_s, w_o_bf, w_o_layer, hs, ln_g[layer, 0], ln_b[layer, 0], alpha)
        act, st = _ffn_up(hp_bf, w_g_all, w_u_all, conv_w_all, conv_b_all, layer, T)
        ffn_p.append(st.reshape(B, -1, CONV_WIDTH - 1, f_pad)[:, -1, :, :d_ff])
        act_s, st_s = _ffn_act_sample(hs_bf, jnp.pad(state_ffn[layer], ((0, 0), (0, 0), (0, pad_f))),
                                      w_g_all, w_u_all, conv_w_all, conv_b_all, layer, DB, TS)
        ffn_s.append(st_s[..., :d_ff])
        hp, hp_bf = _mm_ln(act, w_down_all, layer, hp, ln_g[layer, 1], ln_b[layer, 1], alpha)
        hs, hs_bf = _mm_ln(act_s, w_down_all, layer, hs, ln_g[layer, 1], ln_b[layer, 1], alpha)
    keep = min(max(w for w, _ in B_GROUPS), T)
    kb_p = kv_p[..., :kvw].reshape(B, T, KV_HEADS, HEAD_DIM)
    vb_p = kv_p[..., kvw:].reshape(B, T, KV_HEADS, HEAD_DIM)
    kb_s = kv_s[..., :kvw].reshape(DB, TS, KV_HEADS, HEAD_DIM)
    vb_s = kv_s[..., kvw:].reshape(DB, TS, KV_HEADS, HEAD_DIM)
    return (hp.reshape(B, T, D), hs.reshape(DB, TS, D), jnp.stack(ka_p), jnp.stack(va_p), jnp.stack(kia_p),
            jnp.stack(ka_s), jnp.stack(va_s), jnp.stack(kia_s), kb_p[:, T - keep:], vb_p[:, T - keep:],
            kb_s, vb_s, jnp.stack(ffn_p), jnp.stack(ffn_s))
```

```python
import functools
import math

import numpy as np
import jax
import jax.numpy as jnp
from jax import lax
from jax.experimental import pallas as pl
from jax.experimental.pallas import tpu as pltpu

F32 = jnp.float32
BF16 = jnp.bfloat16
I32 = jnp.int32

HEAD_DIM = 128
N_HEADS = 16
KV_HEADS = 4
GROUP = N_HEADS // KV_HEADS
IDX_HEADS = 16
IDX_DIM = 64
TOPK_MAX = 256
B_GROUPS = ((128, 1), (512, 4), (2048, 16))
NUM_BUCKETS = 32
T5_MAX_DISTANCE = 2048
LN_EPS = 1e-5
NEG = -1e30
CONV_WIDTH = 3

LANES = 128
QB = 128
KC = 512
A2_KC = KC
TZ_OFF = A2_KC - QB
SAT_DIST = T5_MAX_DISTANCE + QB
INT_MIN = -2 ** 31
KEY_NEG_INF = INT_MIN + 0x7FFFFF
VMEM_LIMIT = 56 * 1024 * 1024

LOG2E = 1.4426950408889634
QSCALE = HEAD_DIM ** -0.5 * LOG2E

NT_DIMS = (((1,), (1,)), ((), ()))


def _nt(a, b):
    return lax.dot_general(a, b, NT_DIMS, preferred_element_type=F32)


def _params(sem):
    return pltpu.CompilerParams(dimension_semantics=sem, vmem_limit_bytes=VMEM_LIMIT)


def _mm_kernel(x_ref, w_ref, o_ref):
    o_ref[...] = jnp.dot(x_ref[...], w_ref[...], preferred_element_type=F32).astype(o_ref.dtype)


def _mm(x, w, layer, tm, tn, out_dtype=F32):
    M, K = x.shape
    N = w.shape[2]
    assert M % tm == 0 and N % tn == 0, (M, N, tm, tn)
    return pl.pallas_call(
        _mm_kernel,
        grid=(M // tm, N // tn),
        in_specs=[pl.BlockSpec((tm, K), lambda i, j: (i, 0)),
                  pl.BlockSpec((None, K, tn), lambda i, j: (layer, 0, j))],
        out_specs=pl.BlockSpec((tm, tn), lambda i, j: (i, j)),
        out_shape=jax.ShapeDtypeStruct((M, N), out_dtype),
        compiler_params=_params(("parallel", "arbitrary")),
        name="mm",
    )(x, w)


def _t5_bucket(dist):
    dist = jnp.maximum(dist, 0)
    exact = NUM_BUCKETS // 2
    far = exact + (jnp.log(jnp.maximum(dist, 1).astype(F32) / exact)
                   / math.log(T5_MAX_DISTANCE / exact) * (NUM_BUCKETS - exact)).astype(I32)
    return jnp.where(dist < exact, dist, jnp.minimum(far, NUM_BUCKETS - 1))


def _table_kernel(rb_ref, idx_ref, o_ref):
    slab = 16

    def body(i, carry):
        r0 = pl.multiple_of(i * slab, slab)
        idx = idx_ref[pl.ds(r0, slab), :]
        accs = [jnp.full(idx.shape, NEG, F32)] * N_HEADS
        for k in range(NUM_BUCKETS):
            hit = idx == k
            accs = [jnp.where(hit, rb_ref[k, h], accs[h]) for h in range(N_HEADS)]
        for h in range(N_HEADS):
            o_ref[h, pl.ds(r0, slab), :] = accs[h]
        return carry

    lax.fori_loop(0, idx_ref.shape[0] // slab, body, 0)


def _bias_tables(rel_bias, idx):
    R = idx.shape[0]
    tr = 128
    assert R % tr == 0
    return pl.pallas_call(
        _table_kernel,
        grid=(R // tr,),
        in_specs=[pl.BlockSpec(memory_space=pltpu.SMEM),
                  pl.BlockSpec((tr, LANES), lambda i: (i, 0))],
        out_specs=pl.BlockSpec((N_HEADS, tr, LANES), lambda i: (0, i, 0)),
        out_shape=jax.ShapeDtypeStruct((N_HEADS, R, LANES), F32),
        compiler_params=_params(("arbitrary",)),
        name="bias_tables",
    )(rel_bias, idx)


def _sortable(x):
    bits = lax.bitcast_convert_type(x, I32)
    return bits ^ ((bits >> 31) & jnp.int32(0x7FFFFFFF))


def _kth_largest(count_ge, shape, k):
    def body(t, ans):
        cand_u = ans | (jnp.int32(1) << (31 - t))
        cnt = count_ge(cand_u ^ jnp.int32(INT_MIN))
        return jnp.where(cnt >= k, cand_u, ans)
    ans = lax.fori_loop(0, 32, body, jnp.zeros(shape, I32))
    return ans ^ jnp.int32(INT_MIN)


def _tie_bound(count_eq_below, shape, need, nbits):
    def body(t, ans):
        cand = ans | (jnp.int32(1) << (nbits - 1 - t))
        return jnp.where(count_eq_below(cand) < need, cand, ans)
    return lax.fori_loop(0, nbits, body, jnp.zeros(shape, I32))


A1_TILES = 4


def _a1p_kernel(kin_ref, qi_ref, tail_ref, mask_ref, st_ref, jb_ref, *, topk):
    i = pl.program_id(1)
    T = kin_ref.shape[0]
    NQ = A1_TILES * QB
    KS = KC // A1_TILES
    nchunk = T // KC
    nbits = (T - 1).bit_length()
    tail_t = jnp.concatenate([tail_ref[t * QB:(t + 1) * QB, :].T for t in range(A1_TILES)], axis=1)
    wi_t = tail_t[IDX_DIM:IDX_DIM + IDX_HEADS, :] * (IDX_HEADS ** -0.5 * IDX_DIM ** -0.5)
    qpos = i * NQ + lax.broadcasted_iota(I32, (KC, NQ), 1)
    row = lax.broadcasted_iota(I32, (KC, NQ), 0)
    qpos_s = i * NQ + lax.broadcasted_iota(I32, (KS, NQ), 1)
    row_s = lax.broadcasted_iota(I32, (KS, NQ), 0)

    nj = (i * NQ) // KC + 1

    for c in range(nchunk):
        @pl.when(c < nj)
        def _():
            for sub in range(KC // KS):
                r0 = c * KC + sub * KS
                kc = kin_ref[r0:r0 + KS, :]
                acc = jnp.zeros((KS, NQ), F32)
                for h in range(IDX_HEADS):
                    qh = qi_ref[:, h * IDX_DIM:(h + 1) * IDX_DIM].astype(BF16)
                    acc = acc + jnp.maximum(_nt(kc, qh), 0.0) * wi_t[h:h + 1, :]
                acc = jnp.where(r0 + row_s <= qpos_s, acc, -jnp.inf)
                st_ref[r0:r0 + KS, :] = _sortable(acc)

    def select(nvis):
        def count(pred):
            cnt = jnp.zeros((1, NQ), F32)
            for c in range(nvis):
                blk = st_ref[c * KC:(c + 1) * KC, :]
                cnt = cnt + jnp.sum(jnp.where(pred(blk, c * KC + row), 1.0, 0.0), axis=0, keepdims=True)
            return cnt

        thr = _kth_largest(lambda cand: count(lambda blk, _: blk >= cand), (1, NQ), float(topk))
        need = float(topk) - count(lambda blk, _: blk > thr)
        n_eq = count(lambda blk, _: blk == thr)
        excess = jnp.where((n_eq > need) & (thr != KEY_NEG_INF), 1.0, 0.0)
        jb_ref[...] = jnp.full((1, NQ), T, I32)

        @pl.when(jnp.max(excess) > 0.0)
        def _():
            jb_ref[...] = _tie_bound(
                lambda cand: count(lambda blk, kpos: jnp.where(blk == thr, kpos, T) < cand),
                (1, NQ), need, nbits)

        jb = jb_ref[...]
        for c in range(nchunk):
            if c < nvis:
                blk = st_ref[c * KC:(c + 1) * KC, :]
                kpos = c * KC + row
                rank_pos = jnp.where(blk == thr, kpos, jnp.where(blk > thr, -1, T + 1))
                sel = jnp.where(kpos <= qpos, rank_pos, T + 1) <= jb
                m_t = jnp.where(sel, 0.0, NEG)
                for t in range(A1_TILES):
                    for s4 in range(KC // QB):
                        mask_ref[t, c, :, s4 * QB:(s4 + 1) * QB] = (
                            m_t[s4 * QB:(s4 + 1) * QB, t * QB:(t + 1) * QB].T.astype(mask_ref.dtype))
            else:
                for t in range(A1_TILES):
                    mask_ref[t, c] = jnp.full((QB, KC), NEG, mask_ref.dtype)

    for nvis in range(1, nchunk + 1):
        pl.when(nj == nvis)(functools.partial(select, nvis))


def _a1_prompt(kin_bf, proj, topk):
    B, T, _ = proj.shape
    nq, nc = T // QB, T // KC
    assert topk <= KC
    assert (KC // QB) % A1_TILES == 0 and nq % A1_TILES == 0
    nqs = A1_TILES * QB
    qi_blk = (N_HEADS * HEAD_DIM + 2 * KV_HEADS * HEAD_DIM) // (IDX_HEADS * IDX_DIM)
    tail_blk = (N_HEADS * HEAD_DIM + 2 * KV_HEADS * HEAD_DIM + IDX_HEADS * IDX_DIM) // LANES
    return pl.pallas_call(
        functools.partial(_a1p_kernel, topk=topk),
        grid=(B, nq // A1_TILES),
        in_specs=[pl.BlockSpec((None, T, IDX_DIM), lambda b, i: (b, 0, 0)),
                  pl.BlockSpec((None, nqs, IDX_HEADS * IDX_DIM), lambda b, i: (b, i, qi_blk)),
                  pl.BlockSpec((None, nqs, LANES), lambda b, i: (b, i, tail_blk))],
        out_specs=pl.BlockSpec((None, A1_TILES, nc, QB, KC), lambda b, i: (b, i, 0, 0, 0)),
        out_shape=jax.ShapeDtypeStruct((B, nq, nc, QB, KC), BF16),
        scratch_shapes=[pltpu.VMEM((T, nqs), I32), pltpu.VMEM((1, nqs), I32)],
        compiler_params=_params(("parallel", "arbitrary")),
        name="a1_prompt",
    )(kin_bf, proj, proj)


A2_TILES = 4


def _a2p_kernel(q_ref, k_ref, v_ref, mask_ref, tz_ref, o_ref):
    for t in range(A2_TILES):
        _a2p_tile(pl.program_id(2) * A2_TILES + t, q_ref.at[t * QB:(t + 1) * QB], k_ref, v_ref,
                  mask_ref.at[t], tz_ref, o_ref.at[t * QB:(t + 1) * QB])


def _a2p_tile(i, q_ref, k_ref, v_ref, mask_ref, tz_ref, o_ref):
    nj = (i * QB) // A2_KC + 1
    HP = GROUP
    R = HP * QB
    heads = [list(range(c * HP, (c + 1) * HP)) for c in range(GROUP // HP)]
    qs = [(jnp.concatenate([q_ref[:, r * HEAD_DIM:(r + 1) * HEAD_DIM] for r in hs], axis=0)
           * QSCALE).astype(BF16) for hs in heads]

    def body(j, carry):
        k0 = pl.multiple_of(j * A2_KC, A2_KC)
        kt = k_ref[pl.ds(k0, A2_KC), :].astype(BF16)
        vt = v_ref[pl.ds(k0, A2_KC), :].astype(BF16)
        base = i * QB - j * A2_KC + TZ_OFF
        mk = jnp.concatenate([mask_ref[j * (A2_KC // KC) + c] for c in range(A2_KC // KC)],
                             axis=1).astype(F32)
        mk = jnp.concatenate([mk] * HP, axis=0)
        out = []
        for hs, q, (m, l, acc) in zip(heads, qs, carry):
            bias = jnp.concatenate(
                [jnp.concatenate(
                    [tz_ref[r, pl.ds(pl.multiple_of(base - QB * s4, QB), QB), :]
                     for s4 in range(A2_KC // QB)], axis=1) for r in hs], axis=0)
            s = _nt(q, kt) + bias + mk
            m_new = jnp.maximum(m, jnp.max(s, axis=1, keepdims=True))
            alpha = jnp.exp2(m - m_new)
            p = jnp.exp2(s - m_new)
            l = alpha * l + jnp.sum(p, axis=1, keepdims=True)
            acc = alpha * acc + jnp.dot(p.astype(BF16), vt, preferred_element_type=F32)
            out.append((m_new, l, acc))
        return tuple(out)

    init = tuple((jnp.full((R, 1), NEG, F32), jnp.zeros((R, 1), F32), jnp.zeros((R, HEAD_DIM), F32))
                 for _ in heads)
    final = lax.fori_loop(0, nj, body, init)
    for hs, (m, l, acc) in zip(heads, final):
        res = acc / l
        for n, r in enumerate(hs):
            o_ref[:, r * HEAD_DIM:(r + 1) * HEAD_DIM] = res[n * QB:(n + 1) * QB, :].astype(o_ref.dtype)


def _a2_prompt(proj, mask, tz):
    B, T, _ = proj.shape
    nq, nc = T // QB, T // KC
    gw = GROUP * HEAD_DIM
    k_blk0 = N_HEADS * HEAD_DIM // HEAD_DIM
    v_blk0 = k_blk0 + KV_HEADS
    U = tz.shape[1]
    assert nq % A2_TILES == 0
    qr = A2_TILES * QB
    return pl.pallas_call(
        _a2p_kernel,
        grid=(KV_HEADS, B, nq // A2_TILES),
        in_specs=[pl.BlockSpec((None, qr, gw), lambda g, b, i: (b, i, g)),
                  pl.BlockSpec((None, T, HEAD_DIM), lambda g, b, i: (b, 0, k_blk0 + g)),
                  pl.BlockSpec((None, T, HEAD_DIM), lambda g, b, i: (b, 0, v_blk0 + g)),
                  pl.BlockSpec((None, A2_TILES, nc, QB, KC), lambda g, b, i: (b, i, 0, 0, 0)),
                  pl.BlockSpec((GROUP, U, LANES), lambda g, b, i: (g, 0, 0))],
        out_specs=pl.BlockSpec((None, qr, gw), lambda g, b, i: (b, i, g)),
        out_shape=jax.ShapeDtypeStruct((B, T, N_HEADS * HEAD_DIM), BF16),
        compiler_params=_params(("parallel", "parallel", "arbitrary")),
        name="a2_prompt",
    )(proj, proj, proj, mask, tz)


PAGES_PER_STEP = 32


A1S_PAGES = 128
A1S_PART = 16


def _a1s_kernel(pt_ref, qi_ref, tail_ref, knew_ref, *rest, topk, past):
    page_refs = rest[:A1S_PAGES]
    mask_ref, st_ref, jb_ref = rest[A1S_PAGES:]
    s = pl.program_id(1)
    nt, TS, _ = st_ref.shape
    L = nt * LANES
    nbits = (L - 1).bit_length()
    qi = jnp.concatenate([qi_ref[:, h * IDX_DIM:(h + 1) * IDX_DIM] for h in range(IDX_HEADS)],
                         axis=0).astype(BF16)
    w_col = jnp.concatenate([tail_ref[:, IDX_DIM + h:IDX_DIM + h + 1] for h in range(IDX_HEADS)],
                            axis=0) * (IDX_HEADS ** -0.5 * IDX_DIM ** -0.5)

    def scores(qk):
        sc = jnp.maximum(qk, 0.0) * w_col
        return jnp.sum(sc.reshape(IDX_HEADS, TS, qk.shape[1]), axis=0)

    for part in range(A1S_PAGES // A1S_PART):
        kt = jnp.concatenate([page_refs[part * A1S_PART + p][...] for p in range(A1S_PART)],
                             axis=1).astype(BF16)
        keys = _sortable(scores(jnp.dot(qi, kt, preferred_element_type=F32)))
        for p in range(A1S_PART):
            st_ref[s * A1S_PAGES + part * A1S_PART + p] = keys[:, p * LANES:(p + 1) * LANES]

    @pl.when(s == pl.num_programs(1) - 1)
    def _():
        t = lax.broadcasted_iota(I32, (TS, LANES), 0)
        c = lax.broadcasted_iota(I32, (TS, LANES), 1)
        sc = jnp.where(c <= t, scores(_nt(qi, knew_ref[...])), -jnp.inf)
        st_ref[nt - 1] = _sortable(sc)

        kpos = (lax.broadcasted_iota(I32, (nt, TS, LANES), 0) * LANES
                + lax.broadcasted_iota(I32, (nt, TS, LANES), 2))
        qpos = past + lax.broadcasted_iota(I32, (nt, TS, LANES), 1)

        def count(pred):
            per_lane = jnp.sum(jnp.where(pred(st_ref[...]), 1.0, 0.0), axis=0)
            return jnp.sum(per_lane, axis=1, keepdims=True)

        thr = _kth_largest(lambda cand: count(lambda k: k >= cand[None]), (TS, 1), float(topk))
        thr3 = thr[None]
        need = float(topk) - count(lambda k: k > thr3)
        n_eq = count(lambda k: k == thr3)
        excess = jnp.where((n_eq > need) & (thr != KEY_NEG_INF), 1.0, 0.0)
        jb_ref[...] = jnp.full((TS, 1), L, I32)

        @pl.when(jnp.max(excess) > 0.0)
        def _():
            jb_ref[...] = _tie_bound(
                lambda cand: count(lambda k: jnp.where(k == thr3, kpos, L) < cand[None]),
                (TS, 1), need, nbits)

        keys = st_ref[...]
        rank_pos = jnp.where(keys == thr3, kpos, jnp.where(keys > thr3, -1, L + 1))
        sel = jnp.where(kpos <= qpos, rank_pos, L + 1) <= jb_ref[...][None]
        mask_ref[...] = jnp.where(sel, 0.0, NEG)


def _a1_sample(page_table, proj_s, knew_bf, cache_kidx, layer, topk):
    DB, TS, _ = proj_s.shape
    n_pages = page_table.shape[1]
    page = cache_kidx.shape[3]
    assert page == LANES and n_pages % A1S_PAGES == 0 and topk <= n_pages * page
    nt = n_pages + 1
    qi_blk = (N_HEADS * HEAD_DIM + 2 * KV_HEADS * HEAD_DIM) // (IDX_HEADS * IDX_DIM)
    tail_blk = (N_HEADS * HEAD_DIM + 2 * KV_HEADS * HEAD_DIM + IDX_HEADS * IDX_DIM) // LANES

    def page_spec(p):
        return pl.BlockSpec((None, None, IDX_DIM, page),
                            lambda b, s, pt: (layer, pt[b, s * A1S_PAGES + p], 0, 0))

    grid_spec = pltpu.PrefetchScalarGridSpec(
        num_scalar_prefetch=1,
        grid=(DB, n_pages // A1S_PAGES),
        in_specs=[pl.BlockSpec((None, TS, IDX_HEADS * IDX_DIM), lambda b, s, pt: (b, 0, qi_blk)),
                  pl.BlockSpec((None, TS, LANES), lambda b, s, pt: (b, 0, tail_blk)),
                  pl.BlockSpec((None, LANES, IDX_DIM), lambda b, s, pt: (b, 0, 0))]
                 + [page_spec(p) for p in range(A1S_PAGES)],
        out_specs=pl.BlockSpec((None, nt, TS, LANES), lambda b, s, pt: (b, 0, 0, 0)),
        scratch_shapes=[pltpu.VMEM((nt, TS, LANES), I32), pltpu.VMEM((TS, 1), I32)])
    return pl.pallas_call(
        functools.partial(_a1s_kernel, topk=topk, past=n_pages * page),
        grid_spec=grid_spec,
        out_shape=jax.ShapeDtypeStruct((DB, nt, TS, LANES), F32),
        compiler_params=_params(("parallel", "arbitrary")),
        name="a1_sample",
    )(page_table, proj_s, proj_s, knew_bf, *([cache_kidx] * A1S_PAGES))


def _a2s_kernel(pt_ref, q_ref, mask_ref, tzs_ref, knew_ref, vnew_ref, *rest, past):
    k_pages = rest[:PAGES_PER_STEP]
    v_pages = rest[PAGES_PER_STEP:2 * PAGES_PER_STEP]
    o_ref, m_ref, l_ref, acc_ref = rest[2 * PAGES_PER_STEP:]
    s = pl.program_id(1)
    TS = q_ref.shape[0]
    n_slab = tzs_ref.shape[1]

    @pl.when(s == 0)
    def _():
        m_ref[...] = jnp.full(m_ref.shape, NEG, F32)
        l_ref[...] = jnp.zeros(l_ref.shape, F32)
        acc_ref[...] = jnp.zeros(acc_ref.shape, F32)

    qg = [(jnp.concatenate([q_ref[:, (g * GROUP + r) * HEAD_DIM:(g * GROUP + r + 1) * HEAD_DIM]
                            for r in range(GROUP)], axis=0) * QSCALE).astype(BF16)
          for g in range(KV_HEADS)]

    def update(g, kt, vt, tiles):
        slabs = [jnp.minimum((past // LANES) - t, n_slab - 1) for t in tiles]
        mk = jnp.concatenate([mask_ref[t] for t in tiles], axis=1)
        mk = jnp.concatenate([mk] * GROUP, axis=0)
        bias = jnp.concatenate(
            [jnp.concatenate([tzs_ref[g * GROUP + r, sl] for r in range(GROUP)], axis=0) for sl in slabs],
            axis=1)
        sc = _nt(qg[g], kt) + bias + mk
        m_old = m_ref[g]
        m_new = jnp.maximum(m_old, jnp.max(sc, axis=1, keepdims=True))
        alpha = jnp.exp2(m_old - m_new)
        p = jnp.exp2(sc - m_new)
        l_ref[g] = alpha * l_ref[g] + jnp.sum(p, axis=1, keepdims=True)
        acc_ref[g] = alpha * acc_ref[g] + jnp.dot(p.astype(BF16), vt, preferred_element_type=F32)
        m_ref[g] = m_new

    tiles = [s * PAGES_PER_STEP + p for p in range(PAGES_PER_STEP)]
    for g in range(KV_HEADS):
        rows = pl.ds(g, LANES, stride=KV_HEADS)
        kt = jnp.concatenate([kp[rows, :] for kp in k_pages], axis=0).astype(BF16)
        vt = jnp.concatenate([vp[rows, :] for vp in v_pages], axis=0).astype(BF16)
        update(g, kt, vt, tiles)

    @pl.when(s == pl.num_programs(1) - 1)
    def _():
        for g in range(KV_HEADS):
            update(g, knew_ref[:, g * HEAD_DIM:(g + 1) * HEAD_DIM].astype(BF16),
                   vnew_ref[:, g * HEAD_DIM:(g + 1) * HEAD_DIM].astype(BF16), [past // LANES])
        for g in range(KV_HEADS):
            res = acc_ref[g] / l_ref[g]
            for r in range(GROUP):
                h = g * GROUP + r
                o_ref[:, h * HEAD_DIM:(h + 1) * HEAD_DIM] = res[r * TS:(r + 1) * TS, :].astype(o_ref.dtype)


def _a2_sample(page_table, proj_s, mask_s, tzs, knew, vnew, cache_k, cache_v, layer):
    DB, TS, _ = proj_s.shape
    n_pages = page_table.shape[1]
    page = cache_k.shape[2] // KV_HEADS
    kvw = KV_HEADS * HEAD_DIM
    nt = n_pages + 1

    def page_spec(p):
        return pl.BlockSpec((None, None, page * KV_HEADS, HEAD_DIM),
                            lambda b, s, pt: (layer, pt[b, s * PAGES_PER_STEP + p], 0, 0))

    grid_spec = pltpu.PrefetchScalarGridSpec(
        num_scalar_prefetch=1,
        grid=(DB, n_pages // PAGES_PER_STEP),
        in_specs=[pl.BlockSpec((None, TS, N_HEADS * HEAD_DIM), lambda b, s, pt: (b, 0, 0)),
                  pl.BlockSpec((None, nt, TS, LANES), lambda b, s, pt: (b, 0, 0, 0)),
                  pl.BlockSpec(tzs.shape, lambda b, s, pt: (0, 0, 0, 0)),
                  pl.BlockSpec((None, LANES, kvw), lambda b, s, pt: (b, 0, 0)),
                  pl.BlockSpec((None, LANES, kvw), lambda b, s, pt: (b, 0, 0))]
                 + [page_spec(p) for p in range(PAGES_PER_STEP)] * 2,
        out_specs=pl.BlockSpec((None, TS, N_HEADS * HEAD_DIM), lambda b, s, pt: (b, 0, 0)),
        scratch_shapes=[pltpu.VMEM((KV_HEADS, GROUP * TS, 1), F32),
                        pltpu.VMEM((KV_HEADS, GROUP * TS, 1), F32),
                        pltpu.VMEM((KV_HEADS, GROUP * TS, HEAD_DIM), F32)])
    return pl.pallas_call(
        functools.partial(_a2s_kernel, past=n_pages * page),
        grid_spec=grid_spec,
        out_shape=jax.ShapeDtypeStruct((DB, TS, N_HEADS * HEAD_DIM), BF16),
        compiler_params=_params(("parallel", "arbitrary")),
        name="a2_sample",
    )(page_table, proj_s, mask_s, tzs, knew, vnew,
      *([cache_k] * PAGES_PER_STEP), *([cache_v] * PAGES_PER_STEP))


BQ = 512


def _bp_kernel(*refs):
    n_q = len(B_GROUPS) * GROUP
    q_refs = refs[:n_q]
    k_ref, v_ref, band_ref, o_ref, m_ref, l_ref, acc_ref, og_ref, lseg_ref = refs[n_q:]
    step = pl.program_id(2)
    t0 = step * BQ
    first = t0 == 0
    whole = [n for n, (_, dil) in enumerate(B_GROUPS) if dil * QB > BQ]
    assert len(whole) == 1 and B_GROUPS[whole[0]][1] * QB == k_ref.shape[0]

    def band(n, half):
        return jnp.concatenate([band_ref[r, n, half] for r in range(GROUP)], axis=0)

    @pl.when(first)
    def _():
        n = whole[0]
        dil = B_GROUPS[n][1]
        for rho in range(dil):
            cls = pl.ds(rho, QB, stride=dil)
            qs = (jnp.concatenate([qr[cls, :] for qr in q_refs[n * GROUP:(n + 1) * GROUP]], axis=0)
                  * QSCALE).astype(BF16)
            s = _nt(qs, k_ref[cls, :].astype(BF16)) + band(n, 1)
            m = jnp.max(s, axis=1, keepdims=True)
            p = jnp.exp2(s - m)
            l = jnp.sum(p, axis=1, keepdims=True)
            out = jnp.dot(p.astype(BF16), v_ref[cls, :].astype(BF16), preferred_element_type=F32) / l
            lse = jnp.broadcast_to(m + jnp.log2(l), out.shape)
            for r in range(GROUP):
                og_ref[r, cls, :] = out[r * QB:(r + 1) * QB]
                lseg_ref[r, cls, :] = lse[r * QB:(r + 1) * QB]

    rows_step = pl.ds(pl.multiple_of(t0, BQ), BQ)
    for r in range(GROUP):
        m_ref[r] = lseg_ref[r, rows_step, :]
        l_ref[r] = jnp.ones(l_ref.shape[1:], F32)
        acc_ref[r] = og_ref[r, rows_step, :]

    def stacked(ref, rows):
        return jnp.concatenate([ref[r, rows, :] for r in range(GROUP)], axis=0)

    def update(q_ref, rows_q, nrow, key_rows, biases):
        qs = (jnp.concatenate([qr[rows_q, :] for qr in q_ref], axis=0) * QSCALE).astype(BF16)
        kt = jnp.concatenate([k_ref[kr, :] for kr in key_rows], axis=0).astype(BF16)
        vt = jnp.concatenate([v_ref[kr, :] for kr in key_rows], axis=0).astype(BF16)
        s = _nt(qs, kt) + jnp.concatenate(biases, axis=1)
        m_old = stacked(m_ref, rows_q)
        m_new = jnp.maximum(m_old, jnp.max(s, axis=1, keepdims=True))
        alpha = jnp.exp2(m_old - m_new)
        p = jnp.exp2(s - jnp.concatenate([m_new] * len(key_rows), axis=1))
        l_new = alpha * stacked(l_ref, rows_q) + jnp.sum(p, axis=1, keepdims=True)
        acc = alpha * stacked(acc_ref, rows_q) + jnp.dot(p.astype(BF16), vt, preferred_element_type=F32)
        for r in range(GROUP):
            m_ref[r, rows_q, :] = m_new[r * nrow:(r + 1) * nrow]
            l_ref[r, rows_q, :] = l_new[r * nrow:(r + 1) * nrow]
            acc_ref[r, rows_q, :] = acc[r * nrow:(r + 1) * nrow]

    for n, (window, dil) in enumerate(B_GROUPS):
        assert window // dil == QB
        span = dil * QB
        if n in whole:
            continue
        q_ref = q_refs[n * GROUP:(n + 1) * GROUP]
        for rho in range(dil):
            for mt in range(BQ // span):
                def rows(start):
                    return pl.ds(start, QB, stride=dil) if dil > 1 else pl.ds(start, QB)
                off = rho + span * mt
                prev = jnp.maximum(t0 + off - span, rho) if mt == 0 else t0 + off - span
                bias_a = band(n, 0)
                if mt == 0:
                    bias_a = bias_a + jnp.where(first, NEG, 0.0)
                update(q_ref, rows(off), QB, [rows(prev), rows(t0 + off)], [bias_a, band(n, 1)])

    for r in range(GROUP):
        o_ref[:, r * HEAD_DIM:(r + 1) * HEAD_DIM] = (acc_ref[r] / l_ref[r]).astype(o_ref.dtype)


def _b_prompt(qb, kv, band):
    B, T, _ = qb.shape
    nh = N_HEADS
    hd = HEAD_DIM
    gw = GROUP * hd
    assert T % BQ == 0

    def q_spec(n, r):
        whole =B_GROUPS[n][1] * QB > BQ
        if whole:
            return pl.BlockSpec((None, T, hd), lambda b, g, c: (b, 0, (n * KV_HEADS + g) * GROUP + r))
        return pl.BlockSpec((None, BQ, hd), lambda b, g, c: (b, c, (n * KV_HEADS + g) * GROUP + r))

    q_specs = [q_spec(n, r) for n in range(len(B_GROUPS)) for r in range(GROUP)]
    return pl.pallas_call(
        _bp_kernel,
        grid=(B, KV_HEADS, T // BQ),
        in_specs=q_specs + [
                  pl.BlockSpec((None, T, hd), lambda b, g, c: (b, 0, g)),
                  pl.BlockSpec((None, T, hd), lambda b, g, c: (b, 0, KV_HEADS + g)),
                  pl.BlockSpec((GROUP, 3, 2, QB, LANES), lambda b, g, c: (g, 0, 0, 0, 0))],
        out_specs=pl.BlockSpec((None, BQ, gw), lambda b, g, c: (b, c, g)),
        out_shape=jax.ShapeDtypeStruct((B, T, nh * hd), BF16),
        scratch_shapes=[pltpu.VMEM((GROUP, BQ, LANES), F32), pltpu.VMEM((GROUP, BQ, LANES), F32),
                        pltpu.VMEM((GROUP, BQ, hd), F32),
                        pltpu.VMEM((GROUP, T, hd), F32), pltpu.VMEM((GROUP, T, LANES), F32)],
        compiler_params=_params(("parallel", "parallel", "arbitrary")),
        name="b_prompt",
    )(*([qb] * len(q_specs)), kv, kv, band)


def _bs_kernel(q0_ref, q1_ref, q2_ref, kc_ref, vc_ref, kn_ref, vn_ref, tab_ref, o_ref):
    TS = q0_ref.shape[0]
    W = kc_ref.shape[0] // KV_HEADS
    rows = pl.ds(pl.program_id(1), W, stride=KV_HEADS)
    kc = kc_ref[rows, :].astype(BF16)
    vc = vc_ref[rows, :].astype(BF16)
    kn = kn_ref[...].astype(BF16)
    vn = vn_ref[...].astype(BF16)
    ng = len(B_GROUPS)
    R = GROUP * TS
    q = (jnp.concatenate([q_ref[:, r * HEAD_DIM:(r + 1) * HEAD_DIM]
                          for q_ref in (q0_ref, q1_ref, q2_ref) for r in range(GROUP)], axis=0)
         * QSCALE).astype(BF16)
    tab = jnp.concatenate([tab_ref[n] for n in range(ng)], axis=0)
    sc_c = _nt(q, kc) + tab[:, :W]
    sc_n = _nt(q, kn) + tab[:, W:]
    m_rows = jnp.maximum(jnp.max(sc_c, axis=1, keepdims=True), jnp.max(sc_n, axis=1, keepdims=True))
    m = functools.reduce(jnp.maximum, [m_rows[n * R:(n + 1) * R] for n in range(ng)])
    m = jnp.concatenate([m] * ng, axis=0)
    pc = jnp.exp2(sc_c - m)
    pn = jnp.exp2(sc_n - m)
    l_rows = jnp.sum(pc, axis=1, keepdims=True) + jnp.sum(pn, axis=1, keepdims=True)
    acc_rows = (jnp.dot(pc.astype(BF16), vc, preferred_element_type=F32)
                + jnp.dot(pn.astype(BF16), vn, preferred_element_type=F32))
    l = sum(l_rows[n * R:(n + 1) * R] for n in range(ng))
    acc = sum(acc_rows[n * R:(n + 1) * R] for n in range(ng))
    res = acc / l
    for r in range(GROUP):
        o_ref[:, r * HEAD_DIM:(r + 1) * HEAD_DIM] = res[r * TS:(r + 1) * TS, :].astype(o_ref.dtype)


def _b_sample(qb_s, cache_k, cache_v, kv_new, tab):
    DB, TS, NQ = qb_s.shape
    W = cache_k.shape[1] // KV_HEADS
    hd = HEAD_DIM
    return pl.pallas_call(
        _bs_kernel,
        grid=(DB, KV_HEADS),
        in_specs=[pl.BlockSpec((None, TS, GROUP * hd), lambda b, g: (b, 0, g)),
                  pl.BlockSpec((None, TS, GROUP * hd), lambda b, g: (b, 0, KV_HEADS + g)),
                  pl.BlockSpec((None, TS, GROUP * hd), lambda b, g: (b, 0, 2 * KV_HEADS + g)),
                  pl.BlockSpec((None, W * KV_HEADS, hd), lambda b, g: (b, 0, 0)),
                  pl.BlockSpec((None, W * KV_HEADS, hd), lambda b, g: (b, 0, 0)),
                  pl.BlockSpec((None, LANES, hd), lambda b, g: (b, 0, g)),
                  pl.BlockSpec((None, LANES, hd), lambda b, g: (b, 0, KV_HEADS + g)),
                  pl.BlockSpec((len(B_GROUPS), None, GROUP * TS, W + LANES), lambda b, g: (0, g, 0, 0))],
        out_specs=pl.BlockSpec((None, TS, GROUP * hd), lambda b, g: (b, 0, g)),
        out_shape=jax.ShapeDtypeStruct((DB, TS, N_HEADS * hd), BF16),
        compiler_params=_params(("parallel", "arbitrary")),
        name="b_sample",
    )(qb_s, qb_s, qb_s, cache_k, cache_v, kv_new, kv_new, tab)


FFN_TF = 512
HALO = 16


def _ffn_up_kernel(xm_ref, xh_ref, wg_ref, wu_ref, cw_ref, cb_ref, act_ref, st_ref, xe_ref, *,
                   tiles_per_batch):
    i = pl.program_id(0)
    tm = xm_ref.shape[0]

    @pl.when(pl.program_id(1) == 0)
    def _():
        first = (i % tiles_per_batch) == 0
        xh = xh_ref[...]
        xe_ref[0:HALO, :] = jnp.where(first, jnp.zeros_like(xh), xh)
        xe_ref[HALO:, :] = xm_ref[...]

    gate = jnp.dot(xe_ref[...], wg_ref[...], preferred_element_type=F32)
    up = jnp.dot(xm_ref[...], wu_ref[...], preferred_element_type=F32)
    cw = cw_ref[...]
    conv = cb_ref[...] + ((gate[HALO - 2:HALO - 2 + tm] * cw[0:1] + gate[HALO - 1:HALO - 1 + tm] * cw[1:2])
                          + gate[HALO:] * cw[2:3])
    act_ref[...] = (jax.nn.silu(conv) * up).astype(act_ref.dtype)
    st_ref[...] = gate[HALO + tm - (CONV_WIDTH - 1):, :]


def _ffn_up(x_bf, w_g_bf, w_u_bf, conv_w, conv_b, layer, T):
    M, D = x_bf.shape
    F = conv_w.shape[2]
    tm = 1024
    assert T % tm == 0 and M % T == 0 and F % FFN_TF == 0 and tm % HALO == 0
    nf = F // FFN_TF
    hb = tm // HALO
    return pl.pallas_call(
        functools.partial(_ffn_up_kernel, tiles_per_batch=T // tm),
        grid=(M // tm, nf),
        in_specs=[pl.BlockSpec((tm, D), lambda i, j: (i, 0)),
                  pl.BlockSpec((HALO, D), lambda i, j: (jnp.maximum(i * hb - 1, 0), 0)),
                  pl.BlockSpec((None, D, FFN_TF), lambda i, j: (layer, 0, j)),
                  pl.BlockSpec((None, D, FFN_TF), lambda i, j: (layer, 0, j)),
                  pl.BlockSpec((None, CONV_WIDTH, FFN_TF), lambda i, j: (layer, 0, j)),
                  pl.BlockSpec((None, 1, FFN_TF), lambda i, j: (layer, 0, j))],
        out_specs=[pl.BlockSpec((tm, FFN_TF), lambda i, j: (i, j)),
                   pl.BlockSpec((None, CONV_WIDTH - 1, FFN_TF), lambda i, j: (i, 0, j))],
        out_shape=[jax.ShapeDtypeStruct((M, F), BF16),
                   jax.ShapeDtypeStruct((M // tm, CONV_WIDTH - 1, F), F32)],
        scratch_shapes=[pltpu.VMEM((HALO + tm, D), BF16)],
        compiler_params=_params(("parallel", "arbitrary")),
        name="ffn_up",
    )(x_bf, x_bf, w_g_bf, w_u_bf, conv_w, conv_b)


def _mm_ln_kernel(a_ref, w_ref, x_ref, g_ref, b_ref, h_ref, hb_ref, acc_ref, *, alpha):
    k = pl.program_id(1)

    @pl.when(k == 0)
    def _():
        acc_ref[...] = jnp.zeros_like(acc_ref)

    acc_ref[...] += jnp.dot(a_ref[...], w_ref[...], preferred_element_type=F32)

    @pl.when(k == pl.num_programs(1) - 1)
    def _():
        y = alpha * x_ref[...] + acc_ref[...]
        mu = jnp.mean(y, axis=-1, keepdims=True)
        d = y - mu
        var = jnp.mean(d * d, axis=-1, keepdims=True)
        h = d * lax.rsqrt(var + LN_EPS) * g_ref[...] + b_ref[...]
        h_ref[...] = h
        hb_ref[...] = h.astype(hb_ref.dtype)


def _mm_ln(a_bf, w_bf, layer, x, g, b, alpha):
    M, K = a_bf.shape
    D = w_bf.shape[2]
    tm = min(512, M)
    tk = K // 4 if K > 2048 else K
    assert M % tm == 0 and K % tk == 0 and tk % LANES == 0
    return pl.pallas_call(
        functools.partial(_mm_ln_kernel, alpha=alpha),
        grid=(M // tm, K // tk),
        in_specs=[pl.BlockSpec((tm, tk), lambda i, k: (i, k)),
                  pl.BlockSpec((None, tk, D), lambda i, k: (layer, k, 0)),
                  pl.BlockSpec((tm, D), lambda i, k: (i, 0)),
                  pl.BlockSpec((1, D), lambda i, k: (0, 0)),
                  pl.BlockSpec((1, D), lambda i, k: (0, 0))],
        out_specs=[pl.BlockSpec((tm, D), lambda i, k: (i, 0)),
                   pl.BlockSpec((tm, D), lambda i, k: (i, 0))],
        out_shape=[jax.ShapeDtypeStruct((M, D), F32), jax.ShapeDtypeStruct((M, D), BF16)],
        scratch_shapes=[pltpu.VMEM((tm, D), F32)],
        compiler_params=_params(("parallel", "arbitrary")),
        name="mm_ln",
    )(a_bf, w_bf, x, g.reshape(1, D), b.reshape(1, D))


def _layer_norm(x, g, b):
    mu = x.mean(-1, keepdims=True)
    var = jnp.square(x - mu).mean(-1, keepdims=True)
    return (x - mu) * lax.rsqrt(var + LN_EPS) * g + b


def _pick_tm(M):
    for tm in (1024, 512, 256, 128, 64, 32, 16):
        if M % tm == 0:
            return tm
    raise ValueError(M)


def _pick_tn(N):
    for tn in (1536, 1024, 512, 384, 256, 128):
        if N % tn == 0:
            return tn
    raise ValueError(N)


def _proj(x_bf, w_bf, layer=0):
    return _mm(x_bf, w_bf, layer, _pick_tm(x_bf.shape[0]), _pick_tn(w_bf.shape[2]))


def _ffn_act_sample(hs_bf, state, w_g_bf, w_u_bf, conv_w, conv_b, layer, DB, TS):
    conv_w, conv_b = conv_w[layer], conv_b[layer]
    F = conv_w.shape[1]
    gate = _proj(hs_bf, w_g_bf, layer).reshape(DB, TS, F)
    up = _proj(hs_bf, w_u_bf, layer).reshape(DB, TS, F)
    ext = jnp.concatenate([state, gate], axis=1)
    conv = conv_b + sum(ext[:, j:j + TS] * conv_w[j] for j in range(CONV_WIDTH))
    act = (jax.nn.silu(conv) * up).astype(BF16).reshape(DB * TS, F)
    return act, ext[:, ext.shape[1] - (CONV_WIDTH - 1):]


def _table_indices(T, W, TS):
    U = SAT_DIST + TZ_OFF + QB
    assert T <= T5_MAX_DISTANCE
    u = np.arange(U)[:, None]
    c = np.arange(LANES)[None, :]
    tz_idx = _t5_bucket(jnp.asarray(np.maximum(u - c - TZ_OFF, 0), I32))
    a = np.arange(QB)[:, None]
    band = []
    for window, dil in B_GROUPS:
        for off in (QB, 0):
            d = a + off - c
            ok = (d >= 0) & (d <= window // dil)
            band.append(jnp.where(jnp.asarray(ok), _t5_bucket(jnp.asarray(np.maximum(d, 0) * dil, I32)),
                                  NUM_BUCKETS))
    band_idx = jnp.concatenate(band, axis=0)
    nkt = W // LANES + 1
    t = np.arange(TS)[None, :, None]
    key = (np.arange(nkt)[:, None, None] * LANES + np.arange(LANES)[None, None, :])
    d = W + t - key
    tabs = []
    for window, dil in B_GROUPS:
        ok = (d >= 0) & (d % dil == 0) & (d <= window)
        tabs.append(jnp.where(jnp.asarray(ok), _t5_bucket(jnp.asarray(np.maximum(d, 0), I32)), NUM_BUCKETS))
    samp_idx = jnp.concatenate(tabs, axis=0).reshape(-1, LANES)
    rows = [tz_idx, band_idx, samp_idx]
    total = sum(r.shape[0] for r in rows)
    pad = (-total) % 128
    if pad:
        rows.append(jnp.full((pad, LANES), NUM_BUCKETS, I32))
    return jnp.concatenate(rows, axis=0), U, nkt


def kernel(x_prompt, x_sample, cache_k_a, cache_v_a, cache_kidx_a, cache_k_b, cache_v_b, state_ffn, page_table, a_w_in, a_w_o, a_kn_g, a_kn_b, b_w_kv, b_w_q, b_w_o, ffn_w_up, ffn_conv_w, ffn_conv_b, ffn_w_down, ln_g, ln_b, rel_bias):
    B, T, D = x_prompt.shape
    DB, TS, _ = x_sample.shape
    depth = ffn_w_up.shape[0]
    n_a = a_w_in.shape[0]
    d_ff = ffn_w_down.shape[1]
    W = cache_k_b.shape[1]
    n_pages = page_table.shape[1]
    page = cache_k_a.shape[2]
    past = n_pages * page
    alpha = (2 * depth) ** 0.25
    kvw = KV_HEADS * HEAD_DIM
    a_q = N_HEADS * HEAD_DIM
    a_in = a_w_in.shape[2]
    np_cols = ((a_in + 511) // 512) * 512
    ki0 = a_q + 2 * kvw + IDX_HEADS * IDX_DIM

    idx_all, U, nkt = _table_indices(T, W, TS)
    tabs = _bias_tables(rel_bias * LOG2E, idx_all)
    tz = tabs[:, :U]
    band = tabs[:, U:U + 6 * QB].reshape(N_HEADS, 3, 2, QB, LANES)
    samp = tabs[:, U + 6 * QB:U + 6 * QB + 3 * nkt * TS].reshape(KV_HEADS, GROUP, 3, nkt, TS, LANES)
    samp = samp.transpose(2, 0, 1, 4, 3, 5).reshape(3, KV_HEADS, GROUP * TS, nkt * LANES)
    n_slab = SAT_DIST // LANES + 1
    tzs = tz[:, TZ_OFF:TZ_OFF + n_slab * LANES].reshape(N_HEADS, n_slab, LANES, LANES)[:, :, :TS, :]

    cache_k_a2 = cache_k_a.reshape(n_a, -1, page * KV_HEADS, HEAD_DIM)
    cache_v_a2 = cache_v_a.reshape(n_a, -1, page * KV_HEADS, HEAD_DIM)
    cache_k_b2 = cache_k_b.reshape(DB, W * KV_HEADS, HEAD_DIM)
    cache_v_b2 = cache_v_b.reshape(DB, W * KV_HEADS, HEAD_DIM)
    cache_kidx_t = jnp.swapaxes(cache_kidx_a, 2, 3)

    def pad_rows(x, n):
        return jnp.pad(x, ((0, 0), (0, n - x.shape[1]), (0, 0)))

    f_pad = ((d_ff + FFN_TF - 1) // FFN_TF) * FFN_TF
    pad_f = f_pad - d_ff
    w_g_all = jnp.pad(ffn_w_up[:, :, :d_ff], ((0, 0), (0, 0), (0, pad_f))).astype(BF16)
    w_u_all = jnp.pad(ffn_w_up[:, :, d_ff:], ((0, 0), (0, 0), (0, pad_f))).astype(BF16)
    w_down_all = jnp.pad(ffn_w_down, ((0, 0), (0, pad_f), (0, 0))).astype(BF16)
    conv_w_all = jnp.pad(ffn_conv_w, ((0, 0), (0, 0), (0, pad_f)))
    conv_b_all = jnp.pad(ffn_conv_b, ((0, 0), (0, pad_f))).reshape(depth, 1, f_pad)
    w_in_all = jnp.pad(a_w_in, ((0, 0), (0, 0), (0, np_cols - a_in))).astype(BF16)
    w_o_a, w_o_b = a_w_o.astype(BF16), b_w_o.astype(BF16)
    w_q_all = b_w_q.astype(BF16)
    w_kv_bf = b_w_kv.astype(BF16)[None]

    hp = x_prompt.reshape(B * T, D)
    hs = x_sample.reshape(DB * TS, D)
    hp_bf, hs_bf = hp.astype(BF16), hs.astype(BF16)
    ka_p, va_p, kia_p, ka_s, va_s, kia_s, ffn_p, ffn_s = [], [], [], [], [], [], [], []
    for layer in range(depth):
        if layer < n_a:
            a = layer
            w_o_bf, w_o_layer = w_o_a, a
            proj = _proj(hp_bf, w_in_all, a).reshape(B, T, np_cols)
            k = proj[..., a_q:a_q + kvw].reshape(B, T, KV_HEADS, HEAD_DIM)
            v = proj[..., a_q + kvw:a_q + 2 * kvw].reshape(B, T, KV_HEADS, HEAD_DIM)
            ki = _layer_norm(proj[..., ki0:ki0 + IDX_DIM], a_kn_g[a], a_kn_b[a])
            ka_p.append(k); va_p.append(v); kia_p.append(ki)
            mask = _a1_prompt(ki.astype(BF16), proj, min(TOPK_MAX, T // 4))
            o = _a2_prompt(proj, mask, tz).reshape(B * T, a_q)
            proj_s = _proj(hs_bf, w_in_all, a).reshape(DB, TS, np_cols)
            k_s = proj_s[..., a_q:a_q + kvw]
            v_s = proj_s[..., a_q + kvw:a_q + 2 * kvw]
            ki_s = _layer_norm(proj_s[..., ki0:ki0 + IDX_DIM], a_kn_g[a], a_kn_b[a])
            ka_s.append(k_s.reshape(DB, TS, KV_HEADS, HEAD_DIM))
            va_s.append(v_s.reshape(DB, TS, KV_HEADS, HEAD_DIM))
            kia_s.append(ki_s)
            mask_s = _a1_sample(page_table, proj_s, pad_rows(ki_s, LANES).astype(BF16), cache_kidx_t, a,
                                min(TOPK_MAX, (past + TS) // 4))
            o_s = _a2_sample(page_table, proj_s, mask_s, tzs, pad_rows(k_s, LANES), pad_rows(v_s, LANES),
                             cache_k_a2, cache_v_a2, a).reshape(DB * TS, a_q)
        else:
            if layer == n_a:
                kv_p = _proj(hp_bf, w_kv_bf).reshape(B, T, 2 * kvw)
                kv_s = _proj(hs_bf, w_kv_bf).reshape(DB, TS, 2 * kvw)
                kv_s_pad = pad_rows(kv_s, LANES)
            bl = layer - n_a
            w_o_bf, w_o_layer = w_o_b, bl
            o = _b_prompt(_proj(hp_bf, w_q_all, bl).reshape(B, T, -1), kv_p, band).reshape(B * T, a_q)
            o_s = _b_sample(_proj(hs_bf, w_q_all, bl).reshape(DB, TS, -1), cache_k_b2, cache_v_b2, kv_s_pad,
                            samp).reshape(DB * TS, a_q)
        hp, hp_bf = _mm_ln(o, w_o_bf, w_o_layer, hp, ln_g[layer, 0], ln_b[layer, 0], alpha)
        hs, hs_bf = _mm_ln(o_s, w_o_bf, w_o_layer, hs, ln_g[layer, 0], ln_b[layer, 0], alpha)
        act, st = _ffn_up(hp_bf, w_g_all, w_u_all, conv_w_all, conv_b_all, layer, T)
        ffn_p.append(st.reshape(B, -1, CONV_WIDTH - 1, f_pad)[:, -1, :, :d_ff])
        act_s, st_s = _ffn_act_sample(hs_bf, jnp.pad(state_ffn[layer], ((0, 0), (0, 0), (0, pad_f))),
                                      w_g_all, w_u_all, conv_w_all, conv_b_all, layer, DB, TS)
        ffn_s.append(st_s[..., :d_ff])
        hp, hp_bf = _mm_ln(act, w_down_all, layer, hp, ln_g[layer, 1], ln_b[layer, 1], alpha)
        hs, hs_bf = _mm_ln(act_s, w_down_all, layer, hs, ln_g[layer, 1], ln_b[layer, 1], alpha)
    keep = min(max(w for w, _ in B_GROUPS), T)
    kb_p = kv_p[..., :kvw].reshape(B, T, KV_HEADS, HEAD_DIM)
    vb_p = kv_p[..., kvw:].reshape(B, T, KV_HEADS, HEAD_DIM)
    kb_s = kv_s[..., :kvw].reshape(DB, TS, KV_HEADS, HEAD_DIM)
    vb_s = kv_s[..., kvw:].reshape(DB, TS, KV_HEADS, HEAD_DIM)
    return (hp.reshape(B, T, D), hs.reshape(DB, TS, D), jnp.stack(ka_p), jnp.stack(va_p), jnp.stack(kia_p),
            jnp.stack(ka_s), jnp.stack(va_s), jnp.stack(kia_s), kb_p[:, T - keep:], vb_p[:, T - keep:],
            kb_s, vb_s, jnp.stack(ffn_p), jnp.stack(ffn_s))
```

```python
import functools
import math

import numpy as np
import jax
import jax.numpy as jnp
from jax import lax
from jax.experimental import pallas as pl
from jax.experimental.pallas import tpu as pltpu

F32 = jnp.float32
BF16 = jnp.bfloat16
I32 = jnp.int32

HEAD_DIM = 128
N_HEADS = 16
KV_HEADS = 4
GROUP = N_HEADS // KV_HEADS
IDX_HEADS = 16
IDX_DIM = 64
TOPK_MAX = 256
B_GROUPS = ((128, 1), (512, 4), (2048, 16))
NUM_BUCKETS = 32
T5_MAX_DISTANCE = 2048
LN_EPS = 1e-5
NEG = -1e30
CONV_WIDTH = 3

LANES = 128
QB = 128
KC = 512
A2_KC = KC
TZ_OFF = A2_KC - QB
SAT_DIST = T5_MAX_DISTANCE + QB
INT_MIN = -2 ** 31
KEY_NEG_INF = INT_MIN + 0x7FFFFF
VMEM_LIMIT = 56 * 1024 * 1024

LOG2E = 1.4426950408889634
QSCALE = HEAD_DIM ** -0.5 * LOG2E

NT_DIMS = (((1,), (1,)), ((), ()))


def _nt(a, b):
    return lax.dot_general(a, b, NT_DIMS, preferred_element_type=F32)


def _params(sem):
    return pltpu.CompilerParams(dimension_semantics=sem, vmem_limit_bytes=VMEM_LIMIT)


def _mm_kernel(x_ref, w_ref, o_ref):
    o_ref[...] = jnp.dot(x_ref[...], w_ref[...], preferred_element_type=F32).astype(o_ref.dtype)


def _mm(x, w, layer, tm, tn, out_dtype=F32):
    M, K = x.shape
    N = w.shape[2]
    assert M % tm == 0 and N % tn == 0, (M, N, tm, tn)
    return pl.pallas_call(
        _mm_kernel,
        grid=(M // tm, N // tn),
        in_specs=[pl.BlockSpec((tm, K), lambda i, j: (i, 0)),
                  pl.BlockSpec((None, K, tn), lambda i, j: (layer, 0, j))],
        out_specs=pl.BlockSpec((tm, tn), lambda i, j: (i, j)),
        out_shape=jax.ShapeDtypeStruct((M, N), out_dtype),
        compiler_params=_params(("parallel", "arbitrary")),
        name="mm",
    )(x, w)


def _t5_bucket(dist):
    dist = jnp.maximum(dist, 0)
    exact = NUM_BUCKETS // 2
    far = exact + (jnp.log(jnp.maximum(dist, 1).astype(F32) / exact)
                   / math.log(T5_MAX_DISTANCE / exact) * (NUM_BUCKETS - exact)).astype(I32)
    return jnp.where(dist < exact, dist, jnp.minimum(far, NUM_BUCKETS - 1))


def _table_kernel(rb_ref, idx_ref, o_ref):
    slab = 16

    def body(i, carry):
        r0 = pl.multiple_of(i * slab, slab)
        idx = idx_ref[pl.ds(r0, slab), :]
        accs = [jnp.full(idx.shape, NEG, F32)] * N_HEADS
        for k in range(NUM_BUCKETS):
            hit = idx == k
            accs = [jnp.where(hit, rb_ref[k, h], accs[h]) for h in range(N_HEADS)]
        for h in range(N_HEADS):
            o_ref[h, pl.ds(r0, slab), :] = accs[h]
        return carry

    lax.fori_loop(0, idx_ref.shape[0] // slab, body, 0)


def _bias_tables(rel_bias, idx):
    R = idx.shape[0]
    tr = 128
    assert R % tr == 0
    return pl.pallas_call(
        _table_kernel,
        grid=(R // tr,),
        in_specs=[pl.BlockSpec(memory_space=pltpu.SMEM),
                  pl.BlockSpec((tr, LANES), lambda i: (i, 0))],
        out_specs=pl.BlockSpec((N_HEADS, tr, LANES), lambda i: (0, i, 0)),
        out_shape=jax.ShapeDtypeStruct((N_HEADS, R, LANES), F32),
        compiler_params=_params(("arbitrary",)),
        name="bias_tables",
    )(rel_bias, idx)


def _sortable(x):
    bits = lax.bitcast_convert_type(x, I32)
    return bits ^ ((bits >> 31) & jnp.int32(0x7FFFFFFF))


def _kth_largest(count_ge, shape, k):
    def body(t, ans):
        cand_u = ans | (jnp.int32(1) << (31 - t))
        cnt = count_ge(cand_u ^ jnp.int32(INT_MIN))
        return jnp.where(cnt >= k, cand_u, ans)
    ans = lax.fori_loop(0, 32, body, jnp.zeros(shape, I32))
    return ans ^ jnp.int32(INT_MIN)


def _tie_bound(count_eq_below, shape, need, nbits):
    def body(t, ans):
        cand = ans | (jnp.int32(1) << (nbits - 1 - t))
        return jnp.where(count_eq_below(cand) < need, cand, ans)
    return lax.fori_loop(0, nbits, body, jnp.zeros(shape, I32))


A1_TILES = 4


def _a1p_kernel(kin_ref, qi_ref, tail_ref, mask_ref, st_ref, jb_ref, *, topk):
    i = pl.program_id(1)
    T = kin_ref.shape[0]
    NQ = A1_TILES * QB
    KS = KC // A1_TILES
    nchunk = T // KC
    nbits = (T - 1).bit_length()
    tail_t = jnp.concatenate([tail_ref[t * QB:(t + 1) * QB, :].T for t in range(A1_TILES)], axis=1)
    wi_t = tail_t[IDX_DIM:IDX_DIM + IDX_HEADS, :] * (IDX_HEADS ** -0.5 * IDX_DIM ** -0.5)
    qpos = i * NQ + lax.broadcasted_iota(I32, (KC, NQ), 1)
    row = lax.broadcasted_iota(I32, (KC, NQ), 0)
    qpos_s = i * NQ + lax.broadcasted_iota(I32, (KS, NQ), 1)
    row_s = lax.broadcasted_iota(I32, (KS, NQ), 0)

    nj = (i * NQ) // KC + 1

    for c in range(nchunk):
        @pl.when(c < nj)
        def _():
            for sub in range(KC // KS):
                r0 = c * KC + sub * KS
                kc = kin_ref[r0:r0 + KS, :]
                acc = jnp.zeros((KS, NQ), F32)
                for h in range(IDX_HEADS):
                    qh = qi_ref[:, h * IDX_DIM:(h + 1) * IDX_DIM].astype(BF16)
                    acc = acc + jnp.maximum(_nt(kc, qh), 0.0) * wi_t[h:h + 1, :]
                acc = jnp.where(r0 + row_s <= qpos_s, acc, -jnp.inf)
                st_ref[r0:r0 + KS, :] = _sortable(acc)

    def select(nvis):
        def count(pred):
            cnt = jnp.zeros((1, NQ), F32)
            for c in range(nvis):
                blk = st_ref[c * KC:(c + 1) * KC, :]
                cnt = cnt + jnp.sum(jnp.where(pred(blk, c * KC + row), 1.0, 0.0), axis=0, keepdims=True)
            return cnt

        thr = _kth_largest(lambda cand: count(lambda blk, _: blk >= cand), (1, NQ), float(topk))
        need = float(topk) - count(lambda blk, _: blk > thr)
        n_eq = count(lambda blk, _: blk == thr)
        excess = jnp.where((n_eq > need) & (thr != KEY_NEG_INF), 1.0, 0.0)
        jb_ref[...] = jnp.full((1, NQ), T, I32)

        @pl.when(jnp.max(excess) > 0.0)
        def _():
            jb_ref[...] = _tie_bound(
                lambda cand: count(lambda blk, kpos: jnp.where(blk == thr, kpos, T) < cand),
                (1, NQ), need, nbits)

        jb = jb_ref[...]
        for c in range(nchunk):
            if c < nvis:
                blk = st_ref[c * KC:(c + 1) * KC, :]
                kpos = c * KC + row
                rank_pos = jnp.where(blk == thr, kpos, jnp.where(blk > thr, -1, T + 1))
                sel = jnp.where(kpos <= qpos, rank_pos, T + 1) <= jb
                m_t = jnp.where(sel, 0.0, NEG)
                for t in range(A1_TILES):
                    for s4 in range(KC // QB):
                        mask_ref[t, c, :, s4 * QB:(s4 + 1) * QB] = (
                            m_t[s4 * QB:(s4 + 1) * QB, t * QB:(t + 1) * QB].T.astype(mask_ref.dtype))
            else:
                for t in range(A1_TILES):
                    mask_ref[t, c] = jnp.full((QB, KC), NEG, mask_ref.dtype)

    for nvis in range(1, nchunk + 1):
        pl.when(nj == nvis)(functools.partial(select, nvis))


def _a1_prompt(kin_bf, proj, topk):
    B, T, _ = proj.shape
    nq, nc = T // QB, T // KC
    assert topk <= KC
    assert (KC // QB) % A1_TILES == 0 and nq % A1_TILES == 0
    nqs = A1_TILES * QB
    qi_blk = (N_HEADS * HEAD_DIM + 2 * KV_HEADS * HEAD_DIM) // (IDX_HEADS * IDX_DIM)
    tail_blk = (N_HEADS * HEAD_DIM + 2 * KV_HEADS * HEAD_DIM + IDX_HEADS * IDX_DIM) // LANES
    return pl.pallas_call(
        functools.partial(_a1p_kernel, topk=topk),
        grid=(B, nq // A1_TILES),
        in_specs=[pl.BlockSpec((None, T, IDX_DIM), lambda b, i: (b, 0, 0)),
                  pl.BlockSpec((None, nqs, IDX_HEADS * IDX_DIM), lambda b, i: (b, i, qi_blk)),
                  pl.BlockSpec((None, nqs, LANES), lambda b, i: (b, i, tail_blk))],
        out_specs=pl.BlockSpec((None, A1_TILES, nc, QB, KC), lambda b, i: (b, i, 0, 0, 0)),
        out_shape=jax.ShapeDtypeStruct((B, nq, nc, QB, KC), BF16),
        scratch_shapes=[pltpu.VMEM((T, nqs), I32), pltpu.VMEM((1, nqs), I32)],
        compiler_params=_params(("parallel", "arbitrary")),
        name="a1_prompt",
    )(kin_bf, proj, proj)


A2_TILES = 4


def _a2p_kernel(q_ref, k_ref, v_ref, mask_ref, tz_ref, o_ref):
    for t in range(A2_TILES):
        _a2p_tile(pl.program_id(2) * A2_TILES + t, q_ref.at[t * QB:(t + 1) * QB], k_ref, v_ref,
                  mask_ref.at[t], tz_ref, o_ref.at[t * QB:(t + 1) * QB])


def _a2p_tile(i, q_ref, k_ref, v_ref, mask_ref, tz_ref, o_ref):
    nj = (i * QB) // A2_KC + 1
    HP = GROUP
    R = HP * QB
    heads = [list(range(c * HP, (c + 1) * HP)) for c in range(GROUP // HP)]
    qs = [(jnp.concatenate([q_ref[:, r * HEAD_DIM:(r + 1) * HEAD_DIM] for r in hs], axis=0)
           * QSCALE).astype(BF16) for hs in heads]

    def body(j, carry):
        k0 = pl.multiple_of(j * A2_KC, A2_KC)
        kt = k_ref[pl.ds(k0, A2_KC), :].astype(BF16)
        vt = v_ref[pl.ds(k0, A2_KC), :].astype(BF16)
        base = i * QB - j * A2_KC + TZ_OFF
        mk = jnp.concatenate([mask_ref[j * (A2_KC // KC) + c] for c in range(A2_KC // KC)],
                             axis=1).astype(F32)
        mk = jnp.concatenate([mk] * HP, axis=0)
        out = []
        for hs, q, (m, l, acc) in zip(heads, qs, carry):
            bias = jnp.concatenate(
                [jnp.concatenate(
                    [tz_ref[r, pl.ds(pl.multiple_of(base - QB * s4, QB), QB), :]
                     for s4 in range(A2_KC // QB)], axis=1) for r in hs], axis=0)
            s = _nt(q, kt) + bias + mk
            m_new = jnp.maximum(m, jnp.max(s, axis=1, keepdims=True))
            alpha = jnp.exp2(m - m_new)
            p = jnp.exp2(s - m_new)
            l = alpha * l + jnp.sum(p, axis=1, keepdims=True)
            acc = alpha * acc + jnp.dot(p.astype(BF16), vt, preferred_element_type=F32)
            out.append((m_new, l, acc))
        return tuple(out)

    init = tuple((jnp.full((R, 1), NEG, F32), jnp.zeros((R, 1), F32), jnp.zeros((R, HEAD_DIM), F32))
                 for _ in heads)
    final = lax.fori_loop(0, nj, body, init)
    for hs, (m, l, acc) in zip(heads, final):
        res = acc / l
        for n, r in enumerate(hs):
            o_ref[:, r * HEAD_DIM:(r + 1) * HEAD_DIM] = res[n * QB:(n + 1) * QB, :].astype(o_ref.dtype)


def _a2_prompt(proj, mask, tz):
    B, T, _ = proj.shape
    nq, nc = T // QB, T // KC
    gw = GROUP * HEAD_DIM
    k_blk0 = N_HEADS * HEAD_DIM // HEAD_DIM
    v_blk0 = k_blk0 + KV_HEADS
    U = tz.shape[1]
    assert nq % A2_TILES == 0
    qr = A2_TILES * QB
    return pl.pallas_call(
        _a2p_kernel,
        grid=(KV_HEADS, B, nq // A2_TILES),
        in_specs=[pl.BlockSpec((None, qr, gw), lambda g, b, i: (b, i, g)),
                  pl.BlockSpec((None, T, HEAD_DIM), lambda g, b, i: (b, 0, k_blk0 + g)),
                  pl.BlockSpec((None, T, HEAD_DIM), lambda g, b, i: (b, 0, v_blk0 + g)),
                  pl.BlockSpec((None, A2_TILES, nc, QB, KC), lambda g, b, i: (b, i, 0, 0, 0)),
                  pl.BlockSpec((GROUP, U, LANES), lambda g, b, i: (g, 0, 0))],
        out_specs=pl.BlockSpec((None, qr, gw), lambda g, b, i: (b, i, g)),
        out_shape=jax.ShapeDtypeStruct((B, T, N_HEADS * HEAD_DIM), BF16),
        compiler_params=_params(("parallel", "parallel", "arbitrary")),
        name="a2_prompt",
    )(proj, proj, proj, mask, tz)


PAGES_PER_STEP = 32


A1S_PAGES = 128
A1S_PART = 16


def _a1s_kernel(pt_ref, qi_ref, tail_ref, knew_ref, *rest, topk, past):
    page_refs = rest[:A1S_PAGES]
    mask_ref, st_ref, jb_ref = rest[A1S_PAGES:]
    s = pl.program_id(1)
    nt, TS, _ = st_ref.shape
    L = nt * LANES
    nbits = (L - 1).bit_length()
    qi = jnp.concatenate([qi_ref[:, h * IDX_DIM:(h + 1) * IDX_DIM] for h in range(IDX_HEADS)],
                         axis=0).astype(BF16)
    w_col = jnp.concatenate([tail_ref[:, IDX_DIM + h:IDX_DIM + h + 1] for h in range(IDX_HEADS)],
                            axis=0) * (IDX_HEADS ** -0.5 * IDX_DIM ** -0.5)

    def scores(qk):
        sc = jnp.maximum(qk, 0.0) * w_col
        return jnp.sum(sc.reshape(IDX_HEADS, TS, qk.shape[1]), axis=0)

    for part in range(A1S_PAGES // A1S_PART):
        kt = jnp.concatenate([page_refs[part * A1S_PART + p][...] for p in range(A1S_PART)],
                             axis=1).astype(BF16)
        keys = _sortable(scores(jnp.dot(qi, kt, preferred_element_type=F32)))
        for p in range(A1S_PART):
            st_ref[s * A1S_PAGES + part * A1S_PART + p] = keys[:, p * LANES:(p + 1) * LANES]

    @pl.when(s == pl.num_programs(1) - 1)
    def _():
        t = lax.broadcasted_iota(I32, (TS, LANES), 0)
        c = lax.broadcasted_iota(I32, (TS, LANES), 1)
        sc = jnp.where(c <= t, scores(_nt(qi, knew_ref[...])), -jnp.inf)
        st_ref[nt - 1] = _sortable(sc)

        kpos = (lax.broadcasted_iota(I32, (nt, TS, LANES), 0) * LANES
                + lax.broadcasted_iota(I32, (nt, TS, LANES), 2))
        qpos = past + lax.broadcasted_iota(I32, (nt, TS, LANES), 1)

        def count(pred):
            per_lane = jnp.sum(jnp.where(pred(st_ref[...]), 1.0, 0.0), axis=0)
            return jnp.sum(per_lane, axis=1, keepdims=True)

        thr = _kth_largest(lambda cand: count(lambda k: k >= cand[None]), (TS, 1), float(topk))
        thr3 = thr[None]
        need = float(topk) - count(lambda k: k > thr3)
        n_eq = count(lambda k: k == thr3)
        excess = jnp.where((n_eq > need) & (thr != KEY_NEG_INF), 1.0, 0.0)
        jb_ref[...] = jnp.full((TS, 1), L, I32)

        @pl.when(jnp.max(excess) > 0.0)
        def _():
            jb_ref[...] = _tie_bound(
                lambda cand: count(lambda k: jnp.where(k == thr3, kpos, L) < cand[None]),
                (TS, 1), need, nbits)

        keys = st_ref[...]
        rank_pos = jnp.where(keys == thr3, kpos, jnp.where(keys > thr3, -1, L + 1))
        sel = jnp.where(kpos <= qpos, rank_pos, L + 1) <= jb_ref[...][None]
        mask_ref[...] = jnp.where(sel, 0.0, NEG)


def _a1_sample(page_table, proj_s, knew_bf, cache_kidx, layer, topk):
    DB, TS, _ = proj_s.shape
    n_pages = page_table.shape[1]
    page = cache_kidx.shape[3]
    assert page == LANES and n_pages % A1S_PAGES == 0 and topk <= n_pages * page
    nt = n_pages + 1
    qi_blk = (N_HEADS * HEAD_DIM + 2 * KV_HEADS * HEAD_DIM) // (IDX_HEADS * IDX_DIM)
    tail_blk = (N_HEADS * HEAD_DIM + 2 * KV_HEADS * HEAD_DIM + IDX_HEADS * IDX_DIM) // LANES

    def page_spec(p):
        return pl.BlockSpec((None, None, IDX_DIM, page),
                            lambda b, s, pt: (layer, pt[b, s * A1S_PAGES + p], 0, 0))

    grid_spec = pltpu.PrefetchScalarGridSpec(
        num_scalar_prefetch=1,
        grid=(DB, n_pages // A1S_PAGES),
        in_specs=[pl.BlockSpec((None, TS, IDX_HEADS * IDX_DIM), lambda b, s, pt: (b, 0, qi_blk)),
                  pl.BlockSpec((None, TS, LANES), lambda b, s, pt: (b, 0, tail_blk)),
                  pl.BlockSpec((None, LANES, IDX_DIM), lambda b, s, pt: (b, 0, 0))]
                 + [page_spec(p) for p in range(A1S_PAGES)],
        out_specs=pl.BlockSpec((None, nt, TS, LANES), lambda b, s, pt: (b, 0, 0, 0)),
        scratch_shapes=[pltpu.VMEM((nt, TS, LANES), I32), pltpu.VMEM((TS, 1), I32)])
    return pl.pallas_call(
        functools.partial(_a1s_kernel, topk=topk, past=n_pages * page),
        grid_spec=grid_spec,
        out_shape=jax.ShapeDtypeStruct((DB, nt, TS, LANES), F32),
        compiler_params=_params(("parallel", "arbitrary")),
        name="a1_sample",
    )(page_table, proj_s, proj_s, knew_bf, *([cache_kidx] * A1S_PAGES))


def _a2s_kernel(pt_ref, q_ref, mask_ref, tzs_ref, knew_ref, vnew_ref, *rest, past):
    k_pages = rest[:PAGES_PER_STEP]
    v_pages = rest[PAGES_PER_STEP:2 * PAGES_PER_STEP]
    o_ref, m_ref, l_ref, acc_ref = rest[2 * PAGES_PER_STEP:]
    s = pl.program_id(1)
    TS = q_ref.shape[0]
    n_slab = tzs_ref.shape[1]

    @pl.when(s == 0)
    def _():
        m_ref[...] = jnp.full(m_ref.shape, NEG, F32)
        l_ref[...] = jnp.zeros(l_ref.shape, F32)
        acc_ref[...] = jnp.zeros(acc_ref.shape, F32)

    qg = [(jnp.concatenate([q_ref[:, (g * GROUP + r) * HEAD_DIM:(g * GROUP + r + 1) * HEAD_DIM]
                            for r in range(GROUP)], axis=0) * QSCALE).astype(BF16)
          for g in range(KV_HEADS)]

    def update(g, kt, vt, tiles):
        slabs = [jnp.minimum((past // LANES) - t, n_slab - 1) for t in tiles]
        mk = jnp.concatenate([mask_ref[t] for t in tiles], axis=1)
        mk = jnp.concatenate([mk] * GROUP, axis=0)
        bias = jnp.concatenate(
            [jnp.concatenate([tzs_ref[g * GROUP + r, sl] for r in range(GROUP)], axis=0) for sl in slabs],
            axis=1)
        sc = _nt(qg[g], kt) + bias + mk
        m_old = m_ref[g]
        m_new = jnp.maximum(m_old, jnp.max(sc, axis=1, keepdims=True))
        alpha = jnp.exp2(m_old - m_new)
        p = jnp.exp2(sc - m_new)
        l_ref[g] = alpha * l_ref[g] + jnp.sum(p, axis=1, keepdims=True)
        acc_ref[g] = alpha * acc_ref[g] + jnp.dot(p.astype(BF16), vt, preferred_element_type=F32)
        m_ref[g] = m_new

    tiles = [s * PAGES_PER_STEP + p for p in range(PAGES_PER_STEP)]
    for g in range(KV_HEADS):
        rows = pl.ds(g, LANES, stride=KV_HEADS)
        kt = jnp.concatenate([kp[rows, :] for kp in k_pages], axis=0).astype(BF16)
        vt = jnp.concatenate([vp[rows, :] for vp in v_pages], axis=0).astype(BF16)
        update(g, kt, vt, tiles)

    @pl.when(s == pl.num_programs(1) - 1)
    def _():
        for g in range(KV_HEADS):
            update(g, knew_ref[:, g * HEAD_DIM:(g + 1) * HEAD_DIM].astype(BF16),
                   vnew_ref[:, g * HEAD_DIM:(g + 1) * HEAD_DIM].astype(BF16), [past // LANES])
        for g in range(KV_HEADS):
            res = acc_ref[g] / l_ref[g]
            for r in range(GROUP):
                h = g * GROUP + r
                o_ref[:, h * HEAD_DIM:(h + 1) * HEAD_DIM] = res[r * TS:(r + 1) * TS, :].astype(o_ref.dtype)


def _a2_sample(page_table, proj_s, mask_s, tzs, knew, vnew, cache_k, cache_v, layer):
    DB, TS, _ = proj_s.shape
    n_pages = page_table.shape[1]
    page = cache_k.shape[2] // KV_HEADS
    kvw = KV_HEADS * HEAD_DIM
    nt = n_pages + 1

    def page_spec(p):
        return pl.BlockSpec((None, None, page * KV_HEADS, HEAD_DIM),
                            lambda b, s, pt: (layer, pt[b, s * PAGES_PER_STEP + p], 0, 0))

    grid_spec = pltpu.PrefetchScalarGridSpec(
        num_scalar_prefetch=1,
        grid=(DB, n_pages // PAGES_PER_STEP),
        in_specs=[pl.BlockSpec((None, TS, N_HEADS * HEAD_DIM), lambda b, s, pt: (b, 0, 0)),
                  pl.BlockSpec((None, nt, TS, LANES), lambda b, s, pt: (b, 0, 0, 0)),
                  pl.BlockSpec(tzs.shape, lambda b, s, pt: (0, 0, 0, 0)),
                  pl.BlockSpec((None, LANES, kvw), lambda b, s, pt: (b, 0, 0)),
                  pl.BlockSpec((None, LANES, kvw), lambda b, s, pt: (b, 0, 0))]
                 + [page_spec(p) for p in range(PAGES_PER_STEP)] * 2,
        out_specs=pl.BlockSpec((None, TS, N_HEADS * HEAD_DIM), lambda b, s, pt: (b, 0, 0)),
        scratch_shapes=[pltpu.VMEM((KV_HEADS, GROUP * TS, 1), F32),
                        pltpu.VMEM((KV_HEADS, GROUP * TS, 1), F32),
                        pltpu.VMEM((KV_HEADS, GROUP * TS, HEAD_DIM), F32)])
    return pl.pallas_call(
        functools.partial(_a2s_kernel, past=n_pages * page),
        grid_spec=grid_spec,
        out_shape=jax.ShapeDtypeStruct((DB, TS, N_HEADS * HEAD_DIM), BF16),
        compiler_params=_params(("parallel", "arbitrary")),
        name="a2_sample",
    )(page_table, proj_s, mask_s, tzs, knew, vnew,
      *([cache_k] * PAGES_PER_STEP), *([cache_v] * PAGES_PER_STEP))


BQ = 512


def _bp_kernel(*refs):
    n_q = len(B_GROUPS) * GROUP
    q_refs = refs[:n_q]
    k_ref, v_ref, band_ref, o_ref, m_ref, l_ref, acc_ref, og_ref, lseg_ref = refs[n_q:]
    step = pl.program_id(2)
    t0 = step * BQ
    first = t0 == 0
    whole = [n for n, (_, dil) in enumerate(B_GROUPS) if dil * QB > BQ]
    assert len(whole) == 1 and B_GROUPS[whole[0]][1] * QB == k_ref.shape[0]

    def band(n, half):
        return jnp.concatenate([band_ref[r, n, half] for r in range(GROUP)], axis=0)

    @pl.when(first)
    def _():
        n = whole[0]
        dil = B_GROUPS[n][1]
        for rho in range(dil):
            cls = pl.ds(rho, QB, stride=dil)
            qs = (jnp.concatenate([qr[cls, :] for qr in q_refs[n * GROUP:(n + 1) * GROUP]], axis=0)
                  * QSCALE).astype(BF16)
            s = _nt(qs, k_ref[cls, :].astype(BF16)) + band(n, 1)
            m = jnp.max(s, axis=1, keepdims=True)
            p = jnp.exp2(s - m)
            l = jnp.sum(p, axis=1, keepdims=True)
            out = jnp.dot(p.astype(BF16), v_ref[cls, :].astype(BF16), preferred_element_type=F32) / l
            lse = jnp.broadcast_to(m + jnp.log2(l), out.shape)
            for r in range(GROUP):
                og_ref[r, cls, :] = out[r * QB:(r + 1) * QB]
                lseg_ref[r, cls, :] = lse[r * QB:(r + 1) * QB]

    rows_step = pl.ds(pl.multiple_of(t0, BQ), BQ)
    for r in range(GROUP):
        m_ref[r] = lseg_ref[r, rows_step, :]
        l_ref[r] = jnp.ones(l_ref.shape[1:], F32)
        acc_ref[r] = og_ref[r, rows_step, :]

    def stacked(ref, rows):
        return jnp.concatenate([ref[r, rows, :] for r in range(GROUP)], axis=0)

    def update(q_ref, rows_q, nrow, key_rows, biases):
        qs = (jnp.concatenate([qr[rows_q, :] for qr in q_ref], axis=0) * QSCALE).astype(BF16)
        kt = jnp.concatenate([k_ref[kr, :] for kr in key_rows], axis=0).astype(BF16)
        vt = jnp.concatenate([v_ref[kr, :] for kr in key_rows], axis=0).astype(BF16)
        s = _nt(qs, kt) + jnp.concatenate(biases, axis=1)
        m_old = stacked(m_ref, rows_q)
        m_new = jnp.maximum(m_old, jnp.max(s, axis=1, keepdims=True))
        alpha = jnp.exp2(m_old - m_new)
        p = jnp.exp2(s - jnp.concatenate([m_new] * len(key_rows), axis=1))
        l_new = alpha * stacked(l_ref, rows_q) + jnp.sum(p, axis=1, keepdims=True)
        acc = alpha * stacked(acc_ref, rows_q) + jnp.dot(p.astype(BF16), vt, preferred_element_type=F32)
        for r in range(GROUP):
            m_ref[r, rows_q, :] = m_new[r * nrow:(r + 1) * nrow]
            l_ref[r, rows_q, :] = l_new[r * nrow:(r + 1) * nrow]
            acc_ref[r, rows_q, :] = acc[r * nrow:(r + 1) * nrow]

    for n, (window, dil) in enumerate(B_GROUPS):
        assert window // dil == QB
        span = dil * QB
        if n in whole:
            continue
        q_ref = q_refs[n * GROUP:(n + 1) * GROUP]
        for rho in range(dil):
            for mt in range(BQ // span):
                def rows(start):
                    return pl.ds(start, QB, stride=dil) if dil > 1 else pl.ds(start, QB)
                off = rho + span * mt
                prev = jnp.maximum(t0 + off - span, rho) if mt == 0 else t0 + off - span
                bias_a = band(n, 0)
                if mt == 0:
                    bias_a = bias_a + jnp.where(first, NEG, 0.0)
                update(q_ref, rows(off), QB, [rows(prev), rows(t0 + off)], [bias_a, band(n, 1)])

    for r in range(GROUP):
        o_ref[:, r * HEAD_DIM:(r + 1) * HEAD_DIM] = (acc_ref[r] / l_ref[r]).astype(o_ref.dtype)


def _b_prompt(qb, kv, band):
    B, T, _ = qb.shape
    nh = N_HEADS
    hd = HEAD_DIM
    gw = GROUP * hd
    assert T % BQ == 0

    def q_spec(n, r):
        whole =B_GROUPS[n][1] * QB > BQ
        if whole:
            return pl.BlockSpec((None, T, hd), lambda b, g, c: (b, 0, (n * KV_HEADS + g) * GROUP + r))
        return pl.BlockSpec((None, BQ, hd), lambda b, g, c: (b, c, (n * KV_HEADS + g) * GROUP + r))

    q_specs = [q_spec(n, r) for n in range(len(B_GROUPS)) for r in range(GROUP)]
    return pl.pallas_call(
        _bp_kernel,
        grid=(B, KV_HEADS, T // BQ),
        in_specs=q_specs + [
                  pl.BlockSpec((None, T, hd), lambda b, g, c: (b, 0, g)),
                  pl.BlockSpec((None, T, hd), lambda b, g, c: (b, 0, KV_HEADS + g)),
                  pl.BlockSpec((GROUP, 3, 2, QB, LANES), lambda b, g, c: (g, 0, 0, 0, 0))],
        out_specs=pl.BlockSpec((None, BQ, gw), lambda b, g, c: (b, c, g)),
        out_shape=jax.ShapeDtypeStruct((B, T, nh * hd), BF16),
        scratch_shapes=[pltpu.VMEM((GROUP, BQ, LANES), F32), pltpu.VMEM((GROUP, BQ, LANES), F32),
                        pltpu.VMEM((GROUP, BQ, hd), F32),
                        pltpu.VMEM((GROUP, T, hd), F32), pltpu.VMEM((GROUP, T, LANES), F32)],
        compiler_params=_params(("parallel", "parallel", "arbitrary")),
        name="b_prompt",
    )(*([qb] * len(q_specs)), kv, kv, band)


def _bs_kernel(q0_ref, q1_ref, q2_ref, kc_ref, vc_ref, kn_ref, vn_ref, tab_ref, o_ref):
    TS = q0_ref.shape[0]
    W = kc_ref.shape[0] // KV_HEADS
    rows = pl.ds(pl.program_id(1), W, stride=KV_HEADS)
    kc = kc_ref[rows, :].astype(BF16)
    vc = vc_ref[rows, :].astype(BF16)
    kn = kn_ref[...].astype(BF16)
    vn = vn_ref[...].astype(BF16)
    ng = len(B_GROUPS)
    R = GROUP * TS
    q = (jnp.concatenate([q_ref[:, r * HEAD_DIM:(r + 1) * HEAD_DIM]
                          for q_ref in (q0_ref, q1_ref, q2_ref) for r in range(GROUP)], axis=0)
         * QSCALE).astype(BF16)
    tab = jnp.concatenate([tab_ref[n] for n in range(ng)], axis=0)
    sc_c = _nt(q, kc) + tab[:, :W]
    sc_n = _nt(q, kn) + tab[:, W:]
    m_rows = jnp.maximum(jnp.max(sc_c, axis=1, keepdims=True), jnp.max(sc_n, axis=1, keepdims=True))
    m = functools.reduce(jnp.maximum, [m_rows[n * R:(n + 1) * R] for n in range(ng)])
    m = jnp.concatenate([m] * ng, axis=0)
    pc = jnp.exp2(sc_c - m)
    pn = jnp.exp2(sc_n - m)
    l_rows = jnp.sum(pc, axis=1, keepdims=True) + jnp.sum(pn, axis=1, keepdims=True)
    acc_rows = (jnp.dot(pc.astype(BF16), vc, preferred_element_type=F32)
                + jnp.dot(pn.astype(BF16), vn, preferred_element_type=F32))
    l = sum(l_rows[n * R:(n + 1) * R] for n in range(ng))
    acc = sum(acc_rows[n * R:(n + 1) * R] for n in range(ng))
    res = acc / l
    for r in range(GROUP):
        o_ref[:, r * HEAD_DIM:(r + 1) * HEAD_DIM] = res[r * TS:(r + 1) * TS, :].astype(o_ref.dtype)


def _b_sample(qb_s, cache_k, cache_v, kv_new, tab):
    DB, TS, NQ = qb_s.shape
    W = cache_k.shape[1] // KV_HEADS
    hd = HEAD_DIM
    return pl.pallas_call(
        _bs_kernel,
        grid=(DB, KV_HEADS),
        in_specs=[pl.BlockSpec((None, TS, GROUP * hd), lambda b, g: (b, 0, g)),
                  pl.BlockSpec((None, TS, GROUP * hd), lambda b, g: (b, 0, KV_HEADS + g)),
                  pl.BlockSpec((None, TS, GROUP * hd), lambda b, g: (b, 0, 2 * KV_HEADS + g)),
                  pl.BlockSpec((None, W * KV_HEADS, hd), lambda b, g: (b, 0, 0)),
                  pl.BlockSpec((None, W * KV_HEADS, hd), lambda b, g: (b, 0, 0)),
                  pl.BlockSpec((None, LANES, hd), lambda b, g: (b, 0, g)),
                  pl.BlockSpec((None, LANES, hd), lambda b, g: (b, 0, KV_HEADS + g)),
                  pl.BlockSpec((len(B_GROUPS), None, GROUP * TS, W + LANES), lambda b, g: (0, g, 0, 0))],
        out_specs=pl.BlockSpec((None, TS, GROUP * hd), lambda b, g: (b, 0, g)),
        out_shape=jax.ShapeDtypeStruct((DB, TS, N_HEADS * hd), BF16),
        compiler_params=_params(("parallel", "arbitrary")),
        name="b_sample",
    )(qb_s, qb_s, qb_s, cache_k, cache_v, kv_new, kv_new, tab)


FFN_TF = 512
HALO = 16


def _ffn_up_kernel(xm_ref, xh_ref, wg_ref, wu_ref, cw_ref, cb_ref, act_ref, st_ref, xe_ref, *,
                   tiles_per_batch):
    i = pl.program_id(0)
    tm = xm_ref.shape[0]

    @pl.when(pl.program_id(1) == 0)
    def _():
        first = (i % tiles_per_batch) == 0
        xh = xh_ref[...]
        xe_ref[0:HALO, :] = jnp.where(first, jnp.zeros_like(xh), xh)
        xe_ref[HALO:, :] = xm_ref[...]

    gate = jnp.dot(xe_ref[...], wg_ref[...], preferred_element_type=F32)
    up = jnp.dot(xm_ref[...], wu_ref[...], preferred_element_type=F32)
    cw = cw_ref[...]
    conv = cb_ref[...] + ((gate[HALO - 2:HALO - 2 + tm] * cw[0:1] + gate[HALO - 1:HALO - 1 + tm] * cw[1:2])
                          + gate[HALO:] * cw[2:3])
    act_ref[...] = (jax.nn.silu(conv) * up).astype(act_ref.dtype)
    st_ref[...] = gate[HALO + tm - (CONV_WIDTH - 1):, :]


def _ffn_up(x_bf, w_g_bf, w_u_bf, conv_w, conv_b, layer, T):
    M, D = x_bf.shape
    F = conv_w.shape[2]
    tm = 1024
    assert T % tm == 0 and M % T == 0 and F % FFN_TF == 0 and tm % HALO == 0
    nf = F // FFN_TF
    hb = tm // HALO
    return pl.pallas_call(
        functools.partial(_ffn_up_kernel, tiles_per_batch=T // tm),
        grid=(M // tm, nf),
        in_specs=[pl.BlockSpec((tm, D), lambda i, j: (i, 0)),
                  pl.BlockSpec((HALO, D), lambda i, j: (jnp.maximum(i * hb - 1, 0), 0)),
                  pl.BlockSpec((None, D, FFN_TF), lambda i, j: (layer, 0, j)),
                  pl.BlockSpec((None, D, FFN_TF), lambda i, j: (layer, 0, j)),
                  pl.BlockSpec((None, CONV_WIDTH, FFN_TF), lambda i, j: (layer, 0, j)),
                  pl.BlockSpec((None, 1, FFN_TF), lambda i, j: (layer, 0, j))],
        out_specs=[pl.BlockSpec((tm, FFN_TF), lambda i, j: (i, j)),
                   pl.BlockSpec((None, CONV_WIDTH - 1, FFN_TF), lambda i, j: (i, 0, j))],
        out_shape=[jax.ShapeDtypeStruct((M, F), BF16),
                   jax.ShapeDtypeStruct((M // tm, CONV_WIDTH - 1, F), F32)],
        scratch_shapes=[pltpu.VMEM((HALO + tm, D), BF16)],
        compiler_params=_params(("parallel", "arbitrary")),
        name="ffn_up",
    )(x_bf, x_bf, w_g_bf, w_u_bf, conv_w, conv_b)


def _mm_ln_kernel(a_ref, w_ref, x_ref, g_ref, b_ref, h_ref, hb_ref, *, alpha):
    k = pl.program_id(1)
    part = jnp.dot(a_ref[...], w_ref[...], preferred_element_type=F32)

    @pl.when(k == 0)
    def _():
        h_ref[...] = part

    @pl.when(k > 0)
    def _():
        h_ref[...] += part

    @pl.when(k == pl.num_programs(1) - 1)
    def _():
        y = alpha * x_ref[...] + h_ref[...]
        mu = jnp.mean(y, axis=-1, keepdims=True)
        d = y - mu
        var = jnp.mean(d * d, axis=-1, keepdims=True)
        h = d * lax.rsqrt(var + LN_EPS) * g_ref[...] + b_ref[...]
        h_ref[...] = h
        hb_ref[...] = h.astype(hb_ref.dtype)


def _mm_ln(a_bf, w_bf, layer, x, g, b, alpha):
    M, K = a_bf.shape
    D = w_bf.shape[2]
    tm = min(512, M)
    tk = K // 2 if K > 2048 else K
    assert M % tm == 0 and K % tk == 0 and tk % LANES == 0
    return pl.pallas_call(
        functools.partial(_mm_ln_kernel, alpha=alpha),
        grid=(M // tm, K // tk),
        in_specs=[pl.BlockSpec((tm, tk), lambda i, k: (i, k)),
                  pl.BlockSpec((None, tk, D), lambda i, k: (layer, k, 0)),
                  pl.BlockSpec((tm, D), lambda i, k: (i, 0)),
                  pl.BlockSpec((1, D), lambda i, k: (0, 0)),
                  pl.BlockSpec((1, D), lambda i, k: (0, 0))],
        out_specs=[pl.BlockSpec((tm, D), lambda i, k: (i, 0)),
                   pl.BlockSpec((tm, D), lambda i, k: (i, 0))],
        out_shape=[jax.ShapeDtypeStruct((M, D), F32), jax.ShapeDtypeStruct((M, D), BF16)],
        compiler_params=_params(("parallel", "arbitrary")),
        name="mm_ln",
    )(a_bf, w_bf, x, g.reshape(1, D), b.reshape(1, D))


def _layer_norm(x, g, b):
    mu = x.mean(-1, keepdims=True)
    var = jnp.square(x - mu).mean(-1, keepdims=True)
    return (x - mu) * lax.rsqrt(var + LN_EPS) * g + b


def _pick_tm(M):
    for tm in (1024, 512, 256, 128, 64, 32, 16):
        if M % tm == 0:
            return tm
    raise ValueError(M)


def _pick_tn(N):
    for tn in (1536, 1024, 512, 384, 256, 128):
        if N % tn == 0:
            return tn
    raise ValueError(N)


def _proj(x_bf, w_bf, layer=0):
    return _mm(x_bf, w_bf, layer, _pick_tm(x_bf.shape[0]), _pick_tn(w_bf.shape[2]))


def _ffn_act_sample(hs_bf, state, w_g_bf, w_u_bf, conv_w, conv_b, layer, DB, TS):
    conv_w, conv_b = conv_w[layer], conv_b[layer]
    F = conv_w.shape[1]
    gate = _proj(hs_bf, w_g_bf, layer).reshape(DB, TS, F)
    up = _proj(hs_bf, w_u_bf, layer).reshape(DB, TS, F)
    ext = jnp.concatenate([state, gate], axis=1)
    conv = conv_b + sum(ext[:, j:j + TS] * conv_w[j] for j in range(CONV_WIDTH))
    act = (jax.nn.silu(conv) * up).astype(BF16).reshape(DB * TS, F)
    return act, ext[:, ext.shape[1] - (CONV_WIDTH - 1):]


def _table_indices(T, W, TS):
    U = SAT_DIST + TZ_OFF + QB
    assert T <= T5_MAX_DISTANCE
    u = np.arange(U)[:, None]
    c = np.arange(LANES)[None, :]
    tz_idx = _t5_bucket(jnp.asarray(np.maximum(u - c - TZ_OFF, 0), I32))
    a = np.arange(QB)[:, None]
    band = []
    for window, dil in B_GROUPS:
        for off in (QB, 0):
            d = a + off - c
            ok = (d >= 0) & (d <= window // dil)
            band.append(jnp.where(jnp.asarray(ok), _t5_bucket(jnp.asarray(np.maximum(d, 0) * dil, I32)),
                                  NUM_BUCKETS))
    band_idx = jnp.concatenate(band, axis=0)
    nkt = W // LANES + 1
    t = np.arange(TS)[None, :, None]
    key = (np.arange(nkt)[:, None, None] * LANES + np.arange(LANES)[None, None, :])
    d = W + t - key
    tabs = []
    for window, dil in B_GROUPS:
        ok = (d >= 0) & (d % dil == 0) & (d <= window)
        tabs.append(jnp.where(jnp.asarray(ok), _t5_bucket(jnp.asarray(np.maximum(d, 0), I32)), NUM_BUCKETS))
    samp_idx = jnp.concatenate(tabs, axis=0).reshape(-1, LANES)
    rows = [tz_idx, band_idx, samp_idx]
    total = sum(r.shape[0] for r in rows)
    pad = (-total) % 128
    if pad:
        rows.append(jnp.full((pad, LANES), NUM_BUCKETS, I32))
    return jnp.concatenate(rows, axis=0), U, nkt


def kernel(x_prompt, x_sample, cache_k_a, cache_v_a, cache_kidx_a, cache_k_b, cache_v_b, state_ffn, page_table, a_w_in, a_w_o, a_kn_g, a_kn_b, b_w_kv, b_w_q, b_w_o, ffn_w_up, ffn_conv_w, ffn_conv_b, ffn_w_down, ln_g, ln_b, rel_bias):
    B, T, D = x_prompt.shape
    DB, TS, _ = x_sample.shape
    depth = ffn_w_up.shape[0]
    n_a = a_w_in.shape[0]
    d_ff = ffn_w_down.shape[1]
    W = cache_k_b.shape[1]
    n_pages = page_table.shape[1]
    page = cache_k_a.shape[2]
    past = n_pages * page
    alpha = (2 * depth) ** 0.25
    kvw = KV_HEADS * HEAD_DIM
    a_q = N_HEADS * HEAD_DIM
    a_in = a_w_in.shape[2]
    np_cols = ((a_in + 511) // 512) * 512
    ki0 = a_q + 2 * kvw + IDX_HEADS * IDX_DIM

    idx_all, U, nkt = _table_indices(T, W, TS)
    tabs = _bias_tables(rel_bias * LOG2E, idx_all)
    tz = tabs[:, :U]
    band = tabs[:, U:U + 6 * QB].reshape(N_HEADS, 3, 2, QB, LANES)
    samp = tabs[:, U + 6 * QB:U + 6 * QB + 3 * nkt * TS].reshape(KV_HEADS, GROUP, 3, nkt, TS, LANES)
    samp = samp.transpose(2, 0, 1, 4, 3, 5).reshape(3, KV_HEADS, GROUP * TS, nkt * LANES)
    n_slab = SAT_DIST // LANES + 1
    tzs = tz[:, TZ_OFF:TZ_OFF + n_slab * LANES].reshape(N_HEADS, n_slab, LANES, LANES)[:, :, :TS, :]

    cache_k_a2 = cache_k_a.reshape(n_a, -1, page * KV_HEADS, HEAD_DIM)
    cache_v_a2 = cache_v_a.reshape(n_a, -1, page * KV_HEADS, HEAD_DIM)
    cache_k_b2 = cache_k_b.reshape(DB, W * KV_HEADS, HEAD_DIM)
    cache_v_b2 = cache_v_b.reshape(DB, W * KV_HEADS, HEAD_DIM)
    cache_kidx_t = jnp.swapaxes(cache_kidx_a, 2, 3)

    def pad_rows(x, n):
        return jnp.pad(x, ((0, 0), (0, n - x.shape[1]), (0, 0)))

    f_pad = ((d_ff + FFN_TF - 1) // FFN_TF) * FFN_TF
    pad_f = f_pad - d_ff
    w_g_all = jnp.pad(ffn_w_up[:, :, :d_ff], ((0, 0), (0, 0), (0, pad_f))).astype(BF16)
    w_u_all = jnp.pad(ffn_w_up[:, :, d_ff:], ((0, 0), (0, 0), (0, pad_f))).astype(BF16)
    w_down_all = jnp.pad(ffn_w_down, ((0, 0), (0, pad_f), (0, 0))).astype(BF16)
    conv_w_all = jnp.pad(ffn_conv_w, ((0, 0), (0, 0), (0, pad_f)))
    conv_b_all = jnp.pad(ffn_conv_b, ((0, 0), (0, pad_f))).reshape(depth, 1, f_pad)
    w_in_all = jnp.pad(a_w_in, ((0, 0), (0, 0), (0, np_cols - a_in))).astype(BF16)
    w_o_a, w_o_b = a_w_o.astype(BF16), b_w_o.astype(BF16)
    w_q_all = b_w_q.astype(BF16)
    w_kv_bf = b_w_kv.astype(BF16)[None]

    hp = x_prompt.reshape(B * T, D)
    hs = x_sample.reshape(DB * TS, D)
    hp_bf, hs_bf = hp.astype(BF16), hs.astype(BF16)
    ka_p, va_p, kia_p, ka_s, va_s, kia_s, ffn_p, ffn_s = [], [], [], [], [], [], [], []
    for layer in range(depth):
        if layer < n_a:
            a = layer
            w_o_bf, w_o_layer = w_o_a, a
            proj = _proj(hp_bf, w_in_all, a).reshape(B, T, np_cols)
            k = proj[..., a_q:a_q + kvw].reshape(B, T, KV_HEADS, HEAD_DIM)
            v = proj[..., a_q + kvw:a_q + 2 * kvw].reshape(B, T, KV_HEADS, HEAD_DIM)
            ki = _layer_norm(proj[..., ki0:ki0 + IDX_DIM], a_kn_g[a], a_kn_b[a])
            ka_p.append(k); va_p.append(v); kia_p.append(ki)
            mask = _a1_prompt(ki.astype(BF16), proj, min(TOPK_MAX, T // 4))
            o = _a2_prompt(proj, mask, tz).reshape(B * T, a_q)
            proj_s = _proj(hs_bf, w_in_all, a).reshape(DB, TS, np_cols)
            k_s = proj_s[..., a_q:a_q + kvw]
            v_s = proj_s[..., a_q + kvw:a_q + 2 * kvw]
            ki_s = _layer_norm(proj_s[..., ki0:ki0 + IDX_DIM], a_kn_g[a], a_kn_b[a])
            ka_s.append(k_s.reshape(DB, TS, KV_HEADS, HEAD_DIM))
            va_s.append(v_s.reshape(DB, TS, KV_HEADS, HEAD_DIM))
            kia_s.append(ki_s)
            mask_s = _a1_sample(page_table, proj_s, pad_rows(ki_s, LANES).astype(BF16), cache_kidx_t, a,
                                min(TOPK_MAX, (past + TS) // 4))
            o_s = _a2_sample(page_table, proj_s, mask_s, tzs, pad_rows(k_s, LANES), pad_rows(v_s, LANES),
                             cache_k_a2, cache_v_a2, a).reshape(DB * TS, a_q)
        else:
            if layer == n_a:
                kv_p = _proj(hp_bf, w_kv_bf).reshape(B, T, 2 * kvw)
                kv_s = _proj(hs_bf, w_kv_bf).reshape(DB, TS, 2 * kvw)
                kv_s_pad = pad_rows(kv_s, LANES)
            bl = layer - n_a
            w_o_bf, w_o_layer = w_o_b, bl
            o = _b_prompt(_proj(hp_bf, w_q_all, bl).reshape(B, T, -1), kv_p, band).reshape(B * T, a_q)
            o_s = _b_sample(_proj(hs_bf, w_q_all, bl).reshape(DB, TS, -1), cache_k_b2, cache_v_b2, kv_s_pad,
                            samp).reshape(DB * TS, a_q)
        hp, hp_bf = _mm_ln(o, w_o_bf, w_o_layer, hp, ln_g[layer, 0], ln_b[layer, 0], alpha)
        hs, hs_bf = _mm_ln(o_s, w_o_bf, w_o_layer, hs, ln_g[layer, 0], ln_b[layer, 0], alpha)
        act, st = _ffn_up(hp_bf, w_g_all, w_u_all, conv_w_all, conv_b_all, layer, T)
        ffn_p.append(st.reshape(B, -1, CONV_WIDTH - 1, f_pad)[:, -1, :, :d_ff])
        act_s, st_s = _ffn_act_sample(hs_bf, jnp.pad(state_ffn[layer], ((0, 0), (0, 0), (0, pad_f))),
                                      w_g_all, w_u_all, conv_w_all, conv_b_all, layer, DB, TS)
        ffn_s.append(st_s[..., :d_ff])
        hp, hp_bf = _mm_ln(act, w_down_all, layer, hp, ln_g[layer, 1], ln_b[layer, 1], alpha)
        hs, hs_bf = _mm_ln(act_s, w_down_all, layer, hs, ln_g[layer, 1], ln_b[layer, 1], alpha)
    keep = min(max(w for w, _ in B_GROUPS), T)
    kb_p = kv_p[..., :kvw].reshape(B, T, KV_HEADS, HEAD_DIM)
    vb_p = kv_p[..., kvw:].reshape(B, T, KV_HEADS, HEAD_DIM)
    kb_s = kv_s[..., :kvw].reshape(DB, TS, KV_HEADS, HEAD_DIM)
    vb_s = kv_s[..., kvw:].reshape(DB, TS, KV_HEADS, HEAD_DIM)
    return (hp.reshape(B, T, D), hs.reshape(DB, TS, D), jnp.stack(ka_p), jnp.stack(va_p), jnp.stack(kia_p),
            jnp.stack(ka_s), jnp.stack(va_s), jnp.stack(kia_s), kb_p[:, T - keep:], vb_p[:, T - keep:],
            kb_s, vb_s, jnp.stack(ffn_p), jnp.stack(ffn_s))
```

```python
import functools
import math

import numpy as np
import jax
import jax.numpy as jnp
from jax import lax
from jax.experimental import pallas as pl
from jax.experimental.pallas import tpu as pltpu

F32 = jnp.float32
BF16 = jnp.bfloat16
I32 = jnp.int32

HEAD_DIM = 128
N_HEADS = 16
KV_HEADS = 4
GROUP = N_HEADS // KV_HEADS
IDX_HEADS = 16
IDX_DIM = 64
TOPK_MAX = 256
B_GROUPS = ((128, 1), (512, 4), (2048, 16))
NUM_BUCKETS = 32
T5_MAX_DISTANCE = 2048
LN_EPS = 1e-5
NEG = -1e30
CONV_WIDTH = 3

LANES = 128
QB = 128
KC = 512
A2_KC = KC
TZ_OFF = A2_KC - QB
SAT_DIST = T5_MAX_DISTANCE + QB
INT_MIN = -2 ** 31
KEY_NEG_INF = INT_MIN + 0x7FFFFF
VMEM_LIMIT = 56 * 1024 * 1024

LOG2E = 1.4426950408889634
QSCALE = HEAD_DIM ** -0.5 * LOG2E

NT_DIMS = (((1,), (1,)), ((), ()))


def _nt(a, b):
    return lax.dot_general(a, b, NT_DIMS, preferred_element_type=F32)


def _params(sem):
    return pltpu.CompilerParams(dimension_semantics=sem, vmem_limit_bytes=VMEM_LIMIT)


def _mm_kernel(x_ref, w_ref, o_ref):
    o_ref[...] = jnp.dot(x_ref[...], w_ref[...], preferred_element_type=F32).astype(o_ref.dtype)


def _mm(x, w, layer, tm, tn, out_dtype=F32):
    M, K = x.shape
    N = w.shape[2]
    assert M % tm == 0 and N % tn == 0, (M, N, tm, tn)
    return pl.pallas_call(
        _mm_kernel,
        grid=(M // tm, N // tn),
        in_specs=[pl.BlockSpec((tm, K), lambda i, j: (i, 0)),
                  pl.BlockSpec((None, K, tn), lambda i, j: (layer, 0, j))],
        out_specs=pl.BlockSpec((tm, tn), lambda i, j: (i, j)),
        out_shape=jax.ShapeDtypeStruct((M, N), out_dtype),
        compiler_params=_params(("parallel", "arbitrary")),
        name="mm",
    )(x, w)


def _t5_bucket(dist):
    dist = jnp.maximum(dist, 0)
    exact = NUM_BUCKETS // 2
    far = exact + (jnp.log(jnp.maximum(dist, 1).astype(F32) / exact)
                   / math.log(T5_MAX_DISTANCE / exact) * (NUM_BUCKETS - exact)).astype(I32)
    return jnp.where(dist < exact, dist, jnp.minimum(far, NUM_BUCKETS - 1))


def _table_kernel(rb_ref, idx_ref, o_ref):
    slab = 16

    def body(i, carry):
        r0 = pl.multiple_of(i * slab, slab)
        idx = idx_ref[pl.ds(r0, slab), :]
        accs = [jnp.full(idx.shape, NEG, F32)] * N_HEADS
        for k in range(NUM_BUCKETS):
            hit = idx == k
            accs = [jnp.where(hit, rb_ref[k, h], accs[h]) for h in range(N_HEADS)]
        for h in range(N_HEADS):
            o_ref[h, pl.ds(r0, slab), :] = accs[h]
        return carry

    lax.fori_loop(0, idx_ref.shape[0] // slab, body, 0)


def _bias_tables(rel_bias, idx):
    R = idx.shape[0]
    tr = 128
    assert R % tr == 0
    return pl.pallas_call(
        _table_kernel,
        grid=(R // tr,),
        in_specs=[pl.BlockSpec(memory_space=pltpu.SMEM),
                  pl.BlockSpec((tr, LANES), lambda i: (i, 0))],
        out_specs=pl.BlockSpec((N_HEADS, tr, LANES), lambda i: (0, i, 0)),
        out_shape=jax.ShapeDtypeStruct((N_HEADS, R, LANES), F32),
        compiler_params=_params(("arbitrary",)),
        name="bias_tables",
    )(rel_bias, idx)


def _sortable(x):
    bits = lax.bitcast_convert_type(x, I32)
    return bits ^ ((bits >> 31) & jnp.int32(0x7FFFFFFF))


def _kth_largest(count_ge, shape, k):
    def body(t, ans):
        cand_u = ans | (jnp.int32(1) << (31 - t))
        cnt = count_ge(cand_u ^ jnp.int32(INT_MIN))
        return jnp.where(cnt >= k, cand_u, ans)
    ans = lax.fori_loop(0, 32, body, jnp.zeros(shape, I32))
    return ans ^ jnp.int32(INT_MIN)


def _tie_bound(count_eq_below, shape, need, nbits):
    def body(t, ans):
        cand = ans | (jnp.int32(1) << (nbits - 1 - t))
        return jnp.where(count_eq_below(cand) < need, cand, ans)
    return lax.fori_loop(0, nbits, body, jnp.zeros(shape, I32))


A1_TILES = 4


def _a1p_kernel(kin_ref, qi_ref, tail_ref, mask_ref, st_ref, jb_ref, *, topk):
    i = pl.program_id(1)
    T = kin_ref.shape[0]
    NQ = A1_TILES * QB
    KS = KC // A1_TILES
    nchunk = T // KC
    nbits = (T - 1).bit_length()
    tail_t = jnp.concatenate([tail_ref[t * QB:(t + 1) * QB, :].T for t in range(A1_TILES)], axis=1)
    wi_t = tail_t[IDX_DIM:IDX_DIM + IDX_HEADS, :] * (IDX_HEADS ** -0.5 * IDX_DIM ** -0.5)
    qpos = i * NQ + lax.broadcasted_iota(I32, (KC, NQ), 1)
    row = lax.broadcasted_iota(I32, (KC, NQ), 0)
    qpos_s = i * NQ + lax.broadcasted_iota(I32, (KS, NQ), 1)
    row_s = lax.broadcasted_iota(I32, (KS, NQ), 0)

    nj = (i * NQ) // KC + 1

    for c in range(nchunk):
        @pl.when(c < nj)
        def _():
            for sub in range(KC // KS):
                r0 = c * KC + sub * KS
                kc = kin_ref[r0:r0 + KS, :]
                acc = jnp.zeros((KS, NQ), F32)
                for h in range(IDX_HEADS):
                    qh = qi_ref[:, h * IDX_DIM:(h + 1) * IDX_DIM].astype(BF16)
                    acc = acc + jnp.maximum(_nt(kc, qh), 0.0) * wi_t[h:h + 1, :]
                acc = jnp.where(r0 + row_s <= qpos_s, acc, -jnp.inf)
                st_ref[r0:r0 + KS, :] = _sortable(acc)

    def select(nvis):
        def count(pred):
            cnt = jnp.zeros((1, NQ), F32)
            for c in range(nvis):
                blk = st_ref[c * KC:(c + 1) * KC, :]
                cnt = cnt + jnp.sum(jnp.where(pred(blk, c * KC + row), 1.0, 0.0), axis=0, keepdims=True)
            return cnt

        thr = _kth_largest(lambda cand: count(lambda blk, _: blk >= cand), (1, NQ), float(topk))
        need = float(topk) - count(lambda blk, _: blk > thr)
        n_eq = count(lambda blk, _: blk == thr)
        excess = jnp.where((n_eq > need) & (thr != KEY_NEG_INF), 1.0, 0.0)
        jb_ref[...] = jnp.full((1, NQ), T, I32)

        @pl.when(jnp.max(excess) > 0.0)
        def _():
            jb_ref[...] = _tie_bound(
                lambda cand: count(lambda blk, kpos: jnp.where(blk == thr, kpos, T) < cand),
                (1, NQ), need, nbits)

        jb = jb_ref[...]
        for c in range(nchunk):
            if c < nvis:
                blk = st_ref[c * KC:(c + 1) * KC, :]
                kpos = c * KC + row
                rank_pos = jnp.where(blk == thr, kpos, jnp.where(blk > thr, -1, T + 1))
                sel = jnp.where(kpos <= qpos, rank_pos, T + 1) <= jb
                m_t = jnp.where(sel, 0.0, NEG)
                for t in range(A1_TILES):
                    for s4 in range(KC // QB):
                        mask_ref[t, c, :, s4 * QB:(s4 + 1) * QB] = (
                            m_t[s4 * QB:(s4 + 1) * QB, t * QB:(t + 1) * QB].T.astype(mask_ref.dtype))
            else:
                for t in range(A1_TILES):
                    mask_ref[t, c] = jnp.full((QB, KC), NEG, mask_ref.dtype)

    for nvis in range(1, nchunk + 1):
        pl.when(nj == nvis)(functools.partial(select, nvis))


def _a1_prompt(kin_bf, proj, topk):
    B, T, _ = proj.shape
    nq, nc = T // QB, T // KC
    assert topk <= KC
    assert (KC // QB) % A1_TILES == 0 and nq % A1_TILES == 0
    nqs = A1_TILES * QB
    qi_blk = (N_HEADS * HEAD_DIM + 2 * KV_HEADS * HEAD_DIM) // (IDX_HEADS * IDX_DIM)
    tail_blk = (N_HEADS * HEAD_DIM + 2 * KV_HEADS * HEAD_DIM + IDX_HEADS * IDX_DIM) // LANES
    return pl.pallas_call(
        functools.partial(_a1p_kernel, topk=topk),
        grid=(B, nq // A1_TILES),
        in_specs=[pl.BlockSpec((None, T, IDX_DIM), lambda b, i: (b, 0, 0)),
                  pl.BlockSpec((None, nqs, IDX_HEADS * IDX_DIM), lambda b, i: (b, i, qi_blk)),
                  pl.BlockSpec((None, nqs, LANES), lambda b, i: (b, i, tail_blk))],
        out_specs=pl.BlockSpec((None, A1_TILES, nc, QB, KC), lambda b, i: (b, i, 0, 0, 0)),
        out_shape=jax.ShapeDtypeStruct((B, nq, nc, QB, KC), BF16),
        scratch_shapes=[pltpu.VMEM((T, nqs), I32), pltpu.VMEM((1, nqs), I32)],
        compiler_params=_params(("parallel", "arbitrary")),
        name="a1_prompt",
    )(kin_bf, proj, proj)


A2_TILES = 4


def _a2p_kernel(q_ref, k_ref, v_ref, mask_ref, tz_ref, o_ref):
    for t in range(A2_TILES):
        _a2p_tile(pl.program_id(2) * A2_TILES + t, q_ref.at[t * QB:(t + 1) * QB], k_ref, v_ref,
                  mask_ref.at[t], tz_ref, o_ref.at[t * QB:(t + 1) * QB])


def _a2p_tile(i, q_ref, k_ref, v_ref, mask_ref, tz_ref, o_ref):
    nj = (i * QB) // A2_KC + 1
    HP = GROUP
    R = HP * QB
    heads = [list(range(c * HP, (c + 1) * HP)) for c in range(GROUP // HP)]
    qs = [(jnp.concatenate([q_ref[:, r * HEAD_DIM:(r + 1) * HEAD_DIM] for r in hs], axis=0)
           * QSCALE).astype(BF16) for hs in heads]

    def body(j, carry):
        k0 = pl.multiple_of(j * A2_KC, A2_KC)
        kt = k_ref[pl.ds(k0, A2_KC), :].astype(BF16)
        vt = v_ref[pl.ds(k0, A2_KC), :].astype(BF16)
        base = i * QB - j * A2_KC + TZ_OFF
        mk = jnp.concatenate([mask_ref[j * (A2_KC // KC) + c] for c in range(A2_KC // KC)],
                             axis=1).astype(F32)
        mk = jnp.concatenate([mk] * HP, axis=0)
        out = []
        for hs, q, (m, l, acc) in zip(heads, qs, carry):
            bias = jnp.concatenate(
                [jnp.concatenate(
                    [tz_ref[r, pl.ds(pl.multiple_of(base - QB * s4, QB), QB), :]
                     for s4 in range(A2_KC // QB)], axis=1) for r in hs], axis=0)
            s = _nt(q, kt) + bias + mk
            m_new = jnp.maximum(m, jnp.max(s, axis=1, keepdims=True))
            alpha = jnp.exp2(m - m_new)
            p = jnp.exp2(s - m_new)
            l = alpha * l + jnp.sum(p, axis=1, keepdims=True)
            acc = alpha * acc + jnp.dot(p.astype(BF16), vt, preferred_element_type=F32)
            out.append((m_new, l, acc))
        return tuple(out)

    init = tuple((jnp.full((R, 1), NEG, F32), jnp.zeros((R, 1), F32), jnp.zeros((R, HEAD_DIM), F32))
                 for _ in heads)
    final = lax.fori_loop(0, nj, body, init)
    for hs, (m, l, acc) in zip(heads, final):
        res = acc / l
        for n, r in enumerate(hs):
            o_ref[:, r * HEAD_DIM:(r + 1) * HEAD_DIM] = res[n * QB:(n + 1) * QB, :].astype(o_ref.dtype)


def _a2_prompt(proj, mask, tz):
    B, T, _ = proj.shape
    nq, nc = T // QB, T // KC
    gw = GROUP * HEAD_DIM
    k_blk0 = N_HEADS * HEAD_DIM // HEAD_DIM
    v_blk0 = k_blk0 + KV_HEADS
    U = tz.shape[1]
    assert nq % A2_TILES == 0
    qr = A2_TILES * QB
    return pl.pallas_call(
        _a2p_kernel,
        grid=(KV_HEADS, B, nq // A2_TILES),
        in_specs=[pl.BlockSpec((None, qr, gw), lambda g, b, i: (b, i, g)),
                  pl.BlockSpec((None, T, HEAD_DIM), lambda g, b, i: (b, 0, k_blk0 + g)),
                  pl.BlockSpec((None, T, HEAD_DIM), lambda g, b, i: (b, 0, v_blk0 + g)),
                  pl.BlockSpec((None, A2_TILES, nc, QB, KC), lambda g, b, i: (b, i, 0, 0, 0)),
                  pl.BlockSpec((GROUP, U, LANES), lambda g, b, i: (g, 0, 0))],
        out_specs=pl.BlockSpec((None, qr, gw), lambda g, b, i: (b, i, g)),
        out_shape=jax.ShapeDtypeStruct((B, T, N_HEADS * HEAD_DIM), BF16),
        compiler_params=_params(("parallel", "parallel", "arbitrary")),
        name="a2_prompt",
    )(proj, proj, proj, mask, tz)


PAGES_PER_STEP = 32


A1S_PAGES = 128
A1S_PART = 16


def _a1s_kernel(pt_ref, qi_ref, tail_ref, knew_ref, *rest, topk, past):
    page_refs = rest[:A1S_PAGES]
    mask_ref, st_ref, jb_ref = rest[A1S_PAGES:]
    s = pl.program_id(1)
    nt, TS, _ = st_ref.shape
    L = nt * LANES
    nbits = (L - 1).bit_length()
    qi = jnp.concatenate([qi_ref[:, h * IDX_DIM:(h + 1) * IDX_DIM] for h in range(IDX_HEADS)],
                         axis=0).astype(BF16)
    w_col = jnp.concatenate([tail_ref[:, IDX_DIM + h:IDX_DIM + h + 1] for h in range(IDX_HEADS)],
                            axis=0) * (IDX_HEADS ** -0.5 * IDX_DIM ** -0.5)

    def scores(qk):
        sc = jnp.maximum(qk, 0.0) * w_col
        return jnp.sum(sc.reshape(IDX_HEADS, TS, qk.shape[1]), axis=0)

    for part in range(A1S_PAGES // A1S_PART):
        kt = jnp.concatenate([page_refs[part * A1S_PART + p][...] for p in range(A1S_PART)],
                             axis=1).astype(BF16)
        keys = _sortable(scores(jnp.dot(qi, kt, preferred_element_type=F32)))
        for p in range(A1S_PART):
            st_ref[s * A1S_PAGES + part * A1S_PART + p] = keys[:, p * LANES:(p + 1) * LANES]

    @pl.when(s == pl.num_programs(1) - 1)
    def _():
        t = lax.broadcasted_iota(I32, (TS, LANES), 0)
        c = lax.broadcasted_iota(I32, (TS, LANES), 1)
        sc = jnp.where(c <= t, scores(_nt(qi, knew_ref[...])), -jnp.inf)
        st_ref[nt - 1] = _sortable(sc)

        kpos = (lax.broadcasted_iota(I32, (nt, TS, LANES), 0) * LANES
                + lax.broadcasted_iota(I32, (nt, TS, LANES), 2))
        qpos = past + lax.broadcasted_iota(I32, (nt, TS, LANES), 1)

        def count(pred):
            per_lane = jnp.sum(jnp.where(pred(st_ref[...]), 1.0, 0.0), axis=0)
            return jnp.sum(per_lane, axis=1, keepdims=True)

        thr = _kth_largest(lambda cand: count(lambda k: k >= cand[None]), (TS, 1), float(topk))
        thr3 = thr[None]
        need = float(topk) - count(lambda k: k > thr3)
        n_eq = count(lambda k: k == thr3)
        excess = jnp.where((n_eq > need) & (thr != KEY_NEG_INF), 1.0, 0.0)
        jb_ref[...] = jnp.full((TS, 1), L, I32)

        @pl.when(jnp.max(excess) > 0.0)
        def _():
            jb_ref[...] = _tie_bound(
                lambda cand: count(lambda k: jnp.where(k == thr3, kpos, L) < cand[None]),
                (TS, 1), need, nbits)

        keys = st_ref[...]
        rank_pos = jnp.where(keys == thr3, kpos, jnp.where(keys > thr3, -1, L + 1))
        sel = jnp.where(kpos <= qpos, rank_pos, L + 1) <= jb_ref[...][None]
        mask_ref[...] = jnp.where(sel, 0.0, NEG)


def _a1_sample(page_table, proj_s, knew_bf, cache_kidx, layer, topk):
    DB, TS, _ = proj_s.shape
    n_pages = page_table.shape[1]
    page = cache_kidx.shape[3]
    assert page == LANES and n_pages % A1S_PAGES == 0 and topk <= n_pages * page
    nt = n_pages + 1
    qi_blk = (N_HEADS * HEAD_DIM + 2 * KV_HEADS * HEAD_DIM) // (IDX_HEADS * IDX_DIM)
    tail_blk = (N_HEADS * HEAD_DIM + 2 * KV_HEADS * HEAD_DIM + IDX_HEADS * IDX_DIM) // LANES

    def page_spec(p):
        return pl.BlockSpec((None, None, IDX_DIM, page),
                            lambda b, s, pt: (layer, pt[b, s * A1S_PAGES + p], 0, 0))

    grid_spec = pltpu.PrefetchScalarGridSpec(
        num_scalar_prefetch=1,
        grid=(DB, n_pages // A1S_PAGES),
        in_specs=[pl.BlockSpec((None, TS, IDX_HEADS * IDX_DIM), lambda b, s, pt: (b, 0, qi_blk)),
                  pl.BlockSpec((None, TS, LANES), lambda b, s, pt: (b, 0, tail_blk)),
                  pl.BlockSpec((None, LANES, IDX_DIM), lambda b, s, pt: (b, 0, 0))]
                 + [page_spec(p) for p in range(A1S_PAGES)],
        out_specs=pl.BlockSpec((None, nt, TS, LANES), lambda b, s, pt: (b, 0, 0, 0)),
        scratch_shapes=[pltpu.VMEM((nt, TS, LANES), I32), pltpu.VMEM((TS, 1), I32)])
    return pl.pallas_call(
        functools.partial(_a1s_kernel, topk=topk, past=n_pages * page),
        grid_spec=grid_spec,
        out_shape=jax.ShapeDtypeStruct((DB, nt, TS, LANES), F32),
        compiler_params=_params(("parallel", "arbitrary")),
        name="a1_sample",
    )(page_table, proj_s, proj_s, knew_bf, *([cache_kidx] * A1S_PAGES))


def _a2s_kernel(pt_ref, q_ref, mask_ref, tzs_ref, knew_ref, vnew_ref, *rest, past):
    k_pages = rest[:PAGES_PER_STEP]
    v_pages = rest[PAGES_PER_STEP:2 * PAGES_PER_STEP]
    o_ref, m_ref, l_ref, acc_ref = rest[2 * PAGES_PER_STEP:]
    s = pl.program_id(1)
    TS = q_ref.shape[0]
    n_slab = tzs_ref.shape[1]

    @pl.when(s == 0)
    def _():
        m_ref[...] = jnp.full(m_ref.shape, NEG, F32)
        l_ref[...] = jnp.zeros(l_ref.shape, F32)
        acc_ref[...] = jnp.zeros(acc_ref.shape, F32)

    qg = [(jnp.concatenate([q_ref[:, (g * GROUP + r) * HEAD_DIM:(g * GROUP + r + 1) * HEAD_DIM]
                            for r in range(GROUP)], axis=0) * QSCALE).astype(BF16)
          for g in range(KV_HEADS)]

    def update(g, kt, vt, tiles):
        slabs = [jnp.minimum((past // LANES) - t, n_slab - 1) for t in tiles]
        mk = jnp.concatenate([mask_ref[t] for t in tiles], axis=1)
        mk = jnp.concatenate([mk] * GROUP, axis=0)
        bias = jnp.concatenate(
            [jnp.concatenate([tzs_ref[g * GROUP + r, sl] for r in range(GROUP)], axis=0) for sl in slabs],
            axis=1)
        sc = _nt(qg[g], kt) + bias + mk
        m_old = m_ref[g]
        m_new = jnp.maximum(m_old, jnp.max(sc, axis=1, keepdims=True))
        alpha = jnp.exp2(m_old - m_new)
        p = jnp.exp2(sc - m_new)
        l_ref[g] = alpha * l_ref[g] + jnp.sum(p, axis=1, keepdims=True)
        acc_ref[g] = alpha * acc_ref[g] + jnp.dot(p.astype(BF16), vt, preferred_element_type=F32)
        m_ref[g] = m_new

    tiles = [s * PAGES_PER_STEP + p for p in range(PAGES_PER_STEP)]
    for g in range(KV_HEADS):
        rows = pl.ds(g, LANES, stride=KV_HEADS)
        kt = jnp.concatenate([kp[rows, :] for kp in k_pages], axis=0).astype(BF16)
        vt = jnp.concatenate([vp[rows, :] for vp in v_pages], axis=0).astype(BF16)
        update(g, kt, vt, tiles)

    @pl.when(s == pl.num_programs(1) - 1)
    def _():
        for g in range(KV_HEADS):
            update(g, knew_ref[:, g * HEAD_DIM:(g + 1) * HEAD_DIM].astype(BF16),
                   vnew_ref[:, g * HEAD_DIM:(g + 1) * HEAD_DIM].astype(BF16), [past // LANES])
        for g in range(KV_HEADS):
            res = acc_ref[g] / l_ref[g]
            for r in range(GROUP):
                h = g * GROUP + r
                o_ref[:, h * HEAD_DIM:(h + 1) * HEAD_DIM] = res[r * TS:(r + 1) * TS, :].astype(o_ref.dtype)


def _a2_sample(page_table, proj_s, mask_s, tzs, knew, vnew, cache_k, cache_v, layer):
    DB, TS, _ = proj_s.shape
    n_pages = page_table.shape[1]
    page = cache_k.shape[2] // KV_HEADS
    kvw = KV_HEADS * HEAD_DIM
    nt = n_pages + 1

    def page_spec(p):
        return pl.BlockSpec((None, None, page * KV_HEADS, HEAD_DIM),
                            lambda b, s, pt: (layer, pt[b, s * PAGES_PER_STEP + p], 0, 0))

    grid_spec = pltpu.PrefetchScalarGridSpec(
        num_scalar_prefetch=1,
        grid=(DB, n_pages // PAGES_PER_STEP),
        in_specs=[pl.BlockSpec((None, TS, N_HEADS * HEAD_DIM), lambda b, s, pt: (b, 0, 0)),
                  pl.BlockSpec((None, nt, TS, LANES), lambda b, s, pt: (b, 0, 0, 0)),
                  pl.BlockSpec(tzs.shape, lambda b, s, pt: (0, 0, 0, 0)),
                  pl.BlockSpec((None, LANES, kvw), lambda b, s, pt: (b, 0, 0)),
                  pl.BlockSpec((None, LANES, kvw), lambda b, s, pt: (b, 0, 0))]
                 + [page_spec(p) for p in range(PAGES_PER_STEP)] * 2,
        out_specs=pl.BlockSpec((None, TS, N_HEADS * HEAD_DIM), lambda b, s, pt: (b, 0, 0)),
        scratch_shapes=[pltpu.VMEM((KV_HEADS, GROUP * TS, 1), F32),
                        pltpu.VMEM((KV_HEADS, GROUP * TS, 1), F32),
                        pltpu.VMEM((KV_HEADS, GROUP * TS, HEAD_DIM), F32)])
    return pl.pallas_call(
        functools.partial(_a2s_kernel, past=n_pages * page),
        grid_spec=grid_spec,
        out_shape=jax.ShapeDtypeStruct((DB, TS, N_HEADS * HEAD_DIM), BF16),
        compiler_params=_params(("parallel", "arbitrary")),
        name="a2_sample",
    )(page_table, proj_s, mask_s, tzs, knew, vnew,
      *([cache_k] * PAGES_PER_STEP), *([cache_v] * PAGES_PER_STEP))


BQ = 1024


def _bp_kernel(*refs):
    n_q = len(B_GROUPS) * GROUP
    q_refs = refs[:n_q]
    k_ref, v_ref, band_ref, o_ref, m_ref, l_ref, acc_ref, og_ref, lseg_ref = refs[n_q:]
    step = pl.program_id(2)
    t0 = step * BQ
    first = t0 == 0
    whole = [n for n, (_, dil) in enumerate(B_GROUPS) if dil * QB > BQ]
    assert len(whole) == 1 and B_GROUPS[whole[0]][1] * QB == k_ref.shape[0]

    def band(n, half):
        return jnp.concatenate([band_ref[r, n, half] for r in range(GROUP)], axis=0)

    @pl.when(first)
    def _():
        n = whole[0]
        dil = B_GROUPS[n][1]
        for rho in range(dil):
            cls = pl.ds(rho, QB, stride=dil)
            qs = (jnp.concatenate([qr[cls, :] for qr in q_refs[n * GROUP:(n + 1) * GROUP]], axis=0)
                  * QSCALE).astype(BF16)
            s = _nt(qs, k_ref[cls, :].astype(BF16)) + band(n, 1)
            m = jnp.max(s, axis=1, keepdims=True)
            p = jnp.exp2(s - m)
            l = jnp.sum(p, axis=1, keepdims=True)
            out = jnp.dot(p.astype(BF16), v_ref[cls, :].astype(BF16), preferred_element_type=F32) / l
            lse = jnp.broadcast_to(m + jnp.log2(l), out.shape)
            for r in range(GROUP):
                og_ref[r, cls, :] = out[r * QB:(r + 1) * QB]
                lseg_ref[r, cls, :] = lse[r * QB:(r + 1) * QB]

    rows_step = pl.ds(pl.multiple_of(t0, BQ), BQ)
    for r in range(GROUP):
        m_ref[r] = lseg_ref[r, rows_step, :]
        l_ref[r] = jnp.ones(l_ref.shape[1:], F32)
        acc_ref[r] = og_ref[r, rows_step, :]

    def stacked(ref, rows):
        return jnp.concatenate([ref[r, rows, :] for r in range(GROUP)], axis=0)

    def update(q_ref, rows_q, nrow, key_rows, biases):
        qs = (jnp.concatenate([qr[rows_q, :] for qr in q_ref], axis=0) * QSCALE).astype(BF16)
        kt = jnp.concatenate([k_ref[kr, :] for kr in key_rows], axis=0).astype(BF16)
        vt = jnp.concatenate([v_ref[kr, :] for kr in key_rows], axis=0).astype(BF16)
        s = _nt(qs, kt) + jnp.concatenate(biases, axis=1)
        m_old = stacked(m_ref, rows_q)
        m_new = jnp.maximum(m_old, jnp.max(s, axis=1, keepdims=True))
        alpha = jnp.exp2(m_old - m_new)
        p = jnp.exp2(s - jnp.concatenate([m_new] * len(key_rows), axis=1))
        l_new = alpha * stacked(l_ref, rows_q) + jnp.sum(p, axis=1, keepdims=True)
        acc = alpha * stacked(acc_ref, rows_q) + jnp.dot(p.astype(BF16), vt, preferred_element_type=F32)
        for r in range(GROUP):
            m_ref[r, rows_q, :] = m_new[r * nrow:(r + 1) * nrow]
            l_ref[r, rows_q, :] = l_new[r * nrow:(r + 1) * nrow]
            acc_ref[r, rows_q, :] = acc[r * nrow:(r + 1) * nrow]

    for n, (window, dil) in enumerate(B_GROUPS):
        assert window // dil == QB
        span = dil * QB
        if n in whole:
            continue
        q_ref = q_refs[n * GROUP:(n + 1) * GROUP]
        for rho in range(dil):
            for mt in range(BQ // span):
                def rows(start):
                    return pl.ds(start, QB, stride=dil) if dil > 1 else pl.ds(start, QB)
                off = rho + span * mt
                prev = jnp.maximum(t0 + off - span, rho) if mt == 0 else t0 + off - span
                bias_a = band(n, 0)
                if mt == 0:
                    bias_a = bias_a + jnp.where(first, NEG, 0.0)
                update(q_ref, rows(off), QB, [rows(prev), rows(t0 + off)], [bias_a, band(n, 1)])

    for r in range(GROUP):
        o_ref[:, r * HEAD_DIM:(r + 1) * HEAD_DIM] = (acc_ref[r] / l_ref[r]).astype(o_ref.dtype)


def _b_prompt(qb, kv, band):
    B, T, _ = qb.shape
    nh = N_HEADS
    hd = HEAD_DIM
    gw = GROUP * hd
    assert T % BQ == 0

    def q_spec(n, r):
        whole =B_GROUPS[n][1] * QB > BQ
        if whole:
            return pl.BlockSpec((None, T, hd), lambda b, g, c: (b, 0, (n * KV_HEADS + g) * GROUP + r))
        return pl.BlockSpec((None, BQ, hd), lambda b, g, c: (b, c, (n * KV_HEADS + g) * GROUP + r))

    q_specs = [q_spec(n, r) for n in range(len(B_GROUPS)) for r in range(GROUP)]
    return pl.pallas_call(
        _bp_kernel,
        grid=(B, KV_HEADS, T // BQ),
        in_specs=q_specs + [
                  pl.BlockSpec((None, T, hd), lambda b, g, c: (b, 0, g)),
                  pl.BlockSpec((None, T, hd), lambda b, g, c: (b, 0, KV_HEADS + g)),
                  pl.BlockSpec((GROUP, 3, 2, QB, LANES), lambda b, g, c: (g, 0, 0, 0, 0))],
        out_specs=pl.BlockSpec((None, BQ, gw), lambda b, g, c: (b, c, g)),
        out_shape=jax.ShapeDtypeStruct((B, T, nh * hd), BF16),
        scratch_shapes=[pltpu.VMEM((GROUP, BQ, LANES), F32), pltpu.VMEM((GROUP, BQ, LANES), F32),
                        pltpu.VMEM((GROUP, BQ, hd), F32),
                        pltpu.VMEM((GROUP, T, hd), F32), pltpu.VMEM((GROUP, T, LANES), F32)],
        compiler_params=_params(("parallel", "parallel", "arbitrary")),
        name="b_prompt",
    )(*([qb] * len(q_specs)), kv, kv, band)


def _bs_kernel(q0_ref, q1_ref, q2_ref, kc_ref, vc_ref, kn_ref, vn_ref, tab_ref, o_ref):
    TS = q0_ref.shape[0]
    W = kc_ref.shape[0] // KV_HEADS
    rows = pl.ds(pl.program_id(1), W, stride=KV_HEADS)
    kc = kc_ref[rows, :].astype(BF16)
    vc = vc_ref[rows, :].astype(BF16)
    kn = kn_ref[...].astype(BF16)
    vn = vn_ref[...].astype(BF16)
    ng = len(B_GROUPS)
    R = GROUP * TS
    q = (jnp.concatenate([q_ref[:, r * HEAD_DIM:(r + 1) * HEAD_DIM]
                          for q_ref in (q0_ref, q1_ref, q2_ref) for r in range(GROUP)], axis=0)
         * QSCALE).astype(BF16)
    tab = jnp.concatenate([tab_ref[n] for n in range(ng)], axis=0)
    sc_c = _nt(q, kc) + tab[:, :W]
    sc_n = _nt(q, kn) + tab[:, W:]
    m_rows = jnp.maximum(jnp.max(sc_c, axis=1, keepdims=True), jnp.max(sc_n, axis=1, keepdims=True))
    m = functools.reduce(jnp.maximum, [m_rows[n * R:(n + 1) * R] for n in range(ng)])
    m = jnp.concatenate([m] * ng, axis=0)
    pc = jnp.exp2(sc_c - m)
    pn = jnp.exp2(sc_n - m)
    l_rows = jnp.sum(pc, axis=1, keepdims=True) + jnp.sum(pn, axis=1, keepdims=True)
    acc_rows = (jnp.dot(pc.astype(BF16), vc, preferred_element_type=F32)
                + jnp.dot(pn.astype(BF16), vn, preferred_element_type=F32))
    l = sum(l_rows[n * R:(n + 1) * R] for n in range(ng))
    acc = sum(acc_rows[n * R:(n + 1) * R] for n in range(ng))
    res = acc / l
    for r in range(GROUP):
        o_ref[:, r * HEAD_DIM:(r + 1) * HEAD_DIM] = res[r * TS:(r + 1) * TS, :].astype(o_ref.dtype)


def _b_sample(qb_s, cache_k, cache_v, kv_new, tab):
    DB, TS, NQ = qb_s.shape
    W = cache_k.shape[1] // KV_HEADS
    hd = HEAD_DIM
    return pl.pallas_call(
        _bs_kernel,
        grid=(DB, KV_HEADS),
        in_specs=[pl.BlockSpec((None, TS, GROUP * hd), lambda b, g: (b, 0, g)),
                  pl.BlockSpec((None, TS, GROUP * hd), lambda b, g: (b, 0, KV_HEADS + g)),
                  pl.BlockSpec((None, TS, GROUP * hd), lambda b, g: (b, 0, 2 * KV_HEADS + g)),
                  pl.BlockSpec((None, W * KV_HEADS, hd), lambda b, g: (b, 0, 0)),
                  pl.BlockSpec((None, W * KV_HEADS, hd), lambda b, g: (b, 0, 0)),
                  pl.BlockSpec((None, LANES, hd), lambda b, g: (b, 0, g)),
                  pl.BlockSpec((None, LANES, hd), lambda b, g: (b, 0, KV_HEADS + g)),
                  pl.BlockSpec((len(B_GROUPS), None, GROUP * TS, W + LANES), lambda b, g: (0, g, 0, 0))],
        out_specs=pl.BlockSpec((None, TS, GROUP * hd), lambda b, g: (b, 0, g)),
        out_shape=jax.ShapeDtypeStruct((DB, TS, N_HEADS * hd), BF16),
        compiler_params=_params(("parallel", "arbitrary")),
        name="b_sample",
    )(qb_s, qb_s, qb_s, cache_k, cache_v, kv_new, kv_new, tab)


FFN_TF = 512
HALO = 16


def _ffn_up_kernel(xm_ref, xh_ref, wg_ref, wu_ref, cw_ref, cb_ref, act_ref, st_ref, xe_ref, *,
                   tiles_per_batch):
    i = pl.program_id(0)
    tm = xm_ref.shape[0]

    @pl.when(pl.program_id(1) == 0)
    def _():
        first = (i % tiles_per_batch) == 0
        xh = xh_ref[...]
        xe_ref[0:HALO, :] = jnp.where(first, jnp.zeros_like(xh), xh)
        xe_ref[HALO:, :] = xm_ref[...]

    gate = jnp.dot(xe_ref[...], wg_ref[...], preferred_element_type=F32)
    up = jnp.dot(xm_ref[...], wu_ref[...], preferred_element_type=F32)
    cw = cw_ref[...]
    conv = cb_ref[...] + ((gate[HALO - 2:HALO - 2 + tm] * cw[0:1] + gate[HALO - 1:HALO - 1 + tm] * cw[1:2])
                          + gate[HALO:] * cw[2:3])
    act_ref[...] = (jax.nn.silu(conv) * up).astype(act_ref.dtype)
    st_ref[...] = gate[HALO + tm - (CONV_WIDTH - 1):, :]


def _ffn_up(x_bf, w_g_bf, w_u_bf, conv_w, conv_b, layer, T):
    M, D = x_bf.shape
    F = conv_w.shape[2]
    tm = 1024
    assert T % tm == 0 and M % T == 0 and F % FFN_TF == 0 and tm % HALO == 0
    nf = F // FFN_TF
    hb = tm // HALO
    return pl.pallas_call(
        functools.partial(_ffn_up_kernel, tiles_per_batch=T // tm),
        grid=(M // tm, nf),
        in_specs=[pl.BlockSpec((tm, D), lambda i, j: (i, 0)),
                  pl.BlockSpec((HALO, D), lambda i, j: (jnp.maximum(i * hb - 1, 0), 0)),
                  pl.BlockSpec((None, D, FFN_TF), lambda i, j: (layer, 0, j)),
                  pl.BlockSpec((None, D, FFN_TF), lambda i, j: (layer, 0, j)),
                  pl.BlockSpec((None, CONV_WIDTH, FFN_TF), lambda i, j: (layer, 0, j)),
                  pl.BlockSpec((None, 1, FFN_TF), lambda i, j: (layer, 0, j))],
        out_specs=[pl.BlockSpec((tm, FFN_TF), lambda i, j: (i, j)),
                   pl.BlockSpec((None, CONV_WIDTH - 1, FFN_TF), lambda i, j: (i, 0, j))],
        out_shape=[jax.ShapeDtypeStruct((M, F), BF16),
                   jax.ShapeDtypeStruct((M // tm, CONV_WIDTH - 1, F), F32)],
        scratch_shapes=[pltpu.VMEM((HALO + tm, D), BF16)],
        compiler_params=_params(("parallel", "arbitrary")),
        name="ffn_up",
    )(x_bf, x_bf, w_g_bf, w_u_bf, conv_w, conv_b)


def _mm_ln_kernel(a_ref, w_ref, x_ref, g_ref, b_ref, h_ref, hb_ref, *, alpha):
    k = pl.program_id(1)
    part = jnp.dot(a_ref[...], w_ref[...], preferred_element_type=F32)

    @pl.when(k == 0)
    def _():
        h_ref[...] = part

    @pl.when(k > 0)
    def _():
        h_ref[...] += part

    @pl.when(k == pl.num_programs(1) - 1)
    def _():
        y = alpha * x_ref[...] + h_ref[...]
        mu = jnp.mean(y, axis=-1, keepdims=True)
        d = y - mu
        var = jnp.mean(d * d, axis=-1, keepdims=True)
        h = d * lax.rsqrt(var + LN_EPS) * g_ref[...] + b_ref[...]
        h_ref[...] = h
        hb_ref[...] = h.astype(hb_ref.dtype)


def _mm_ln(a_bf, w_bf, layer, x, g, b, alpha):
    M, K = a_bf.shape
    D = w_bf.shape[2]
    tm = min(512, M)
    tk = K // 2 if K > 2048 else K
    assert M % tm == 0 and K % tk == 0 and tk % LANES == 0
    return pl.pallas_call(
        functools.partial(_mm_ln_kernel, alpha=alpha),
        grid=(M // tm, K // tk),
        in_specs=[pl.BlockSpec((tm, tk), lambda i, k: (i, k)),
                  pl.BlockSpec((None, tk, D), lambda i, k: (layer, k, 0)),
                  pl.BlockSpec((tm, D), lambda i, k: (i, 0)),
                  pl.BlockSpec((1, D), lambda i, k: (0, 0)),
                  pl.BlockSpec((1, D), lambda i, k: (0, 0))],
        out_specs=[pl.BlockSpec((tm, D), lambda i, k: (i, 0)),
                   pl.BlockSpec((tm, D), lambda i, k: (i, 0))],
        out_shape=[jax.ShapeDtypeStruct((M, D), F32), jax.ShapeDtypeStruct((M, D), BF16)],
        compiler_params=_params(("parallel", "arbitrary")),
        name="mm_ln",
    )(a_bf, w_bf, x, g.reshape(1, D), b.reshape(1, D))


def _layer_norm(x, g, b):
    mu = x.mean(-1, keepdims=True)
    var = jnp.square(x - mu).mean(-1, keepdims=True)
    return (x - mu) * lax.rsqrt(var + LN_EPS) * g + b


def _pick_tm(M):
    for tm in (1024, 512, 256, 128, 64, 32, 16):
        if M % tm == 0:
            return tm
    raise ValueError(M)


def _pick_tn(N):
    for tn in (1536, 1024, 512, 384, 256, 128):
        if N % tn == 0:
            return tn
    raise ValueError(N)


def _proj(x_bf, w_bf, layer=0):
    return _mm(x_bf, w_bf, layer, _pick_tm(x_bf.shape[0]), _pick_tn(w_bf.shape[2]))


def _ffn_act_sample(hs_bf, state, w_g_bf, w_u_bf, conv_w, conv_b, layer, DB, TS):
    conv_w, conv_b = conv_w[layer], conv_b[layer]
    F = conv_w.shape[1]
    gate = _proj(hs_bf, w_g_bf, layer).reshape(DB, TS, F)
    up = _proj(hs_bf, w_u_bf, layer).reshape(DB, TS, F)
    ext = jnp.concatenate([state, gate], axis=1)
    conv = conv_b + sum(ext[:, j:j + TS] * conv_w[j] for j in range(CONV_WIDTH))
    act = (jax.nn.silu(conv) * up).astype(BF16).reshape(DB * TS, F)
    return act, ext[:, ext.shape[1] - (CONV_WIDTH - 1):]


def _table_indices(T, W, TS):
    U = SAT_DIST + TZ_OFF + QB
    assert T <= T5_MAX_DISTANCE
    u = np.arange(U)[:, None]
    c = np.arange(LANES)[None, :]
    tz_idx = _t5_bucket(jnp.asarray(np.maximum(u - c - TZ_OFF, 0), I32))
    a = np.arange(QB)[:, None]
    band = []
    for window, dil in B_GROUPS:
        for off in (QB, 0):
            d = a + off - c
            ok = (d >= 0) & (d <= window // dil)
            band.append(jnp.where(jnp.asarray(ok), _t5_bucket(jnp.asarray(np.maximum(d, 0) * dil, I32)),
                                  NUM_BUCKETS))
    band_idx = jnp.concatenate(band, axis=0)
    nkt = W // LANES + 1
    t = np.arange(TS)[None, :, None]
    key = (np.arange(nkt)[:, None, None] * LANES + np.arange(LANES)[None, None, :])
    d = W + t - key
    tabs = []
    for window, dil in B_GROUPS:
        ok = (d >= 0) & (d % dil == 0) & (d <= window)
        tabs.append(jnp.where(jnp.asarray(ok), _t5_bucket(jnp.asarray(np.maximum(d, 0), I32)), NUM_BUCKETS))
    samp_idx = jnp.concatenate(tabs, axis=0).reshape(-1, LANES)
    rows = [tz_idx, band_idx, samp_idx]
    total = sum(r.shape[0] for r in rows)
    pad = (-total) % 128
    if pad:
        rows.append(jnp.full((pad, LANES), NUM_BUCKETS, I32))
    return jnp.concatenate(rows, axis=0), U, nkt


def kernel(x_prompt, x_sample, cache_k_a, cache_v_a, cache_kidx_a, cache_k_b, cache_v_b, state_ffn, page_table, a_w_in, a_w_o, a_kn_g, a_kn_b, b_w_kv, b_w_q, b_w_o, ffn_w_up, ffn_conv_w, ffn_conv_b, ffn_w_down, ln_g, ln_b, rel_bias):
    B, T, D = x_prompt.shape
    DB, TS, _ = x_sample.shape
    depth = ffn_w_up.shape[0]
    n_a = a_w_in.shape[0]
    d_ff = ffn_w_down.shape[1]
    W = cache_k_b.shape[1]
    n_pages = page_table.shape[1]
    page = cache_k_a.shape[2]
    past = n_pages * page
    alpha = (2 * depth) ** 0.25
    kvw = KV_HEADS * HEAD_DIM
    a_q = N_HEADS * HEAD_DIM
    a_in = a_w_in.shape[2]
    np_cols = ((a_in + 511) // 512) * 512
    ki0 = a_q + 2 * kvw + IDX_HEADS * IDX_DIM

    idx_all, U, nkt = _table_indices(T, W, TS)
    tabs = _bias_tables(rel_bias * LOG2E, idx_all)
    tz = tabs[:, :U]
    band = tabs[:, U:U + 6 * QB].reshape(N_HEADS, 3, 2, QB, LANES)
    samp = tabs[:, U + 6 * QB:U + 6 * QB + 3 * nkt * TS].reshape(KV_HEADS, GROUP, 3, nkt, TS, LANES)
    samp = samp.transpose(2, 0, 1, 4, 3, 5).reshape(3, KV_HEADS, GROUP * TS, nkt * LANES)
    n_slab = SAT_DIST // LANES + 1
    tzs = tz[:, TZ_OFF:TZ_OFF + n_slab * LANES].reshape(N_HEADS, n_slab, LANES, LANES)[:, :, :TS, :]

    cache_k_a2 = cache_k_a.reshape(n_a, -1, page * KV_HEADS, HEAD_DIM)
    cache_v_a2 = cache_v_a.reshape(n_a, -1, page * KV_HEADS, HEAD_DIM)
    cache_k_b2 = cache_k_b.reshape(DB, W * KV_HEADS, HEAD_DIM)
    cache_v_b2 = cache_v_b.reshape(DB, W * KV_HEADS, HEAD_DIM)
    cache_kidx_t = jnp.swapaxes(cache_kidx_a, 2, 3)

    def pad_rows(x, n):
        return jnp.pad(x, ((0, 0), (0, n - x.shape[1]), (0, 0)))

    f_pad = ((d_ff + FFN_TF - 1) // FFN_TF) * FFN_TF
    pad_f = f_pad - d_ff
    w_g_all = jnp.pad(ffn_w_up[:, :, :d_ff], ((0, 0), (0, 0), (0, pad_f))).astype(BF16)
    w_u_all = jnp.pad(ffn_w_up[:, :, d_ff:], ((0, 0), (0, 0), (0, pad_f))).astype(BF16)
    w_down_all = jnp.pad(ffn_w_down, ((0, 0), (0, pad_f), (0, 0))).astype(BF16)
    conv_w_all = jnp.pad(ffn_conv_w, ((0, 0), (0, 0), (0, pad_f)))
    conv_b_all = jnp.pad(ffn_conv_b, ((0, 0), (0, pad_f))).reshape(depth, 1, f_pad)
    w_in_all = jnp.pad(a_w_in, ((0, 0), (0, 0), (0, np_cols - a_in))).astype(BF16)
    w_o_a, w_o_b = a_w_o.astype(BF16), b_w_o.astype(BF16)
    w_q_all = b_w_q.astype(BF16)
    w_kv_bf = b_w_kv.astype(BF16)[None]

    hp = x_prompt.reshape(B * T, D)
    hs = x_sample.reshape(DB * TS, D)
    hp_bf, hs_bf = hp.astype(BF16), hs.astype(BF16)
    ka_p, va_p, kia_p, ka_s, va_s, kia_s, ffn_p, ffn_s = [], [], [], [], [], [], [], []
    for layer in range(depth):
        if layer < n_a:
            a = layer
            w_o_bf, w_o_layer = w_o_a, a
            proj = _proj(hp_bf, w_in_all, a).reshape(B, T, np_cols)
            k = proj[..., a_q:a_q + kvw].reshape(B, T, KV_HEADS, HEAD_DIM)
            v = proj[..., a_q + kvw:a_q + 2 * kvw].reshape(B, T, KV_HEADS, HEAD_DIM)
            ki = _layer_norm(proj[..., ki0:ki0 + IDX_DIM], a_kn_g[a], a_kn_b[a])
            ka_p.append(k); va_p.append(v); kia_p.append(ki)
            mask = _a1_prompt(ki.astype(BF16), proj, min(TOPK_MAX, T // 4))
            o = _a2_prompt(proj, mask, tz).reshape(B * T, a_q)
            proj_s = _proj(hs_bf, w_in_all, a).reshape(DB, TS, np_cols)
            k_s = proj_s[..., a_q:a_q + kvw]
            v_s = proj_s[..., a_q + kvw:a_q + 2 * kvw]
            ki_s = _layer_norm(proj_s[..., ki0:ki0 + IDX_DIM], a_kn_g[a], a_kn_b[a])
            ka_s.append(k_s.reshape(DB, TS, KV_HEADS, HEAD_DIM))
            va_s.append(v_s.reshape(DB, TS, KV_HEADS, HEAD_DIM))
            kia_s.append(ki_s)
            mask_s = _a1_sample(page_table, proj_s, pad_rows(ki_s, LANES).astype(BF16), cache_kidx_t, a,
                                min(TOPK_MAX, (past + TS) // 4))
            o_s = _a2_sample(page_table, proj_s, mask_s, tzs, pad_rows(k_s, LANES), pad_rows(v_s, LANES),
                             cache_k_a2, cache_v_a2, a).reshape(DB * TS, a_q)
        else:
            if layer == n_a:
                kv_p = _proj(hp_bf, w_kv_bf).reshape(B, T, 2 * kvw)
                kv_s = _proj(hs_bf, w_kv_bf).reshape(DB, TS, 2 * kvw)
                kv_s_pad = pad_rows(kv_s, LANES)
            bl = layer - n_a
            w_o_bf, w_o_layer = w_o_b, bl
            o = _b_prompt(_proj(hp_bf, w_q_all, bl).reshape(B, T, -1), kv_p, band).reshape(B * T, a_q)
            o_s = _b_sample(_proj(hs_bf, w_q_all, bl).reshape(DB, TS, -1), cache_k_b2, cache_v_b2, kv_s_pad,
                            samp).reshape(DB * TS, a_q)
        hp, hp_bf = _mm_ln(o, w_o_bf, w_o_layer, hp, ln_g[layer, 0], ln_b[layer, 0], alpha)
        hs, hs_bf = _mm_ln(o_s, w_o_bf, w_o_layer, hs, ln_g[layer, 0], ln_b[layer, 0], alpha)
        act, st = _ffn_up(hp_bf, w_g_all, w_u_all, conv_w_all, conv_b_all, layer, T)
        ffn_p.append(st.reshape(B, -1, CONV_WIDTH - 1, f_pad)[:, -1, :, :d_ff])
        act_s, st_s = _ffn_act_sample(hs_bf, jnp.pad(state_ffn[layer], ((0, 0), (0, 0), (0, pad_f))),
                                      w_g_all, w_u_all, conv_w_all, conv_b_all, layer, DB, TS)
        ffn_s.append(st_s[..., :d_ff])
        hp, hp_bf = _mm_ln(act, w_down_all, layer, hp, ln_g[layer, 1], ln_b[layer, 1], alpha)
        hs, hs_bf = _mm_ln(act_s, w_down_all, layer, hs, ln_g[layer, 1], ln_b[layer, 1], alpha)
    keep = min(max(w for w, _ in B_GROUPS), T)
    kb_p = kv_p[..., :kvw].reshape(B, T, KV_HEADS, HEAD_DIM)
    vb_p = kv_p[..., kvw:].reshape(B, T, KV_HEADS, HEAD_DIM)
    kb_s = kv_s[..., :kvw].reshape(DB, TS, KV_HEADS, HEAD_DIM)
    vb_s = kv_s[..., kvw:].reshape(DB, TS, KV_HEADS, HEAD_DIM)
    return (hp.reshape(B, T, D), hs.reshape(DB, TS, D), jnp.stack(ka_p), jnp.stack(va_p), jnp.stack(kia_p),
            jnp.stack(ka_s), jnp.stack(va_s), jnp.stack(kia_s), kb_p[:, T - keep:], vb_p[:, T - keep:],
            kb_s, vb_s, jnp.stack(ffn_p), jnp.stack(ffn_s))
```

```python
import functools
import math

import numpy as np
import jax
import jax.numpy as jnp
from jax import lax
from jax.experimental import pallas as pl
from jax.experimental.pallas import tpu as pltpu

F32 = jnp.float32
BF16 = jnp.bfloat16
I32 = jnp.int32

HEAD_DIM = 128
N_HEADS = 16
KV_HEADS = 4
GROUP = N_HEADS // KV_HEADS
IDX_HEADS = 16
IDX_DIM = 64
TOPK_MAX = 256
B_GROUPS = ((128, 1), (512, 4), (2048, 16))
NUM_BUCKETS = 32
T5_MAX_DISTANCE = 2048
LN_EPS = 1e-5
NEG = -1e30
CONV_WIDTH = 3

LANES = 128
QB = 128
KC = 512
A2_KC = KC
TZ_OFF = A2_KC - QB
SAT_DIST = T5_MAX_DISTANCE + QB
INT_MIN = -2 ** 31
KEY_NEG_INF = INT_MIN + 0x7FFFFF
VMEM_LIMIT = 56 * 1024 * 1024

LOG2E = 1.4426950408889634
QSCALE = HEAD_DIM ** -0.5 * LOG2E

NT_DIMS = (((1,), (1,)), ((), ()))


def _nt(a, b):
    return lax.dot_general(a, b, NT_DIMS, preferred_element_type=F32)


def _params(sem):
    return pltpu.CompilerParams(dimension_semantics=sem, vmem_limit_bytes=VMEM_LIMIT)


def _mm_kernel(x_ref, w_ref, o_ref):
    o_ref[...] = jnp.dot(x_ref[...], w_ref[...], preferred_element_type=F32).astype(o_ref.dtype)


def _mm(x, w, layer, tm, tn, out_dtype=F32):
    M, K = x.shape
    N = w.shape[2]
    assert M % tm == 0 and N % tn == 0, (M, N, tm, tn)
    return pl.pallas_call(
        _mm_kernel,
        grid=(M // tm, N // tn),
        in_specs=[pl.BlockSpec((tm, K), lambda i, j: (i, 0)),
                  pl.BlockSpec((None, K, tn), lambda i, j: (layer, 0, j))],
        out_specs=pl.BlockSpec((tm, tn), lambda i, j: (i, j)),
        out_shape=jax.ShapeDtypeStruct((M, N), out_dtype),
        compiler_params=_params(("parallel", "arbitrary")),
        name="mm",
    )(x, w)


def _t5_bucket(dist):
    dist = jnp.maximum(dist, 0)
    exact = NUM_BUCKETS // 2
    far = exact + (jnp.log(jnp.maximum(dist, 1).astype(F32) / exact)
                   / math.log(T5_MAX_DISTANCE / exact) * (NUM_BUCKETS - exact)).astype(I32)
    return jnp.where(dist < exact, dist, jnp.minimum(far, NUM_BUCKETS - 1))


def _table_kernel(rb_ref, idx_ref, o_ref):
    idx = idx_ref[...]
    for h in range(N_HEADS):
        acc = jnp.full(idx.shape, NEG, F32)
        for k in range(NUM_BUCKETS):
            acc = jnp.where(idx == k, rb_ref[k, h], acc)
        o_ref[h] = acc


def _bias_tables(rel_bias, idx):
    R = idx.shape[0]
    tr = 128
    assert R % tr == 0
    return pl.pallas_call(
        _table_kernel,
        grid=(R // tr,),
        in_specs=[pl.BlockSpec(memory_space=pltpu.SMEM),
                  pl.BlockSpec((tr, LANES), lambda i: (i, 0))],
        out_specs=pl.BlockSpec((N_HEADS, tr, LANES), lambda i: (0, i, 0)),
        out_shape=jax.ShapeDtypeStruct((N_HEADS, R, LANES), F32),
        compiler_params=_params(("arbitrary",)),
        name="bias_tables",
    )(rel_bias, idx)


def _sortable(x):
    bits = lax.bitcast_convert_type(x, I32)
    return bits ^ ((bits >> 31) & jnp.int32(0x7FFFFFFF))


def _kth_largest(count_ge, shape, k):
    def body(t, ans):
        cand_u = ans | (jnp.int32(1) << (31 - t))
        cnt = count_ge(cand_u ^ jnp.int32(INT_MIN))
        return jnp.where(cnt >= k, cand_u, ans)
    ans = lax.fori_loop(0, 32, body, jnp.zeros(shape, I32))
    return ans ^ jnp.int32(INT_MIN)


def _tie_bound(count_eq_below, shape, need, nbits):
    def body(t, ans):
        cand = ans | (jnp.int32(1) << (nbits - 1 - t))
        return jnp.where(count_eq_below(cand) < need, cand, ans)
    return lax.fori_loop(0, nbits, body, jnp.zeros(shape, I32))


A1_TILES = 4


def _a1p_kernel(kin_ref, qi_ref, tail_ref, mask_ref, st_ref, jb_ref, *, topk):
    i = pl.program_id(1)
    T = kin_ref.shape[0]
    NQ = A1_TILES * QB
    KS = KC // A1_TILES
    nchunk = T // KC
    nbits = (T - 1).bit_length()
    tail_t = jnp.concatenate([tail_ref[t * QB:(t + 1) * QB, :].T for t in range(A1_TILES)], axis=1)
    wi_t = tail_t[IDX_DIM:IDX_DIM + IDX_HEADS, :] * (IDX_HEADS ** -0.5 * IDX_DIM ** -0.5)
    qpos = i * NQ + lax.broadcasted_iota(I32, (KC, NQ), 1)
    row = lax.broadcasted_iota(I32, (KC, NQ), 0)
    qpos_s = i * NQ + lax.broadcasted_iota(I32, (KS, NQ), 1)
    row_s = lax.broadcasted_iota(I32, (KS, NQ), 0)

    nj = (i * NQ) // KC + 1

    for c in range(nchunk):
        @pl.when(c < nj)
        def _():
            for sub in range(KC // KS):
                r0 = c * KC + sub * KS
                kc = kin_ref[r0:r0 + KS, :]
                acc = jnp.zeros((KS, NQ), F32)
                for h in range(IDX_HEADS):
                    qh = qi_ref[:, h * IDX_DIM:(h + 1) * IDX_DIM].astype(BF16)
                    acc = acc + jnp.maximum(_nt(kc, qh), 0.0) * wi_t[h:h + 1, :]
                acc = jnp.where(r0 + row_s <= qpos_s, acc, -jnp.inf)
                st_ref[r0:r0 + KS, :] = _sortable(acc)

    def select(nvis):
        def count(pred):
            cnt = jnp.zeros((1, NQ), F32)
            for c in range(nvis):
                blk = st_ref[c * KC:(c + 1) * KC, :]
                cnt = cnt + jnp.sum(jnp.where(pred(blk, c * KC + row), 1.0, 0.0), axis=0, keepdims=True)
            return cnt

        thr = _kth_largest(lambda cand: count(lambda blk, _: blk >= cand), (1, NQ), float(topk))
        need = float(topk) - count(lambda blk, _: blk > thr)
        n_eq = count(lambda blk, _: blk == thr)
        excess = jnp.where((n_eq > need) & (thr != KEY_NEG_INF), 1.0, 0.0)
        jb_ref[...] = jnp.full((1, NQ), T, I32)

        @pl.when(jnp.max(excess) > 0.0)
        def _():
            jb_ref[...] = _tie_bound(
                lambda cand: count(lambda blk, kpos: jnp.where(blk == thr, kpos, T) < cand),
                (1, NQ), need, nbits)

        jb = jb_ref[...]
        for c in range(nchunk):
            if c < nvis:
                blk = st_ref[c * KC:(c + 1) * KC, :]
                kpos = c * KC + row
                rank_pos = jnp.where(blk == thr, kpos, jnp.where(blk > thr, -1, T + 1))
                sel = jnp.where(kpos <= qpos, rank_pos, T + 1) <= jb
                m_t = jnp.where(sel, 0.0, NEG)
                for t in range(A1_TILES):
                    for s4 in range(KC // QB):
                        mask_ref[t, c, :, s4 * QB:(s4 + 1) * QB] = (
                            m_t[s4 * QB:(s4 + 1) * QB, t * QB:(t + 1) * QB].T.astype(mask_ref.dtype))
            else:
                for t in range(A1_TILES):
                    mask_ref[t, c] = jnp.full((QB, KC), NEG, mask_ref.dtype)

    for nvis in range(1, nchunk + 1):
        pl.when(nj == nvis)(functools.partial(select, nvis))


def _a1_prompt(kin_bf, proj, topk):
    B, T, _ = proj.shape
    nq, nc = T // QB, T // KC
    assert topk <= KC
    assert (KC // QB) % A1_TILES == 0 and nq % A1_TILES == 0
    nqs = A1_TILES * QB
    qi_blk = (N_HEADS * HEAD_DIM + 2 * KV_HEADS * HEAD_DIM) // (IDX_HEADS * IDX_DIM)
    tail_blk = (N_HEADS * HEAD_DIM + 2 * KV_HEADS * HEAD_DIM + IDX_HEADS * IDX_DIM) // LANES
    return pl.pallas_call(
        functools.partial(_a1p_kernel, topk=topk),
        grid=(B, nq // A1_TILES),
        in_specs=[pl.BlockSpec((None, T, IDX_DIM), lambda b, i: (b, 0, 0)),
                  pl.BlockSpec((None, nqs, IDX_HEADS * IDX_DIM), lambda b, i: (b, i, qi_blk)),
                  pl.BlockSpec((None, nqs, LANES), lambda b, i: (b, i, tail_blk))],
        out_specs=pl.BlockSpec((None, A1_TILES, nc, QB, KC), lambda b, i: (b, i, 0, 0, 0)),
        out_shape=jax.ShapeDtypeStruct((B, nq, nc, QB, KC), BF16),
        scratch_shapes=[pltpu.VMEM((T, nqs), I32), pltpu.VMEM((1, nqs), I32)],
        compiler_params=_params(("parallel", "arbitrary")),
        name="a1_prompt",
    )(kin_bf, proj, proj)


A2_TILES = 8


def _a2p_kernel(q_ref, k_ref, v_ref, mask_ref, tz_ref, o_ref):
    for t in range(A2_TILES):
        _a2p_tile(pl.program_id(2) * A2_TILES + t, q_ref.at[t * QB:(t + 1) * QB], k_ref, v_ref,
                  mask_ref.at[t], tz_ref, o_ref.at[t * QB:(t + 1) * QB])


def _a2p_tile(i, q_ref, k_ref, v_ref, mask_ref, tz_ref, o_ref):
    nj = (i * QB) // A2_KC + 1
    HP = GROUP
    R = HP * QB
    heads = [list(range(c * HP, (c + 1) * HP)) for c in range(GROUP // HP)]
    qs = [(jnp.concatenate([q_ref[:, r * HEAD_DIM:(r + 1) * HEAD_DIM] for r in hs], axis=0)
           * QSCALE).astype(BF16) for hs in heads]

    def body(j, carry):
        k0 = pl.multiple_of(j * A2_KC, A2_KC)
        kt = k_ref[pl.ds(k0, A2_KC), :].astype(BF16)
        vt = v_ref[pl.ds(k0, A2_KC), :].astype(BF16)
        base = i * QB - j * A2_KC + TZ_OFF
        mk = jnp.concatenate([mask_ref[j * (A2_KC // KC) + c] for c in range(A2_KC // KC)],
                             axis=1).astype(F32)
        mk = jnp.concatenate([mk] * HP, axis=0)
        out = []
        for hs, q, (m, l, acc) in zip(heads, qs, carry):
            bias = jnp.concatenate(
                [jnp.concatenate(
                    [tz_ref[r, pl.ds(pl.multiple_of(base - QB * s4, QB), QB), :]
                     for s4 in range(A2_KC // QB)], axis=1) for r in hs], axis=0)
            s = _nt(q, kt) + bias + mk
            m_new = jnp.maximum(m, jnp.max(s, axis=1, keepdims=True))
            alpha = jnp.exp2(m - m_new)
            p = jnp.exp2(s - m_new)
            l = alpha * l + jnp.sum(p, axis=1, keepdims=True)
            acc = alpha * acc + jnp.dot(p.astype(BF16), vt, preferred_element_type=F32)
            out.append((m_new, l, acc))
        return tuple(out)

    init = tuple((jnp.full((R, 1), NEG, F32), jnp.zeros((R, 1), F32), jnp.zeros((R, HEAD_DIM), F32))
                 for _ in heads)
    final = lax.fori_loop(0, nj, body, init)
    for hs, (m, l, acc) in zip(heads, final):
        res = acc / l
        for n, r in enumerate(hs):
            o_ref[:, r * HEAD_DIM:(r + 1) * HEAD_DIM] = res[n * QB:(n + 1) * QB, :].astype(o_ref.dtype)


def _a2_prompt(proj, mask, tz):
    B, T, _ = proj.shape
    nq, nc = T // QB, T // KC
    gw = GROUP * HEAD_DIM
    k_blk0 = N_HEADS * HEAD_DIM // HEAD_DIM
    v_blk0 = k_blk0 + KV_HEADS
    U = tz.shape[1]
    assert nq % A2_TILES == 0
    qr = A2_TILES * QB
    return pl.pallas_call(
        _a2p_kernel,
        grid=(KV_HEADS, B, nq // A2_TILES),
        in_specs=[pl.BlockSpec((None, qr, gw), lambda g, b, i: (b, i, g)),
                  pl.BlockSpec((None, T, HEAD_DIM), lambda g, b, i: (b, 0, k_blk0 + g)),
                  pl.BlockSpec((None, T, HEAD_DIM), lambda g, b, i: (b, 0, v_blk0 + g)),
                  pl.BlockSpec((None, A2_TILES, nc, QB, KC), lambda g, b, i: (b, i, 0, 0, 0)),
                  pl.BlockSpec((GROUP, U, LANES), lambda g, b, i: (g, 0, 0))],
        out_specs=pl.BlockSpec((None, qr, gw), lambda g, b, i: (b, i, g)),
        out_shape=jax.ShapeDtypeStruct((B, T, N_HEADS * HEAD_DIM), BF16),
        compiler_params=_params(("parallel", "parallel", "arbitrary")),
        name="a2_prompt",
    )(proj, proj, proj, mask, tz)


PAGES_PER_STEP = 32


A1S_PAGES = 128
A1S_PART = 16


def _a1s_kernel(pt_ref, qi_ref, tail_ref, knew_ref, *rest, topk, past):
    page_refs = rest[:A1S_PAGES]
    mask_ref, st_ref, jb_ref = rest[A1S_PAGES:]
    s = pl.program_id(1)
    nt, TS, _ = st_ref.shape
    L = nt * LANES
    nbits = (L - 1).bit_length()
    qi = jnp.concatenate([qi_ref[:, h * IDX_DIM:(h + 1) * IDX_DIM] for h in range(IDX_HEADS)],
                         axis=0).astype(BF16)
    w_col = jnp.concatenate([tail_ref[:, IDX_DIM + h:IDX_DIM + h + 1] for h in range(IDX_HEADS)],
                            axis=0) * (IDX_HEADS ** -0.5 * IDX_DIM ** -0.5)

    def scores(qk):
        sc = jnp.maximum(qk, 0.0) * w_col
        return jnp.sum(sc.reshape(IDX_HEADS, TS, qk.shape[1]), axis=0)

    for part in range(A1S_PAGES // A1S_PART):
        kt = jnp.concatenate([page_refs[part * A1S_PART + p][...] for p in range(A1S_PART)],
                             axis=1).astype(BF16)
        keys = _sortable(scores(jnp.dot(qi, kt, preferred_element_type=F32)))
        for p in range(A1S_PART):
            st_ref[s * A1S_PAGES + part * A1S_PART + p] = keys[:, p * LANES:(p + 1) * LANES]

    @pl.when(s == pl.num_programs(1) - 1)
    def _():
        t = lax.broadcasted_iota(I32, (TS, LANES), 0)
        c = lax.broadcasted_iota(I32, (TS, LANES), 1)
        sc = jnp.where(c <= t, scores(_nt(qi, knew_ref[...])), -jnp.inf)
        st_ref[nt - 1] = _sortable(sc)

        kpos = (lax.broadcasted_iota(I32, (nt, TS, LANES), 0) * LANES
                + lax.broadcasted_iota(I32, (nt, TS, LANES), 2))
        qpos = past + lax.broadcasted_iota(I32, (nt, TS, LANES), 1)

        def count(pred):
            per_lane = jnp.sum(jnp.where(pred(st_ref[...]), 1.0, 0.0), axis=0)
            return jnp.sum(per_lane, axis=1, keepdims=True)

        thr = _kth_largest(lambda cand: count(lambda k: k >= cand[None]), (TS, 1), float(topk))
        thr3 = thr[None]
        need = float(topk) - count(lambda k: k > thr3)
        n_eq = count(lambda k: k == thr3)
        excess = jnp.where((n_eq > need) & (thr != KEY_NEG_INF), 1.0, 0.0)
        jb_ref[...] = jnp.full((TS, 1), L, I32)

        @pl.when(jnp.max(excess) > 0.0)
        def _():
            jb_ref[...] = _tie_bound(
                lambda cand: count(lambda k: jnp.where(k == thr3, kpos, L) < cand[None]),
                (TS, 1), need, nbits)

        keys = st_ref[...]
        rank_pos = jnp.where(keys == thr3, kpos, jnp.where(keys > thr3, -1, L + 1))
        sel = jnp.where(kpos <= qpos, rank_pos, L + 1) <= jb_ref[...][None]
        mask_ref[...] = jnp.where(sel, 0.0, NEG)


def _a1_sample(page_table, proj_s, knew_bf, cache_kidx, layer, topk):
    DB, TS, _ = proj_s.shape
    n_pages = page_table.shape[1]
    page = cache_kidx.shape[3]
    assert page == LANES and n_pages % A1S_PAGES == 0 and topk <= n_pages * page
    nt = n_pages + 1
    qi_blk = (N_HEADS * HEAD_DIM + 2 * KV_HEADS * HEAD_DIM) // (IDX_HEADS * IDX_DIM)
    tail_blk = (N_HEADS * HEAD_DIM + 2 * KV_HEADS * HEAD_DIM + IDX_HEADS * IDX_DIM) // LANES

    def page_spec(p):
        return pl.BlockSpec((None, None, IDX_DIM, page),
                            lambda b, s, pt: (layer, pt[b, s * A1S_PAGES + p], 0, 0))

    grid_spec = pltpu.PrefetchScalarGridSpec(
        num_scalar_prefetch=1,
        grid=(DB, n_pages // A1S_PAGES),
        in_specs=[pl.BlockSpec((None, TS, IDX_HEADS * IDX_DIM), lambda b, s, pt: (b, 0, qi_blk)),
                  pl.BlockSpec((None, TS, LANES), lambda b, s, pt: (b, 0, tail_blk)),
                  pl.BlockSpec((None, LANES, IDX_DIM), lambda b, s, pt: (b, 0, 0))]
                 + [page_spec(p) for p in range(A1S_PAGES)],
        out_specs=pl.BlockSpec((None, nt, TS, LANES), lambda b, s, pt: (b, 0, 0, 0)),
        scratch_shapes=[pltpu.VMEM((nt, TS, LANES), I32), pltpu.VMEM((TS, 1), I32)])
    return pl.pallas_call(
        functools.partial(_a1s_kernel, topk=topk, past=n_pages * page),
        grid_spec=grid_spec,
        out_shape=jax.ShapeDtypeStruct((DB, nt, TS, LANES), F32),
        compiler_params=_params(("parallel", "arbitrary")),
        name="a1_sample",
    )(page_table, proj_s, proj_s, knew_bf, *([cache_kidx] * A1S_PAGES))


def _a2s_kernel(pt_ref, q_ref, mask_ref, tzs_ref, knew_ref, vnew_ref, *rest, past):
    k_pages = rest[:PAGES_PER_STEP]
    v_pages = rest[PAGES_PER_STEP:2 * PAGES_PER_STEP]
    o_ref, m_ref, l_ref, acc_ref = rest[2 * PAGES_PER_STEP:]
    s = pl.program_id(1)
    TS = q_ref.shape[0]
    n_slab = tzs_ref.shape[1]

    @pl.when(s == 0)
    def _():
        m_ref[...] = jnp.full(m_ref.shape, NEG, F32)
        l_ref[...] = jnp.zeros(l_ref.shape, F32)
        acc_ref[...] = jnp.zeros(acc_ref.shape, F32)

    qg = [(jnp.concatenate([q_ref[:, (g * GROUP + r) * HEAD_DIM:(g * GROUP + r + 1) * HEAD_DIM]
                            for r in range(GROUP)], axis=0) * QSCALE).astype(BF16)
          for g in range(KV_HEADS)]

    def update(g, kt, vt, tiles):
        slabs = [jnp.minimum((past // LANES) - t, n_slab - 1) for t in tiles]
        mk = jnp.concatenate([mask_ref[t] for t in tiles], axis=1)
        mk = jnp.concatenate([mk] * GROUP, axis=0)
        bias = jnp.concatenate(
            [jnp.concatenate([tzs_ref[g * GROUP + r, sl] for r in range(GROUP)], axis=0) for sl in slabs],
            axis=1)
        sc = _nt(qg[g], kt) + bias + mk
        m_old = m_ref[g]
        m_new = jnp.maximum(m_old, jnp.max(sc, axis=1, keepdims=True))
        alpha = jnp.exp2(m_old - m_new)
        p = jnp.exp2(sc - m_new)
        l_ref[g] = alpha * l_ref[g] + jnp.sum(p, axis=1, keepdims=True)
        acc_ref[g] = alpha * acc_ref[g] + jnp.dot(p.astype(BF16), vt, preferred_element_type=F32)
        m_ref[g] = m_new

    tiles = [s * PAGES_PER_STEP + p for p in range(PAGES_PER_STEP)]
    for g in range(KV_HEADS):
        rows = pl.ds(g, LANES, stride=KV_HEADS)
        kt = jnp.concatenate([kp[rows, :] for kp in k_pages], axis=0).astype(BF16)
        vt = jnp.concatenate([vp[rows, :] for vp in v_pages], axis=0).astype(BF16)
        update(g, kt, vt, tiles)

    @pl.when(s == pl.num_programs(1) - 1)
    def _():
        for g in range(KV_HEADS):
            update(g, knew_ref[:, g * HEAD_DIM:(g + 1) * HEAD_DIM].astype(BF16),
                   vnew_ref[:, g * HEAD_DIM:(g + 1) * HEAD_DIM].astype(BF16), [past // LANES])
        for g in range(KV_HEADS):
            res = acc_ref[g] / l_ref[g]
            for r in range(GROUP):
                h = g * GROUP + r
                o_ref[:, h * HEAD_DIM:(h + 1) * HEAD_DIM] = res[r * TS:(r + 1) * TS, :].astype(o_ref.dtype)


def _a2_sample(page_table, proj_s, mask_s, tzs, knew, vnew, cache_k, cache_v, layer):
    DB, TS, _ = proj_s.shape
    n_pages = page_table.shape[1]
    page = cache_k.shape[2] // KV_HEADS
    kvw = KV_HEADS * HEAD_DIM
    nt = n_pages + 1

    def page_spec(p):
        return pl.BlockSpec((None, None, page * KV_HEADS, HEAD_DIM),
                            lambda b, s, pt: (layer, pt[b, s * PAGES_PER_STEP + p], 0, 0))

    grid_spec = pltpu.PrefetchScalarGridSpec(
        num_scalar_prefetch=1,
        grid=(DB, n_pages // PAGES_PER_STEP),
        in_specs=[pl.BlockSpec((None, TS, N_HEADS * HEAD_DIM), lambda b, s, pt: (b, 0, 0)),
                  pl.BlockSpec((None, nt, TS, LANES), lambda b, s, pt: (b, 0, 0, 0)),
                  pl.BlockSpec(tzs.shape, lambda b, s, pt: (0, 0, 0, 0)),
                  pl.BlockSpec((None, LANES, kvw), lambda b, s, pt: (b, 0, 0)),
                  pl.BlockSpec((None, LANES, kvw), lambda b, s, pt: (b, 0, 0))]
                 + [page_spec(p) for p in range(PAGES_PER_STEP)] * 2,
        out_specs=pl.BlockSpec((None, TS, N_HEADS * HEAD_DIM), lambda b, s, pt: (b, 0, 0)),
        scratch_shapes=[pltpu.VMEM((KV_HEADS, GROUP * TS, 1), F32),
                        pltpu.VMEM((KV_HEADS, GROUP * TS, 1), F32),
                        pltpu.VMEM((KV_HEADS, GROUP * TS, HEAD_DIM), F32)])
    return pl.pallas_call(
        functools.partial(_a2s_kernel, past=n_pages * page),
        grid_spec=grid_spec,
        out_shape=jax.ShapeDtypeStruct((DB, TS, N_HEADS * HEAD_DIM), BF16),
        compiler_params=_params(("parallel", "arbitrary")),
        name="a2_sample",
    )(page_table, proj_s, mask_s, tzs, knew, vnew,
      *([cache_k] * PAGES_PER_STEP), *([cache_v] * PAGES_PER_STEP))


BQ = 1024


def _bp_kernel(*refs):
    n_q = len(B_GROUPS) * GROUP
    q_refs = refs[:n_q]
    k_ref, v_ref, band_ref, o_ref, m_ref, l_ref, acc_ref, og_ref, lseg_ref = refs[n_q:]
    step = pl.program_id(2)
    t0 = step * BQ
    first = t0 == 0
    whole = [n for n, (_, dil) in enumerate(B_GROUPS) if dil * QB > BQ]
    assert len(whole) == 1 and B_GROUPS[whole[0]][1] * QB == k_ref.shape[0]

    def band(n, half):
        return jnp.concatenate([band_ref[r, n, half] for r in range(GROUP)], axis=0)

    @pl.when(first)
    def _():
        n = whole[0]
        dil = B_GROUPS[n][1]
        for rho in range(dil):
            cls = pl.ds(rho, QB, stride=dil)
            qs = (jnp.concatenate([qr[cls, :] for qr in q_refs[n * GROUP:(n + 1) * GROUP]], axis=0)
                  * QSCALE).astype(BF16)
            s = _nt(qs, k_ref[cls, :].astype(BF16)) + band(n, 1)
            m = jnp.max(s, axis=1, keepdims=True)
            p = jnp.exp2(s - m)
            l = jnp.sum(p, axis=1, keepdims=True)
            out = jnp.dot(p.astype(BF16), v_ref[cls, :].astype(BF16), preferred_element_type=F32) / l
            lse = jnp.broadcast_to(m + jnp.log2(l), out.shape)
            for r in range(GROUP):
                og_ref[r, cls, :] = out[r * QB:(r + 1) * QB]
                lseg_ref[r, cls, :] = lse[r * QB:(r + 1) * QB]

    rows_step = pl.ds(pl.multiple_of(t0, BQ), BQ)
    for r in range(GROUP):
        m_ref[r] = lseg_ref[r, rows_step, :]
        l_ref[r] = jnp.ones(l_ref.shape[1:], F32)
        acc_ref[r] = og_ref[r, rows_step, :]

    def stacked(ref, rows):
        return jnp.concatenate([ref[r, rows, :] for r in range(GROUP)], axis=0)

    def update(q_ref, rows_q, nrow, key_rows, biases):
        qs = (jnp.concatenate([qr[rows_q, :] for qr in q_ref], axis=0) * QSCALE).astype(BF16)
        kt = jnp.concatenate([k_ref[kr, :] for kr in key_rows], axis=0).astype(BF16)
        vt = jnp.concatenate([v_ref[kr, :] for kr in key_rows], axis=0).astype(BF16)
        s = _nt(qs, kt) + jnp.concatenate(biases, axis=1)
        m_old = stacked(m_ref, rows_q)
        m_new = jnp.maximum(m_old, jnp.max(s, axis=1, keepdims=True))
        alpha = jnp.exp2(m_old - m_new)
        p = jnp.exp2(s - jnp.concatenate([m_new] * len(key_rows), axis=1))
        l_new = alpha * stacked(l_ref, rows_q) + jnp.sum(p, axis=1, keepdims=True)
        acc = alpha * stacked(acc_ref, rows_q) + jnp.dot(p.astype(BF16), vt, preferred_element_type=F32)
        for r in range(GROUP):
            m_ref[r, rows_q, :] = m_new[r * nrow:(r + 1) * nrow]
            l_ref[r, rows_q, :] = l_new[r * nrow:(r + 1) * nrow]
            acc_ref[r, rows_q, :] = acc[r * nrow:(r + 1) * nrow]

    for n, (window, dil) in enumerate(B_GROUPS):
        assert window // dil == QB
        span = dil * QB
        if n in whole:
            continue
        q_ref = q_refs[n * GROUP:(n + 1) * GROUP]
        for rho in range(dil):
            for mt in range(BQ // span):
                def rows(start):
                    return pl.ds(start, QB, stride=dil) if dil > 1 else pl.ds(start, QB)
                off = rho + span * mt
                prev = jnp.maximum(t0 + off - span, rho) if mt == 0 else t0 + off - span
                bias_a = band(n, 0)
                if mt == 0:
                    bias_a = bias_a + jnp.where(first, NEG, 0.0)
                update(q_ref, rows(off), QB, [rows(prev), rows(t0 + off)], [bias_a, band(n, 1)])

    for r in range(GROUP):
        o_ref[:, r * HEAD_DIM:(r + 1) * HEAD_DIM] = (acc_ref[r] / l_ref[r]).astype(o_ref.dtype)


def _b_prompt(qb, kv, band):
    B, T, _ = qb.shape
    nh = N_HEADS
    hd = HEAD_DIM
    gw = GROUP * hd
    assert T % BQ == 0

    def q_spec(n, r):
        whole =B_GROUPS[n][1] * QB > BQ
        if whole:
            return pl.BlockSpec((None, T, hd), lambda b, g, c: (b, 0, (n * KV_HEADS + g) * GROUP + r))
        return pl.BlockSpec((None, BQ, hd), lambda b, g, c: (b, c, (n * KV_HEADS + g) * GROUP + r))

    q_specs = [q_spec(n, r) for n in range(len(B_GROUPS)) for r in range(GROUP)]
    return pl.pallas_call(
        _bp_kernel,
        grid=(B, KV_HEADS, T // BQ),
        in_specs=q_specs + [
                  pl.BlockSpec((None, T, hd), lambda b, g, c: (b, 0, g)),
                  pl.BlockSpec((None, T, hd), lambda b, g, c: (b, 0, KV_HEADS + g)),
                  pl.BlockSpec((GROUP, 3, 2, QB, LANES), lambda b, g, c: (g, 0, 0, 0, 0))],
        out_specs=pl.BlockSpec((None, BQ, gw), lambda b, g, c: (b, c, g)),
        out_shape=jax.ShapeDtypeStruct((B, T, nh * hd), BF16),
        scratch_shapes=[pltpu.VMEM((GROUP, BQ, LANES), F32), pltpu.VMEM((GROUP, BQ, LANES), F32),
                        pltpu.VMEM((GROUP, BQ, hd), F32),
                        pltpu.VMEM((GROUP, T, hd), F32), pltpu.VMEM((GROUP, T, LANES), F32)],
        compiler_params=_params(("parallel", "parallel", "arbitrary")),
        name="b_prompt",
    )(*([qb] * len(q_specs)), kv, kv, band)


def _bs_kernel(q0_ref, q1_ref, q2_ref, kc_ref, vc_ref, kn_ref, vn_ref, tab_ref, o_ref):
    TS = q0_ref.shape[0]
    W = kc_ref.shape[0] // KV_HEADS
    rows = pl.ds(pl.program_id(1), W, stride=KV_HEADS)
    kc = kc_ref[rows, :].astype(BF16)
    vc = vc_ref[rows, :].astype(BF16)
    kn = kn_ref[...].astype(BF16)
    vn = vn_ref[...].astype(BF16)
    ng = len(B_GROUPS)
    R = GROUP * TS
    q = (jnp.concatenate([q_ref[:, r * HEAD_DIM:(r + 1) * HEAD_DIM]
                          for q_ref in (q0_ref, q1_ref, q2_ref) for r in range(GROUP)], axis=0)
         * QSCALE).astype(BF16)
    tab = jnp.concatenate([tab_ref[n] for n in range(ng)], axis=0)
    sc_c = _nt(q, kc) + tab[:, :W]
    sc_n = _nt(q, kn) + tab[:, W:]
    m_rows = jnp.maximum(jnp.max(sc_c, axis=1, keepdims=True), jnp.max(sc_n, axis=1, keepdims=True))
    m = functools.reduce(jnp.maximum, [m_rows[n * R:(n + 1) * R] for n in range(ng)])
    m = jnp.concatenate([m] * ng, axis=0)
    pc = jnp.exp2(sc_c - m)
    pn = jnp.exp2(sc_n - m)
    l_rows = jnp.sum(pc, axis=1, keepdims=True) + jnp.sum(pn, axis=1, keepdims=True)
    acc_rows = (jnp.dot(pc.astype(BF16), vc, preferred_element_type=F32)
                + jnp.dot(pn.astype(BF16), vn, preferred_element_type=F32))
    l = sum(l_rows[n * R:(n + 1) * R] for n in range(ng))
    acc = sum(acc_rows[n * R:(n + 1) * R] for n in range(ng))
    res = acc / l
    for r in range(GROUP):
        o_ref[:, r * HEAD_DIM:(r + 1) * HEAD_DIM] = res[r * TS:(r + 1) * TS, :].astype(o_ref.dtype)


def _b_sample(qb_s, cache_k, cache_v, kv_new, tab):
    DB, TS, NQ = qb_s.shape
    W = cache_k.shape[1] // KV_HEADS
    hd = HEAD_DIM
    return pl.pallas_call(
        _bs_kernel,
        grid=(DB, KV_HEADS),
        in_specs=[pl.BlockSpec((None, TS, GROUP * hd), lambda b, g: (b, 0, g)),
                  pl.BlockSpec((None, TS, GROUP * hd), lambda b, g: (b, 0, KV_HEADS + g)),
                  pl.BlockSpec((None, TS, GROUP * hd), lambda b, g: (b, 0, 2 * KV_HEADS + g)),
                  pl.BlockSpec((None, W * KV_HEADS, hd), lambda b, g: (b, 0, 0)),
                  pl.BlockSpec((None, W * KV_HEADS, hd), lambda b, g: (b, 0, 0)),
                  pl.BlockSpec((None, LANES, hd), lambda b, g: (b, 0, g)),
                  pl.BlockSpec((None, LANES, hd), lambda b, g: (b, 0, KV_HEADS + g)),
                  pl.BlockSpec((len(B_GROUPS), None, GROUP * TS, W + LANES), lambda b, g: (0, g, 0, 0))],
        out_specs=pl.BlockSpec((None, TS, GROUP * hd), lambda b, g: (b, 0, g)),
        out_shape=jax.ShapeDtypeStruct((DB, TS, N_HEADS * hd), BF16),
        compiler_params=_params(("parallel", "arbitrary")),
        name="b_sample",
    )(qb_s, qb_s, qb_s, cache_k, cache_v, kv_new, kv_new, tab)


FFN_TF = 512
HALO = 16


def _ffn_up_kernel(xm_ref, xh_ref, wg_ref, wu_ref, cw_ref, cb_ref, act_ref, st_ref, xe_ref, *,
                   tiles_per_batch):
    i = pl.program_id(0)
    tm = xm_ref.shape[0]

    @pl.when(pl.program_id(1) == 0)
    def _():
        first = (i % tiles_per_batch) == 0
        xh = xh_ref[...]
        xe_ref[0:HALO, :] = jnp.where(first, jnp.zeros_like(xh), xh)
        xe_ref[HALO:, :] = xm_ref[...]

    gate = jnp.dot(xe_ref[...], wg_ref[...], preferred_element_type=F32)
    up = jnp.dot(xm_ref[...], wu_ref[...], preferred_element_type=F32)
    cw = cw_ref[...]
    conv = cb_ref[...] + ((gate[HALO - 2:HALO - 2 + tm] * cw[0:1] + gate[HALO - 1:HALO - 1 + tm] * cw[1:2])
                          + gate[HALO:] * cw[2:3])
    act_ref[...] = (jax.nn.silu(conv) * up).astype(act_ref.dtype)
    st_ref[...] = gate[HALO + tm - (CONV_WIDTH - 1):, :]


def _ffn_up(x_bf, w_g_bf, w_u_bf, conv_w, conv_b, layer, T):
    M, D = x_bf.shape
    F = conv_w.shape[2]
    tm = 1024
    assert T % tm == 0 and M % T == 0 and F % FFN_TF == 0 and tm % HALO == 0
    nf = F // FFN_TF
    hb = tm // HALO
    return pl.pallas_call(
        functools.partial(_ffn_up_kernel, tiles_per_batch=T // tm),
        grid=(M // tm, nf),
        in_specs=[pl.BlockSpec((tm, D), lambda i, j: (i, 0)),
                  pl.BlockSpec((HALO, D), lambda i, j: (jnp.maximum(i * hb - 1, 0), 0)),
                  pl.BlockSpec((None, D, FFN_TF), lambda i, j: (layer, 0, j)),
                  pl.BlockSpec((None, D, FFN_TF), lambda i, j: (layer, 0, j)),
                  pl.BlockSpec((None, CONV_WIDTH, FFN_TF), lambda i, j: (layer, 0, j)),
                  pl.BlockSpec((None, 1, FFN_TF), lambda i, j: (layer, 0, j))],
        out_specs=[pl.BlockSpec((tm, FFN_TF), lambda i, j: (i, j)),
                   pl.BlockSpec((None, CONV_WIDTH - 1, FFN_TF), lambda i, j: (i, 0, j))],
        out_shape=[jax.ShapeDtypeStruct((M, F), BF16),
                   jax.ShapeDtypeStruct((M // tm, CONV_WIDTH - 1, F), F32)],
        scratch_shapes=[pltpu.VMEM((HALO + tm, D), BF16)],
        compiler_params=_params(("parallel", "arbitrary")),
        name="ffn_up",
    )(x_bf, x_bf, w_g_bf, w_u_bf, conv_w, conv_b)


def _mm_ln_kernel(a_ref, w_ref, x_ref, g_ref, b_ref, h_ref, hb_ref, *, alpha):
    k = pl.program_id(1)
    part = jnp.dot(a_ref[...], w_ref[...], preferred_element_type=F32)

    @pl.when(k == 0)
    def _():
        h_ref[...] = part

    @pl.when(k > 0)
    def _():
        h_ref[...] += part

    @pl.when(k == pl.num_programs(1) - 1)
    def _():
        y = alpha * x_ref[...] + h_ref[...]
        mu = jnp.mean(y, axis=-1, keepdims=True)
        d = y - mu
        var = jnp.mean(d * d, axis=-1, keepdims=True)
        h = d * lax.rsqrt(var + LN_EPS) * g_ref[...] + b_ref[...]
        h_ref[...] = h
        hb_ref[...] = h.astype(hb_ref.dtype)


def _mm_ln(a_bf, w_bf, layer, x, g, b, alpha):
    M, K = a_bf.shape
    D = w_bf.shape[2]
    tm = min(512, M)
    tk = K // 2 if K > 2048 else K
    assert M % tm == 0 and K % tk == 0 and tk % LANES == 0
    return pl.pallas_call(
        functools.partial(_mm_ln_kernel, alpha=alpha),
        grid=(M // tm, K // tk),
        in_specs=[pl.BlockSpec((tm, tk), lambda i, k: (i, k)),
                  pl.BlockSpec((None, tk, D), lambda i, k: (layer, k, 0)),
                  pl.BlockSpec((tm, D), lambda i, k: (i, 0)),
                  pl.BlockSpec((1, D), lambda i, k: (0, 0)),
                  pl.BlockSpec((1, D), lambda i, k: (0, 0))],
        out_specs=[pl.BlockSpec((tm, D), lambda i, k: (i, 0)),
                   pl.BlockSpec((tm, D), lambda i, k: (i, 0))],
        out_shape=[jax.ShapeDtypeStruct((M, D), F32), jax.ShapeDtypeStruct((M, D), BF16)],
        compiler_params=_params(("parallel", "arbitrary")),
        name="mm_ln",
    )(a_bf, w_bf, x, g.reshape(1, D), b.reshape(1, D))


def _layer_norm(x, g, b):
    mu = x.mean(-1, keepdims=True)
    var = jnp.square(x - mu).mean(-1, keepdims=True)
    return (x - mu) * lax.rsqrt(var + LN_EPS) * g + b


def _pick_tm(M):
    for tm in (1024, 512, 256, 128, 64, 32, 16):
        if M % tm == 0:
            return tm
    raise ValueError(M)


def _pick_tn(N):
    for tn in (1536, 1024, 512, 384, 256, 128):
        if N % tn == 0:
            return tn
    raise ValueError(N)


def _proj(x_bf, w_bf, layer=0):
    return _mm(x_bf, w_bf, layer, _pick_tm(x_bf.shape[0]), _pick_tn(w_bf.shape[2]))


def _ffn_act_sample(hs_bf, state, w_g_bf, w_u_bf, conv_w, conv_b, layer, DB, TS):
    conv_w, conv_b = conv_w[layer], conv_b[layer]
    F = conv_w.shape[1]
    gate = _proj(hs_bf, w_g_bf, layer).reshape(DB, TS, F)
    up = _proj(hs_bf, w_u_bf, layer).reshape(DB, TS, F)
    ext = jnp.concatenate([state, gate], axis=1)
    conv = conv_b + sum(ext[:, j:j + TS] * conv_w[j] for j in range(CONV_WIDTH))
    act = (jax.nn.silu(conv) * up).astype(BF16).reshape(DB * TS, F)
    return act, ext[:, ext.shape[1] - (CONV_WIDTH - 1):]


def _table_indices(T, W, TS):
    U = SAT_DIST + TZ_OFF + QB
    assert T <= T5_MAX_DISTANCE
    u = np.arange(U)[:, None]
    c = np.arange(LANES)[None, :]
    tz_idx = _t5_bucket(jnp.asarray(np.maximum(u - c - TZ_OFF, 0), I32))
    a = np.arange(QB)[:, None]
    band = []
    for window, dil in B_GROUPS:
        for off in (QB, 0):
            d = a + off - c
            ok = (d >= 0) & (d <= window // dil)
            band.append(jnp.where(jnp.asarray(ok), _t5_bucket(jnp.asarray(np.maximum(d, 0) * dil, I32)),
                                  NUM_BUCKETS))
    band_idx = jnp.concatenate(band, axis=0)
    nkt = W // LANES + 1
    t = np.arange(TS)[None, :, None]
    key = (np.arange(nkt)[:, None, None] * LANES + np.arange(LANES)[None, None, :])
    d = W + t - key
    tabs = []
    for window, dil in B_GROUPS:
        ok = (d >= 0) & (d % dil == 0) & (d <= window)
        tabs.append(jnp.where(jnp.asarray(ok), _t5_bucket(jnp.asarray(np.maximum(d, 0), I32)), NUM_BUCKETS))
    samp_idx = jnp.concatenate(tabs, axis=0).reshape(-1, LANES)
    rows = [tz_idx, band_idx, samp_idx]
    total = sum(r.shape[0] for r in rows)
    pad = (-total) % 128
    if pad:
        rows.append(jnp.full((pad, LANES), NUM_BUCKETS, I32))
    return jnp.concatenate(rows, axis=0), U, nkt


def kernel(x_prompt, x_sample, cache_k_a, cache_v_a, cache_kidx_a, cache_k_b, cache_v_b, state_ffn, page_table, a_w_in, a_w_o, a_kn_g, a_kn_b, b_w_kv, b_w_q, b_w_o, ffn_w_up, ffn_conv_w, ffn_conv_b, ffn_w_down, ln_g, ln_b, rel_bias):
    B, T, D = x_prompt.shape
    DB, TS, _ = x_sample.shape
    depth = ffn_w_up.shape[0]
    n_a = a_w_in.shape[0]
    d_ff = ffn_w_down.shape[1]
    W = cache_k_b.shape[1]
    n_pages = page_table.shape[1]
    page = cache_k_a.shape[2]
    past = n_pages * page
    alpha = (2 * depth) ** 0.25
    kvw = KV_HEADS * HEAD_DIM
    a_q = N_HEADS * HEAD_DIM
    a_in = a_w_in.shape[2]
    np_cols = ((a_in + 511) // 512) * 512
    ki0 = a_q + 2 * kvw + IDX_HEADS * IDX_DIM

    idx_all, U, nkt = _table_indices(T, W, TS)
    tabs = _bias_tables(rel_bias * LOG2E, idx_all)
    tz = tabs[:, :U]
    band = tabs[:, U:U + 6 * QB].reshape(N_HEADS, 3, 2, QB, LANES)
    samp = tabs[:, U + 6 * QB:U + 6 * QB + 3 * nkt * TS].reshape(KV_HEADS, GROUP, 3, nkt, TS, LANES)
    samp = samp.transpose(2, 0, 1, 4, 3, 5).reshape(3, KV_HEADS, GROUP * TS, nkt * LANES)
    n_slab = SAT_DIST // LANES + 1
    tzs = tz[:, TZ_OFF:TZ_OFF + n_slab * LANES].reshape(N_HEADS, n_slab, LANES, LANES)[:, :, :TS, :]

    cache_k_a2 = cache_k_a.reshape(n_a, -1, page * KV_HEADS, HEAD_DIM)
    cache_v_a2 = cache_v_a.reshape(n_a, -1, page * KV_HEADS, HEAD_DIM)
    cache_k_b2 = cache_k_b.reshape(DB, W * KV_HEADS, HEAD_DIM)
    cache_v_b2 = cache_v_b.reshape(DB, W * KV_HEADS, HEAD_DIM)
    cache_kidx_t = jnp.swapaxes(cache_kidx_a, 2, 3)

    def pad_rows(x, n):
        return jnp.pad(x, ((0, 0), (0, n - x.shape[1]), (0, 0)))

    f_pad = ((d_ff + FFN_TF - 1) // FFN_TF) * FFN_TF
    pad_f = f_pad - d_ff
    w_g_all = jnp.pad(ffn_w_up[:, :, :d_ff], ((0, 0), (0, 0), (0, pad_f))).astype(BF16)
    w_u_all = jnp.pad(ffn_w_up[:, :, d_ff:], ((0, 0), (0, 0), (0, pad_f))).astype(BF16)
    w_down_all = jnp.pad(ffn_w_down, ((0, 0), (0, pad_f), (0, 0))).astype(BF16)
    conv_w_all = jnp.pad(ffn_conv_w, ((0, 0), (0, 0), (0, pad_f)))
    conv_b_all = jnp.pad(ffn_conv_b, ((0, 0), (0, pad_f))).reshape(depth, 1, f_pad)
    w_in_all = jnp.pad(a_w_in, ((0, 0), (0, 0), (0, np_cols - a_in))).astype(BF16)
    w_o_a, w_o_b = a_w_o.astype(BF16), b_w_o.astype(BF16)
    w_q_all = b_w_q.astype(BF16)
    w_kv_bf = b_w_kv.astype(BF16)[None]

    hp = x_prompt.reshape(B * T, D)
    hs = x_sample.reshape(DB * TS, D)
    hp_bf, hs_bf = hp.astype(BF16), hs.astype(BF16)
    ka_p, va_p, kia_p, ka_s, va_s, kia_s, ffn_p, ffn_s = [], [], [], [], [], [], [], []
    for layer in range(depth):
        if layer < n_a:
            a = layer
            w_o_bf, w_o_layer = w_o_a, a
            proj = _proj(hp_bf, w_in_all, a).reshape(B, T, np_cols)
            k = proj[..., a_q:a_q + kvw].reshape(B, T, KV_HEADS, HEAD_DIM)
            v = proj[..., a_q + kvw:a_q + 2 * kvw].reshape(B, T, KV_HEADS, HEAD_DIM)
            ki = _layer_norm(proj[..., ki0:ki0 + IDX_DIM], a_kn_g[a], a_kn_b[a])
            ka_p.append(k); va_p.append(v); kia_p.append(ki)
            mask = _a1_prompt(ki.astype(BF16), proj, min(TOPK_MAX, T // 4))
            o = _a2_prompt(proj, mask, tz).reshape(B * T, a_q)
            proj_s = _proj(hs_bf, w_in_all, a).reshape(DB, TS, np_cols)
            k_s = proj_s[..., a_q:a_q + kvw]
            v_s = proj_s[..., a_q + kvw:a_q + 2 * kvw]
            ki_s = _layer_norm(proj_s[..., ki0:ki0 + IDX_DIM], a_kn_g[a], a_kn_b[a])
            ka_s.append(k_s.reshape(DB, TS, KV_HEADS, HEAD_DIM))
            va_s.append(v_s.reshape(DB, TS, KV_HEADS, HEAD_DIM))
            kia_s.append(ki_s)
            mask_s = _a1_sample(page_table, proj_s, pad_rows(ki_s, LANES).astype(BF16), cache_kidx_t, a,
                                min(TOPK_MAX, (past + TS) // 4))
            o_s = _a2_sample(page_table, proj_s, mask_s, tzs, pad_rows(k_s, LANES), pad_rows(v_s, LANES),
                             cache_k_a2, cache_v_a2, a).reshape(DB * TS, a_q)
        else:
            if layer == n_a:
                kv_p = _proj(hp_bf, w_kv_bf).reshape(B, T, 2 * kvw)
                kv_s = _proj(hs_bf, w_kv_bf).reshape(DB, TS, 2 * kvw)
                kv_s_pad = pad_rows(kv_s, LANES)
            bl = layer - n_a
            w_o_bf, w_o_layer = w_o_b, bl
            o = _b_prompt(_proj(hp_bf, w_q_all, bl).reshape(B, T, -1), kv_p, band).reshape(B * T, a_q)
            o_s = _b_sample(_proj(hs_bf, w_q_all, bl).reshape(DB, TS, -1), cache_k_b2, cache_v_b2, kv_s_pad,
                            samp).reshape(DB * TS, a_q)
        hp, hp_bf = _mm_ln(o, w_o_bf, w_o_layer, hp, ln_g[layer, 0], ln_b[layer, 0], alpha)
        hs, hs_bf = _mm_ln(o_s, w_o_bf, w_o_layer, hs, ln_g[layer, 0], ln_b[layer, 0], alpha)
        act, st = _ffn_up(hp_bf, w_g_all, w_u_all, conv_w_all, conv_b_all, layer, T)
        ffn_p.append(st.reshape(B, -1, CONV_WIDTH - 1, f_pad)[:, -1, :, :d_ff])
        act_s, st_s = _ffn_act_sample(hs_bf, jnp.pad(state_ffn[layer], ((0, 0), (0, 0), (0, pad_f))),
                                      w_g_all, w_u_all, conv_w_all, conv_b_all, layer, DB, TS)
        ffn_s.append(st_s[..., :d_ff])
        hp, hp_bf = _mm_ln(act, w_down_all, layer, hp, ln_g[layer, 1], ln_b[layer, 1], alpha)
        hs, hs_bf = _mm_ln(act_s, w_down_all, layer, hs, ln_g[layer, 1], ln_b[layer, 1], alpha)
    keep = min(max(w for w, _ in B_GROUPS), T)
    kb_p = kv_p[..., :kvw].reshape(B, T, KV_HEADS, HEAD_DIM)
    vb_p = kv_p[..., kvw:].reshape(B, T, KV_HEADS, HEAD_DIM)
    kb_s = kv_s[..., :kvw].reshape(DB, TS, KV_HEADS, HEAD_DIM)
    vb_s = kv_s[..., kvw:].reshape(DB, TS, KV_HEADS, HEAD_DIM)
    return (hp.reshape(B, T, D), hs.reshape(DB, TS, D), jnp.stack(ka_p), jnp.stack(va_p), jnp.stack(kia_p),
            jnp.stack(ka_s), jnp.stack(va_s), jnp.stack(kia_s), kb_p[:, T - keep:], vb_p[:, T - keep:],
            kb_s, vb_s, jnp.stack(ffn_p), jnp.stack(ffn_s))
```
